```python
import math
import jax, jax.numpy as jnp
from jax import lax
import numpy as np


D_MODEL = 1024
BATCH = 8
SEQ = 4096
DEPTH = 4
DEC_BATCH = 16
DEC_SEQ = 16
PAST_LEN = 4096

CHUNK = 64
R_HEADS = 8
R_HD = 64
R_DIM = R_HEADS * R_HD
W_LORA = 64
A_LORA = 64
G_LORA = 128
R_PROJ = 3 * R_DIM + W_LORA + A_LORA + G_LORA
A_HEADS = 8
A_KV = 2
A_GROUP = A_HEADS // A_KV
A_HD = 64
A_DIM = A_HEADS * A_HD
KV_DIM = A_KV * A_HD
IN_COLS = R_PROJ + A_DIM + 2 * KV_DIM
MIX_DIM = R_DIM + A_DIM
WINDOW = 128
WIN_CHUNKS = WINDOW // CHUNK
KEY_BLOCK = (WIN_CHUNKS + 1) * CHUNK
NUM_BUCKETS = 32
MAX_DISTANCE = 128
PEER_HEADS = 8
N_KEYS = 128
N_EXPERTS = N_KEYS * N_KEYS
PK_DIM = 256
PK_HALF = PK_DIM // 2
PEER_TOPK = 16
PEER_BLOCK = 256
NORM_EPS = 1e-6
GN_EPS = 64e-5
NEG_INF = -1e30

kernel_name = 'hymba_rwkv7_swa_peer_stream_step'

F32 = jnp.float32


def rms_norm(x, g, eps=NORM_EPS):
    xf = x.astype(F32)
    y = xf * lax.rsqrt(jnp.mean(xf * xf, axis=-1, keepdims=True) + eps)
    return (y * g.astype(F32)).astype(x.dtype)


def t5_bucket(rel):
    nb = NUM_BUCKETS // 2
    max_exact = nb // 2
    ret = jnp.where(rel > 0, nb, 0)
    n = jnp.abs(rel)
    nf = jnp.maximum(n, 1).astype(F32)
    large = max_exact + (jnp.log(nf / max_exact) / math.log(MAX_DISTANCE / max_exact)
                         * (nb - max_exact)).astype(jnp.int32)
    large = jnp.minimum(large, nb - 1)
    return ret + jnp.where(n < max_exact, n, large)


def rel_bias_block(rel_bias, n_q, n_k, n_before):
    rel = (jnp.arange(n_k)[None, :] - n_before) - jnp.arange(n_q)[:, None]
    b = rel_bias[t5_bucket(rel)].astype(F32)
    return jnp.transpose(b, (2, 0, 1)).reshape(A_KV, A_GROUP, n_q, n_k)


def sink_softmax(s, sink):
    sk = sink[:, :, None]
    m = jnp.maximum(s.max(-1), sk)
    p = jnp.exp(s - m[..., None])
    return p / (p.sum(-1) + jnp.exp(sk - m))[..., None]


def split_projection(h, w_in, q_g, k_g):
    b, t, _ = h.shape
    z = h @ w_in
    zr, q, k, v = jnp.split(z, [R_PROJ, R_PROJ + A_DIM, R_PROJ + A_DIM + KV_DIM], axis=-1)
    q = rms_norm(q.reshape(b, t, A_HEADS, A_HD), q_g)
    k = rms_norm(k.reshape(b, t, A_KV, A_HD), k_g)
    return zr, q, k, v.reshape(b, t, A_KV, A_HD)


def rwkv7_mix(zr, shift0, wkv0, p):
    bt, t, _ = zr.shape
    prev = jnp.concatenate([shift0[:, None, :].astype(zr.dtype), zr[:, :-1]], axis=1)
    zs = (zr + (prev - zr) * p['mu']).astype(F32)
    r, k, v, w_lo, a_lo, g_lo = jnp.split(
        zs, [R_DIM, 2 * R_DIM, 3 * R_DIM, 3 * R_DIM + W_LORA, 3 * R_DIM + W_LORA + A_LORA], axis=-1)
    w_log = -jax.nn.softplus(-(p['w0'].astype(F32) + jnp.tanh(w_lo) @ p['w2'].astype(F32))) - 0.5
    decay = jnp.exp(-jnp.exp(w_log))
    a = jax.nn.sigmoid(p['a0'].astype(F32) + a_lo @ p['a2'].astype(F32))
    g = jax.nn.sigmoid(g_lo) @ p['g2'].astype(F32)
    heads = lambda u: u.reshape(bt, t, R_HEADS, R_HD)
    kk = heads(k * p['k_k'].astype(F32))
    kk = kk / jnp.maximum(jnp.sqrt(jnp.sum(kk * kk, axis=-1, keepdims=True)), 1e-12)
    k = k * (1.0 + (a - 1.0) * p['k_a'].astype(F32))
    r, k, v, decay, a = map(heads, (r, k, v, decay, a))

    def step(S, inp):
        r_t, w_t, k_t, v_t, kk_t, a_t = inp
        sa = jnp.einsum('bhvk,bhk->bhv', S, -kk_t)
        S = (S * w_t[:, :, None, :] + sa[..., None] * (kk_t * a_t)[:, :, None, :]
             + v_t[..., None] * k_t[:, :, None, :])
        return S, jnp.einsum('bhvk,bhk->bhv', S, r_t)

    tm = lambda u: jnp.moveaxis(u, 1, 0)
    S_fin, y = lax.scan(step, wkv0, (tm(r), tm(decay), tm(k), tm(v), tm(kk), tm(a)))
    y = jnp.moveaxis(y, 0, 1)
    mu_y = y.mean(-1, keepdims=True)
    var_y = jnp.mean((y - mu_y) ** 2, axis=-1, keepdims=True)
    yn = ((y - mu_y) * lax.rsqrt(var_y + GN_EPS)).reshape(bt, t, R_DIM)
    yn = yn * p['gn_g'].astype(F32) + p['gn_b'].astype(F32)
    bonus = jnp.sum(r * k * p['r_k'].astype(F32), axis=-1, keepdims=True) * v
    out = (yn + bonus.reshape(bt, t, R_DIM)) * g
    return out, S_fin, zr[:, -1]


def swa_prompt(q, k, v, sink, bias):
    b, s = q.shape[:2]
    nc = s // CHUNK
    qb = q.reshape(b, nc, CHUNK, A_KV, A_GROUP, A_HD)

    def band(u):
        uc = u.reshape(b, nc, CHUNK, A_KV, A_HD)
        up = jnp.pad(uc, ((0, 0), (WIN_CHUNKS, 0), (0, 0), (0, 0), (0, 0)))
        return jnp.concatenate([up[:, i:i + nc] for i in range(WIN_CHUNKS + 1)], axis=2)

    kb, vb = band(k), band(v)
    sc = jnp.einsum('bnqhgd,bnkhd->bnhgqk', qb, kb).astype(F32) * (A_HD ** -0.5) + bias
    key_chunk = jnp.arange(nc)[:, None] + (jnp.arange(KEY_BLOCK) // CHUNK - WIN_CHUNKS)[None, :]
    sc = jnp.where((key_chunk >= 0)[None, :, None, None, None, :], sc, NEG_INF)
    probs = sink_softmax(sc, sink)
    o = jnp.einsum('bnhgqk,bnkhd->bnqhgd', probs.astype(vb.dtype), vb)
    return o.reshape(b, s, A_DIM)


def swa_sample(q, k, v, k_cache, v_cache, sink, bias):
    b, t = q.shape[:2]
    kk = jnp.concatenate([k_cache.astype(k.dtype), k], axis=1)
    vv = jnp.concatenate([v_cache.astype(v.dtype), v], axis=1)
    qb = q.reshape(b, t, A_KV, A_GROUP, A_HD)
    sc = jnp.einsum('bqhgd,bkhd->bhgqk', qb, kk).astype(F32) * (A_HD ** -0.5) + bias
    probs = sink_softmax(sc, sink)
    o = jnp.einsum('bhgqk,bkhd->bqhgd', probs.astype(vv.dtype), vv)
    return o.reshape(b, t, A_DIM)


def peer_ffn(h, p):
    shape = h.shape
    hf = h.reshape(-1, D_MODEL)
    n = hf.shape[0]
    nb = -(-n // PEER_BLOCK)
    hb = jnp.pad(hf, ((0, nb * PEER_BLOCK - n), (0, 0))).reshape(nb, PEER_BLOCK, D_MODEL)

    def block(xb):
        qq = (xb @ p['w_pq']).astype(F32).reshape(PEER_BLOCK, PEER_HEADS, 2, PK_HALF)
        s = jnp.einsum('thpd,pnd->thpn', qq, p['sub_keys'].astype(F32))
        sv, si = lax.top_k(s, PEER_TOPK)
        cand = (sv[:, :, 0, :, None] + sv[:, :, 1, None, :]).reshape(PEER_BLOCK, PEER_HEADS, PEER_TOPK * PEER_TOPK)
        cidx = (si[:, :, 0, :, None] * N_KEYS + si[:, :, 1, None, :]).reshape(PEER_BLOCK, PEER_HEADS, PEER_TOPK * PEER_TOPK)
        top_s, top_i = lax.top_k(cand, PEER_TOPK)
        eidx = jnp.take_along_axis(cidx, top_i, axis=-1)
        gate = jax.nn.softmax(top_s, axis=-1)
        u = p['peer_u'][eidx]
        act = jax.nn.gelu(jnp.einsum('thkd,td->thk', u, xb).astype(F32), approximate=False)
        vrows = p['peer_v'][eidx]
        return jnp.einsum('thk,thkd->td', (gate * act).astype(vrows.dtype), vrows)

    out = lax.map(block, hb).reshape(nb * PEER_BLOCK, D_MODEL)[:n]
    return out.reshape(shape).astype(h.dtype)


def setup_inputs(seed: int = 0) -> dict:
    key = jax.random.key(seed)
    ks = jax.random.split(key, 32)
    nrm = lambda k, shp, sc: jax.random.normal(k, shp, F32) * sc
    n_cache = min(WINDOW, PAST_LEN)
    return {
        'x_prompt': nrm(ks[0], (BATCH, SEQ, D_MODEL), 1.0),
        'x_sample': nrm(ks[1], (DEC_BATCH, DEC_SEQ, D_MODEL), 1.0),
        'state_rwkv_wkv': nrm(ks[2], (DEPTH, DEC_BATCH, R_HEADS, R_HD, R_HD), 0.3),
        'state_rwkv_shift': nrm(ks[3], (DEPTH, DEC_BATCH, R_PROJ), 1.0),
        'cache_swa_k': nrm(ks[4], (DEPTH, DEC_BATCH, n_cache, A_KV, A_HD), 1.0),
        'cache_swa_v': nrm(ks[5], (DEPTH, DEC_BATCH, n_cache, A_KV, A_HD), 1.0),
        'ln1_g': 1.0 + nrm(ks[6], (DEPTH, D_MODEL), 0.02),
        'w_in': nrm(ks[7], (DEPTH, D_MODEL, IN_COLS), D_MODEL ** -0.5),
        'mu_shift': jax.random.uniform(ks[8], (DEPTH, R_PROJ), F32),
        'w0': jax.random.uniform(ks[9], (DEPTH, R_DIM), F32, -6.0, -1.0),
        'w2': nrm(ks[10], (DEPTH, W_LORA, R_DIM), 0.1),
        'a0': nrm(ks[11], (DEPTH, R_DIM), 0.1),
        'a2': nrm(ks[12], (DEPTH, A_LORA, R_DIM), A_LORA ** -0.5),
        'g2': nrm(ks[13], (DEPTH, G_LORA, R_DIM), G_LORA ** -0.5),
        'k_k': 0.85 + nrm(ks[14], (DEPTH, R_DIM), 0.02),
        'k_a': 1.0 + nrm(ks[15], (DEPTH, R_DIM), 0.02),
        'r_k': nrm(ks[16], (DEPTH, R_HEADS, R_HD), 0.1),
        'gn_g': 1.0 + nrm(ks[17], (DEPTH, R_DIM), 0.02),
        'gn_b': nrm(ks[18], (DEPTH, R_DIM), 0.01),
        'q_norm_g': 1.0 + nrm(ks[19], (DEPTH, A_HD), 0.02),
        'k_norm_g': 1.0 + nrm(ks[20], (DEPTH, A_HD), 0.02),
        'attn_sink': nrm(ks[21], (DEPTH, A_HEADS), 0.5),
        'rel_bias': nrm(ks[22], (NUM_BUCKETS, A_HEADS), 0.5),
        'w_out': nrm(ks[23], (DEPTH, MIX_DIM, D_MODEL), MIX_DIM ** -0.5),
        'ln2_g': 1.0 + nrm(ks[24], (DEPTH, D_MODEL), 0.02),
        'w_pq': nrm(ks[25], (DEPTH, D_MODEL, PEER_HEADS * PK_DIM), D_MODEL ** -0.5),
        'sub_keys': nrm(ks[26], (DEPTH, 2, N_KEYS, PK_HALF), PK_HALF ** -0.5),
        'peer_u': nrm(ks[27], (DEPTH, N_EXPERTS, D_MODEL), D_MODEL ** -0.5),
        'peer_v': nrm(ks[28], (DEPTH, N_EXPERTS, D_MODEL), (PEER_HEADS * PEER_TOPK) ** -0.5),
    }


def reference(x_prompt, x_sample, state_rwkv_wkv, state_rwkv_shift, cache_swa_k, cache_swa_v,
              ln1_g, w_in, mu_shift, w0, w2, a0, a2, g2, k_k, k_a, r_k, gn_g, gn_b,
              q_norm_g, k_norm_g, attn_sink, rel_bias, w_out, ln2_g, w_pq, sub_keys, peer_u, peer_v):
    b_p, s_p = x_prompt.shape[:2]
    b_s, t_s = x_sample.shape[:2]
    n_cache = cache_swa_k.shape[2]
    n_keep = min(WINDOW, s_p)
    bias_p = rel_bias_block(rel_bias, CHUNK, KEY_BLOCK, WIN_CHUNKS * CHUNK)
    bias_s = rel_bias_block(rel_bias, t_s, n_cache + t_s, n_cache)
    xp, xs = x_prompt, x_sample
    pw, psh, pk, pv, sw, ssh, sk, sv = [], [], [], [], [], [], [], []
    for l in range(DEPTH):
        rp = {'mu': mu_shift[l], 'w0': w0[l], 'w2': w2[l], 'a0': a0[l], 'a2': a2[l], 'g2': g2[l],
              'k_k': k_k[l], 'k_a': k_a[l], 'r_k': r_k[l], 'gn_g': gn_g[l], 'gn_b': gn_b[l]}
        pp = {'w_pq': w_pq[l], 'sub_keys': sub_keys[l], 'peer_u': peer_u[l], 'peer_v': peer_v[l]}
        sink = attn_sink[l].astype(F32).reshape(A_KV, A_GROUP)
        zr, q, k, v = split_projection(rms_norm(xp, ln1_g[l]), w_in[l], q_norm_g[l], k_norm_g[l])
        r_out, wkv_new, sh_new = rwkv7_mix(zr, jnp.zeros((b_p, R_PROJ), zr.dtype),
                                           jnp.zeros((b_p, R_HEADS, R_HD, R_HD), F32), rp)
        a_out = swa_prompt(q, k, v, sink, bias_p)
        xp = xp + jnp.concatenate([r_out.astype(xp.dtype), a_out.astype(xp.dtype)], axis=-1) @ w_out[l]
        xp = xp + peer_ffn(rms_norm(xp, ln2_g[l]), pp)
        pw.append(wkv_new.astype(xp.dtype)); psh.append(sh_new)
        pk.append(k[:, s_p - n_keep:]); pv.append(v[:, s_p - n_keep:])
        zr, q, k, v = split_projection(rms_norm(xs, ln1_g[l]), w_in[l], q_norm_g[l], k_norm_g[l])
        r_out, wkv_new, sh_new = rwkv7_mix(zr, state_rwkv_shift[l], state_rwkv_wkv[l].astype(F32), rp)
        a_out = swa_sample(q, k, v, cache_swa_k[l], cache_swa_v[l], sink, bias_s)
        xs = xs + jnp.concatenate([r_out.astype(xs.dtype), a_out.astype(xs.dtype)], axis=-1) @ w_out[l]
        xs = xs + peer_ffn(rms_norm(xs, ln2_g[l]), pp)
        sw.append(wkv_new.astype(xs.dtype)); ssh.append(sh_new)
        sk.append(k); sv.append(v)
    p_wkv, p_shift, p_k, p_v = jnp.stack(pw), jnp.stack(psh), jnp.stack(pk), jnp.stack(pv)
    s_wkv, s_shift, s_k, s_v = jnp.stack(sw), jnp.stack(ssh), jnp.stack(sk), jnp.stack(sv)
    return (xp, xs, p_wkv, p_shift, p_k, p_v, s_wkv, s_shift, s_k, s_v)
```

```python
import functools
import math

import numpy as np
import jax
import jax.numpy as jnp
from jax import lax
from jax.experimental import pallas as pl
from jax.experimental.pallas import tpu as pltpu

F32 = jnp.float32
BF16 = jnp.bfloat16
I32 = jnp.int32

D_MODEL = 1024
CHUNK = 64
R_HEADS = 8
R_HD = 64
R_DIM = R_HEADS * R_HD
W_LORA = 64
A_LORA = 64
G_LORA = 128
R_PROJ = 3 * R_DIM + W_LORA + A_LORA + G_LORA
A_HEADS = 8
A_KV = 2
A_GROUP = A_HEADS // A_KV
A_HD = 64
A_DIM = A_HEADS * A_HD
KV_DIM = A_KV * A_HD
IN_COLS = R_PROJ + A_DIM + 2 * KV_DIM
WINDOW = 128
WIN_CHUNKS = WINDOW // CHUNK
NUM_BUCKETS = 32
MAX_DISTANCE = 128
PEER_HEADS = 8
N_KEYS = 128
PK_DIM = 256
PK_HALF = PK_DIM // 2
PEER_TOPK = 16
N_SEL = PEER_HEADS * PEER_TOPK
NORM_EPS = 1e-6
GN_EPS = 64e-5
NEG_INF = -1e30

LANES = 128
N_PAIRS = R_DIM // LANES
VMEM_LIMIT = 48 * 1024 * 1024
N_CAND = PEER_TOPK + 8 * (PEER_TOPK - 1)


def _params(sem):
    return pltpu.CompilerParams(dimension_semantics=sem, vmem_limit_bytes=VMEM_LIMIT)


def _mm(a, b):
    return jnp.dot(a, b, preferred_element_type=F32)


def _mm_nt(a, b):
    return lax.dot_general(a, b, (((1,), (1,)), ((), ())), preferred_element_type=F32)


def _mm_tn(a, b):
    return lax.dot_general(a, b, (((0,), (0,)), ((), ())), preferred_element_type=F32)


def _split2(a):
    hi = a.astype(BF16)
    return hi, (a - hi.astype(F32)).astype(BF16)


def _split3(a):
    hi = a.astype(BF16)
    r = a - hi.astype(F32)
    mid = r.astype(BF16)
    return hi, mid, (r - mid.astype(F32)).astype(BF16)


def _x3(mm, a, b):
    ah, al = _split2(a)
    bh, bl = _split2(b)
    return mm(ah, bh) + mm(ah, bl) + mm(al, bh)


def _exact_lhs(mm, a_bf16, b):
    b0, b1, b2 = _split3(b)
    return mm(a_bf16, b0) + mm(a_bf16, b1) + mm(a_bf16, b2)


def _seg_sum(x, e_bf16):
    x0, x1, x2 = _split3(x)
    return _mm(x0, e_bf16) + _mm(x1, e_bf16) + _mm(x2, e_bf16)


def _sigmoid(x):
    return 1.0 / (1.0 + jnp.exp(-x))


def _norm_matmul_kernel(x_ref, g_ref, w_ref, o_ref, h_ref):
    @pl.when(pl.program_id(1) == 0)
    def _():
        x = x_ref[...]
        ms = jnp.mean(x * x, axis=-1, keepdims=True)
        h_ref[...] = (x * lax.rsqrt(ms + NORM_EPS) * g_ref[...]).astype(BF16)

    o_ref[...] = _mm(h_ref[...], w_ref[...])


def _norm_matmul(x, g, w_bf16, tn):
    n, k = x.shape
    m = w_bf16.shape[1]
    tm = min(n, 512)
    return pl.pallas_call(
        _norm_matmul_kernel,
        grid=(n // tm, m // tn),
        in_specs=[pl.BlockSpec((tm, k), lambda i, j: (i, 0)),
                  pl.BlockSpec((1, k), lambda i, j: (0, 0)),
                  pl.BlockSpec((k, tn), lambda i, j: (0, j))],
        out_specs=pl.BlockSpec((tm, tn), lambda i, j: (i, j)),
        out_shape=jax.ShapeDtypeStruct((n, m), F32),
        scratch_shapes=[pltpu.VMEM((tm, k), BF16)],
        compiler_params=_params(("parallel", "arbitrary")),
        name="norm_inproj",
    )(x, g, w_bf16)


def _prep_kernel(z_ref, zp_ref, sh_ref, mu_ref, w0_ref, a0_ref, kk_ref, ka_ref, rk_ref,
                 w2_ref, a2_ref, g2_ref, e_ref, qg_ref, kg_ref,
                 r_o, k_o, v_o, kk_o, b_o, lw_o, g_o, bo_o, qn_o, kn_o):
    i = pl.program_id(1)
    zt = z_ref[0]
    tp = zt.shape[0]
    zr = zt[:, :R_PROJ]
    prev_row = jnp.where(i == 0, sh_ref[0], zp_ref[0][7:8, :R_PROJ])
    row = lax.broadcasted_iota(I32, (tp, 1), 0)
    prev = jnp.where(row == 0, prev_row, pltpu.roll(zr, 1, axis=0))
    zs = zr + (prev - zr) * mu_ref[...]
    r = zs[:, 0:R_DIM]
    k = zs[:, R_DIM:2 * R_DIM]
    v = zs[:, 2 * R_DIM:3 * R_DIM]
    lo = zs[:, 3 * R_DIM:3 * R_DIM + W_LORA + A_LORA]
    g_lo = zs[:, 3 * R_DIM + W_LORA + A_LORA:R_PROJ]
    e = e_ref[...]
    w_in = -(w0_ref[...] + _mm(jnp.tanh(lo).astype(BF16), w2_ref[...]))
    softplus = jnp.maximum(w_in, 0.0) + jnp.log1p(jnp.exp(-jnp.abs(w_in)))
    w_log = -softplus - 0.5
    lw_o[0] = -jnp.exp(w_log)
    a = _sigmoid(a0_ref[...] + _mm(lo.astype(BF16), a2_ref[...]))
    g_o[0] = _mm(_sigmoid(g_lo).astype(BF16), g2_ref[...])
    kk = k * kk_ref[...]
    kk = kk / jnp.maximum(jnp.sqrt(_seg_sum(kk * kk, e)), 1e-12)
    k2 = k * (1.0 + (a - 1.0) * ka_ref[...])
    r_o[0] = r
    k_o[0] = k2
    v_o[0] = v
    kk_o[0] = kk
    b_o[0] = kk * a
    bo_o[0] = _seg_sum(r * k2 * rk_ref[...], e) * v
    q = zt[:, R_PROJ:R_PROJ + A_DIM]
    qn_o[0] = q * lax.rsqrt(_seg_sum(q * q, e) * (1.0 / A_HD) + NORM_EPS) * qg_ref[...]
    kx = zt[:, R_PROJ + A_DIM:R_PROJ + A_DIM + KV_DIM]
    e_kv = e[:KV_DIM, :KV_DIM]
    kn_o[0] = kx * lax.rsqrt(_seg_sum(kx * kx, e_kv) * (1.0 / A_HD) + NORM_EPS) * kg_ref[...]


def _prep(z, shift0, lp):
    b, t, _ = z.shape
    tp = min(t, 256)
    row = lambda a: a.reshape(1, -1)
    vec = lambda n: pl.BlockSpec((1, n), lambda bi, i: (0, 0))
    full = lambda s: pl.BlockSpec(s, lambda bi, i: (0, 0))
    wide = pl.BlockSpec((1, tp, R_DIM), lambda bi, i: (bi, i, 0))
    outs = [jax.ShapeDtypeStruct((b, t, R_DIM), F32)] * 9 + [jax.ShapeDtypeStruct((b, t, KV_DIM), F32)]
    return pl.pallas_call(
        _prep_kernel,
        grid=(b, t // tp),
        in_specs=[pl.BlockSpec((1, tp, IN_COLS), lambda bi, i: (bi, i, 0)),
                  pl.BlockSpec((1, 8, IN_COLS), lambda bi, i: (bi, jnp.maximum(i * (tp // 8) - 1, 0), 0)),
                  pl.BlockSpec((1, 1, R_PROJ), lambda bi, i: (bi, 0, 0)),
                  vec(R_PROJ), vec(R_DIM), vec(R_DIM), vec(R_DIM), vec(R_DIM), vec(R_DIM),
                  full((LANES, R_DIM)), full((LANES, R_DIM)), full((G_LORA, R_DIM)),
                  full((R_DIM, R_DIM)), vec(A_DIM), vec(KV_DIM)],
        out_specs=[wide] * 9 + [pl.BlockSpec((1, tp, KV_DIM), lambda bi, i: (bi, i, 0))],
        out_shape=outs,
        compiler_params=_params(("parallel", "parallel")),
        name="rwkv_prep",
    )(z, z, shift0.reshape(b, 1, R_PROJ), row(lp["mu"]), row(lp["w0"]), row(lp["a0"]), row(lp["k_k"]),
      row(lp["k_a"]), row(lp["r_k"]), lp["w2p"], lp["a2p"], lp["g2"], lp["e64"], row(lp["q_gain"]),
      row(lp["k_gain"]))


def _stack_heads(x, lo_mask):
    return jnp.concatenate([jnp.where(lo_mask, x, 0.0), jnp.where(lo_mask, 0.0, x)], axis=0)


def _chunk_kernel(r_ref, k_ref, v_ref, kk_ref, b_ref, lw_ref, s0_ref, y_ref, sf_ref, s_ref):
    c = pl.program_id(1)
    L = CHUNK

    @pl.when(c == 0)
    def _():
        s_ref[...] = s0_ref[0]

    lane = lax.broadcasted_iota(I32, (1, LANES), 1)
    lo_mask = lane < R_HD
    rr = lax.broadcasted_iota(I32, (L, 2 * L), 0)
    cc = lax.broadcasted_iota(I32, (L, 2 * L), 1)
    cc = jnp.where(cc >= L, cc - L, cc)
    strict = rr > cc
    incl = rr >= cc
    t_r = lax.broadcasted_iota(I32, (L, L), 0)
    t_c = lax.broadcasted_iota(I32, (L, L), 1)
    tri = (t_r >= t_c).astype(BF16)
    col2 = lax.broadcasted_iota(I32, (L, 2 * L), 1) < L
    eye_r = lax.broadcasted_iota(I32, (2 * L, 2 * L), 0)
    eye_c = lax.broadcasted_iota(I32, (2 * L, 2 * L), 1)
    eye = (eye_r == eye_c).astype(F32)

    for p in range(N_PAIRS):
        sl = slice(p * LANES, (p + 1) * LANES)
        r = r_ref[0][:, sl]
        k = k_ref[0][:, sl]
        v = v_ref[0][:, sl]
        kk = kk_ref[0][:, sl]
        bb = b_ref[0][:, sl]
        lw = lw_ref[0][:, sl]
        s0 = s_ref[p]

        cl = _exact_lhs(_mm, tri, lw)
        cl_last = cl[L - 1:L, :]
        e_neg = jnp.exp(-cl)
        e_last = jnp.exp(cl_last - cl)
        rt = r * jnp.exp(cl)
        at = kk * jnp.exp(cl - lw)
        kt = k * e_neg
        bt = bb * e_neg

        lhs = jnp.concatenate([at, rt], axis=0)
        rhs = jnp.concatenate([_stack_heads(kt, lo_mask), _stack_heads(bt, lo_mask)], axis=0)
        gm = _x3(_mm_nt, lhs, rhs)
        mk = jnp.where(strict, gm[:L, :2 * L], 0.0)
        mb = jnp.where(strict, gm[:L, 2 * L:], 0.0)
        hk = jnp.where(incl, gm[L:, :2 * L], 0.0)
        hb = jnp.where(incl, gm[L:, 2 * L:], 0.0)

        nil = -jnp.concatenate([jnp.where(col2, mb, 0.0), jnp.where(col2, 0.0, mb)], axis=0)
        tinv = eye + nil
        for _ in range(5):
            nil = _x3(_mm, nil, nil)
            tinv = tinv + _x3(_mm, nil, tinv)

        vs = _stack_heads(v, lo_mask)
        rhs_u = -(_x3(_mm_nt, at, s0) + _x3(_mm, mk, vs))
        us = _x3(_mm, tinv, _stack_heads(rhs_u, lo_mask))
        y = _x3(_mm_nt, rt, s0) + _x3(_mm, hk, vs) + _x3(_mm, hb, us)
        y_ref[0, :, sl] = y
        s_ref[p] = (s0 * jnp.exp(cl_last)
                    + _x3(_mm_tn, vs, _stack_heads(k * e_last, lo_mask))
                    + _x3(_mm_tn, us, _stack_heads(bb * e_last, lo_mask)))

    @pl.when(c == pl.num_programs(1) - 1)
    def _():
        sf_ref[0] = s_ref[...]


def _chunk_scan(r, k, v, kk, bb, lw, s0_bd):
    b, t, _ = r.shape
    wide = pl.BlockSpec((1, CHUNK, R_DIM), lambda bi, c: (bi, c, 0))
    st = pl.BlockSpec((1, N_PAIRS, LANES, LANES), lambda bi, c: (bi, 0, 0, 0))
    return pl.pallas_call(
        _chunk_kernel,
        grid=(b, t // CHUNK),
        in_specs=[wide] * 6 + [st],
        out_specs=[wide, st],
        out_shape=[jax.ShapeDtypeStruct((b, t, R_DIM), F32),
                   jax.ShapeDtypeStruct((b, N_PAIRS, LANES, LANES), F32)],
        scratch_shapes=[pltpu.VMEM((N_PAIRS, LANES, LANES), F32)],
        compiler_params=_params(("parallel", "arbitrary")),
        name="rwkv_chunk",
    )(r, k, v, kk, bb, lw, s0_bd)


def _swa_kernel(*refs, n_seg, banded):
    q_ref = refs[0]
    k_refs = refs[1:1 + n_seg]
    v_refs = refs[1 + n_seg:1 + 2 * n_seg]
    bias_ref, sink_ref, o_ref = refs[1 + 2 * n_seg:]
    c = pl.program_id(1)
    q = q_ref[0]
    kcat = jnp.concatenate([kr[0] for kr in k_refs], axis=0)
    vcat = jnp.concatenate([vr[0] for vr in v_refs], axis=0)
    n_k = kcat.shape[0]
    lane = lax.broadcasted_iota(I32, (1, LANES), 1)
    lo_mask = lane < A_HD
    k_sw = pltpu.roll(kcat, A_HD, axis=1)
    v_sw = pltpu.roll(vcat, A_HD, axis=1)
    k_dup = [jnp.where(lo_mask, kcat, k_sw).astype(BF16), jnp.where(lo_mask, k_sw, kcat).astype(BF16)]
    v_dup = [jnp.where(lo_mask, vcat, v_sw).astype(BF16), jnp.where(lo_mask, v_sw, vcat).astype(BF16)]
    if banded:
        key_chunk = c - WIN_CHUNKS + lax.broadcasted_iota(I32, (1, n_k), 1) // CHUNK
        valid = key_chunk >= 0
    for pair in range(A_HEADS // 2):
        qp = q[:, pair * LANES:(pair + 1) * LANES]
        outs = []
        for w in range(2):
            hq = 2 * pair + w
            kvh = hq // A_GROUP
            qm = jnp.where(lo_mask if w == 0 else jnp.logical_not(lo_mask), qp, 0.0).astype(BF16)
            s = _mm_nt(qm, k_dup[kvh]) * (A_HD ** -0.5) + bias_ref[hq]
            if banded:
                s = jnp.where(valid, s, NEG_INF)
            sink = sink_ref[0:1, hq:hq + 1]
            m = jnp.maximum(jnp.max(s, axis=-1, keepdims=True), sink)
            pr = jnp.exp(s - m)
            den = jnp.sum(pr, axis=-1, keepdims=True) + jnp.exp(sink - m)
            outs.append(_mm((pr / den).astype(BF16), v_dup[kvh]))
        o_ref[0, :, pair * LANES:(pair + 1) * LANES] = jnp.where(lo_mask, outs[0], outs[1])


def _swa_prompt(qn, kn, z, bias, sink):
    b, t, _ = qn.shape
    v_col = (R_PROJ + A_DIM + KV_DIM) // KV_DIM
    seg = lambda s, col: pl.BlockSpec(
        (1, CHUNK, KV_DIM), lambda bi, c: (bi, jnp.maximum(c - WIN_CHUNKS + s, 0), col))
    n_seg = WIN_CHUNKS + 1
    return pl.pallas_call(
        functools.partial(_swa_kernel, n_seg=n_seg, banded=True),
        grid=(b, t // CHUNK),
        in_specs=[pl.BlockSpec((1, CHUNK, A_DIM), lambda bi, c: (bi, c, 0))]
                 + [seg(s, 0) for s in range(n_seg)] + [seg(s, v_col) for s in range(n_seg)]
                 + [pl.BlockSpec(bias.shape, lambda bi, c: (0, 0, 0)),
                    pl.BlockSpec((1, A_HEADS), lambda bi, c: (0, 0))],
        out_specs=pl.BlockSpec((1, CHUNK, A_DIM), lambda bi, c: (bi, c, 0)),
        out_shape=jax.ShapeDtypeStruct((b, t, A_DIM), F32),
        compiler_params=_params(("parallel", "parallel")),
        name="swa_prompt",
    )(qn, *([kn] * n_seg), *([z] * n_seg), bias, sink.reshape(1, A_HEADS))


def _swa_sample(qn, kn, v_new, k_cache, v_cache, bias, sink):
    b, t, _ = qn.shape
    n_cache = k_cache.shape[1]
    cur = lambda w: pl.BlockSpec((1, t, w), lambda bi, c: (bi, 0, 0))
    old = pl.BlockSpec((1, n_cache, KV_DIM), lambda bi, c: (bi, 0, 0))
    return pl.pallas_call(
        functools.partial(_swa_kernel, n_seg=2, banded=False),
        grid=(b, 1),
        in_specs=[cur(A_DIM), old, cur(KV_DIM), old, cur(KV_DIM),
                  pl.BlockSpec(bias.shape, lambda bi, c: (0, 0, 0)),
                  pl.BlockSpec((1, A_HEADS), lambda bi, c: (0, 0))],
        out_specs=cur(A_DIM),
        out_shape=jax.ShapeDtypeStruct((b, t, A_DIM), F32),
        compiler_params=_params(("parallel", "parallel")),
        name="swa_sample",
    )(qn, k_cache, kn, v_cache, v_new, bias, sink.reshape(1, A_HEADS))


def _mix_out_kernel(y_ref, bo_ref, g_ref, a_ref, x_ref, gng_ref, gnb_ref, e_ref, wo_ref, ln_ref, wq_ref,
                    xo_ref, h_ref, qq_ref):
    e = e_ref[...]
    y = y_ref[...]
    mu = _seg_sum(y, e) * (1.0 / R_HD)
    d = y - mu
    var = _seg_sum(d * d, e) * (1.0 / R_HD)
    yn = d * lax.rsqrt(var + GN_EPS) * gng_ref[...] + gnb_ref[...]
    mix_r = ((yn + bo_ref[...]) * g_ref[...]).astype(BF16)
    x = (x_ref[...] + _mm(mix_r, wo_ref[:R_DIM, :]) + _mm(a_ref[...].astype(BF16), wo_ref[R_DIM:, :]))
    xo_ref[...] = x
    ms = jnp.mean(x * x, axis=-1, keepdims=True)
    h = x * lax.rsqrt(ms + NORM_EPS) * ln_ref[...]
    h_ref[...] = h
    qq_ref[...] = _mm(h.astype(BF16), wq_ref[...])


def _mix_out(y, bonus, g, a_out, x, lp):
    n = x.shape[0]
    tm = min(n, 256)
    nq = PEER_HEADS * PK_DIM
    rows = lambda w: pl.BlockSpec((tm, w), lambda i: (i, 0))
    full = lambda s: pl.BlockSpec(s, lambda i: (0, 0))
    return pl.pallas_call(
        _mix_out_kernel,
        grid=(n // tm,),
        in_specs=[rows(R_DIM), rows(R_DIM), rows(R_DIM), rows(A_DIM), rows(D_MODEL),
                  full((1, R_DIM)), full((1, R_DIM)), full((R_DIM, R_DIM)),
                  full((D_MODEL, D_MODEL)), full((1, D_MODEL)), full((D_MODEL, nq))],
        out_specs=[rows(D_MODEL), rows(D_MODEL), rows(nq)],
        out_shape=[jax.ShapeDtypeStruct((n, D_MODEL), F32), jax.ShapeDtypeStruct((n, D_MODEL), F32),
                   jax.ShapeDtypeStruct((n, nq), F32)],
        compiler_params=_params(("parallel",)),
        name="mix_out_query",
    )(y, bonus, g, a_out, x, lp["gn_g"].reshape(1, -1), lp["gn_b"].reshape(1, -1), lp["e64"],
      lp["w_out"], lp["ln2_g"].reshape(1, -1), lp["w_pq"])


def _pick_rounds(s_ref, n_rows, emit):
    tb = s_ref.shape[1]
    rowid = lax.broadcasted_iota(I32, (n_rows, tb), 0)
    for rnd in range(PEER_TOPK):
        s = s_ref[...]
        m = jnp.max(s, axis=0, keepdims=True)
        idx = jnp.min(jnp.where(s == m, rowid, n_rows), axis=0, keepdims=True)
        hit = rowid == idx
        s_ref[...] = jnp.where(hit, -jnp.inf, s)
        emit(rnd, m, idx, hit)


def _topk_kernel(qq_ref, keys_ref, ei_ref, gate_ref, s_ref, sv_ref, si_ref, c_ref, ci_ref, ts_ref):
    tb = qq_ref.shape[0]
    for half in range(2):
        qh = qq_ref[:, half * PK_HALF:(half + 1) * PK_HALF]
        s_ref[...] = _x3(_mm_nt, keys_ref[half], qh)

        def emit1(rnd, m, idx, hit, half=half):
            sv_ref[half, rnd:rnd + 1, :] = m
            si_ref[half, rnd:rnd + 1, :] = idx

        _pick_rounds(s_ref, N_KEYS, emit1)

    row0 = 0
    for a in range(PEER_TOPK):
        nb = PEER_TOPK if a == 0 else 8
        c_ref[row0:row0 + nb, :] = sv_ref[0, a:a + 1, :] + sv_ref[1, 0:nb, :]
        ci_ref[row0:row0 + nb, :] = si_ref[0, a:a + 1, :] * N_KEYS + si_ref[1, 0:nb, :]
        row0 += nb

    def emit2(rnd, m, idx, hit):
        ts_ref[rnd:rnd + 1, :] = m
        ei_ref[0, rnd:rnd + 1, :] = jnp.max(jnp.where(hit, ci_ref[...], -1), axis=0, keepdims=True)

    _pick_rounds(c_ref, N_CAND, emit2)
    ts = ts_ref[...]
    ex = jnp.exp(ts - ts[0:1, :])
    gate_ref[0] = ex / jnp.sum(ex, axis=0, keepdims=True)


def _topk(qq, sub_keys):
    n = qq.shape[0]
    tb = min(n, 512)
    out = pl.BlockSpec((1, PEER_TOPK, tb), lambda i, h: (h, 0, i))
    return pl.pallas_call(
        _topk_kernel,
        grid=(n // tb, PEER_HEADS),
        in_specs=[pl.BlockSpec((tb, PK_DIM), lambda i, h: (i, h)),
                  pl.BlockSpec((2, N_KEYS, PK_HALF), lambda i, h: (0, 0, 0))],
        out_specs=[out, out],
        out_shape=[jax.ShapeDtypeStruct((PEER_HEADS, PEER_TOPK, n), I32),
                   jax.ShapeDtypeStruct((PEER_HEADS, PEER_TOPK, n), F32)],
        scratch_shapes=[pltpu.VMEM((N_KEYS, tb), F32), pltpu.VMEM((2, PEER_TOPK, tb), F32),
                        pltpu.VMEM((2, PEER_TOPK, tb), I32), pltpu.VMEM((N_CAND, tb), F32),
                        pltpu.VMEM((N_CAND, tb), I32), pltpu.VMEM((PEER_TOPK, tb), F32)],
        compiler_params=_params(("parallel", "parallel")),
        name="peer_topk",
    )(qq, sub_keys)


def _row_copies(idx_ref, t, slot, u_hbm, v_hbm, ubuf, vbuf, sem, k):
    e = idx_ref[t, k]
    return (pltpu.make_async_copy(u_hbm.at[pl.ds(e, 1)], ubuf.at[slot, pl.ds(k, 1)], sem.at[0, slot]),
            pltpu.make_async_copy(v_hbm.at[pl.ds(e, 1)], vbuf.at[slot, pl.ds(k, 1)], sem.at[1, slot]))


def _experts_kernel(idx_ref, gate_ref, h_ref, x_ref, u_hbm, v_hbm, o_ref, ubuf, vbuf, sem):
    tb = h_ref.shape[0]

    def issue(t, slot):
        def body(k, carry):
            cu, cv = _row_copies(idx_ref, t, slot, u_hbm, v_hbm, ubuf, vbuf, sem, k)
            cu.start()
            cv.start()
            return carry
        lax.fori_loop(0, N_SEL, body, 0, unroll=8)

    def wait(slot):
        pltpu.make_async_copy(u_hbm.at[pl.ds(0, N_SEL)], ubuf.at[slot], sem.at[0, slot]).wait()
        pltpu.make_async_copy(v_hbm.at[pl.ds(0, N_SEL)], vbuf.at[slot], sem.at[1, slot]).wait()

    eye = (lax.broadcasted_iota(I32, (N_SEL, N_SEL), 0) == lax.broadcasted_iota(I32, (N_SEL, N_SEL), 1))
    issue(0, 0)

    def token(t, carry):
        slot = t % 2

        @pl.when(t + 1 < tb)
        def _():
            issue(t + 1, 1 - slot)

        wait(slot)
        hrow = h_ref[pl.ds(t, 1), :]
        hid = jnp.sum(ubuf[slot] * hrow, axis=1, keepdims=True)
        act = 0.5 * hid * (1.0 + lax.erf(hid * np.float32(np.sqrt(0.5))))
        gcol = jnp.sum(jnp.where(eye, gate_ref[pl.ds(t, 1), :], 0.0), axis=1, keepdims=True)
        out = jnp.sum((gcol * act) * vbuf[slot], axis=0, keepdims=True)
        o_ref[pl.ds(t, 1), :] = x_ref[pl.ds(t, 1), :] + out
        return carry

    lax.fori_loop(0, tb, token, 0)


def _experts(eidx, gate, h, x, peer_u, peer_v):
    n = x.shape[0]
    tb = min(n, 64)
    rows = lambda w: pl.BlockSpec((tb, w), lambda i: (i, 0))
    return pl.pallas_call(
        _experts_kernel,
        grid=(n // tb,),
        in_specs=[pl.BlockSpec((tb, N_SEL), lambda i: (i, 0), memory_space=pltpu.SMEM),
                  rows(N_SEL), rows(D_MODEL), rows(D_MODEL),
                  pl.BlockSpec(memory_space=pl.ANY), pl.BlockSpec(memory_space=pl.ANY)],
        out_specs=rows(D_MODEL),
        out_shape=jax.ShapeDtypeStruct((n, D_MODEL), F32),
        scratch_shapes=[pltpu.VMEM((2, N_SEL, D_MODEL), F32), pltpu.VMEM((2, N_SEL, D_MODEL), F32),
                        pltpu.SemaphoreType.DMA((2, 2))],
        compiler_params=_params(("parallel",)),
        name="peer_experts",
    )(eidx, gate, h, x, peer_u, peer_v)


def _t5_bucket(rel):
    nb = NUM_BUCKETS // 2
    max_exact = nb // 2
    ret = jnp.where(rel > 0, nb, 0)
    n = jnp.abs(rel)
    nf = jnp.maximum(n, 1).astype(F32)
    large = max_exact + (jnp.log(nf / max_exact) / math.log(MAX_DISTANCE / max_exact)
                         * (nb - max_exact)).astype(I32)
    large = jnp.minimum(large, nb - 1)
    return ret + jnp.where(n < max_exact, n, large)


def _rel_bias(rel_bias, n_q, n_k, n_before):
    rel = (jnp.arange(n_k)[None, :] - n_before) - jnp.arange(n_q)[:, None]
    return jnp.transpose(rel_bias[_t5_bucket(rel)].astype(F32), (2, 0, 1))


def _state_to_pairs(wkv):
    b = wkv.shape[0]
    s = wkv.reshape(b, N_PAIRS, 2, R_HD, R_HD)
    z = jnp.zeros_like(s[:, :, 0])
    top = jnp.concatenate([s[:, :, 0], z], axis=-1)
    bot = jnp.concatenate([z, s[:, :, 1]], axis=-1)
    return jnp.concatenate([top, bot], axis=-2)


def _pairs_to_state(s_bd):
    b = s_bd.shape[0]
    return jnp.stack([s_bd[:, :, :R_HD, :R_HD], s_bd[:, :, R_HD:, R_HD:]], axis=2).reshape(
        b, R_HEADS, R_HD, R_HD)


def _layer(x, shift0, wkv0, kv_cache, bias, lp):
    b, t, _ = x.shape
    n = b * t
    z = _norm_matmul(x.reshape(n, D_MODEL), lp["ln1_g"].reshape(1, -1), lp["w_in"], 512).reshape(b, t, IN_COLS)
    r, k2, v, kk, bb, lw, g, bonus, qn, kn = _prep(z, shift0, lp)
    v_new = z[:, :, R_PROJ + A_DIM + KV_DIM:]
    if t % CHUNK:
        pad = lambda a: jnp.pad(a, ((0, 0), (0, CHUNK - t % CHUNK), (0, 0)))
        y, s_fin = _chunk_scan(*(pad(a) for a in (r, k2, v, kk, bb, lw)), _state_to_pairs(wkv0))
        y = y[:, :t]
    else:
        y, s_fin = _chunk_scan(r, k2, v, kk, bb, lw, _state_to_pairs(wkv0))
    if kv_cache is None:
        a_out = _swa_prompt(qn, kn, z, bias, lp["sink"])
    else:
        a_out = _swa_sample(qn, kn, v_new, kv_cache[0].reshape(b, -1, KV_DIM),
                            kv_cache[1].reshape(b, -1, KV_DIM), bias, lp["sink"])
    flat = lambda a: a.reshape(n, a.shape[-1])
    x1, h, qq = _mix_out(flat(y), flat(bonus), flat(g), flat(a_out), flat(x), lp)
    eidx, gate = _topk(qq, lp["sub_keys"])
    sel = lambda a: jnp.transpose(a, (2, 0, 1)).reshape(n, N_SEL)
    x2 = _experts(sel(eidx), sel(gate), h, x1, lp["peer_u"], lp["peer_v"]).reshape(b, t, D_MODEL)
    return (x2, _pairs_to_state(s_fin), z[:, -1, :R_PROJ],
            kn.reshape(b, t, A_KV, A_HD), v_new.reshape(b, t, A_KV, A_HD))


def kernel(x_prompt, x_sample, state_rwkv_wkv, state_rwkv_shift, cache_swa_k, cache_swa_v, ln1_g, w_in,
           mu_shift, w0, w2, a0, a2, g2, k_k, k_a, r_k, gn_g, gn_b, q_norm_g, k_norm_g, attn_sink,
           rel_bias, w_out, ln2_g, w_pq, sub_keys, peer_u, peer_v):
    depth = w_in.shape[0]
    b_p, s_p = x_prompt.shape[:2]
    b_s, t_s = x_sample.shape[:2]
    n_cache = cache_swa_k.shape[2]
    n_keep = min(WINDOW, s_p)
    bias_p = _rel_bias(rel_bias, CHUNK, (WIN_CHUNKS + 1) * CHUNK, WIN_CHUNKS * CHUNK)
    bias_s = _rel_bias(rel_bias, t_s, n_cache + t_s, n_cache)
    head_id = jnp.arange(R_DIM) // R_HD
    e64 = (head_id[:, None] == head_id[None, :]).astype(BF16)
    zpad = jnp.zeros((LANES - W_LORA, R_DIM), F32)
    xp, xs = x_prompt, x_sample
    outs = [[] for _ in range(8)]
    for l in range(depth):
        lp = {
            "ln1_g": ln1_g[l], "w_in": w_in[l].astype(BF16), "mu": mu_shift[l], "w0": w0[l], "a0": a0[l],
            "k_k": k_k[l], "k_a": k_a[l], "r_k": r_k[l].reshape(-1), "gn_g": gn_g[l], "gn_b": gn_b[l],
            "w2p": jnp.concatenate([w2[l], zpad], axis=0).astype(BF16),
            "a2p": jnp.concatenate([zpad, a2[l]], axis=0).astype(BF16),
            "g2": g2[l].astype(BF16), "e64": e64,
            "q_gain": jnp.tile(q_norm_g[l], A_HEADS), "k_gain": jnp.tile(k_norm_g[l], A_KV),
            "sink": attn_sink[l].astype(F32), "w_out": w_out[l].astype(BF16), "ln2_g": ln2_g[l],
            "w_pq": w_pq[l].astype(BF16), "sub_keys": sub_keys[l], "peer_u": peer_u[l], "peer_v": peer_v[l],
        }
        xp, wkv_p, sh_p, k_p, v_p = _layer(
            xp, jnp.zeros((b_p, R_PROJ), F32), jnp.zeros((b_p, R_HEADS, R_HD, R_HD), F32), None, bias_p, lp)
        xs, wkv_s, sh_s, k_s, v_s = _layer(
            xs, state_rwkv_shift[l], state_rwkv_wkv[l].astype(F32), (cache_swa_k[l], cache_swa_v[l]),
            bias_s, lp)
        for lst, val in zip(outs, (wkv_p, sh_p, k_p[:, s_p - n_keep:], v_p[:, s_p - n_keep:],
                                   wkv_s, sh_s, k_s, v_s)):
            lst.append(val)
    return (xp, xs) + tuple(jnp.stack(o) for o in outs)
```

```python
import functools
import math

import numpy as np
import jax
import jax.numpy as jnp
from jax import lax
from jax.experimental import pallas as pl
from jax.experimental.pallas import tpu as pltpu
from jax.experimental.pallas import tpu_sc as plsc

F32 = jnp.float32
BF16 = jnp.bfloat16
I32 = jnp.int32

D_MODEL = 1024
CHUNK = 64
R_HEADS = 8
R_HD = 64
R_DIM = R_HEADS * R_HD
W_LORA = 64
A_LORA = 64
G_LORA = 128
R_PROJ = 3 * R_DIM + W_LORA + A_LORA + G_LORA
A_HEADS = 8
A_KV = 2
A_GROUP = A_HEADS // A_KV
A_HD = 64
A_DIM = A_HEADS * A_HD
KV_DIM = A_KV * A_HD
IN_COLS = R_PROJ + A_DIM + 2 * KV_DIM
WINDOW = 128
WIN_CHUNKS = WINDOW // CHUNK
NUM_BUCKETS = 32
MAX_DISTANCE = 128
PEER_HEADS = 8
N_KEYS = 128
PK_DIM = 256
PK_HALF = PK_DIM // 2
PEER_TOPK = 16
N_SEL = PEER_HEADS * PEER_TOPK
NORM_EPS = 1e-6
GN_EPS = 64e-5
NEG_INF = -1e30

LANES = 128
N_PAIRS = R_DIM // LANES
VMEM_LIMIT = 48 * 1024 * 1024
N_CAND = PEER_TOPK + 8 * (PEER_TOPK - 1)
SC_CORES = 2
SC_SUBCORES = 16
SC_LANES = 16
SC_WORKERS = SC_CORES * SC_SUBCORES
SC_TOK_BLK = 8
SC_GROUP = SC_LANES
SC_N_GROUPS = N_SEL // SC_GROUP
SC_FCHUNK = 16 * SC_LANES


def _params(sem):
    return pltpu.CompilerParams(dimension_semantics=sem, vmem_limit_bytes=VMEM_LIMIT)


def _mm(a, b):
    return jnp.dot(a, b, preferred_element_type=F32)


def _mm_nt(a, b):
    return lax.dot_general(a, b, (((1,), (1,)), ((), ())), preferred_element_type=F32)


def _mm_tn(a, b):
    return lax.dot_general(a, b, (((0,), (0,)), ((), ())), preferred_element_type=F32)


def _split2(a):
    hi = a.astype(BF16)
    return hi, (a - hi.astype(F32)).astype(BF16)


def _split3(a):
    hi = a.astype(BF16)
    r = a - hi.astype(F32)
    mid = r.astype(BF16)
    return hi, mid, (r - mid.astype(F32)).astype(BF16)


def _x3(mm, a, b):
    ah, al = _split2(a)
    bh, bl = _split2(b)
    return mm(ah, bh) + mm(ah, bl) + mm(al, bh)


def _exact_lhs(mm, a_bf16, b):
    b0, b1, b2 = _split3(b)
    return mm(a_bf16, b0) + mm(a_bf16, b1) + mm(a_bf16, b2)


def _seg_sum(x, e_bf16):
    x0, x1, x2 = _split3(x)
    return _mm(x0, e_bf16) + _mm(x1, e_bf16) + _mm(x2, e_bf16)


def _sigmoid(x):
    return 1.0 / (1.0 + jnp.exp(-x))


def _norm_matmul_kernel(x_ref, g_ref, w_ref, o_ref, h_ref):
    @pl.when(pl.program_id(1) == 0)
    def _():
        x = x_ref[...]
        ms = jnp.mean(x * x, axis=-1, keepdims=True)
        h_ref[...] = (x * lax.rsqrt(ms + NORM_EPS) * g_ref[...]).astype(BF16)

    o_ref[...] = _mm(h_ref[...], w_ref[...])


def _norm_matmul(x, g, w_bf16, tn):
    n, k = x.shape
    m = w_bf16.shape[1]
    tm = min(n, 512)
    return pl.pallas_call(
        _norm_matmul_kernel,
        grid=(n // tm, m // tn),
        in_specs=[pl.BlockSpec((tm, k), lambda i, j: (i, 0)),
                  pl.BlockSpec((1, k), lambda i, j: (0, 0)),
                  pl.BlockSpec((k, tn), lambda i, j: (0, j))],
        out_specs=pl.BlockSpec((tm, tn), lambda i, j: (i, j)),
        out_shape=jax.ShapeDtypeStruct((n, m), F32),
        scratch_shapes=[pltpu.VMEM((tm, k), BF16)],
        compiler_params=_params(("parallel", "arbitrary")),
        name="norm_inproj",
    )(x, g, w_bf16)


def _prep_kernel(z_ref, zp_ref, sh_ref, mu_ref, w0_ref, a0_ref, kk_ref, ka_ref, rk_ref,
                 w2_ref, a2_ref, g2_ref, e_ref, qg_ref, kg_ref,
                 r_o, k_o, v_o, kk_o, b_o, lw_o, g_o, bo_o, qn_o, kn_o):
    i = pl.program_id(1)
    zt = z_ref[0]
    tp = zt.shape[0]
    zr = zt[:, :R_PROJ]
    prev_row = jnp.where(i == 0, sh_ref[0], zp_ref[0][7:8, :R_PROJ])
    row = lax.broadcasted_iota(I32, (tp, 1), 0)
    prev = jnp.where(row == 0, prev_row, pltpu.roll(zr, 1, axis=0))
    zs = zr + (prev - zr) * mu_ref[...]
    r = zs[:, 0:R_DIM]
    k = zs[:, R_DIM:2 * R_DIM]
    v = zs[:, 2 * R_DIM:3 * R_DIM]
    lo = zs[:, 3 * R_DIM:3 * R_DIM + W_LORA + A_LORA]
    g_lo = zs[:, 3 * R_DIM + W_LORA + A_LORA:R_PROJ]
    e = e_ref[...]
    w_in = -(w0_ref[...] + _mm(jnp.tanh(lo).astype(BF16), w2_ref[...]))
    softplus = jnp.maximum(w_in, 0.0) + jnp.log1p(jnp.exp(-jnp.abs(w_in)))
    w_log = -softplus - 0.5
    lw_o[0] = -jnp.exp(w_log)
    a = _sigmoid(a0_ref[...] + _mm(lo.astype(BF16), a2_ref[...]))
    g_o[0] = _mm(_sigmoid(g_lo).astype(BF16), g2_ref[...])
    kk = k * kk_ref[...]
    kk = kk / jnp.maximum(jnp.sqrt(_seg_sum(kk * kk, e)), 1e-12)
    k2 = k * (1.0 + (a - 1.0) * ka_ref[...])
    r_o[0] = r
    k_o[0] = k2
    v_o[0] = v
    kk_o[0] = kk
    b_o[0] = kk * a
    bo_o[0] = _seg_sum(r * k2 * rk_ref[...], e) * v
    q = zt[:, R_PROJ:R_PROJ + A_DIM]
    qn_o[0] = q * lax.rsqrt(_seg_sum(q * q, e) * (1.0 / A_HD) + NORM_EPS) * qg_ref[...]
    kx = zt[:, R_PROJ + A_DIM:R_PROJ + A_DIM + KV_DIM]
    e_kv = e[:KV_DIM, :KV_DIM]
    kn_o[0] = kx * lax.rsqrt(_seg_sum(kx * kx, e_kv) * (1.0 / A_HD) + NORM_EPS) * kg_ref[...]


def _prep(z, shift0, lp):
    b, t, _ = z.shape
    tp = min(t, 256)
    row = lambda a: a.reshape(1, -1)
    vec = lambda n: pl.BlockSpec((1, n), lambda bi, i: (0, 0))
    full = lambda s: pl.BlockSpec(s, lambda bi, i: (0, 0))
    wide = pl.BlockSpec((1, tp, R_DIM), lambda bi, i: (bi, i, 0))
    outs = [jax.ShapeDtypeStruct((b, t, R_DIM), F32)] * 9 + [jax.ShapeDtypeStruct((b, t, KV_DIM), F32)]
    return pl.pallas_call(
        _prep_kernel,
        grid=(b, t // tp),
        in_specs=[pl.BlockSpec((1, tp, IN_COLS), lambda bi, i: (bi, i, 0)),
                  pl.BlockSpec((1, 8, IN_COLS), lambda bi, i: (bi, jnp.maximum(i * (tp // 8) - 1, 0), 0)),
                  pl.BlockSpec((1, 1, R_PROJ), lambda bi, i: (bi, 0, 0)),
                  vec(R_PROJ), vec(R_DIM), vec(R_DIM), vec(R_DIM), vec(R_DIM), vec(R_DIM),
                  full((LANES, R_DIM)), full((LANES, R_DIM)), full((G_LORA, R_DIM)),
                  full((R_DIM, R_DIM)), vec(A_DIM), vec(KV_DIM)],
        out_specs=[wide] * 9 + [pl.BlockSpec((1, tp, KV_DIM), lambda bi, i: (bi, i, 0))],
        out_shape=outs,
        compiler_params=_params(("parallel", "parallel")),
        name="rwkv_prep",
    )(z, z, shift0.reshape(b, 1, R_PROJ), row(lp["mu"]), row(lp["w0"]), row(lp["a0"]), row(lp["k_k"]),
      row(lp["k_a"]), row(lp["r_k"]), lp["w2p"], lp["a2p"], lp["g2"], lp["e64"], row(lp["q_gain"]),
      row(lp["k_gain"]))


def _stack_heads(x, lo_mask):
    return jnp.concatenate([jnp.where(lo_mask, x, 0.0), jnp.where(lo_mask, 0.0, x)], axis=0)


def _chunk_kernel(r_ref, k_ref, v_ref, kk_ref, b_ref, lw_ref, s0_ref, y_ref, sf_ref, s_ref):
    c = pl.program_id(1)
    L = CHUNK

    @pl.when(c == 0)
    def _():
        s_ref[...] = s0_ref[0]

    lane = lax.broadcasted_iota(I32, (1, LANES), 1)
    lo_mask = lane < R_HD
    rr = lax.broadcasted_iota(I32, (L, 2 * L), 0)
    cc = lax.broadcasted_iota(I32, (L, 2 * L), 1)
    cc = jnp.where(cc >= L, cc - L, cc)
    strict = rr > cc
    incl = rr >= cc
    t_r = lax.broadcasted_iota(I32, (L, L), 0)
    t_c = lax.broadcasted_iota(I32, (L, L), 1)
    tri = (t_r >= t_c).astype(BF16)
    col2 = lax.broadcasted_iota(I32, (L, 2 * L), 1) < L
    eye_r = lax.broadcasted_iota(I32, (2 * L, 2 * L), 0)
    eye_c = lax.broadcasted_iota(I32, (2 * L, 2 * L), 1)
    eye = (eye_r == eye_c).astype(F32)

    for p in range(N_PAIRS):
        sl = slice(p * LANES, (p + 1) * LANES)
        r = r_ref[0][:, sl]
        k = k_ref[0][:, sl]
        v = v_ref[0][:, sl]
        kk = kk_ref[0][:, sl]
        bb = b_ref[0][:, sl]
        lw = lw_ref[0][:, sl]
        s0 = s_ref[p]

        cl = _exact_lhs(_mm, tri, lw)
        cl_last = cl[L - 1:L, :]
        e_neg = jnp.exp(-cl)
        e_last = jnp.exp(cl_last - cl)
        rt = r * jnp.exp(cl)
        at = kk * jnp.exp(cl - lw)
        kt = k * e_neg
        bt = bb * e_neg

        lhs = jnp.concatenate([at, rt], axis=0)
        rhs = jnp.concatenate([_stack_heads(kt, lo_mask), _stack_heads(bt, lo_mask)], axis=0)
        gm = _x3(_mm_nt, lhs, rhs)
        mk = jnp.where(strict, gm[:L, :2 * L], 0.0)
        mb = jnp.where(strict, gm[:L, 2 * L:], 0.0)
        hk = jnp.where(incl, gm[L:, :2 * L], 0.0)
        hb = jnp.where(incl, gm[L:, 2 * L:], 0.0)

        nil = -jnp.concatenate([jnp.where(col2, mb, 0.0), jnp.where(col2, 0.0, mb)], axis=0)
        tinv = eye + nil
        for _ in range(5):
            nil = _x3(_mm, nil, nil)
            tinv = tinv + _x3(_mm, nil, tinv)

        vs = _stack_heads(v, lo_mask)
        rhs_u = -(_x3(_mm_nt, at, s0) + _x3(_mm, mk, vs))
        us = _x3(_mm, tinv, _stack_heads(rhs_u, lo_mask))
        y = _x3(_mm_nt, rt, s0) + _x3(_mm, hk, vs) + _x3(_mm, hb, us)
        y_ref[0, :, sl] = y
        s_ref[p] = (s0 * jnp.exp(cl_last)
                    + _x3(_mm_tn, vs, _stack_heads(k * e_last, lo_mask))
                    + _x3(_mm_tn, us, _stack_heads(bb * e_last, lo_mask)))

    @pl.when(c == pl.num_programs(1) - 1)
    def _():
        sf_ref[0] = s_ref[...]


def _chunk_scan(r, k, v, kk, bb, lw, s0_bd):
    b, t, _ = r.shape
    wide = pl.BlockSpec((1, CHUNK, R_DIM), lambda bi, c: (bi, c, 0))
    st = pl.BlockSpec((1, N_PAIRS, LANES, LANES), lambda bi, c: (bi, 0, 0, 0))
    return pl.pallas_call(
        _chunk_kernel,
        grid=(b, t // CHUNK),
        in_specs=[wide] * 6 + [st],
        out_specs=[wide, st],
        out_shape=[jax.ShapeDtypeStruct((b, t, R_DIM), F32),
                   jax.ShapeDtypeStruct((b, N_PAIRS, LANES, LANES), F32)],
        scratch_shapes=[pltpu.VMEM((N_PAIRS, LANES, LANES), F32)],
        compiler_params=_params(("parallel", "arbitrary")),
        name="rwkv_chunk",
    )(r, k, v, kk, bb, lw, s0_bd)


def _swa_kernel(*refs, n_seg, banded):
    q_ref = refs[0]
    k_refs = refs[1:1 + n_seg]
    v_refs = refs[1 + n_seg:1 + 2 * n_seg]
    bias_ref, sink_ref, o_ref = refs[1 + 2 * n_seg:]
    c = pl.program_id(1)
    q = q_ref[0]
    kcat = jnp.concatenate([kr[0] for kr in k_refs], axis=0)
    vcat = jnp.concatenate([vr[0] for vr in v_refs], axis=0)
    n_k = kcat.shape[0]
    lane = lax.broadcasted_iota(I32, (1, LANES), 1)
    lo_mask = lane < A_HD
    k_sw = pltpu.roll(kcat, A_HD, axis=1)
    v_sw = pltpu.roll(vcat, A_HD, axis=1)
    k_dup = [jnp.where(lo_mask, kcat, k_sw).astype(BF16), jnp.where(lo_mask, k_sw, kcat).astype(BF16)]
    v_dup = [jnp.where(lo_mask, vcat, v_sw).astype(BF16), jnp.where(lo_mask, v_sw, vcat).astype(BF16)]
    if banded:
        key_chunk = c - WIN_CHUNKS + lax.broadcasted_iota(I32, (1, n_k), 1) // CHUNK
        valid = key_chunk >= 0
    for pair in range(A_HEADS // 2):
        qp = q[:, pair * LANES:(pair + 1) * LANES]
        outs = []
        for w in range(2):
            hq = 2 * pair + w
            kvh = hq // A_GROUP
            qm = jnp.where(lo_mask if w == 0 else jnp.logical_not(lo_mask), qp, 0.0).astype(BF16)
            s = _mm_nt(qm, k_dup[kvh]) * (A_HD ** -0.5) + bias_ref[hq]
            if banded:
                s = jnp.where(valid, s, NEG_INF)
            sink = sink_ref[0:1, hq:hq + 1]
            m = jnp.maximum(jnp.max(s, axis=-1, keepdims=True), sink)
            pr = jnp.exp(s - m)
            den = jnp.sum(pr, axis=-1, keepdims=True) + jnp.exp(sink - m)
            outs.append(_mm((pr / den).astype(BF16), v_dup[kvh]))
        o_ref[0, :, pair * LANES:(pair + 1) * LANES] = jnp.where(lo_mask, outs[0], outs[1])


def _swa_prompt(qn, kn, z, bias, sink):
    b, t, _ = qn.shape
    v_col = (R_PROJ + A_DIM + KV_DIM) // KV_DIM
    seg = lambda s, col: pl.BlockSpec(
        (1, CHUNK, KV_DIM), lambda bi, c: (bi, jnp.maximum(c - WIN_CHUNKS + s, 0), col))
    n_seg = WIN_CHUNKS + 1
    return pl.pallas_call(
        functools.partial(_swa_kernel, n_seg=n_seg, banded=True),
        grid=(b, t // CHUNK),
        in_specs=[pl.BlockSpec((1, CHUNK, A_DIM), lambda bi, c: (bi, c, 0))]
                 + [seg(s, 0) for s in range(n_seg)] + [seg(s, v_col) for s in range(n_seg)]
                 + [pl.BlockSpec(bias.shape, lambda bi, c: (0, 0, 0)),
                    pl.BlockSpec((1, A_HEADS), lambda bi, c: (0, 0))],
        out_specs=pl.BlockSpec((1, CHUNK, A_DIM), lambda bi, c: (bi, c, 0)),
        out_shape=jax.ShapeDtypeStruct((b, t, A_DIM), F32),
        compiler_params=_params(("parallel", "parallel")),
        name="swa_prompt",
    )(qn, *([kn] * n_seg), *([z] * n_seg), bias, sink.reshape(1, A_HEADS))


def _swa_sample(qn, kn, v_new, k_cache, v_cache, bias, sink):
    b, t, _ = qn.shape
    n_cache = k_cache.shape[1]
    cur = lambda w: pl.BlockSpec((1, t, w), lambda bi, c: (bi, 0, 0))
    old = pl.BlockSpec((1, n_cache, KV_DIM), lambda bi, c: (bi, 0, 0))
    return pl.pallas_call(
        functools.partial(_swa_kernel, n_seg=2, banded=False),
        grid=(b, 1),
        in_specs=[cur(A_DIM), old, cur(KV_DIM), old, cur(KV_DIM),
                  pl.BlockSpec(bias.shape, lambda bi, c: (0, 0, 0)),
                  pl.BlockSpec((1, A_HEADS), lambda bi, c: (0, 0))],
        out_specs=cur(A_DIM),
        out_shape=jax.ShapeDtypeStruct((b, t, A_DIM), F32),
        compiler_params=_params(("parallel", "parallel")),
        name="swa_sample",
    )(qn, k_cache, kn, v_cache, v_new, bias, sink.reshape(1, A_HEADS))


def _mix_out_kernel(y_ref, bo_ref, g_ref, a_ref, x_ref, gng_ref, gnb_ref, e_ref, wo_ref, ln_ref, wq_ref,
                    xo_ref, h_ref, qq_ref):
    e = e_ref[...]
    y = y_ref[...]
    mu = _seg_sum(y, e) * (1.0 / R_HD)
    d = y - mu
    var = _seg_sum(d * d, e) * (1.0 / R_HD)
    yn = d * lax.rsqrt(var + GN_EPS) * gng_ref[...] + gnb_ref[...]
    mix_r = ((yn + bo_ref[...]) * g_ref[...]).astype(BF16)
    x = (x_ref[...] + _mm(mix_r, wo_ref[:R_DIM, :]) + _mm(a_ref[...].astype(BF16), wo_ref[R_DIM:, :]))
    xo_ref[...] = x
    ms = jnp.mean(x * x, axis=-1, keepdims=True)
    h = x * lax.rsqrt(ms + NORM_EPS) * ln_ref[...]
    h_ref[...] = h
    qq_ref[...] = _mm(h.astype(BF16), wq_ref[...])


def _mix_out(y, bonus, g, a_out, x, lp):
    n = x.shape[0]
    tm = min(n, 256)
    nq = PEER_HEADS * PK_DIM
    rows = lambda w: pl.BlockSpec((tm, w), lambda i: (i, 0))
    full = lambda s: pl.BlockSpec(s, lambda i: (0, 0))
    return pl.pallas_call(
        _mix_out_kernel,
        grid=(n // tm,),
        in_specs=[rows(R_DIM), rows(R_DIM), rows(R_DIM), rows(A_DIM), rows(D_MODEL),
                  full((1, R_DIM)), full((1, R_DIM)), full((R_DIM, R_DIM)),
                  full((D_MODEL, D_MODEL)), full((1, D_MODEL)), full((D_MODEL, nq))],
        out_specs=[rows(D_MODEL), rows(D_MODEL), rows(nq)],
        out_shape=[jax.ShapeDtypeStruct((n, D_MODEL), F32), jax.ShapeDtypeStruct((n, D_MODEL), F32),
                   jax.ShapeDtypeStruct((n, nq), F32)],
        compiler_params=_params(("parallel",)),
        name="mix_out_query",
    )(y, bonus, g, a_out, x, lp["gn_g"].reshape(1, -1), lp["gn_b"].reshape(1, -1), lp["e64"],
      lp["w_out"], lp["ln2_g"].reshape(1, -1), lp["w_pq"])


def _pick_rounds(s_ref, n_rows, emit):
    tb = s_ref.shape[1]
    rowid = lax.broadcasted_iota(I32, (n_rows, tb), 0)
    for rnd in range(PEER_TOPK):
        s = s_ref[...]
        m = jnp.max(s, axis=0, keepdims=True)
        idx = jnp.min(jnp.where(s == m, rowid, n_rows), axis=0, keepdims=True)
        hit = rowid == idx
        s_ref[...] = jnp.where(hit, -jnp.inf, s)
        emit(rnd, m, idx, hit)


def _topk_kernel(qq_ref, keys_ref, ei_ref, gate_ref, s_ref, sv_ref, si_ref, c_ref, ci_ref, ts_ref):
    tb = qq_ref.shape[0]
    for half in range(2):
        qh = qq_ref[:, half * PK_HALF:(half + 1) * PK_HALF]
        s_ref[...] = _x3(_mm_nt, keys_ref[half], qh)

        def emit1(rnd, m, idx, hit, half=half):
            sv_ref[half, rnd:rnd + 1, :] = m
            si_ref[half, rnd:rnd + 1, :] = idx

        _pick_rounds(s_ref, N_KEYS, emit1)

    row0 = 0
    for a in range(PEER_TOPK):
        nb = PEER_TOPK if a == 0 else 8
        c_ref[row0:row0 + nb, :] = sv_ref[0, a:a + 1, :] + sv_ref[1, 0:nb, :]
        ci_ref[row0:row0 + nb, :] = si_ref[0, a:a + 1, :] * N_KEYS + si_ref[1, 0:nb, :]
        row0 += nb

    def emit2(rnd, m, idx, hit):
        ts_ref[rnd:rnd + 1, :] = m
        ei_ref[0, rnd:rnd + 1, :] = jnp.max(jnp.where(hit, ci_ref[...], -1), axis=0, keepdims=True)

    _pick_rounds(c_ref, N_CAND, emit2)
    ts = ts_ref[...]
    ex = jnp.exp(ts - ts[0:1, :])
    gate_ref[0] = ex / jnp.sum(ex, axis=0, keepdims=True)


def _topk(qq, sub_keys):
    n = qq.shape[0]
    tb = min(n, 512)
    out = pl.BlockSpec((1, PEER_TOPK, tb), lambda i, h: (h, 0, i))
    return pl.pallas_call(
        _topk_kernel,
        grid=(n // tb, PEER_HEADS),
        in_specs=[pl.BlockSpec((tb, PK_DIM), lambda i, h: (i, h)),
                  pl.BlockSpec((2, N_KEYS, PK_HALF), lambda i, h: (0, 0, 0))],
        out_specs=[out, out],
        out_shape=[jax.ShapeDtypeStruct((PEER_HEADS, PEER_TOPK, n), I32),
                   jax.ShapeDtypeStruct((PEER_HEADS, PEER_TOPK, n), F32)],
        scratch_shapes=[pltpu.VMEM((N_KEYS, tb), F32), pltpu.VMEM((2, PEER_TOPK, tb), F32),
                        pltpu.VMEM((2, PEER_TOPK, tb), I32), pltpu.VMEM((N_CAND, tb), F32),
                        pltpu.VMEM((N_CAND, tb), I32), pltpu.VMEM((PEER_TOPK, tb), F32)],
        compiler_params=_params(("parallel", "parallel")),
        name="peer_topk",
    )(qq, sub_keys)


def _expert_gather(tbl_hbm, idx_v, rows_v, sem, tt, g, buf):
    return pltpu.make_async_copy(tbl_hbm.at[idx_v[pl.ds(tt * N_SEL + g * SC_GROUP, SC_GROUP)]],
                                 rows_v.at[buf], sem.at[buf])


def _sc_token_blocks(n, tbl_hbm, idx_v, rows_v, sem, load_block, compute, store_block):
    wid = lax.axis_index("s") * SC_CORES + lax.axis_index("c")
    npw = n // SC_WORKERS
    steps = SC_TOK_BLK * SC_N_GROUPS

    @pl.loop(0, npw // SC_TOK_BLK)
    def _(blk):
        tok0 = wid * npw + blk * SC_TOK_BLK
        load_block(tok0)
        _expert_gather(tbl_hbm, idx_v, rows_v, sem, 0, 0, 0).start()

        @pl.loop(0, steps, step=2)
        def _(s0):
            for b in range(2):
                s = s0 + b

                @pl.when(s + 1 < steps)
                def _():
                    _expert_gather(tbl_hbm, idx_v, rows_v, sem, (s + 1) // SC_N_GROUPS, (s + 1) % SC_N_GROUPS,
                                   1 - b).start()

                _expert_gather(tbl_hbm, idx_v, rows_v, sem, s // SC_N_GROUPS, s % SC_N_GROUPS, b).wait()
                compute(s // SC_N_GROUPS, s % SC_N_GROUPS, rows_v.at[b])

        store_block(tok0)


def _sc_mesh():
    return plsc.VectorSubcoreMesh(core_axis_name="c", subcore_axis_name="s",
                                  num_cores=SC_CORES, num_subcores=SC_SUBCORES)


def _sc_hidden(eidx, h, table):
    n = h.shape[0]
    nj = SC_FCHUNK // SC_LANES

    @functools.partial(
        pl.kernel, out_type=jax.ShapeDtypeStruct((n * N_SEL,), F32), mesh=_sc_mesh(),
        compiler_params=pltpu.CompilerParams(needs_layout_passes=False),
        scratch_types=[pltpu.VMEM((SC_TOK_BLK * N_SEL,), I32), pltpu.VMEM((SC_TOK_BLK * D_MODEL,), F32),
                       pltpu.VMEM((2, SC_GROUP, D_MODEL), F32), pltpu.VMEM((SC_TOK_BLK * N_SEL,), F32),
                       pltpu.SemaphoreType.DMA((2,))],
        name="peer_hidden_sc")
    def run(eidx_hbm, h_hbm, tbl_hbm, hid_hbm, idx_v, h_v, rows_v, hid_v, sem):
        lane = lax.iota(I32, SC_LANES)

        def load_block(tok0):
            pltpu.sync_copy(eidx_hbm.at[pl.ds(tok0 * N_SEL, SC_TOK_BLK * N_SEL)], idx_v)
            pltpu.sync_copy(h_hbm.at[pl.ds(tok0 * D_MODEL, SC_TOK_BLK * D_MODEL)], h_v)

        def compute(tt, g, rows):
            def chunk(jc, accs):
                base = jc * SC_FCHUNK
                hv = [h_v[pl.ds(tt * D_MODEL + base + j * SC_LANES, SC_LANES)] for j in range(nj)]
                new = []
                for r in range(SC_GROUP):
                    a = accs[r]
                    for j in range(nj):
                        a = a + rows[r, pl.ds(base + j * SC_LANES, SC_LANES)] * hv[j]
                    new.append(a)
                return tuple(new)

            accs = lax.fori_loop(0, D_MODEL // SC_FCHUNK, chunk,
                                 tuple(jnp.zeros((SC_LANES,), F32) for _ in range(SC_GROUP)))
            tot = jnp.zeros((SC_LANES,), F32)
            for r in range(SC_GROUP):
                tot = jnp.where(lane == r, jnp.sum(accs[r]), tot)
            hid_v[pl.ds(tt * N_SEL + g * SC_GROUP, SC_GROUP)] = tot

        def store_block(tok0):
            pltpu.sync_copy(hid_v, hid_hbm.at[pl.ds(tok0 * N_SEL, SC_TOK_BLK * N_SEL)])

        _sc_token_blocks(n, tbl_hbm, idx_v, rows_v, sem, load_block, compute, store_block)

    return run(eidx.reshape(-1), h.reshape(-1), table).reshape(n, N_SEL)


def _sc_combine(eidx, coef, x, table):
    n = x.shape[0]
    nj = SC_FCHUNK // SC_LANES

    @functools.partial(
        pl.kernel, out_type=jax.ShapeDtypeStruct((n * D_MODEL,), F32), mesh=_sc_mesh(),
        compiler_params=pltpu.CompilerParams(needs_layout_passes=False),
        scratch_types=[pltpu.VMEM((SC_TOK_BLK * N_SEL,), I32), pltpu.VMEM((SC_TOK_BLK * N_SEL,), F32),
                       pltpu.VMEM((2, SC_GROUP, D_MODEL), F32), pltpu.VMEM((SC_TOK_BLK * D_MODEL,), F32),
                       pltpu.SemaphoreType.DMA((2,))],
        name="peer_combine_sc")
    def run(eidx_hbm, c_hbm, x_hbm, tbl_hbm, out_hbm, idx_v, c_v, rows_v, out_v, sem):
        lane = lax.iota(I32, SC_LANES)

        def load_block(tok0):
            pltpu.sync_copy(eidx_hbm.at[pl.ds(tok0 * N_SEL, SC_TOK_BLK * N_SEL)], idx_v)
            pltpu.sync_copy(c_hbm.at[pl.ds(tok0 * N_SEL, SC_TOK_BLK * N_SEL)], c_v)
            pltpu.sync_copy(x_hbm.at[pl.ds(tok0 * D_MODEL, SC_TOK_BLK * D_MODEL)], out_v)

        def compute(tt, g, rows):
            cvec = c_v[pl.ds(tt * N_SEL + g * SC_GROUP, SC_GROUP)]
            coefs = [jnp.sum(jnp.where(lane == r, cvec, 0.0)) for r in range(SC_GROUP)]

            def chunk(jc, carry):
                base = jc * SC_FCHUNK
                obase = tt * D_MODEL + base
                accs = [out_v[pl.ds(obase + j * SC_LANES, SC_LANES)] for j in range(nj)]
                for r in range(SC_GROUP):
                    for j in range(nj):
                        accs[j] = accs[j] + rows[r, pl.ds(base + j * SC_LANES, SC_LANES)] * coefs[r]
                for j in range(nj):
                    out_v[pl.ds(obase + j * SC_LANES, SC_LANES)] = accs[j]
                return carry

            lax.fori_loop(0, D_MODEL // SC_FCHUNK, chunk, 0)

        def store_block(tok0):
            pltpu.sync_copy(out_v, out_hbm.at[pl.ds(tok0 * D_MODEL, SC_TOK_BLK * D_MODEL)])

        _sc_token_blocks(n, tbl_hbm, idx_v, rows_v, sem, load_block, compute, store_block)

    return run(eidx.reshape(-1), coef.reshape(-1), x.reshape(-1), table).reshape(n, D_MODEL)


def _gate_act_kernel(hid_ref, gate_ref, o_ref):
    hid = hid_ref[...]
    o_ref[...] = gate_ref[...] * (0.5 * hid * (1.0 + lax.erf(hid * np.float32(np.sqrt(0.5)))))


def _gate_act(hid, gate):
    n = hid.shape[0]
    tm = min(n, 2048)
    rows = pl.BlockSpec((tm, N_SEL), lambda i: (i, 0))
    return pl.pallas_call(
        _gate_act_kernel, grid=(n // tm,), in_specs=[rows, rows], out_specs=rows,
        out_shape=jax.ShapeDtypeStruct((n, N_SEL), F32), compiler_params=_params(("parallel",)),
        name="peer_gate_act",
    )(hid, gate)


def _t5_bucket(rel):
    nb = NUM_BUCKETS // 2
    max_exact = nb // 2
    ret = jnp.where(rel > 0, nb, 0)
    n = jnp.abs(rel)
    nf = jnp.maximum(n, 1).astype(F32)
    large = max_exact + (jnp.log(nf / max_exact) / math.log(MAX_DISTANCE / max_exact)
                         * (nb - max_exact)).astype(I32)
    large = jnp.minimum(large, nb - 1)
    return ret + jnp.where(n < max_exact, n, large)


def _rel_bias(rel_bias, n_q, n_k, n_before):
    rel = (jnp.arange(n_k)[None, :] - n_before) - jnp.arange(n_q)[:, None]
    return jnp.transpose(rel_bias[_t5_bucket(rel)].astype(F32), (2, 0, 1))


def _state_to_pairs(wkv):
    b = wkv.shape[0]
    s = wkv.reshape(b, N_PAIRS, 2, R_HD, R_HD)
    z = jnp.zeros_like(s[:, :, 0])
    top = jnp.concatenate([s[:, :, 0], z], axis=-1)
    bot = jnp.concatenate([z, s[:, :, 1]], axis=-1)
    return jnp.concatenate([top, bot], axis=-2)


def _pairs_to_state(s_bd):
    b = s_bd.shape[0]
    return jnp.stack([s_bd[:, :, :R_HD, :R_HD], s_bd[:, :, R_HD:, R_HD:]], axis=2).reshape(
        b, R_HEADS, R_HD, R_HD)


def _layer(x, shift0, wkv0, kv_cache, bias, lp):
    b, t, _ = x.shape
    n = b * t
    z = _norm_matmul(x.reshape(n, D_MODEL), lp["ln1_g"].reshape(1, -1), lp["w_in"], 512).reshape(b, t, IN_COLS)
    r, k2, v, kk, bb, lw, g, bonus, qn, kn = _prep(z, shift0, lp)
    v_new = z[:, :, R_PROJ + A_DIM + KV_DIM:]
    if t % CHUNK:
        pad = lambda a: jnp.pad(a, ((0, 0), (0, CHUNK - t % CHUNK), (0, 0)))
        y, s_fin = _chunk_scan(*(pad(a) for a in (r, k2, v, kk, bb, lw)), _state_to_pairs(wkv0))
        y = y[:, :t]
    else:
        y, s_fin = _chunk_scan(r, k2, v, kk, bb, lw, _state_to_pairs(wkv0))
    if kv_cache is None:
        a_out = _swa_prompt(qn, kn, z, bias, lp["sink"])
    else:
        a_out = _swa_sample(qn, kn, v_new, kv_cache[0].reshape(b, -1, KV_DIM),
                            kv_cache[1].reshape(b, -1, KV_DIM), bias, lp["sink"])
    flat = lambda a: a.reshape(n, a.shape[-1])
    x1, h, qq = _mix_out(flat(y), flat(bonus), flat(g), flat(a_out), flat(x), lp)
    eidx, gate = _topk(qq, lp["sub_keys"])
    sel = lambda a: jnp.transpose(a, (2, 0, 1)).reshape(n, N_SEL)
    eidx = sel(eidx)
    coef = _gate_act(_sc_hidden(eidx, h, lp["peer_u"]), sel(gate))
    x2 = _sc_combine(eidx, coef, x1, lp["peer_v"]).reshape(b, t, D_MODEL)
    return (x2, _pairs_to_state(s_fin), z[:, -1, :R_PROJ],
            kn.reshape(b, t, A_KV, A_HD), v_new.reshape(b, t, A_KV, A_HD))


def kernel(x_prompt, x_sample, state_rwkv_wkv, state_rwkv_shift, cache_swa_k, cache_swa_v, ln1_g, w_in,
           mu_shift, w0, w2, a0, a2, g2, k_k, k_a, r_k, gn_g, gn_b, q_norm_g, k_norm_g, attn_sink,
           rel_bias, w_out, ln2_g, w_pq, sub_keys, peer_u, peer_v):
    depth = w_in.shape[0]
    b_p, s_p = x_prompt.shape[:2]
    b_s, t_s = x_sample.shape[:2]
    n_cache = cache_swa_k.shape[2]
    n_keep = min(WINDOW, s_p)
    bias_p = _rel_bias(rel_bias, CHUNK, (WIN_CHUNKS + 1) * CHUNK, WIN_CHUNKS * CHUNK)
    bias_s = _rel_bias(rel_bias, t_s, n_cache + t_s, n_cache)
    head_id = jnp.arange(R_DIM) // R_HD
    e64 = (head_id[:, None] == head_id[None, :]).astype(BF16)
    zpad = jnp.zeros((LANES - W_LORA, R_DIM), F32)
    xp, xs = x_prompt, x_sample
    outs = [[] for _ in range(8)]
    for l in range(depth):
        lp = {
            "ln1_g": ln1_g[l], "w_in": w_in[l].astype(BF16), "mu": mu_shift[l], "w0": w0[l], "a0": a0[l],
            "k_k": k_k[l], "k_a": k_a[l], "r_k": r_k[l].reshape(-1), "gn_g": gn_g[l], "gn_b": gn_b[l],
            "w2p": jnp.concatenate([w2[l], zpad], axis=0).astype(BF16),
            "a2p": jnp.concatenate([zpad, a2[l]], axis=0).astype(BF16),
            "g2": g2[l].astype(BF16), "e64": e64,
            "q_gain": jnp.tile(q_norm_g[l], A_HEADS), "k_gain": jnp.tile(k_norm_g[l], A_KV),
            "sink": attn_sink[l].astype(F32), "w_out": w_out[l].astype(BF16), "ln2_g": ln2_g[l],
            "w_pq": w_pq[l].astype(BF16), "sub_keys": sub_keys[l], "peer_u": peer_u[l], "peer_v": peer_v[l],
        }
        xp, wkv_p, sh_p, k_p, v_p = _layer(
            xp, jnp.zeros((b_p, R_PROJ), F32), jnp.zeros((b_p, R_HEADS, R_HD, R_HD), F32), None, bias_p, lp)
        xs, wkv_s, sh_s, k_s, v_s = _layer(
            xs, state_rwkv_shift[l], state_rwkv_wkv[l].astype(F32), (cache_swa_k[l], cache_swa_v[l]),
            bias_s, lp)
        for lst, val in zip(outs, (wkv_p, sh_p, k_p[:, s_p - n_keep:], v_p[:, s_p - n_keep:],
                                   wkv_s, sh_s, k_s, v_s)):
            lst.append(val)
    return (xp, xs) + tuple(jnp.stack(o) for o in outs)
```

```python
import functools
import math

import numpy as np
import jax
import jax.numpy as jnp
from jax import lax
from jax.experimental import pallas as pl
from jax.experimental.pallas import tpu as pltpu
from jax.experimental.pallas import tpu_sc as plsc

F32 = jnp.float32
BF16 = jnp.bfloat16
I32 = jnp.int32

D_MODEL = 1024
CHUNK = 64
R_HEADS = 8
R_HD = 64
R_DIM = R_HEADS * R_HD
W_LORA = 64
A_LORA = 64
G_LORA = 128
R_PROJ = 3 * R_DIM + W_LORA + A_LORA + G_LORA
A_HEADS = 8
A_KV = 2
A_GROUP = A_HEADS // A_KV
A_HD = 64
A_DIM = A_HEADS * A_HD
KV_DIM = A_KV * A_HD
IN_COLS = R_PROJ + A_DIM + 2 * KV_DIM
WINDOW = 128
WIN_CHUNKS = WINDOW // CHUNK
NUM_BUCKETS = 32
MAX_DISTANCE = 128
PEER_HEADS = 8
N_KEYS = 128
PK_DIM = 256
PK_HALF = PK_DIM // 2
PEER_TOPK = 16
N_SEL = PEER_HEADS * PEER_TOPK
NORM_EPS = 1e-6
GN_EPS = 64e-5
NEG_INF = -1e30

LANES = 128
N_PAIRS = R_DIM // LANES
VMEM_LIMIT = 48 * 1024 * 1024
N_CAND = PEER_TOPK + 8 * (PEER_TOPK - 1)
SC_CORES = 2
SC_SUBCORES = 16
SC_LANES = 16
SC_WORKERS = SC_CORES * SC_SUBCORES
SC_TOK_BLK = 8
SC_GROUP = SC_LANES
SC_N_GROUPS = N_SEL // SC_GROUP
SC_RSUB = 8


def _params(sem):
    return pltpu.CompilerParams(dimension_semantics=sem, vmem_limit_bytes=VMEM_LIMIT)


def _mm(a, b):
    return jnp.dot(a, b, preferred_element_type=F32)


def _mm_nt(a, b):
    return lax.dot_general(a, b, (((1,), (1,)), ((), ())), preferred_element_type=F32)


def _mm_tn(a, b):
    return lax.dot_general(a, b, (((0,), (0,)), ((), ())), preferred_element_type=F32)


def _split2(a):
    hi = a.astype(BF16)
    return hi, (a - hi.astype(F32)).astype(BF16)


def _split3(a):
    hi = a.astype(BF16)
    r = a - hi.astype(F32)
    mid = r.astype(BF16)
    return hi, mid, (r - mid.astype(F32)).astype(BF16)


def _x3(mm, a, b):
    ah, al = _split2(a)
    bh, bl = _split2(b)
    return mm(ah, bh) + mm(ah, bl) + mm(al, bh)


def _exact_lhs(mm, a_bf16, b):
    b0, b1, b2 = _split3(b)
    return mm(a_bf16, b0) + mm(a_bf16, b1) + mm(a_bf16, b2)


def _seg_sum(x, e_bf16):
    x0, x1, x2 = _split3(x)
    return _mm(x0, e_bf16) + _mm(x1, e_bf16) + _mm(x2, e_bf16)


def _sigmoid(x):
    return 1.0 / (1.0 + jnp.exp(-x))


def _norm_matmul_kernel(x_ref, g_ref, w_ref, o_ref, h_ref):
    @pl.when(pl.program_id(1) == 0)
    def _():
        x = x_ref[...]
        ms = jnp.mean(x * x, axis=-1, keepdims=True)
        h_ref[...] = (x * lax.rsqrt(ms + NORM_EPS) * g_ref[...]).astype(BF16)

    o_ref[...] = _mm(h_ref[...], w_ref[...])


def _norm_matmul(x, g, w_bf16, tn):
    n, k = x.shape
    m = w_bf16.shape[1]
    tm = min(n, 512)
    return pl.pallas_call(
        _norm_matmul_kernel,
        grid=(n // tm, m // tn),
        in_specs=[pl.BlockSpec((tm, k), lambda i, j: (i, 0)),
                  pl.BlockSpec((1, k), lambda i, j: (0, 0)),
                  pl.BlockSpec((k, tn), lambda i, j: (0, j))],
        out_specs=pl.BlockSpec((tm, tn), lambda i, j: (i, j)),
        out_shape=jax.ShapeDtypeStruct((n, m), F32),
        scratch_shapes=[pltpu.VMEM((tm, k), BF16)],
        compiler_params=_params(("parallel", "arbitrary")),
        name="norm_inproj",
    )(x, g, w_bf16)


def _prep_kernel(z_ref, zp_ref, sh_ref, mu_ref, w0_ref, a0_ref, kk_ref, ka_ref, rk_ref,
                 w2_ref, a2_ref, g2_ref, e_ref, qg_ref, kg_ref,
                 r_o, k_o, v_o, kk_o, b_o, lw_o, g_o, bo_o, qn_o, kn_o):
    i = pl.program_id(1)
    zt = z_ref[0]
    tp = zt.shape[0]
    zr = zt[:, :R_PROJ]
    prev_row = jnp.where(i == 0, sh_ref[0], zp_ref[0][7:8, :R_PROJ])
    row = lax.broadcasted_iota(I32, (tp, 1), 0)
    prev = jnp.where(row == 0, prev_row, pltpu.roll(zr, 1, axis=0))
    zs = zr + (prev - zr) * mu_ref[...]
    r = zs[:, 0:R_DIM]
    k = zs[:, R_DIM:2 * R_DIM]
    v = zs[:, 2 * R_DIM:3 * R_DIM]
    lo = zs[:, 3 * R_DIM:3 * R_DIM + W_LORA + A_LORA]
    g_lo = zs[:, 3 * R_DIM + W_LORA + A_LORA:R_PROJ]
    e = e_ref[...]
    w_in = -(w0_ref[...] + _mm(jnp.tanh(lo).astype(BF16), w2_ref[...]))
    softplus = jnp.maximum(w_in, 0.0) + jnp.log1p(jnp.exp(-jnp.abs(w_in)))
    w_log = -softplus - 0.5
    lw_o[0] = -jnp.exp(w_log)
    a = _sigmoid(a0_ref[...] + _mm(lo.astype(BF16), a2_ref[...]))
    g_o[0] = _mm(_sigmoid(g_lo).astype(BF16), g2_ref[...])
    kk = k * kk_ref[...]
    kk = kk / jnp.maximum(jnp.sqrt(_seg_sum(kk * kk, e)), 1e-12)
    k2 = k * (1.0 + (a - 1.0) * ka_ref[...])
    r_o[0] = r
    k_o[0] = k2
    v_o[0] = v
    kk_o[0] = kk
    b_o[0] = kk * a
    bo_o[0] = _seg_sum(r * k2 * rk_ref[...], e) * v
    q = zt[:, R_PROJ:R_PROJ + A_DIM]
    qn_o[0] = q * lax.rsqrt(_seg_sum(q * q, e) * (1.0 / A_HD) + NORM_EPS) * qg_ref[...]
    kx = zt[:, R_PROJ + A_DIM:R_PROJ + A_DIM + KV_DIM]
    e_kv = e[:KV_DIM, :KV_DIM]
    kn_o[0] = kx * lax.rsqrt(_seg_sum(kx * kx, e_kv) * (1.0 / A_HD) + NORM_EPS) * kg_ref[...]


def _prep(z, shift0, lp):
    b, t, _ = z.shape
    tp = min(t, 256)
    row = lambda a: a.reshape(1, -1)
    vec = lambda n: pl.BlockSpec((1, n), lambda bi, i: (0, 0))
    full = lambda s: pl.BlockSpec(s, lambda bi, i: (0, 0))
    wide = pl.BlockSpec((1, tp, R_DIM), lambda bi, i: (bi, i, 0))
    outs = [jax.ShapeDtypeStruct((b, t, R_DIM), F32)] * 9 + [jax.ShapeDtypeStruct((b, t, KV_DIM), F32)]
    return pl.pallas_call(
        _prep_kernel,
        grid=(b, t // tp),
        in_specs=[pl.BlockSpec((1, tp, IN_COLS), lambda bi, i: (bi, i, 0)),
                  pl.BlockSpec((1, 8, IN_COLS), lambda bi, i: (bi, jnp.maximum(i * (tp // 8) - 1, 0), 0)),
                  pl.BlockSpec((1, 1, R_PROJ), lambda bi, i: (bi, 0, 0)),
                  vec(R_PROJ), vec(R_DIM), vec(R_DIM), vec(R_DIM), vec(R_DIM), vec(R_DIM),
                  full((LANES, R_DIM)), full((LANES, R_DIM)), full((G_LORA, R_DIM)),
                  full((R_DIM, R_DIM)), vec(A_DIM), vec(KV_DIM)],
        out_specs=[wide] * 9 + [pl.BlockSpec((1, tp, KV_DIM), lambda bi, i: (bi, i, 0))],
        out_shape=outs,
        compiler_params=_params(("parallel", "parallel")),
        name="rwkv_prep",
    )(z, z, shift0.reshape(b, 1, R_PROJ), row(lp["mu"]), row(lp["w0"]), row(lp["a0"]), row(lp["k_k"]),
      row(lp["k_a"]), row(lp["r_k"]), lp["w2p"], lp["a2p"], lp["g2"], lp["e64"], row(lp["q_gain"]),
      row(lp["k_gain"]))


def _stack_heads(x, lo_mask):
    return jnp.concatenate([jnp.where(lo_mask, x, 0.0), jnp.where(lo_mask, 0.0, x)], axis=0)


def _chunk_kernel(r_ref, k_ref, v_ref, kk_ref, b_ref, lw_ref, s0_ref, y_ref, sf_ref, s_ref):
    c = pl.program_id(1)
    L = CHUNK

    @pl.when(c == 0)
    def _():
        s_ref[...] = s0_ref[0]

    lane = lax.broadcasted_iota(I32, (1, LANES), 1)
    lo_mask = lane < R_HD
    rr = lax.broadcasted_iota(I32, (L, 2 * L), 0)
    cc = lax.broadcasted_iota(I32, (L, 2 * L), 1)
    cc = jnp.where(cc >= L, cc - L, cc)
    strict = rr > cc
    incl = rr >= cc
    t_r = lax.broadcasted_iota(I32, (L, L), 0)
    t_c = lax.broadcasted_iota(I32, (L, L), 1)
    tri = (t_r >= t_c).astype(BF16)
    col2 = lax.broadcasted_iota(I32, (L, 2 * L), 1) < L
    eye_r = lax.broadcasted_iota(I32, (2 * L, 2 * L), 0)
    eye_c = lax.broadcasted_iota(I32, (2 * L, 2 * L), 1)
    eye = (eye_r == eye_c).astype(F32)

    for p in range(N_PAIRS):
        sl = slice(p * LANES, (p + 1) * LANES)
        r = r_ref[0][:, sl]
        k = k_ref[0][:, sl]
        v = v_ref[0][:, sl]
        kk = kk_ref[0][:, sl]
        bb = b_ref[0][:, sl]
        lw = lw_ref[0][:, sl]
        s0 = s_ref[p]

        cl = _exact_lhs(_mm, tri, lw)
        cl_last = cl[L - 1:L, :]
        e_neg = jnp.exp(-cl)
        e_last = jnp.exp(cl_last - cl)
        rt = r * jnp.exp(cl)
        at = kk * jnp.exp(cl - lw)
        kt = k * e_neg
        bt = bb * e_neg

        lhs = jnp.concatenate([at, rt], axis=0)
        rhs = jnp.concatenate([_stack_heads(kt, lo_mask), _stack_heads(bt, lo_mask)], axis=0)
        gm = _x3(_mm_nt, lhs, rhs)
        mk = jnp.where(strict, gm[:L, :2 * L], 0.0)
        mb = jnp.where(strict, gm[:L, 2 * L:], 0.0)
        hk = jnp.where(incl, gm[L:, :2 * L], 0.0)
        hb = jnp.where(incl, gm[L:, 2 * L:], 0.0)

        nil = -jnp.concatenate([jnp.where(col2, mb, 0.0), jnp.where(col2, 0.0, mb)], axis=0)
        tinv = eye + nil
        for _ in range(5):
            nil = _x3(_mm, nil, nil)
            tinv = tinv + _x3(_mm, nil, tinv)

        vs = _stack_heads(v, lo_mask)
        rhs_u = -(_x3(_mm_nt, at, s0) + _x3(_mm, mk, vs))
        us = _x3(_mm, tinv, _stack_heads(rhs_u, lo_mask))
        y = _x3(_mm_nt, rt, s0) + _x3(_mm, hk, vs) + _x3(_mm, hb, us)
        y_ref[0, :, sl] = y
        s_ref[p] = (s0 * jnp.exp(cl_last)
                    + _x3(_mm_tn, vs, _stack_heads(k * e_last, lo_mask))
                    + _x3(_mm_tn, us, _stack_heads(bb * e_last, lo_mask)))

    @pl.when(c == pl.num_programs(1) - 1)
    def _():
        sf_ref[0] = s_ref[...]


def _chunk_scan(r, k, v, kk, bb, lw, s0_bd):
    b, t, _ = r.shape
    wide = pl.BlockSpec((1, CHUNK, R_DIM), lambda bi, c: (bi, c, 0))
    st = pl.BlockSpec((1, N_PAIRS, LANES, LANES), lambda bi, c: (bi, 0, 0, 0))
    return pl.pallas_call(
        _chunk_kernel,
        grid=(b, t // CHUNK),
        in_specs=[wide] * 6 + [st],
        out_specs=[wide, st],
        out_shape=[jax.ShapeDtypeStruct((b, t, R_DIM), F32),
                   jax.ShapeDtypeStruct((b, N_PAIRS, LANES, LANES), F32)],
        scratch_shapes=[pltpu.VMEM((N_PAIRS, LANES, LANES), F32)],
        compiler_params=_params(("parallel", "arbitrary")),
        name="rwkv_chunk",
    )(r, k, v, kk, bb, lw, s0_bd)


def _swa_kernel(*refs, n_seg, banded):
    q_ref = refs[0]
    k_refs = refs[1:1 + n_seg]
    v_refs = refs[1 + n_seg:1 + 2 * n_seg]
    bias_ref, sink_ref, o_ref = refs[1 + 2 * n_seg:]
    c = pl.program_id(1)
    q = q_ref[0]
    kcat = jnp.concatenate([kr[0] for kr in k_refs], axis=0)
    vcat = jnp.concatenate([vr[0] for vr in v_refs], axis=0)
    n_k = kcat.shape[0]
    lane = lax.broadcasted_iota(I32, (1, LANES), 1)
    lo_mask = lane < A_HD
    k_sw = pltpu.roll(kcat, A_HD, axis=1)
    v_sw = pltpu.roll(vcat, A_HD, axis=1)
    k_dup = [jnp.where(lo_mask, kcat, k_sw).astype(BF16), jnp.where(lo_mask, k_sw, kcat).astype(BF16)]
    v_dup = [jnp.where(lo_mask, vcat, v_sw).astype(BF16), jnp.where(lo_mask, v_sw, vcat).astype(BF16)]
    if banded:
        key_chunk = c - WIN_CHUNKS + lax.broadcasted_iota(I32, (1, n_k), 1) // CHUNK
        valid = key_chunk >= 0
    for pair in range(A_HEADS // 2):
        qp = q[:, pair * LANES:(pair + 1) * LANES]
        outs = []
        for w in range(2):
            hq = 2 * pair + w
            kvh = hq // A_GROUP
            qm = jnp.where(lo_mask if w == 0 else jnp.logical_not(lo_mask), qp, 0.0).astype(BF16)
            s = _mm_nt(qm, k_dup[kvh]) * (A_HD ** -0.5) + bias_ref[hq]
            if banded:
                s = jnp.where(valid, s, NEG_INF)
            sink = sink_ref[0:1, hq:hq + 1]
            m = jnp.maximum(jnp.max(s, axis=-1, keepdims=True), sink)
            pr = jnp.exp(s - m)
            den = jnp.sum(pr, axis=-1, keepdims=True) + jnp.exp(sink - m)
            outs.append(_mm((pr / den).astype(BF16), v_dup[kvh]))
        o_ref[0, :, pair * LANES:(pair + 1) * LANES] = jnp.where(lo_mask, outs[0], outs[1])


def _swa_prompt(qn, kn, z, bias, sink):
    b, t, _ = qn.shape
    v_col = (R_PROJ + A_DIM + KV_DIM) // KV_DIM
    seg = lambda s, col: pl.BlockSpec(
        (1, CHUNK, KV_DIM), lambda bi, c: (bi, jnp.maximum(c - WIN_CHUNKS + s, 0), col))
    n_seg = WIN_CHUNKS + 1
    return pl.pallas_call(
        functools.partial(_swa_kernel, n_seg=n_seg, banded=True),
        grid=(b, t // CHUNK),
        in_specs=[pl.BlockSpec((1, CHUNK, A_DIM), lambda bi, c: (bi, c, 0))]
                 + [seg(s, 0) for s in range(n_seg)] + [seg(s, v_col) for s in range(n_seg)]
                 + [pl.BlockSpec(bias.shape, lambda bi, c: (0, 0, 0)),
                    pl.BlockSpec((1, A_HEADS), lambda bi, c: (0, 0))],
        out_specs=pl.BlockSpec((1, CHUNK, A_DIM), lambda bi, c: (bi, c, 0)),
        out_shape=jax.ShapeDtypeStruct((b, t, A_DIM), F32),
        compiler_params=_params(("parallel", "parallel")),
        name="swa_prompt",
    )(qn, *([kn] * n_seg), *([z] * n_seg), bias, sink.reshape(1, A_HEADS))


def _swa_sample(qn, kn, v_new, k_cache, v_cache, bias, sink):
    b, t, _ = qn.shape
    n_cache = k_cache.shape[1]
    cur = lambda w: pl.BlockSpec((1, t, w), lambda bi, c: (bi, 0, 0))
    old = pl.BlockSpec((1, n_cache, KV_DIM), lambda bi, c: (bi, 0, 0))
    return pl.pallas_call(
        functools.partial(_swa_kernel, n_seg=2, banded=False),
        grid=(b, 1),
        in_specs=[cur(A_DIM), old, cur(KV_DIM), old, cur(KV_DIM),
                  pl.BlockSpec(bias.shape, lambda bi, c: (0, 0, 0)),
                  pl.BlockSpec((1, A_HEADS), lambda bi, c: (0, 0))],
        out_specs=cur(A_DIM),
        out_shape=jax.ShapeDtypeStruct((b, t, A_DIM), F32),
        compiler_params=_params(("parallel", "parallel")),
        name="swa_sample",
    )(qn, k_cache, kn, v_cache, v_new, bias, sink.reshape(1, A_HEADS))


def _mix_out_kernel(y_ref, bo_ref, g_ref, a_ref, x_ref, gng_ref, gnb_ref, e_ref, wo_ref, ln_ref, wq_ref,
                    xo_ref, h_ref, qq_ref):
    e = e_ref[...]
    y = y_ref[...]
    mu = _seg_sum(y, e) * (1.0 / R_HD)
    d = y - mu
    var = _seg_sum(d * d, e) * (1.0 / R_HD)
    yn = d * lax.rsqrt(var + GN_EPS) * gng_ref[...] + gnb_ref[...]
    mix_r = ((yn + bo_ref[...]) * g_ref[...]).astype(BF16)
    x = (x_ref[...] + _mm(mix_r, wo_ref[:R_DIM, :]) + _mm(a_ref[...].astype(BF16), wo_ref[R_DIM:, :]))
    xo_ref[...] = x
    ms = jnp.mean(x * x, axis=-1, keepdims=True)
    h = x * lax.rsqrt(ms + NORM_EPS) * ln_ref[...]
    h_ref[...] = h
    qq_ref[...] = _mm(h.astype(BF16), wq_ref[...])


def _mix_out(y, bonus, g, a_out, x, lp):
    n = x.shape[0]
    tm = min(n, 256)
    nq = PEER_HEADS * PK_DIM
    rows = lambda w: pl.BlockSpec((tm, w), lambda i: (i, 0))
    full = lambda s: pl.BlockSpec(s, lambda i: (0, 0))
    return pl.pallas_call(
        _mix_out_kernel,
        grid=(n // tm,),
        in_specs=[rows(R_DIM), rows(R_DIM), rows(R_DIM), rows(A_DIM), rows(D_MODEL),
                  full((1, R_DIM)), full((1, R_DIM)), full((R_DIM, R_DIM)),
                  full((D_MODEL, D_MODEL)), full((1, D_MODEL)), full((D_MODEL, nq))],
        out_specs=[rows(D_MODEL), rows(D_MODEL), rows(nq)],
        out_shape=[jax.ShapeDtypeStruct((n, D_MODEL), F32), jax.ShapeDtypeStruct((n, D_MODEL), F32),
                   jax.ShapeDtypeStruct((n, nq), F32)],
        compiler_params=_params(("parallel",)),
        name="mix_out_query",
    )(y, bonus, g, a_out, x, lp["gn_g"].reshape(1, -1), lp["gn_b"].reshape(1, -1), lp["e64"],
      lp["w_out"], lp["ln2_g"].reshape(1, -1), lp["w_pq"])


def _pick_rounds(s_ref, n_rows, emit):
    tb = s_ref.shape[1]
    rowid = lax.broadcasted_iota(I32, (n_rows, tb), 0)
    for rnd in range(PEER_TOPK):
        s = s_ref[...]
        m = jnp.max(s, axis=0, keepdims=True)
        idx = jnp.min(jnp.where(s == m, rowid, n_rows), axis=0, keepdims=True)
        hit = rowid == idx
        s_ref[...] = jnp.where(hit, -jnp.inf, s)
        emit(rnd, m, idx, hit)


def _topk_kernel(qq_ref, keys_ref, ei_ref, gate_ref, s_ref, sv_ref, si_ref, c_ref, ci_ref, ts_ref):
    tb = qq_ref.shape[0]
    for half in range(2):
        qh = qq_ref[:, half * PK_HALF:(half + 1) * PK_HALF]
        s_ref[...] = _x3(_mm_nt, keys_ref[half], qh)

        def emit1(rnd, m, idx, hit, half=half):
            sv_ref[half, rnd:rnd + 1, :] = m
            si_ref[half, rnd:rnd + 1, :] = idx

        _pick_rounds(s_ref, N_KEYS, emit1)

    row0 = 0
    for a in range(PEER_TOPK):
        nb = PEER_TOPK if a == 0 else 8
        c_ref[row0:row0 + nb, :] = sv_ref[0, a:a + 1, :] + sv_ref[1, 0:nb, :]
        ci_ref[row0:row0 + nb, :] = si_ref[0, a:a + 1, :] * N_KEYS + si_ref[1, 0:nb, :]
        row0 += nb

    def emit2(rnd, m, idx, hit):
        ts_ref[rnd:rnd + 1, :] = m
        ei_ref[0, rnd:rnd + 1, :] = jnp.max(jnp.where(hit, ci_ref[...], -1), axis=0, keepdims=True)

    _pick_rounds(c_ref, N_CAND, emit2)
    ts = ts_ref[...]
    ex = jnp.exp(ts - ts[0:1, :])
    gate_ref[0] = ex / jnp.sum(ex, axis=0, keepdims=True)


def _topk(qq, sub_keys):
    n = qq.shape[0]
    tb = min(n, 512)
    out = pl.BlockSpec((1, PEER_TOPK, tb), lambda i, h: (h, 0, i))
    return pl.pallas_call(
        _topk_kernel,
        grid=(n // tb, PEER_HEADS),
        in_specs=[pl.BlockSpec((tb, PK_DIM), lambda i, h: (i, h)),
                  pl.BlockSpec((2, N_KEYS, PK_HALF), lambda i, h: (0, 0, 0))],
        out_specs=[out, out],
        out_shape=[jax.ShapeDtypeStruct((PEER_HEADS, PEER_TOPK, n), I32),
                   jax.ShapeDtypeStruct((PEER_HEADS, PEER_TOPK, n), F32)],
        scratch_shapes=[pltpu.VMEM((N_KEYS, tb), F32), pltpu.VMEM((2, PEER_TOPK, tb), F32),
                        pltpu.VMEM((2, PEER_TOPK, tb), I32), pltpu.VMEM((N_CAND, tb), F32),
                        pltpu.VMEM((N_CAND, tb), I32), pltpu.VMEM((PEER_TOPK, tb), F32)],
        compiler_params=_params(("parallel", "parallel")),
        name="peer_topk",
    )(qq, sub_keys)


def _expert_gather(tbl_hbm, idx_v, rows_v, sem, tt, g, buf):
    return pltpu.make_async_copy(tbl_hbm.at[idx_v[pl.ds(tt * N_SEL + g * SC_GROUP, SC_GROUP)]],
                                 rows_v.at[buf], sem.at[buf])


def _sc_token_blocks(n, tbl_hbm, idx_v, rows_v, sem, load_block, compute, store_block):
    wid = lax.axis_index("s") * SC_CORES + lax.axis_index("c")
    npw = n // SC_WORKERS
    steps = SC_TOK_BLK * SC_N_GROUPS

    @pl.loop(0, npw // SC_TOK_BLK)
    def _(blk):
        tok0 = wid * npw + blk * SC_TOK_BLK
        load_block(tok0)
        _expert_gather(tbl_hbm, idx_v, rows_v, sem, 0, 0, 0).start()

        @pl.loop(0, steps, step=2)
        def _(s0):
            for b in range(2):
                s = s0 + b

                @pl.when(s + 1 < steps)
                def _():
                    _expert_gather(tbl_hbm, idx_v, rows_v, sem, (s + 1) // SC_N_GROUPS, (s + 1) % SC_N_GROUPS,
                                   1 - b).start()

                _expert_gather(tbl_hbm, idx_v, rows_v, sem, s // SC_N_GROUPS, s % SC_N_GROUPS, b).wait()
                compute(s // SC_N_GROUPS, s % SC_N_GROUPS, rows_v.at[b])

        store_block(tok0)


def _sc_mesh():
    return plsc.VectorSubcoreMesh(core_axis_name="c", subcore_axis_name="s",
                                  num_cores=SC_CORES, num_subcores=SC_SUBCORES)


def _sc_hidden(eidx, h, table):
    n = h.shape[0]

    @functools.partial(
        pl.kernel, out_type=jax.ShapeDtypeStruct((n * N_SEL,), F32), mesh=_sc_mesh(),
        compiler_params=pltpu.CompilerParams(needs_layout_passes=False),
        scratch_types=[pltpu.VMEM((SC_TOK_BLK * N_SEL,), I32), pltpu.VMEM((SC_TOK_BLK * D_MODEL,), F32),
                       pltpu.VMEM((2, SC_GROUP, D_MODEL), F32), pltpu.VMEM((SC_TOK_BLK * N_SEL,), F32),
                       pltpu.SemaphoreType.DMA((2,))],
        name="peer_hidden_sc")
    def run(eidx_hbm, h_hbm, tbl_hbm, hid_hbm, idx_v, h_v, rows_v, hid_v, sem):
        lane = lax.iota(I32, SC_LANES)

        def load_block(tok0):
            pltpu.sync_copy(eidx_hbm.at[pl.ds(tok0 * N_SEL, SC_TOK_BLK * N_SEL)], idx_v)
            pltpu.sync_copy(h_hbm.at[pl.ds(tok0 * D_MODEL, SC_TOK_BLK * D_MODEL)], h_v)

        def compute(tt, g, rows):
            accs = []
            for sub in range(SC_GROUP // SC_RSUB):
                zero = tuple(jnp.zeros((SC_LANES,), F32) for _ in range(SC_RSUB))

                @plsc.parallel_loop(0, D_MODEL // SC_LANES, unroll=4, carry=zero)
                def part(j, acc):
                    hvj = h_v[pl.ds(tt * D_MODEL + j * SC_LANES, SC_LANES)]
                    return tuple(acc[r] + rows[sub * SC_RSUB + r, pl.ds(j * SC_LANES, SC_LANES)] * hvj
                                 for r in range(SC_RSUB))

                accs.extend(part)
            tot = jnp.zeros((SC_LANES,), F32)
            for r in range(SC_GROUP):
                tot = jnp.where(lane == r, jnp.sum(accs[r]), tot)
            hid_v[pl.ds(tt * N_SEL + g * SC_GROUP, SC_GROUP)] = tot

        def store_block(tok0):
            pltpu.sync_copy(hid_v, hid_hbm.at[pl.ds(tok0 * N_SEL, SC_TOK_BLK * N_SEL)])

        _sc_token_blocks(n, tbl_hbm, idx_v, rows_v, sem, load_block, compute, store_block)

    return run(eidx.reshape(-1), h.reshape(-1), table).reshape(n, N_SEL)


def _sc_combine(eidx, coef, x, table):
    n = x.shape[0]

    @functools.partial(
        pl.kernel, out_type=jax.ShapeDtypeStruct((n * D_MODEL,), F32), mesh=_sc_mesh(),
        compiler_params=pltpu.CompilerParams(needs_layout_passes=False),
        scratch_types=[pltpu.VMEM((SC_TOK_BLK * N_SEL,), I32), pltpu.VMEM((SC_TOK_BLK * N_SEL,), F32),
                       pltpu.VMEM((2, SC_GROUP, D_MODEL), F32), pltpu.VMEM((SC_TOK_BLK * D_MODEL,), F32),
                       pltpu.SemaphoreType.DMA((2,))],
        name="peer_combine_sc")
    def run(eidx_hbm, c_hbm, x_hbm, tbl_hbm, out_hbm, idx_v, c_v, rows_v, out_v, sem):
        lane = lax.iota(I32, SC_LANES)

        def load_block(tok0):
            pltpu.sync_copy(eidx_hbm.at[pl.ds(tok0 * N_SEL, SC_TOK_BLK * N_SEL)], idx_v)
            pltpu.sync_copy(c_hbm.at[pl.ds(tok0 * N_SEL, SC_TOK_BLK * N_SEL)], c_v)
            pltpu.sync_copy(x_hbm.at[pl.ds(tok0 * D_MODEL, SC_TOK_BLK * D_MODEL)], out_v)

        def compute(tt, g, rows):
            cvec = c_v[pl.ds(tt * N_SEL + g * SC_GROUP, SC_GROUP)]
            coefs = [jnp.sum(jnp.where(lane == r, cvec, 0.0)) for r in range(SC_GROUP)]

            @plsc.parallel_loop(0, D_MODEL // SC_LANES, unroll=2)
            def _(j):
                sl = pl.ds(tt * D_MODEL + j * SC_LANES, SC_LANES)
                acc = out_v[sl]
                for r in range(SC_GROUP):
                    acc = acc + rows[r, pl.ds(j * SC_LANES, SC_LANES)] * coefs[r]
                out_v[sl] = acc

        def store_block(tok0):
            pltpu.sync_copy(out_v, out_hbm.at[pl.ds(tok0 * D_MODEL, SC_TOK_BLK * D_MODEL)])

        _sc_token_blocks(n, tbl_hbm, idx_v, rows_v, sem, load_block, compute, store_block)

    return run(eidx.reshape(-1), coef.reshape(-1), x.reshape(-1), table).reshape(n, D_MODEL)


def _gate_act_kernel(hid_ref, gate_ref, o_ref):
    hid = hid_ref[...]
    o_ref[...] = gate_ref[...] * (0.5 * hid * (1.0 + lax.erf(hid * np.float32(np.sqrt(0.5)))))


def _gate_act(hid, gate):
    n = hid.shape[0]
    tm = min(n, 2048)
    rows = pl.BlockSpec((tm, N_SEL), lambda i: (i, 0))
    return pl.pallas_call(
        _gate_act_kernel, grid=(n // tm,), in_specs=[rows, rows], out_specs=rows,
        out_shape=jax.ShapeDtypeStruct((n, N_SEL), F32), compiler_params=_params(("parallel",)),
        name="peer_gate_act",
    )(hid, gate)


def _t5_bucket(rel):
    nb = NUM_BUCKETS // 2
    max_exact = nb // 2
    ret = jnp.where(rel > 0, nb, 0)
    n = jnp.abs(rel)
    nf = jnp.maximum(n, 1).astype(F32)
    large = max_exact + (jnp.log(nf / max_exact) / math.log(MAX_DISTANCE / max_exact)
                         * (nb - max_exact)).astype(I32)
    large = jnp.minimum(large, nb - 1)
    return ret + jnp.where(n < max_exact, n, large)


def _rel_bias(rel_bias, n_q, n_k, n_before):
    rel = (jnp.arange(n_k)[None, :] - n_before) - jnp.arange(n_q)[:, None]
    return jnp.transpose(rel_bias[_t5_bucket(rel)].astype(F32), (2, 0, 1))


def _state_to_pairs(wkv):
    b = wkv.shape[0]
    s = wkv.reshape(b, N_PAIRS, 2, R_HD, R_HD)
    z = jnp.zeros_like(s[:, :, 0])
    top = jnp.concatenate([s[:, :, 0], z], axis=-1)
    bot = jnp.concatenate([z, s[:, :, 1]], axis=-1)
    return jnp.concatenate([top, bot], axis=-2)


def _pairs_to_state(s_bd):
    b = s_bd.shape[0]
    return jnp.stack([s_bd[:, :, :R_HD, :R_HD], s_bd[:, :, R_HD:, R_HD:]], axis=2).reshape(
        b, R_HEADS, R_HD, R_HD)


def _layer(x, shift0, wkv0, kv_cache, bias, lp):
    b, t, _ = x.shape
    n = b * t
    z = _norm_matmul(x.reshape(n, D_MODEL), lp["ln1_g"].reshape(1, -1), lp["w_in"], 512).reshape(b, t, IN_COLS)
    r, k2, v, kk, bb, lw, g, bonus, qn, kn = _prep(z, shift0, lp)
    v_new = z[:, :, R_PROJ + A_DIM + KV_DIM:]
    if t % CHUNK:
        pad = lambda a: jnp.pad(a, ((0, 0), (0, CHUNK - t % CHUNK), (0, 0)))
        y, s_fin = _chunk_scan(*(pad(a) for a in (r, k2, v, kk, bb, lw)), _state_to_pairs(wkv0))
        y = y[:, :t]
    else:
        y, s_fin = _chunk_scan(r, k2, v, kk, bb, lw, _state_to_pairs(wkv0))
    if kv_cache is None:
        a_out = _swa_prompt(qn, kn, z, bias, lp["sink"])
    else:
        a_out = _swa_sample(qn, kn, v_new, kv_cache[0].reshape(b, -1, KV_DIM),
                            kv_cache[1].reshape(b, -1, KV_DIM), bias, lp["sink"])
    flat = lambda a: a.reshape(n, a.shape[-1])
    x1, h, qq = _mix_out(flat(y), flat(bonus), flat(g), flat(a_out), flat(x), lp)
    eidx, gate = _topk(qq, lp["sub_keys"])
    sel = lambda a: jnp.transpose(a, (2, 0, 1)).reshape(n, N_SEL)
    eidx = sel(eidx)
    coef = _gate_act(_sc_hidden(eidx, h, lp["peer_u"]), sel(gate))
    x2 = _sc_combine(eidx, coef, x1, lp["peer_v"]).reshape(b, t, D_MODEL)
    return (x2, _pairs_to_state(s_fin), z[:, -1, :R_PROJ],
            kn.reshape(b, t, A_KV, A_HD), v_new.reshape(b, t, A_KV, A_HD))


def kernel(x_prompt, x_sample, state_rwkv_wkv, state_rwkv_shift, cache_swa_k, cache_swa_v, ln1_g, w_in,
           mu_shift, w0, w2, a0, a2, g2, k_k, k_a, r_k, gn_g, gn_b, q_norm_g, k_norm_g, attn_sink,
           rel_bias, w_out, ln2_g, w_pq, sub_keys, peer_u, peer_v):
    depth = w_in.shape[0]
    b_p, s_p = x_prompt.shape[:2]
    b_s, t_s = x_sample.shape[:2]
    n_cache = cache_swa_k.shape[2]
    n_keep = min(WINDOW, s_p)
    bias_p = _rel_bias(rel_bias, CHUNK, (WIN_CHUNKS + 1) * CHUNK, WIN_CHUNKS * CHUNK)
    bias_s = _rel_bias(rel_bias, t_s, n_cache + t_s, n_cache)
    head_id = jnp.arange(R_DIM) // R_HD
    e64 = (head_id[:, None] == head_id[None, :]).astype(BF16)
    zpad = jnp.zeros((LANES - W_LORA, R_DIM), F32)
    xp, xs = x_prompt, x_sample
    outs = [[] for _ in range(8)]
    for l in range(depth):
        lp = {
            "ln1_g": ln1_g[l], "w_in": w_in[l].astype(BF16), "mu": mu_shift[l], "w0": w0[l], "a0": a0[l],
            "k_k": k_k[l], "k_a": k_a[l], "r_k": r_k[l].reshape(-1), "gn_g": gn_g[l], "gn_b": gn_b[l],
            "w2p": jnp.concatenate([w2[l], zpad], axis=0).astype(BF16),
            "a2p": jnp.concatenate([zpad, a2[l]], axis=0).astype(BF16),
            "g2": g2[l].astype(BF16), "e64": e64,
            "q_gain": jnp.tile(q_norm_g[l], A_HEADS), "k_gain": jnp.tile(k_norm_g[l], A_KV),
            "sink": attn_sink[l].astype(F32), "w_out": w_out[l].astype(BF16), "ln2_g": ln2_g[l],
            "w_pq": w_pq[l].astype(BF16), "sub_keys": sub_keys[l], "peer_u": peer_u[l], "peer_v": peer_v[l],
        }
        xp, wkv_p, sh_p, k_p, v_p = _layer(
            xp, jnp.zeros((b_p, R_PROJ), F32), jnp.zeros((b_p, R_HEADS, R_HD, R_HD), F32), None, bias_p, lp)
        xs, wkv_s, sh_s, k_s, v_s = _layer(
            xs, state_rwkv_shift[l], state_rwkv_wkv[l].astype(F32), (cache_swa_k[l], cache_swa_v[l]),
            bias_s, lp)
        for lst, val in zip(outs, (wkv_p, sh_p, k_p[:, s_p - n_keep:], v_p[:, s_p - n_keep:],
                                   wkv_s, sh_s, k_s, v_s)):
            lst.append(val)
    return (xp, xs) + tuple(jnp.stack(o) for o in outs)
```

```python
import functools
import math

import numpy as np
import jax
import jax.numpy as jnp
from jax import lax
from jax.experimental import pallas as pl
from jax.experimental.pallas import tpu as pltpu
from jax.experimental.pallas import tpu_sc as plsc

F32 = jnp.float32
BF16 = jnp.bfloat16
I32 = jnp.int32

D_MODEL = 1024
CHUNK = 64
R_HEADS = 8
R_HD = 64
R_DIM = R_HEADS * R_HD
W_LORA = 64
A_LORA = 64
G_LORA = 128
R_PROJ = 3 * R_DIM + W_LORA + A_LORA + G_LORA
A_HEADS = 8
A_KV = 2
A_GROUP = A_HEADS // A_KV
A_HD = 64
A_DIM = A_HEADS * A_HD
KV_DIM = A_KV * A_HD
IN_COLS = R_PROJ + A_DIM + 2 * KV_DIM
WINDOW = 128
WIN_CHUNKS = WINDOW // CHUNK
NUM_BUCKETS = 32
MAX_DISTANCE = 128
PEER_HEADS = 8
N_KEYS = 128
PK_DIM = 256
PK_HALF = PK_DIM // 2
PEER_TOPK = 16
N_SEL = PEER_HEADS * PEER_TOPK
NORM_EPS = 1e-6
GN_EPS = 64e-5
NEG_INF = -1e30

LANES = 128
N_PAIRS = R_DIM // LANES
VMEM_LIMIT = 48 * 1024 * 1024
N_CAND = PEER_TOPK + 8 * (PEER_TOPK - 1)
SC_CORES = 2
SC_SUBCORES = 16
SC_LANES = 16
SC_WORKERS = SC_CORES * SC_SUBCORES
SC_TOK_BLK = 8
SC_GROUP = SC_LANES
SC_N_GROUPS = N_SEL // SC_GROUP
SC_RSUB = 8
SC_ROW_TILE = (D_MODEL // LANES, LANES)


def _params(sem):
    return pltpu.CompilerParams(dimension_semantics=sem, vmem_limit_bytes=VMEM_LIMIT)


def _mm(a, b):
    return jnp.dot(a, b, preferred_element_type=F32)


def _mm_nt(a, b):
    return lax.dot_general(a, b, (((1,), (1,)), ((), ())), preferred_element_type=F32)


def _mm_tn(a, b):
    return lax.dot_general(a, b, (((0,), (0,)), ((), ())), preferred_element_type=F32)


def _split2(a):
    hi = a.astype(BF16)
    return hi, (a - hi.astype(F32)).astype(BF16)


def _split3(a):
    hi = a.astype(BF16)
    r = a - hi.astype(F32)
    mid = r.astype(BF16)
    return hi, mid, (r - mid.astype(F32)).astype(BF16)


def _x3(mm, a, b):
    ah, al = _split2(a)
    bh, bl = _split2(b)
    return mm(ah, bh) + mm(ah, bl) + mm(al, bh)


def _exact_lhs(mm, a_bf16, b):
    b0, b1, b2 = _split3(b)
    return mm(a_bf16, b0) + mm(a_bf16, b1) + mm(a_bf16, b2)


def _seg_sum(x, e_bf16):
    x0, x1, x2 = _split3(x)
    return _mm(x0, e_bf16) + _mm(x1, e_bf16) + _mm(x2, e_bf16)


def _sigmoid(x):
    return 1.0 / (1.0 + jnp.exp(-x))


def _norm_matmul_kernel(x_ref, g_ref, w_ref, o_ref, h_ref):
    @pl.when(pl.program_id(1) == 0)
    def _():
        x = x_ref[...]
        ms = jnp.mean(x * x, axis=-1, keepdims=True)
        h_ref[...] = (x * lax.rsqrt(ms + NORM_EPS) * g_ref[...]).astype(BF16)

    o_ref[...] = _mm(h_ref[...], w_ref[...])


def _norm_matmul(x, g, w_bf16, tn):
    n, k = x.shape
    m = w_bf16.shape[1]
    tm = min(n, 512)
    return pl.pallas_call(
        _norm_matmul_kernel,
        grid=(n // tm, m // tn),
        in_specs=[pl.BlockSpec((tm, k), lambda i, j: (i, 0)),
                  pl.BlockSpec((1, k), lambda i, j: (0, 0)),
                  pl.BlockSpec((k, tn), lambda i, j: (0, j))],
        out_specs=pl.BlockSpec((tm, tn), lambda i, j: (i, j)),
        out_shape=jax.ShapeDtypeStruct((n, m), F32),
        scratch_shapes=[pltpu.VMEM((tm, k), BF16)],
        compiler_params=_params(("parallel", "arbitrary")),
        name="norm_inproj",
    )(x, g, w_bf16)


def _prep_kernel(z_ref, zp_ref, sh_ref, mu_ref, w0_ref, a0_ref, kk_ref, ka_ref, rk_ref,
                 w2_ref, a2_ref, g2_ref, e_ref, qg_ref, kg_ref,
                 r_o, k_o, v_o, kk_o, b_o, lw_o, g_o, bo_o, qn_o, kn_o):
    i = pl.program_id(1)
    zt = z_ref[0]
    tp = zt.shape[0]
    zr = zt[:, :R_PROJ]
    prev_row = jnp.where(i == 0, sh_ref[0], zp_ref[0][7:8, :R_PROJ])
    row = lax.broadcasted_iota(I32, (tp, 1), 0)
    prev = jnp.where(row == 0, prev_row, pltpu.roll(zr, 1, axis=0))
    zs = zr + (prev - zr) * mu_ref[...]
    r = zs[:, 0:R_DIM]
    k = zs[:, R_DIM:2 * R_DIM]
    v = zs[:, 2 * R_DIM:3 * R_DIM]
    lo = zs[:, 3 * R_DIM:3 * R_DIM + W_LORA + A_LORA]
    g_lo = zs[:, 3 * R_DIM + W_LORA + A_LORA:R_PROJ]
    e = e_ref[...]
    w_in = -(w0_ref[...] + _mm(jnp.tanh(lo).astype(BF16), w2_ref[...]))
    softplus = jnp.maximum(w_in, 0.0) + jnp.log1p(jnp.exp(-jnp.abs(w_in)))
    w_log = -softplus - 0.5
    lw_o[0] = -jnp.exp(w_log)
    a = _sigmoid(a0_ref[...] + _mm(lo.astype(BF16), a2_ref[...]))
    g_o[0] = _mm(_sigmoid(g_lo).astype(BF16), g2_ref[...])
    kk = k * kk_ref[...]
    kk = kk / jnp.maximum(jnp.sqrt(_seg_sum(kk * kk, e)), 1e-12)
    k2 = k * (1.0 + (a - 1.0) * ka_ref[...])
    r_o[0] = r
    k_o[0] = k2
    v_o[0] = v
    kk_o[0] = kk
    b_o[0] = kk * a
    bo_o[0] = _seg_sum(r * k2 * rk_ref[...], e) * v
    q = zt[:, R_PROJ:R_PROJ + A_DIM]
    qn_o[0] = q * lax.rsqrt(_seg_sum(q * q, e) * (1.0 / A_HD) + NORM_EPS) * qg_ref[...]
    kx = zt[:, R_PROJ + A_DIM:R_PROJ + A_DIM + KV_DIM]
    e_kv = e[:KV_DIM, :KV_DIM]
    kn_o[0] = kx * lax.rsqrt(_seg_sum(kx * kx, e_kv) * (1.0 / A_HD) + NORM_EPS) * kg_ref[...]


def _prep(z, shift0, lp):
    b, t, _ = z.shape
    tp = min(t, 256)
    row = lambda a: a.reshape(1, -1)
    vec = lambda n: pl.BlockSpec((1, n), lambda bi, i: (0, 0))
    full = lambda s: pl.BlockSpec(s, lambda bi, i: (0, 0))
    wide = pl.BlockSpec((1, tp, R_DIM), lambda bi, i: (bi, i, 0))
    outs = [jax.ShapeDtypeStruct((b, t, R_DIM), F32)] * 9 + [jax.ShapeDtypeStruct((b, t, KV_DIM), F32)]
    return pl.pallas_call(
        _prep_kernel,
        grid=(b, t // tp),
        in_specs=[pl.BlockSpec((1, tp, IN_COLS), lambda bi, i: (bi, i, 0)),
                  pl.BlockSpec((1, 8, IN_COLS), lambda bi, i: (bi, jnp.maximum(i * (tp // 8) - 1, 0), 0)),
                  pl.BlockSpec((1, 1, R_PROJ), lambda bi, i: (bi, 0, 0)),
                  vec(R_PROJ), vec(R_DIM), vec(R_DIM), vec(R_DIM), vec(R_DIM), vec(R_DIM),
                  full((LANES, R_DIM)), full((LANES, R_DIM)), full((G_LORA, R_DIM)),
                  full((R_DIM, R_DIM)), vec(A_DIM), vec(KV_DIM)],
        out_specs=[wide] * 9 + [pl.BlockSpec((1, tp, KV_DIM), lambda bi, i: (bi, i, 0))],
        out_shape=outs,
        compiler_params=_params(("parallel", "parallel")),
        name="rwkv_prep",
    )(z, z, shift0.reshape(b, 1, R_PROJ), row(lp["mu"]), row(lp["w0"]), row(lp["a0"]), row(lp["k_k"]),
      row(lp["k_a"]), row(lp["r_k"]), lp["w2p"], lp["a2p"], lp["g2"], lp["e64"], row(lp["q_gain"]),
      row(lp["k_gain"]))


def _stack_heads(x, lo_mask):
    return jnp.concatenate([jnp.where(lo_mask, x, 0.0), jnp.where(lo_mask, 0.0, x)], axis=0)


def _chunk_kernel(r_ref, k_ref, v_ref, kk_ref, b_ref, lw_ref, s0_ref, y_ref, sf_ref, s_ref):
    c = pl.program_id(1)
    L = CHUNK

    @pl.when(c == 0)
    def _():
        s_ref[...] = s0_ref[0]

    lane = lax.broadcasted_iota(I32, (1, LANES), 1)
    lo_mask = lane < R_HD
    rr = lax.broadcasted_iota(I32, (L, 2 * L), 0)
    cc = lax.broadcasted_iota(I32, (L, 2 * L), 1)
    cc = jnp.where(cc >= L, cc - L, cc)
    strict = rr > cc
    incl = rr >= cc
    t_r = lax.broadcasted_iota(I32, (L, L), 0)
    t_c = lax.broadcasted_iota(I32, (L, L), 1)
    tri = (t_r >= t_c).astype(BF16)
    col2 = lax.broadcasted_iota(I32, (L, 2 * L), 1) < L
    eye_r = lax.broadcasted_iota(I32, (2 * L, 2 * L), 0)
    eye_c = lax.broadcasted_iota(I32, (2 * L, 2 * L), 1)
    eye = (eye_r == eye_c).astype(F32)

    for p in range(N_PAIRS):
        sl = slice(p * LANES, (p + 1) * LANES)
        r = r_ref[0][:, sl]
        k = k_ref[0][:, sl]
        v = v_ref[0][:, sl]
        kk = kk_ref[0][:, sl]
        bb = b_ref[0][:, sl]
        lw = lw_ref[0][:, sl]
        s0 = s_ref[p]

        cl = _exact_lhs(_mm, tri, lw)
        cl_last = cl[L - 1:L, :]
        e_neg = jnp.exp(-cl)
        e_last = jnp.exp(cl_last - cl)
        rt = r * jnp.exp(cl)
        at = kk * jnp.exp(cl - lw)
        kt = k * e_neg
        bt = bb * e_neg

        lhs = jnp.concatenate([at, rt], axis=0)
        rhs = jnp.concatenate([_stack_heads(kt, lo_mask), _stack_heads(bt, lo_mask)], axis=0)
        gm = _x3(_mm_nt, lhs, rhs)
        mk = jnp.where(strict, gm[:L, :2 * L], 0.0)
        mb = jnp.where(strict, gm[:L, 2 * L:], 0.0)
        hk = jnp.where(incl, gm[L:, :2 * L], 0.0)
        hb = jnp.where(incl, gm[L:, 2 * L:], 0.0)

        nil = -jnp.concatenate([jnp.where(col2, mb, 0.0), jnp.where(col2, 0.0, mb)], axis=0)
        tinv = eye + nil
        for _ in range(5):
            nil = _x3(_mm, nil, nil)
            tinv = tinv + _x3(_mm, nil, tinv)

        vs = _stack_heads(v, lo_mask)
        rhs_u = -(_x3(_mm_nt, at, s0) + _x3(_mm, mk, vs))
        us = _x3(_mm, tinv, _stack_heads(rhs_u, lo_mask))
        y = _x3(_mm_nt, rt, s0) + _x3(_mm, hk, vs) + _x3(_mm, hb, us)
        y_ref[0, :, sl] = y
        s_ref[p] = (s0 * jnp.exp(cl_last)
                    + _x3(_mm_tn, vs, _stack_heads(k * e_last, lo_mask))
                    + _x3(_mm_tn, us, _stack_heads(bb * e_last, lo_mask)))

    @pl.when(c == pl.num_programs(1) - 1)
    def _():
        sf_ref[0] = s_ref[...]


def _chunk_scan(r, k, v, kk, bb, lw, s0_bd):
    b, t, _ = r.shape
    wide = pl.BlockSpec((1, CHUNK, R_DIM), lambda bi, c: (bi, c, 0))
    st = pl.BlockSpec((1, N_PAIRS, LANES, LANES), lambda bi, c: (bi, 0, 0, 0))
    return pl.pallas_call(
        _chunk_kernel,
        grid=(b, t // CHUNK),
        in_specs=[wide] * 6 + [st],
        out_specs=[wide, st],
        out_shape=[jax.ShapeDtypeStruct((b, t, R_DIM), F32),
                   jax.ShapeDtypeStruct((b, N_PAIRS, LANES, LANES), F32)],
        scratch_shapes=[pltpu.VMEM((N_PAIRS, LANES, LANES), F32)],
        compiler_params=_params(("parallel", "arbitrary")),
        name="rwkv_chunk",
    )(r, k, v, kk, bb, lw, s0_bd)


def _swa_kernel(*refs, n_seg, banded):
    q_ref = refs[0]
    k_refs = refs[1:1 + n_seg]
    v_refs = refs[1 + n_seg:1 + 2 * n_seg]
    bias_ref, sink_ref, o_ref = refs[1 + 2 * n_seg:]
    c = pl.program_id(1)
    q = q_ref[0]
    kcat = jnp.concatenate([kr[0] for kr in k_refs], axis=0)
    vcat = jnp.concatenate([vr[0] for vr in v_refs], axis=0)
    n_k = kcat.shape[0]
    lane = lax.broadcasted_iota(I32, (1, LANES), 1)
    lo_mask = lane < A_HD
    k_sw = pltpu.roll(kcat, A_HD, axis=1)
    v_sw = pltpu.roll(vcat, A_HD, axis=1)
    k_dup = [jnp.where(lo_mask, kcat, k_sw).astype(BF16), jnp.where(lo_mask, k_sw, kcat).astype(BF16)]
    v_dup = [jnp.where(lo_mask, vcat, v_sw).astype(BF16), jnp.where(lo_mask, v_sw, vcat).astype(BF16)]
    if banded:
        key_chunk = c - WIN_CHUNKS + lax.broadcasted_iota(I32, (1, n_k), 1) // CHUNK
        valid = key_chunk >= 0
    for pair in range(A_HEADS // 2):
        qp = q[:, pair * LANES:(pair + 1) * LANES]
        outs = []
        for w in range(2):
            hq = 2 * pair + w
            kvh = hq // A_GROUP
            qm = jnp.where(lo_mask if w == 0 else jnp.logical_not(lo_mask), qp, 0.0).astype(BF16)
            s = _mm_nt(qm, k_dup[kvh]) * (A_HD ** -0.5) + bias_ref[hq]
            if banded:
                s = jnp.where(valid, s, NEG_INF)
            sink = sink_ref[0:1, hq:hq + 1]
            m = jnp.maximum(jnp.max(s, axis=-1, keepdims=True), sink)
            pr = jnp.exp(s - m)
            den = jnp.sum(pr, axis=-1, keepdims=True) + jnp.exp(sink - m)
            outs.append(_mm((pr / den).astype(BF16), v_dup[kvh]))
        o_ref[0, :, pair * LANES:(pair + 1) * LANES] = jnp.where(lo_mask, outs[0], outs[1])


def _swa_prompt(qn, kn, z, bias, sink):
    b, t, _ = qn.shape
    v_col = (R_PROJ + A_DIM + KV_DIM) // KV_DIM
    seg = lambda s, col: pl.BlockSpec(
        (1, CHUNK, KV_DIM), lambda bi, c: (bi, jnp.maximum(c - WIN_CHUNKS + s, 0), col))
    n_seg = WIN_CHUNKS + 1
    return pl.pallas_call(
        functools.partial(_swa_kernel, n_seg=n_seg, banded=True),
        grid=(b, t // CHUNK),
        in_specs=[pl.BlockSpec((1, CHUNK, A_DIM), lambda bi, c: (bi, c, 0))]
                 + [seg(s, 0) for s in range(n_seg)] + [seg(s, v_col) for s in range(n_seg)]
                 + [pl.BlockSpec(bias.shape, lambda bi, c: (0, 0, 0)),
                    pl.BlockSpec((1, A_HEADS), lambda bi, c: (0, 0))],
        out_specs=pl.BlockSpec((1, CHUNK, A_DIM), lambda bi, c: (bi, c, 0)),
        out_shape=jax.ShapeDtypeStruct((b, t, A_DIM), F32),
        compiler_params=_params(("parallel", "parallel")),
        name="swa_prompt",
    )(qn, *([kn] * n_seg), *([z] * n_seg), bias, sink.reshape(1, A_HEADS))


def _swa_sample(qn, kn, v_new, k_cache, v_cache, bias, sink):
    b, t, _ = qn.shape
    n_cache = k_cache.shape[1]
    cur = lambda w: pl.BlockSpec((1, t, w), lambda bi, c: (bi, 0, 0))
    old = pl.BlockSpec((1, n_cache, KV_DIM), lambda bi, c: (bi, 0, 0))
    return pl.pallas_call(
        functools.partial(_swa_kernel, n_seg=2, banded=False),
        grid=(b, 1),
        in_specs=[cur(A_DIM), old, cur(KV_DIM), old, cur(KV_DIM),
                  pl.BlockSpec(bias.shape, lambda bi, c: (0, 0, 0)),
                  pl.BlockSpec((1, A_HEADS), lambda bi, c: (0, 0))],
        out_specs=cur(A_DIM),
        out_shape=jax.ShapeDtypeStruct((b, t, A_DIM), F32),
        compiler_params=_params(("parallel", "parallel")),
        name="swa_sample",
    )(qn, k_cache, kn, v_cache, v_new, bias, sink.reshape(1, A_HEADS))


def _mix_out_kernel(y_ref, bo_ref, g_ref, a_ref, x_ref, gng_ref, gnb_ref, e_ref, wo_ref, ln_ref, wq_ref,
                    xo_ref, h_ref, qq_ref):
    e = e_ref[...]
    y = y_ref[...]
    mu = _seg_sum(y, e) * (1.0 / R_HD)
    d = y - mu
    var = _seg_sum(d * d, e) * (1.0 / R_HD)
    yn = d * lax.rsqrt(var + GN_EPS) * gng_ref[...] + gnb_ref[...]
    mix_r = ((yn + bo_ref[...]) * g_ref[...]).astype(BF16)
    x = (x_ref[...] + _mm(mix_r, wo_ref[:R_DIM, :]) + _mm(a_ref[...].astype(BF16), wo_ref[R_DIM:, :]))
    xo_ref[...] = x
    ms = jnp.mean(x * x, axis=-1, keepdims=True)
    h = x * lax.rsqrt(ms + NORM_EPS) * ln_ref[...]
    h_ref[...] = h
    qq_ref[...] = _mm(h.astype(BF16), wq_ref[...])


def _mix_out(y, bonus, g, a_out, x, lp):
    n = x.shape[0]
    tm = min(n, 256)
    nq = PEER_HEADS * PK_DIM
    rows = lambda w: pl.BlockSpec((tm, w), lambda i: (i, 0))
    full = lambda s: pl.BlockSpec(s, lambda i: (0, 0))
    return pl.pallas_call(
        _mix_out_kernel,
        grid=(n // tm,),
        in_specs=[rows(R_DIM), rows(R_DIM), rows(R_DIM), rows(A_DIM), rows(D_MODEL),
                  full((1, R_DIM)), full((1, R_DIM)), full((R_DIM, R_DIM)),
                  full((D_MODEL, D_MODEL)), full((1, D_MODEL)), full((D_MODEL, nq))],
        out_specs=[rows(D_MODEL), rows(D_MODEL), rows(nq)],
        out_shape=[jax.ShapeDtypeStruct((n, D_MODEL), F32), jax.ShapeDtypeStruct((n, D_MODEL), F32),
                   jax.ShapeDtypeStruct((n, nq), F32)],
        compiler_params=_params(("parallel",)),
        name="mix_out_query",
    )(y, bonus, g, a_out, x, lp["gn_g"].reshape(1, -1), lp["gn_b"].reshape(1, -1), lp["e64"],
      lp["w_out"], lp["ln2_g"].reshape(1, -1), lp["w_pq"])


def _pick_rounds(s_ref, n_rows, emit):
    tb = s_ref.shape[1]
    rowid = lax.broadcasted_iota(I32, (n_rows, tb), 0)
    for rnd in range(PEER_TOPK):
        s = s_ref[...]
        m = jnp.max(s, axis=0, keepdims=True)
        idx = jnp.min(jnp.where(s == m, rowid, n_rows), axis=0, keepdims=True)
        hit = rowid == idx
        s_ref[...] = jnp.where(hit, -jnp.inf, s)
        emit(rnd, m, idx, hit)


def _topk_kernel(qq_ref, keys_ref, ei_ref, gate_ref, s_ref, sv_ref, si_ref, c_ref, ci_ref, ts_ref):
    tb = qq_ref.shape[0]
    for half in range(2):
        qh = qq_ref[:, half * PK_HALF:(half + 1) * PK_HALF]
        s_ref[...] = _x3(_mm_nt, keys_ref[half], qh)

        def emit1(rnd, m, idx, hit, half=half):
            sv_ref[half, rnd:rnd + 1, :] = m
            si_ref[half, rnd:rnd + 1, :] = idx

        _pick_rounds(s_ref, N_KEYS, emit1)

    row0 = 0
    for a in range(PEER_TOPK):
        nb = PEER_TOPK if a == 0 else 8
        c_ref[row0:row0 + nb, :] = sv_ref[0, a:a + 1, :] + sv_ref[1, 0:nb, :]
        ci_ref[row0:row0 + nb, :] = si_ref[0, a:a + 1, :] * N_KEYS + si_ref[1, 0:nb, :]
        row0 += nb

    def emit2(rnd, m, idx, hit):
        ts_ref[rnd:rnd + 1, :] = m
        ei_ref[0, rnd:rnd + 1, :] = jnp.max(jnp.where(hit, ci_ref[...], -1), axis=0, keepdims=True)

    _pick_rounds(c_ref, N_CAND, emit2)
    ts = ts_ref[...]
    ex = jnp.exp(ts - ts[0:1, :])
    gate_ref[0] = ex / jnp.sum(ex, axis=0, keepdims=True)


def _topk(qq, sub_keys):
    n = qq.shape[0]
    tb = min(n, 512)
    out = pl.BlockSpec((1, PEER_TOPK, tb), lambda i, h: (h, 0, i))
    return pl.pallas_call(
        _topk_kernel,
        grid=(n // tb, PEER_HEADS),
        in_specs=[pl.BlockSpec((tb, PK_DIM), lambda i, h: (i, h)),
                  pl.BlockSpec((2, N_KEYS, PK_HALF), lambda i, h: (0, 0, 0))],
        out_specs=[out, out],
        out_shape=[jax.ShapeDtypeStruct((PEER_HEADS, PEER_TOPK, n), I32),
                   jax.ShapeDtypeStruct((PEER_HEADS, PEER_TOPK, n), F32)],
        scratch_shapes=[pltpu.VMEM((N_KEYS, tb), F32), pltpu.VMEM((2, PEER_TOPK, tb), F32),
                        pltpu.VMEM((2, PEER_TOPK, tb), I32), pltpu.VMEM((N_CAND, tb), F32),
                        pltpu.VMEM((N_CAND, tb), I32), pltpu.VMEM((PEER_TOPK, tb), F32)],
        compiler_params=_params(("parallel", "parallel")),
        name="peer_topk",
    )(qq, sub_keys)


def _expert_gather(tbl_hbm, idx_v, rows_v, sem, tt, g, buf):
    return pltpu.make_async_copy(tbl_hbm.at[idx_v[pl.ds(tt * N_SEL + g * SC_GROUP, SC_GROUP)]],
                                 rows_v.at[buf], sem.at[buf])


def _sc_token_blocks(n, tbl_hbm, idx_v, rows_v, sem, load_block, compute, store_block):
    wid = lax.axis_index("s") * SC_CORES + lax.axis_index("c")
    npw = n // SC_WORKERS
    steps = SC_TOK_BLK * SC_N_GROUPS

    @pl.loop(0, npw // SC_TOK_BLK)
    def _(blk):
        tok0 = wid * npw + blk * SC_TOK_BLK
        load_block(tok0)
        _expert_gather(tbl_hbm, idx_v, rows_v, sem, 0, 0, 0).start()

        @pl.loop(0, steps, step=2)
        def _(s0):
            for b in range(2):
                s = s0 + b

                @pl.when(s + 1 < steps)
                def _():
                    _expert_gather(tbl_hbm, idx_v, rows_v, sem, (s + 1) // SC_N_GROUPS, (s + 1) % SC_N_GROUPS,
                                   1 - b).start()

                _expert_gather(tbl_hbm, idx_v, rows_v, sem, s // SC_N_GROUPS, s % SC_N_GROUPS, b).wait()
                compute(s // SC_N_GROUPS, s % SC_N_GROUPS, rows_v.at[b])

        store_block(tok0)


def _row_chunk(j):
    per_line = LANES // SC_LANES
    return j // per_line, pl.ds((j % per_line) * SC_LANES, SC_LANES)


def _sc_mesh():
    return plsc.VectorSubcoreMesh(core_axis_name="c", subcore_axis_name="s",
                                  num_cores=SC_CORES, num_subcores=SC_SUBCORES)


def _sc_hidden(eidx, h, table):
    n = h.shape[0]

    @functools.partial(
        pl.kernel, out_type=jax.ShapeDtypeStruct((n * N_SEL,), F32), mesh=_sc_mesh(),
        compiler_params=pltpu.CompilerParams(needs_layout_passes=False),
        scratch_types=[pltpu.VMEM((SC_TOK_BLK * N_SEL,), I32), pltpu.VMEM((SC_TOK_BLK * D_MODEL,), F32),
                       pltpu.VMEM((2, SC_GROUP) + SC_ROW_TILE, F32), pltpu.VMEM((SC_TOK_BLK * N_SEL,), F32),
                       pltpu.SemaphoreType.DMA((2,))],
        name="peer_hidden_sc")
    def run(eidx_hbm, h_hbm, tbl_hbm, hid_hbm, idx_v, h_v, rows_v, hid_v, sem):
        lane = lax.iota(I32, SC_LANES)

        def load_block(tok0):
            pltpu.sync_copy(eidx_hbm.at[pl.ds(tok0 * N_SEL, SC_TOK_BLK * N_SEL)], idx_v)
            pltpu.sync_copy(h_hbm.at[pl.ds(tok0 * D_MODEL, SC_TOK_BLK * D_MODEL)], h_v)

        def compute(tt, g, rows):
            accs = []
            for sub in range(SC_GROUP // SC_RSUB):
                zero = tuple(jnp.zeros((SC_LANES,), F32) for _ in range(SC_RSUB))

                @plsc.parallel_loop(0, D_MODEL // SC_LANES, unroll=4, carry=zero)
                def part(j, acc):
                    hvj = h_v[pl.ds(tt * D_MODEL + j * SC_LANES, SC_LANES)]
                    return tuple(acc[r] + rows[(sub * SC_RSUB + r, *_row_chunk(j))] * hvj
                                 for r in range(SC_RSUB))

                accs.extend(part)
            tot = jnp.zeros((SC_LANES,), F32)
            for r in range(SC_GROUP):
                tot = jnp.where(lane == r, jnp.sum(accs[r]), tot)
            hid_v[pl.ds(tt * N_SEL + g * SC_GROUP, SC_GROUP)] = tot

        def store_block(tok0):
            pltpu.sync_copy(hid_v, hid_hbm.at[pl.ds(tok0 * N_SEL, SC_TOK_BLK * N_SEL)])

        _sc_token_blocks(n, tbl_hbm, idx_v, rows_v, sem, load_block, compute, store_block)

    return run(eidx.reshape(-1), h.reshape(-1), table).reshape(n, N_SEL)


def _sc_combine(eidx, coef, x, table):
    n = x.shape[0]

    @functools.partial(
        pl.kernel, out_type=jax.ShapeDtypeStruct((n * D_MODEL,), F32), mesh=_sc_mesh(),
        compiler_params=pltpu.CompilerParams(needs_layout_passes=False),
        scratch_types=[pltpu.VMEM((SC_TOK_BLK * N_SEL,), I32), pltpu.VMEM((SC_TOK_BLK * N_SEL,), F32),
                       pltpu.VMEM((2, SC_GROUP) + SC_ROW_TILE, F32), pltpu.VMEM((SC_TOK_BLK * D_MODEL,), F32),
                       pltpu.SemaphoreType.DMA((2,))],
        name="peer_combine_sc")
    def run(eidx_hbm, c_hbm, x_hbm, tbl_hbm, out_hbm, idx_v, c_v, rows_v, out_v, sem):
        lane = lax.iota(I32, SC_LANES)

        def load_block(tok0):
            pltpu.sync_copy(eidx_hbm.at[pl.ds(tok0 * N_SEL, SC_TOK_BLK * N_SEL)], idx_v)
            pltpu.sync_copy(c_hbm.at[pl.ds(tok0 * N_SEL, SC_TOK_BLK * N_SEL)], c_v)
            pltpu.sync_copy(x_hbm.at[pl.ds(tok0 * D_MODEL, SC_TOK_BLK * D_MODEL)], out_v)

        def compute(tt, g, rows):
            cvec = c_v[pl.ds(tt * N_SEL + g * SC_GROUP, SC_GROUP)]
            coefs = [jnp.sum(jnp.where(lane == r, cvec, 0.0)) for r in range(SC_GROUP)]

            @plsc.parallel_loop(0, D_MODEL // SC_LANES, unroll=2)
            def _(j):
                sl = pl.ds(tt * D_MODEL + j * SC_LANES, SC_LANES)
                acc = out_v[sl]
                for r in range(SC_GROUP):
                    acc = acc + rows[(r, *_row_chunk(j))] * coefs[r]
                out_v[sl] = acc

        def store_block(tok0):
            pltpu.sync_copy(out_v, out_hbm.at[pl.ds(tok0 * D_MODEL, SC_TOK_BLK * D_MODEL)])

        _sc_token_blocks(n, tbl_hbm, idx_v, rows_v, sem, load_block, compute, store_block)

    return run(eidx.reshape(-1), coef.reshape(-1), x.reshape(-1), table).reshape(n, D_MODEL)


def _gate_act_kernel(hid_ref, gate_ref, o_ref):
    hid = hid_ref[...]
    o_ref[...] = gate_ref[...] * (0.5 * hid * (1.0 + lax.erf(hid * np.float32(np.sqrt(0.5)))))


def _gate_act(hid, gate):
    n = hid.shape[0]
    tm = min(n, 2048)
    rows = pl.BlockSpec((tm, N_SEL), lambda i: (i, 0))
    return pl.pallas_call(
        _gate_act_kernel, grid=(n // tm,), in_specs=[rows, rows], out_specs=rows,
        out_shape=jax.ShapeDtypeStruct((n, N_SEL), F32), compiler_params=_params(("parallel",)),
        name="peer_gate_act",
    )(hid, gate)


def _t5_bucket(rel):
    nb = NUM_BUCKETS // 2
    max_exact = nb // 2
    ret = jnp.where(rel > 0, nb, 0)
    n = jnp.abs(rel)
    nf = jnp.maximum(n, 1).astype(F32)
    large = max_exact + (jnp.log(nf / max_exact) / math.log(MAX_DISTANCE / max_exact)
                         * (nb - max_exact)).astype(I32)
    large = jnp.minimum(large, nb - 1)
    return ret + jnp.where(n < max_exact, n, large)


def _rel_bias(rel_bias, n_q, n_k, n_before):
    rel = (jnp.arange(n_k)[None, :] - n_before) - jnp.arange(n_q)[:, None]
    return jnp.transpose(rel_bias[_t5_bucket(rel)].astype(F32), (2, 0, 1))


def _state_to_pairs(wkv):
    b = wkv.shape[0]
    s = wkv.reshape(b, N_PAIRS, 2, R_HD, R_HD)
    z = jnp.zeros_like(s[:, :, 0])
    top = jnp.concatenate([s[:, :, 0], z], axis=-1)
    bot = jnp.concatenate([z, s[:, :, 1]], axis=-1)
    return jnp.concatenate([top, bot], axis=-2)


def _pairs_to_state(s_bd):
    b = s_bd.shape[0]
    return jnp.stack([s_bd[:, :, :R_HD, :R_HD], s_bd[:, :, R_HD:, R_HD:]], axis=2).reshape(
        b, R_HEADS, R_HD, R_HD)


def _layer(x, shift0, wkv0, kv_cache, bias, lp):
    b, t, _ = x.shape
    n = b * t
    z = _norm_matmul(x.reshape(n, D_MODEL), lp["ln1_g"].reshape(1, -1), lp["w_in"], 512).reshape(b, t, IN_COLS)
    r, k2, v, kk, bb, lw, g, bonus, qn, kn = _prep(z, shift0, lp)
    v_new = z[:, :, R_PROJ + A_DIM + KV_DIM:]
    if t % CHUNK:
        pad = lambda a: jnp.pad(a, ((0, 0), (0, CHUNK - t % CHUNK), (0, 0)))
        y, s_fin = _chunk_scan(*(pad(a) for a in (r, k2, v, kk, bb, lw)), _state_to_pairs(wkv0))
        y = y[:, :t]
    else:
        y, s_fin = _chunk_scan(r, k2, v, kk, bb, lw, _state_to_pairs(wkv0))
    if kv_cache is None:
        a_out = _swa_prompt(qn, kn, z, bias, lp["sink"])
    else:
        a_out = _swa_sample(qn, kn, v_new, kv_cache[0].reshape(b, -1, KV_DIM),
                            kv_cache[1].reshape(b, -1, KV_DIM), bias, lp["sink"])
    flat = lambda a: a.reshape(n, a.shape[-1])
    x1, h, qq = _mix_out(flat(y), flat(bonus), flat(g), flat(a_out), flat(x), lp)
    eidx, gate = _topk(qq, lp["sub_keys"])
    sel = lambda a: jnp.transpose(a, (2, 0, 1)).reshape(n, N_SEL)
    eidx = sel(eidx)
    coef = _gate_act(_sc_hidden(eidx, h, lp["peer_u"]), sel(gate))
    x2 = _sc_combine(eidx, coef, x1, lp["peer_v"]).reshape(b, t, D_MODEL)
    return (x2, _pairs_to_state(s_fin), z[:, -1, :R_PROJ],
            kn.reshape(b, t, A_KV, A_HD), v_new.reshape(b, t, A_KV, A_HD))


def kernel(x_prompt, x_sample, state_rwkv_wkv, state_rwkv_shift, cache_swa_k, cache_swa_v, ln1_g, w_in,
           mu_shift, w0, w2, a0, a2, g2, k_k, k_a, r_k, gn_g, gn_b, q_norm_g, k_norm_g, attn_sink,
           rel_bias, w_out, ln2_g, w_pq, sub_keys, peer_u, peer_v):
    depth = w_in.shape[0]
    b_p, s_p = x_prompt.shape[:2]
    b_s, t_s = x_sample.shape[:2]
    n_cache = cache_swa_k.shape[2]
    n_keep = min(WINDOW, s_p)
    bias_p = _rel_bias(rel_bias, CHUNK, (WIN_CHUNKS + 1) * CHUNK, WIN_CHUNKS * CHUNK)
    bias_s = _rel_bias(rel_bias, t_s, n_cache + t_s, n_cache)
    head_id = jnp.arange(R_DIM) // R_HD
    e64 = (head_id[:, None] == head_id[None, :]).astype(BF16)
    zpad = jnp.zeros((LANES - W_LORA, R_DIM), F32)
    xp, xs = x_prompt, x_sample
    outs = [[] for _ in range(8)]
    for l in range(depth):
        lp = {
            "ln1_g": ln1_g[l], "w_in": w_in[l].astype(BF16), "mu": mu_shift[l], "w0": w0[l], "a0": a0[l],
            "k_k": k_k[l], "k_a": k_a[l], "r_k": r_k[l].reshape(-1), "gn_g": gn_g[l], "gn_b": gn_b[l],
            "w2p": jnp.concatenate([w2[l], zpad], axis=0).astype(BF16),
            "a2p": jnp.concatenate([zpad, a2[l]], axis=0).astype(BF16),
            "g2": g2[l].astype(BF16), "e64": e64,
            "q_gain": jnp.tile(q_norm_g[l], A_HEADS), "k_gain": jnp.tile(k_norm_g[l], A_KV),
            "sink": attn_sink[l].astype(F32), "w_out": w_out[l].astype(BF16), "ln2_g": ln2_g[l],
            "w_pq": w_pq[l].astype(BF16), "sub_keys": sub_keys[l], "peer_u": peer_u[l].reshape((-1,) + SC_ROW_TILE), "peer_v": peer_v[l].reshape((-1,) + SC_ROW_TILE),
        }
        xp, wkv_p, sh_p, k_p, v_p = _layer(
            xp, jnp.zeros((b_p, R_PROJ), F32), jnp.zeros((b_p, R_HEADS, R_HD, R_HD), F32), None, bias_p, lp)
        xs, wkv_s, sh_s, k_s, v_s = _layer(
            xs, state_rwkv_shift[l], state_rwkv_wkv[l].astype(F32), (cache_swa_k[l], cache_swa_v[l]),
            bias_s, lp)
        for lst, val in zip(outs, (wkv_p, sh_p, k_p[:, s_p - n_keep:], v_p[:, s_p - n_keep:],
                                   wkv_s, sh_s, k_s, v_s)):
            lst.append(val)
    return (xp, xs) + tuple(jnp.stack(o) for o in outs)
```

```python
import functools
import math

import numpy as np
import jax
import jax.numpy as jnp
from jax import lax
from jax.experimental import pallas as pl
from jax.experimental.pallas import tpu as pltpu
from jax.experimental.pallas import tpu_sc as plsc

F32 = jnp.float32
BF16 = jnp.bfloat16
I32 = jnp.int32

D_MODEL = 1024
CHUNK = 64
R_HEADS = 8
R_HD = 64
R_DIM = R_HEADS * R_HD
W_LORA = 64
A_LORA = 64
G_LORA = 128
R_PROJ = 3 * R_DIM + W_LORA + A_LORA + G_LORA
A_HEADS = 8
A_KV = 2
A_GROUP = A_HEADS // A_KV
A_HD = 64
A_DIM = A_HEADS * A_HD
KV_DIM = A_KV * A_HD
IN_COLS = R_PROJ + A_DIM + 2 * KV_DIM
WINDOW = 128
WIN_CHUNKS = WINDOW // CHUNK
NUM_BUCKETS = 32
MAX_DISTANCE = 128
PEER_HEADS = 8
N_KEYS = 128
PK_DIM = 256
PK_HALF = PK_DIM // 2
PEER_TOPK = 16
N_SEL = PEER_HEADS * PEER_TOPK
NORM_EPS = 1e-6
GN_EPS = 64e-5
NEG_INF = -1e30

LANES = 128
N_PAIRS = R_DIM // LANES
VMEM_LIMIT = 48 * 1024 * 1024
N_CAND = PEER_TOPK + 8 * (PEER_TOPK - 1)
SC_CORES = 2
SC_SUBCORES = 16
SC_LANES = 16
SC_WORKERS = SC_CORES * SC_SUBCORES
SC_TOK_BLK = 8
SC_GROUP = SC_LANES
SC_N_GROUPS = N_SEL // SC_GROUP
SC_RSUB = 8
SC_ROW_TILE = (D_MODEL // LANES, LANES)
PROMPT_GROUPS = 2


def _params(sem):
    return pltpu.CompilerParams(dimension_semantics=sem, vmem_limit_bytes=VMEM_LIMIT)


def _mm(a, b):
    return jnp.dot(a, b, preferred_element_type=F32)


def _mm_nt(a, b):
    return lax.dot_general(a, b, (((1,), (1,)), ((), ())), preferred_element_type=F32)


def _mm_tn(a, b):
    return lax.dot_general(a, b, (((0,), (0,)), ((), ())), preferred_element_type=F32)


def _split2(a):
    hi = a.astype(BF16)
    return hi, (a - hi.astype(F32)).astype(BF16)


def _split3(a):
    hi = a.astype(BF16)
    r = a - hi.astype(F32)
    mid = r.astype(BF16)
    return hi, mid, (r - mid.astype(F32)).astype(BF16)


def _x3(mm, a, b):
    ah, al = _split2(a)
    bh, bl = _split2(b)
    return mm(ah, bh) + mm(ah, bl) + mm(al, bh)


def _exact_lhs(mm, a_bf16, b):
    b0, b1, b2 = _split3(b)
    return mm(a_bf16, b0) + mm(a_bf16, b1) + mm(a_bf16, b2)


def _seg_sum(x, e_bf16):
    x0, x1, x2 = _split3(x)
    return _mm(x0, e_bf16) + _mm(x1, e_bf16) + _mm(x2, e_bf16)


def _sigmoid(x):
    return 1.0 / (1.0 + jnp.exp(-x))


def _norm_matmul_kernel(x_ref, g_ref, w_ref, o_ref, h_ref):
    @pl.when(pl.program_id(1) == 0)
    def _():
        x = x_ref[...]
        ms = jnp.mean(x * x, axis=-1, keepdims=True)
        h_ref[...] = (x * lax.rsqrt(ms + NORM_EPS) * g_ref[...]).astype(BF16)

    o_ref[...] = _mm(h_ref[...], w_ref[...])


def _norm_matmul(x, g, w_bf16, tn):
    n, k = x.shape
    m = w_bf16.shape[1]
    tm = min(n, 512)
    return pl.pallas_call(
        _norm_matmul_kernel,
        grid=(n // tm, m // tn),
        in_specs=[pl.BlockSpec((tm, k), lambda i, j: (i, 0)),
                  pl.BlockSpec((1, k), lambda i, j: (0, 0)),
                  pl.BlockSpec((k, tn), lambda i, j: (0, j))],
        out_specs=pl.BlockSpec((tm, tn), lambda i, j: (i, j)),
        out_shape=jax.ShapeDtypeStruct((n, m), F32),
        scratch_shapes=[pltpu.VMEM((tm, k), BF16)],
        compiler_params=_params(("parallel", "arbitrary")),
        name="norm_inproj",
    )(x, g, w_bf16)


def _prep_kernel(z_ref, zp_ref, sh_ref, mu_ref, w0_ref, a0_ref, kk_ref, ka_ref, rk_ref,
                 w2_ref, a2_ref, g2_ref, e_ref, qg_ref, kg_ref,
                 r_o, k_o, v_o, kk_o, b_o, lw_o, g_o, bo_o, qn_o, kn_o):
    i = pl.program_id(1)
    zt = z_ref[0]
    tp = zt.shape[0]
    zr = zt[:, :R_PROJ]
    prev_row = jnp.where(i == 0, sh_ref[0], zp_ref[0][7:8, :R_PROJ])
    row = lax.broadcasted_iota(I32, (tp, 1), 0)
    prev = jnp.where(row == 0, prev_row, pltpu.roll(zr, 1, axis=0))
    zs = zr + (prev - zr) * mu_ref[...]
    r = zs[:, 0:R_DIM]
    k = zs[:, R_DIM:2 * R_DIM]
    v = zs[:, 2 * R_DIM:3 * R_DIM]
    lo = zs[:, 3 * R_DIM:3 * R_DIM + W_LORA + A_LORA]
    g_lo = zs[:, 3 * R_DIM + W_LORA + A_LORA:R_PROJ]
    e = e_ref[...]
    w_in = -(w0_ref[...] + _mm(jnp.tanh(lo).astype(BF16), w2_ref[...]))
    softplus = jnp.maximum(w_in, 0.0) + jnp.log1p(jnp.exp(-jnp.abs(w_in)))
    w_log = -softplus - 0.5
    lw_o[0] = -jnp.exp(w_log)
    a = _sigmoid(a0_ref[...] + _mm(lo.astype(BF16), a2_ref[...]))
    g_o[0] = _mm(_sigmoid(g_lo).astype(BF16), g2_ref[...])
    kk = k * kk_ref[...]
    kk = kk / jnp.maximum(jnp.sqrt(_seg_sum(kk * kk, e)), 1e-12)
    k2 = k * (1.0 + (a - 1.0) * ka_ref[...])
    r_o[0] = r
    k_o[0] = k2
    v_o[0] = v
    kk_o[0] = kk
    b_o[0] = kk * a
    bo_o[0] = _seg_sum(r * k2 * rk_ref[...], e) * v
    q = zt[:, R_PROJ:R_PROJ + A_DIM]
    qn_o[0] = q * lax.rsqrt(_seg_sum(q * q, e) * (1.0 / A_HD) + NORM_EPS) * qg_ref[...]
    kx = zt[:, R_PROJ + A_DIM:R_PROJ + A_DIM + KV_DIM]
    e_kv = e[:KV_DIM, :KV_DIM]
    kn_o[0] = kx * lax.rsqrt(_seg_sum(kx * kx, e_kv) * (1.0 / A_HD) + NORM_EPS) * kg_ref[...]


def _prep(z, shift0, lp):
    b, t, _ = z.shape
    tp = min(t, 256)
    row = lambda a: a.reshape(1, -1)
    vec = lambda n: pl.BlockSpec((1, n), lambda bi, i: (0, 0))
    full = lambda s: pl.BlockSpec(s, lambda bi, i: (0, 0))
    wide = pl.BlockSpec((1, tp, R_DIM), lambda bi, i: (bi, i, 0))
    outs = [jax.ShapeDtypeStruct((b, t, R_DIM), F32)] * 9 + [jax.ShapeDtypeStruct((b, t, KV_DIM), F32)]
    return pl.pallas_call(
        _prep_kernel,
        grid=(b, t // tp),
        in_specs=[pl.BlockSpec((1, tp, IN_COLS), lambda bi, i: (bi, i, 0)),
                  pl.BlockSpec((1, 8, IN_COLS), lambda bi, i: (bi, jnp.maximum(i * (tp // 8) - 1, 0), 0)),
                  pl.BlockSpec((1, 1, R_PROJ), lambda bi, i: (bi, 0, 0)),
                  vec(R_PROJ), vec(R_DIM), vec(R_DIM), vec(R_DIM), vec(R_DIM), vec(R_DIM),
                  full((LANES, R_DIM)), full((LANES, R_DIM)), full((G_LORA, R_DIM)),
                  full((R_DIM, R_DIM)), vec(A_DIM), vec(KV_DIM)],
        out_specs=[wide] * 9 + [pl.BlockSpec((1, tp, KV_DIM), lambda bi, i: (bi, i, 0))],
        out_shape=outs,
        compiler_params=_params(("parallel", "parallel")),
        name="rwkv_prep",
    )(z, z, shift0.reshape(b, 1, R_PROJ), row(lp["mu"]), row(lp["w0"]), row(lp["a0"]), row(lp["k_k"]),
      row(lp["k_a"]), row(lp["r_k"]), lp["w2p"], lp["a2p"], lp["g2"], lp["e64"], row(lp["q_gain"]),
      row(lp["k_gain"]))


def _stack_heads(x, lo_mask):
    return jnp.concatenate([jnp.where(lo_mask, x, 0.0), jnp.where(lo_mask, 0.0, x)], axis=0)


def _chunk_kernel(r_ref, k_ref, v_ref, kk_ref, b_ref, lw_ref, s0_ref, y_ref, sf_ref, s_ref):
    c = pl.program_id(1)
    L = CHUNK

    @pl.when(c == 0)
    def _():
        s_ref[...] = s0_ref[0]

    lane = lax.broadcasted_iota(I32, (1, LANES), 1)
    lo_mask = lane < R_HD
    rr = lax.broadcasted_iota(I32, (L, 2 * L), 0)
    cc = lax.broadcasted_iota(I32, (L, 2 * L), 1)
    cc = jnp.where(cc >= L, cc - L, cc)
    strict = rr > cc
    incl = rr >= cc
    t_r = lax.broadcasted_iota(I32, (L, L), 0)
    t_c = lax.broadcasted_iota(I32, (L, L), 1)
    tri = (t_r >= t_c).astype(BF16)
    col2 = lax.broadcasted_iota(I32, (L, 2 * L), 1) < L
    eye_r = lax.broadcasted_iota(I32, (2 * L, 2 * L), 0)
    eye_c = lax.broadcasted_iota(I32, (2 * L, 2 * L), 1)
    eye = (eye_r == eye_c).astype(F32)

    for p in range(N_PAIRS):
        sl = slice(p * LANES, (p + 1) * LANES)
        r = r_ref[0][:, sl]
        k = k_ref[0][:, sl]
        v = v_ref[0][:, sl]
        kk = kk_ref[0][:, sl]
        bb = b_ref[0][:, sl]
        lw = lw_ref[0][:, sl]
        s0 = s_ref[p]

        cl = _exact_lhs(_mm, tri, lw)
        cl_last = cl[L - 1:L, :]
        e_neg = jnp.exp(-cl)
        e_last = jnp.exp(cl_last - cl)
        rt = r * jnp.exp(cl)
        at = kk * jnp.exp(cl - lw)
        kt = k * e_neg
        bt = bb * e_neg

        lhs = jnp.concatenate([at, rt], axis=0)
        rhs = jnp.concatenate([_stack_heads(kt, lo_mask), _stack_heads(bt, lo_mask)], axis=0)
        gm = _x3(_mm_nt, lhs, rhs)
        mk = jnp.where(strict, gm[:L, :2 * L], 0.0)
        mb = jnp.where(strict, gm[:L, 2 * L:], 0.0)
        hk = jnp.where(incl, gm[L:, :2 * L], 0.0)
        hb = jnp.where(incl, gm[L:, 2 * L:], 0.0)

        nil = -jnp.concatenate([jnp.where(col2, mb, 0.0), jnp.where(col2, 0.0, mb)], axis=0)
        tinv = eye + nil
        for _ in range(5):
            nil = _x3(_mm, nil, nil)
            tinv = tinv + _x3(_mm, nil, tinv)

        vs = _stack_heads(v, lo_mask)
        rhs_u = -(_x3(_mm_nt, at, s0) + _x3(_mm, mk, vs))
        us = _x3(_mm, tinv, _stack_heads(rhs_u, lo_mask))
        y = _x3(_mm_nt, rt, s0) + _x3(_mm, hk, vs) + _x3(_mm, hb, us)
        y_ref[0, :, sl] = y
        s_ref[p] = (s0 * jnp.exp(cl_last)
                    + _x3(_mm_tn, vs, _stack_heads(k * e_last, lo_mask))
                    + _x3(_mm_tn, us, _stack_heads(bb * e_last, lo_mask)))

    @pl.when(c == pl.num_programs(1) - 1)
    def _():
        sf_ref[0] = s_ref[...]


def _chunk_scan(r, k, v, kk, bb, lw, s0_bd):
    b, t, _ = r.shape
    wide = pl.BlockSpec((1, CHUNK, R_DIM), lambda bi, c: (bi, c, 0))
    st = pl.BlockSpec((1, N_PAIRS, LANES, LANES), lambda bi, c: (bi, 0, 0, 0))
    return pl.pallas_call(
        _chunk_kernel,
        grid=(b, t // CHUNK),
        in_specs=[wide] * 6 + [st],
        out_specs=[wide, st],
        out_shape=[jax.ShapeDtypeStruct((b, t, R_DIM), F32),
                   jax.ShapeDtypeStruct((b, N_PAIRS, LANES, LANES), F32)],
        scratch_shapes=[pltpu.VMEM((N_PAIRS, LANES, LANES), F32)],
        compiler_params=_params(("parallel", "arbitrary")),
        name="rwkv_chunk",
    )(r, k, v, kk, bb, lw, s0_bd)


def _swa_kernel(*refs, n_seg, banded):
    q_ref = refs[0]
    k_refs = refs[1:1 + n_seg]
    v_refs = refs[1 + n_seg:1 + 2 * n_seg]
    bias_ref, sink_ref, o_ref = refs[1 + 2 * n_seg:]
    c = pl.program_id(1)
    q = q_ref[0]
    kcat = jnp.concatenate([kr[0] for kr in k_refs], axis=0)
    vcat = jnp.concatenate([vr[0] for vr in v_refs], axis=0)
    n_k = kcat.shape[0]
    lane = lax.broadcasted_iota(I32, (1, LANES), 1)
    lo_mask = lane < A_HD
    k_sw = pltpu.roll(kcat, A_HD, axis=1)
    v_sw = pltpu.roll(vcat, A_HD, axis=1)
    k_dup = [jnp.where(lo_mask, kcat, k_sw).astype(BF16), jnp.where(lo_mask, k_sw, kcat).astype(BF16)]
    v_dup = [jnp.where(lo_mask, vcat, v_sw).astype(BF16), jnp.where(lo_mask, v_sw, vcat).astype(BF16)]
    if banded:
        key_chunk = c - WIN_CHUNKS + lax.broadcasted_iota(I32, (1, n_k), 1) // CHUNK
        valid = key_chunk >= 0
    for pair in range(A_HEADS // 2):
        qp = q[:, pair * LANES:(pair + 1) * LANES]
        outs = []
        for w in range(2):
            hq = 2 * pair + w
            kvh = hq // A_GROUP
            qm = jnp.where(lo_mask if w == 0 else jnp.logical_not(lo_mask), qp, 0.0).astype(BF16)
            s = _mm_nt(qm, k_dup[kvh]) * (A_HD ** -0.5) + bias_ref[hq]
            if banded:
                s = jnp.where(valid, s, NEG_INF)
            sink = sink_ref[0:1, hq:hq + 1]
            m = jnp.maximum(jnp.max(s, axis=-1, keepdims=True), sink)
            pr = jnp.exp(s - m)
            den = jnp.sum(pr, axis=-1, keepdims=True) + jnp.exp(sink - m)
            outs.append(_mm((pr / den).astype(BF16), v_dup[kvh]))
        o_ref[0, :, pair * LANES:(pair + 1) * LANES] = jnp.where(lo_mask, outs[0], outs[1])


def _swa_prompt(qn, kn, z, bias, sink):
    b, t, _ = qn.shape
    v_col = (R_PROJ + A_DIM + KV_DIM) // KV_DIM
    seg = lambda s, col: pl.BlockSpec(
        (1, CHUNK, KV_DIM), lambda bi, c: (bi, jnp.maximum(c - WIN_CHUNKS + s, 0), col))
    n_seg = WIN_CHUNKS + 1
    return pl.pallas_call(
        functools.partial(_swa_kernel, n_seg=n_seg, banded=True),
        grid=(b, t // CHUNK),
        in_specs=[pl.BlockSpec((1, CHUNK, A_DIM), lambda bi, c: (bi, c, 0))]
                 + [seg(s, 0) for s in range(n_seg)] + [seg(s, v_col) for s in range(n_seg)]
                 + [pl.BlockSpec(bias.shape, lambda bi, c: (0, 0, 0)),
                    pl.BlockSpec((1, A_HEADS), lambda bi, c: (0, 0))],
        out_specs=pl.BlockSpec((1, CHUNK, A_DIM), lambda bi, c: (bi, c, 0)),
        out_shape=jax.ShapeDtypeStruct((b, t, A_DIM), F32),
        compiler_params=_params(("parallel", "parallel")),
        name="swa_prompt",
    )(qn, *([kn] * n_seg), *([z] * n_seg), bias, sink.reshape(1, A_HEADS))


def _swa_sample(qn, kn, v_new, k_cache, v_cache, bias, sink):
    b, t, _ = qn.shape
    n_cache = k_cache.shape[1]
    cur = lambda w: pl.BlockSpec((1, t, w), lambda bi, c: (bi, 0, 0))
    old = pl.BlockSpec((1, n_cache, KV_DIM), lambda bi, c: (bi, 0, 0))
    return pl.pallas_call(
        functools.partial(_swa_kernel, n_seg=2, banded=False),
        grid=(b, 1),
        in_specs=[cur(A_DIM), old, cur(KV_DIM), old, cur(KV_DIM),
                  pl.BlockSpec(bias.shape, lambda bi, c: (0, 0, 0)),
                  pl.BlockSpec((1, A_HEADS), lambda bi, c: (0, 0))],
        out_specs=cur(A_DIM),
        out_shape=jax.ShapeDtypeStruct((b, t, A_DIM), F32),
        compiler_params=_params(("parallel", "parallel")),
        name="swa_sample",
    )(qn, k_cache, kn, v_cache, v_new, bias, sink.reshape(1, A_HEADS))


def _mix_out_kernel(y_ref, bo_ref, g_ref, a_ref, x_ref, gng_ref, gnb_ref, e_ref, wo_ref, ln_ref, wq_ref,
                    xo_ref, h_ref, qq_ref):
    e = e_ref[...]
    y = y_ref[...]
    mu = _seg_sum(y, e) * (1.0 / R_HD)
    d = y - mu
    var = _seg_sum(d * d, e) * (1.0 / R_HD)
    yn = d * lax.rsqrt(var + GN_EPS) * gng_ref[...] + gnb_ref[...]
    mix_r = ((yn + bo_ref[...]) * g_ref[...]).astype(BF16)
    x = (x_ref[...] + _mm(mix_r, wo_ref[:R_DIM, :]) + _mm(a_ref[...].astype(BF16), wo_ref[R_DIM:, :]))
    xo_ref[...] = x
    ms = jnp.mean(x * x, axis=-1, keepdims=True)
    h = x * lax.rsqrt(ms + NORM_EPS) * ln_ref[...]
    h_ref[...] = h
    qq_ref[...] = _mm(h.astype(BF16), wq_ref[...])


def _mix_out(y, bonus, g, a_out, x, lp):
    n = x.shape[0]
    tm = min(n, 256)
    nq = PEER_HEADS * PK_DIM
    rows = lambda w: pl.BlockSpec((tm, w), lambda i: (i, 0))
    full = lambda s: pl.BlockSpec(s, lambda i: (0, 0))
    return pl.pallas_call(
        _mix_out_kernel,
        grid=(n // tm,),
        in_specs=[rows(R_DIM), rows(R_DIM), rows(R_DIM), rows(A_DIM), rows(D_MODEL),
                  full((1, R_DIM)), full((1, R_DIM)), full((R_DIM, R_DIM)),
                  full((D_MODEL, D_MODEL)), full((1, D_MODEL)), full((D_MODEL, nq))],
        out_specs=[rows(D_MODEL), rows(D_MODEL), rows(nq)],
        out_shape=[jax.ShapeDtypeStruct((n, D_MODEL), F32), jax.ShapeDtypeStruct((n, D_MODEL), F32),
                   jax.ShapeDtypeStruct((n, nq), F32)],
        compiler_params=_params(("parallel",)),
        name="mix_out_query",
    )(y, bonus, g, a_out, x, lp["gn_g"].reshape(1, -1), lp["gn_b"].reshape(1, -1), lp["e64"],
      lp["w_out"], lp["ln2_g"].reshape(1, -1), lp["w_pq"])


def _pick_rounds(s_ref, n_rows, emit):
    tb = s_ref.shape[1]
    rowid = lax.broadcasted_iota(I32, (n_rows, tb), 0)
    for rnd in range(PEER_TOPK):
        s = s_ref[...]
        m = jnp.max(s, axis=0, keepdims=True)
        idx = jnp.min(jnp.where(s == m, rowid, n_rows), axis=0, keepdims=True)
        hit = rowid == idx
        s_ref[...] = jnp.where(hit, -jnp.inf, s)
        emit(rnd, m, idx, hit)


def _topk_kernel(qq_ref, keys_ref, ei_ref, gate_ref, s_ref, sv_ref, si_ref, c_ref, ci_ref, ts_ref):
    tb = qq_ref.shape[0]
    for half in range(2):
        qh = qq_ref[:, half * PK_HALF:(half + 1) * PK_HALF]
        s_ref[...] = _x3(_mm_nt, keys_ref[half], qh)

        def emit1(rnd, m, idx, hit, half=half):
            sv_ref[half, rnd:rnd + 1, :] = m
            si_ref[half, rnd:rnd + 1, :] = idx

        _pick_rounds(s_ref, N_KEYS, emit1)

    row0 = 0
    for a in range(PEER_TOPK):
        nb = PEER_TOPK if a == 0 else 8
        c_ref[row0:row0 + nb, :] = sv_ref[0, a:a + 1, :] + sv_ref[1, 0:nb, :]
        ci_ref[row0:row0 + nb, :] = si_ref[0, a:a + 1, :] * N_KEYS + si_ref[1, 0:nb, :]
        row0 += nb

    def emit2(rnd, m, idx, hit):
        ts_ref[rnd:rnd + 1, :] = m
        ei_ref[0, rnd:rnd + 1, :] = jnp.max(jnp.where(hit, ci_ref[...], -1), axis=0, keepdims=True)

    _pick_rounds(c_ref, N_CAND, emit2)
    ts = ts_ref[...]
    ex = jnp.exp(ts - ts[0:1, :])
    gate_ref[0] = ex / jnp.sum(ex, axis=0, keepdims=True)


def _topk(qq, sub_keys):
    n = qq.shape[0]
    tb = min(n, 512)
    out = pl.BlockSpec((1, PEER_TOPK, tb), lambda i, h: (h, 0, i))
    return pl.pallas_call(
        _topk_kernel,
        grid=(n // tb, PEER_HEADS),
        in_specs=[pl.BlockSpec((tb, PK_DIM), lambda i, h: (i, h)),
                  pl.BlockSpec((2, N_KEYS, PK_HALF), lambda i, h: (0, 0, 0))],
        out_specs=[out, out],
        out_shape=[jax.ShapeDtypeStruct((PEER_HEADS, PEER_TOPK, n), I32),
                   jax.ShapeDtypeStruct((PEER_HEADS, PEER_TOPK, n), F32)],
        scratch_shapes=[pltpu.VMEM((N_KEYS, tb), F32), pltpu.VMEM((2, PEER_TOPK, tb), F32),
                        pltpu.VMEM((2, PEER_TOPK, tb), I32), pltpu.VMEM((N_CAND, tb), F32),
                        pltpu.VMEM((N_CAND, tb), I32), pltpu.VMEM((PEER_TOPK, tb), F32)],
        compiler_params=_params(("parallel", "parallel")),
        name="peer_topk",
    )(qq, sub_keys)


def _expert_gather(tbl_hbm, idx_v, rows_v, sem, tt, g, buf):
    return pltpu.make_async_copy(tbl_hbm.at[idx_v[pl.ds(tt * N_SEL + g * SC_GROUP, SC_GROUP)]],
                                 rows_v.at[buf], sem.at[buf])


def _sc_token_blocks(n, tbl_hbm, idx_v, rows_v, sem, load_block, compute, store_block):
    wid = lax.axis_index("s") * SC_CORES + lax.axis_index("c")
    npw = n // SC_WORKERS
    steps = SC_TOK_BLK * SC_N_GROUPS

    @pl.loop(0, npw // SC_TOK_BLK)
    def _(blk):
        tok0 = wid * npw + blk * SC_TOK_BLK
        load_block(tok0)
        _expert_gather(tbl_hbm, idx_v, rows_v, sem, 0, 0, 0).start()

        @pl.loop(0, steps, step=2)
        def _(s0):
            for b in range(2):
                s = s0 + b

                @pl.when(s + 1 < steps)
                def _():
                    _expert_gather(tbl_hbm, idx_v, rows_v, sem, (s + 1) // SC_N_GROUPS, (s + 1) % SC_N_GROUPS,
                                   1 - b).start()

                _expert_gather(tbl_hbm, idx_v, rows_v, sem, s // SC_N_GROUPS, s % SC_N_GROUPS, b).wait()
                compute(s // SC_N_GROUPS, s % SC_N_GROUPS, rows_v.at[b])

        store_block(tok0)


def _row_chunk(j):
    per_line = LANES // SC_LANES
    return j // per_line, pl.ds((j % per_line) * SC_LANES, SC_LANES)


def _sc_mesh():
    return plsc.VectorSubcoreMesh(core_axis_name="c", subcore_axis_name="s",
                                  num_cores=SC_CORES, num_subcores=SC_SUBCORES)


def _sc_hidden(eidx, h, table):
    n = h.shape[0]

    @functools.partial(
        pl.kernel, out_type=jax.ShapeDtypeStruct((n * N_SEL,), F32), mesh=_sc_mesh(),
        compiler_params=pltpu.CompilerParams(needs_layout_passes=False),
        scratch_types=[pltpu.VMEM((SC_TOK_BLK * N_SEL,), I32), pltpu.VMEM((SC_TOK_BLK * D_MODEL,), F32),
                       pltpu.VMEM((2, SC_GROUP) + SC_ROW_TILE, F32), pltpu.VMEM((SC_TOK_BLK * N_SEL,), F32),
                       pltpu.SemaphoreType.DMA((2,))],
        name="peer_hidden_sc")
    def run(eidx_hbm, h_hbm, tbl_hbm, hid_hbm, idx_v, h_v, rows_v, hid_v, sem):
        lane = lax.iota(I32, SC_LANES)

        def load_block(tok0):
            pltpu.sync_copy(eidx_hbm.at[pl.ds(tok0 * N_SEL, SC_TOK_BLK * N_SEL)], idx_v)
            pltpu.sync_copy(h_hbm.at[pl.ds(tok0 * D_MODEL, SC_TOK_BLK * D_MODEL)], h_v)

        def compute(tt, g, rows):
            accs = []
            for sub in range(SC_GROUP // SC_RSUB):
                zero = tuple(jnp.zeros((SC_LANES,), F32) for _ in range(SC_RSUB))

                @plsc.parallel_loop(0, D_MODEL // SC_LANES, unroll=4, carry=zero)
                def part(j, acc):
                    hvj = h_v[pl.ds(tt * D_MODEL + j * SC_LANES, SC_LANES)]
                    return tuple(acc[r] + rows[(sub * SC_RSUB + r, *_row_chunk(j))] * hvj
                                 for r in range(SC_RSUB))

                accs.extend(part)
            tot = jnp.zeros((SC_LANES,), F32)
            for r in range(SC_GROUP):
                tot = jnp.where(lane == r, jnp.sum(accs[r]), tot)
            hid_v[pl.ds(tt * N_SEL + g * SC_GROUP, SC_GROUP)] = tot

        def store_block(tok0):
            pltpu.sync_copy(hid_v, hid_hbm.at[pl.ds(tok0 * N_SEL, SC_TOK_BLK * N_SEL)])

        _sc_token_blocks(n, tbl_hbm, idx_v, rows_v, sem, load_block, compute, store_block)

    return run(eidx.reshape(-1), h.reshape(-1), table).reshape(n, N_SEL)


def _sc_combine(eidx, coef, x, table):
    n = x.shape[0]

    @functools.partial(
        pl.kernel, out_type=jax.ShapeDtypeStruct((n * D_MODEL,), F32), mesh=_sc_mesh(),
        compiler_params=pltpu.CompilerParams(needs_layout_passes=False),
        scratch_types=[pltpu.VMEM((SC_TOK_BLK * N_SEL,), I32), pltpu.VMEM((SC_TOK_BLK * N_SEL,), F32),
                       pltpu.VMEM((2, SC_GROUP) + SC_ROW_TILE, F32), pltpu.VMEM((SC_TOK_BLK * D_MODEL,), F32),
                       pltpu.SemaphoreType.DMA((2,))],
        name="peer_combine_sc")
    def run(eidx_hbm, c_hbm, x_hbm, tbl_hbm, out_hbm, idx_v, c_v, rows_v, out_v, sem):
        lane = lax.iota(I32, SC_LANES)

        def load_block(tok0):
            pltpu.sync_copy(eidx_hbm.at[pl.ds(tok0 * N_SEL, SC_TOK_BLK * N_SEL)], idx_v)
            pltpu.sync_copy(c_hbm.at[pl.ds(tok0 * N_SEL, SC_TOK_BLK * N_SEL)], c_v)
            pltpu.sync_copy(x_hbm.at[pl.ds(tok0 * D_MODEL, SC_TOK_BLK * D_MODEL)], out_v)

        def compute(tt, g, rows):
            cvec = c_v[pl.ds(tt * N_SEL + g * SC_GROUP, SC_GROUP)]
            coefs = [jnp.sum(jnp.where(lane == r, cvec, 0.0)) for r in range(SC_GROUP)]

            @plsc.parallel_loop(0, D_MODEL // SC_LANES, unroll=2)
            def _(j):
                sl = pl.ds(tt * D_MODEL + j * SC_LANES, SC_LANES)
                acc = out_v[sl]
                for r in range(SC_GROUP):
                    acc = acc + rows[(r, *_row_chunk(j))] * coefs[r]
                out_v[sl] = acc

        def store_block(tok0):
            pltpu.sync_copy(out_v, out_hbm.at[pl.ds(tok0 * D_MODEL, SC_TOK_BLK * D_MODEL)])

        _sc_token_blocks(n, tbl_hbm, idx_v, rows_v, sem, load_block, compute, store_block)

    return run(eidx.reshape(-1), coef.reshape(-1), x.reshape(-1), table).reshape(n, D_MODEL)


def _gate_act_kernel(hid_ref, gate_ref, o_ref):
    hid = hid_ref[...]
    o_ref[...] = gate_ref[...] * (0.5 * hid * (1.0 + lax.erf(hid * np.float32(np.sqrt(0.5)))))


def _gate_act(hid, gate):
    n = hid.shape[0]
    tm = min(n, 2048)
    rows = pl.BlockSpec((tm, N_SEL), lambda i: (i, 0))
    return pl.pallas_call(
        _gate_act_kernel, grid=(n // tm,), in_specs=[rows, rows], out_specs=rows,
        out_shape=jax.ShapeDtypeStruct((n, N_SEL), F32), compiler_params=_params(("parallel",)),
        name="peer_gate_act",
    )(hid, gate)


def _t5_bucket(rel):
    nb = NUM_BUCKETS // 2
    max_exact = nb // 2
    ret = jnp.where(rel > 0, nb, 0)
    n = jnp.abs(rel)
    nf = jnp.maximum(n, 1).astype(F32)
    large = max_exact + (jnp.log(nf / max_exact) / math.log(MAX_DISTANCE / max_exact)
                         * (nb - max_exact)).astype(I32)
    large = jnp.minimum(large, nb - 1)
    return ret + jnp.where(n < max_exact, n, large)


def _rel_bias(rel_bias, n_q, n_k, n_before):
    rel = (jnp.arange(n_k)[None, :] - n_before) - jnp.arange(n_q)[:, None]
    return jnp.transpose(rel_bias[_t5_bucket(rel)].astype(F32), (2, 0, 1))


def _state_to_pairs(wkv):
    b = wkv.shape[0]
    s = wkv.reshape(b, N_PAIRS, 2, R_HD, R_HD)
    z = jnp.zeros_like(s[:, :, 0])
    top = jnp.concatenate([s[:, :, 0], z], axis=-1)
    bot = jnp.concatenate([z, s[:, :, 1]], axis=-1)
    return jnp.concatenate([top, bot], axis=-2)


def _pairs_to_state(s_bd):
    b = s_bd.shape[0]
    return jnp.stack([s_bd[:, :, :R_HD, :R_HD], s_bd[:, :, R_HD:, R_HD:]], axis=2).reshape(
        b, R_HEADS, R_HD, R_HD)


def _layer(x, shift0, wkv0, kv_cache, bias, lp):
    b, t, _ = x.shape
    n = b * t
    z = _norm_matmul(x.reshape(n, D_MODEL), lp["ln1_g"].reshape(1, -1), lp["w_in"], 512).reshape(b, t, IN_COLS)
    r, k2, v, kk, bb, lw, g, bonus, qn, kn = _prep(z, shift0, lp)
    v_new = z[:, :, R_PROJ + A_DIM + KV_DIM:]
    if t % CHUNK:
        pad = lambda a: jnp.pad(a, ((0, 0), (0, CHUNK - t % CHUNK), (0, 0)))
        y, s_fin = _chunk_scan(*(pad(a) for a in (r, k2, v, kk, bb, lw)), _state_to_pairs(wkv0))
        y = y[:, :t]
    else:
        y, s_fin = _chunk_scan(r, k2, v, kk, bb, lw, _state_to_pairs(wkv0))
    if kv_cache is None:
        a_out = _swa_prompt(qn, kn, z, bias, lp["sink"])
    else:
        a_out = _swa_sample(qn, kn, v_new, kv_cache[0].reshape(b, -1, KV_DIM),
                            kv_cache[1].reshape(b, -1, KV_DIM), bias, lp["sink"])
    flat = lambda a: a.reshape(n, a.shape[-1])
    x1, h, qq = _mix_out(flat(y), flat(bonus), flat(g), flat(a_out), flat(x), lp)
    eidx, gate = _topk(qq, lp["sub_keys"])
    sel = lambda a: jnp.transpose(a, (2, 0, 1)).reshape(n, N_SEL)
    eidx = sel(eidx)
    coef = _gate_act(_sc_hidden(eidx, h, lp["peer_u"]), sel(gate))
    x2 = _sc_combine(eidx, coef, x1, lp["peer_v"]).reshape(b, t, D_MODEL)
    return (x2, _pairs_to_state(s_fin), z[:, -1, :R_PROJ],
            kn.reshape(b, t, A_KV, A_HD), v_new.reshape(b, t, A_KV, A_HD))


def kernel(x_prompt, x_sample, state_rwkv_wkv, state_rwkv_shift, cache_swa_k, cache_swa_v, ln1_g, w_in,
           mu_shift, w0, w2, a0, a2, g2, k_k, k_a, r_k, gn_g, gn_b, q_norm_g, k_norm_g, attn_sink,
           rel_bias, w_out, ln2_g, w_pq, sub_keys, peer_u, peer_v):
    depth = w_in.shape[0]
    b_p, s_p = x_prompt.shape[:2]
    b_s, t_s = x_sample.shape[:2]
    n_cache = cache_swa_k.shape[2]
    n_keep = min(WINDOW, s_p)
    bias_p = _rel_bias(rel_bias, CHUNK, (WIN_CHUNKS + 1) * CHUNK, WIN_CHUNKS * CHUNK)
    bias_s = _rel_bias(rel_bias, t_s, n_cache + t_s, n_cache)
    head_id = jnp.arange(R_DIM) // R_HD
    e64 = (head_id[:, None] == head_id[None, :]).astype(BF16)
    zpad = jnp.zeros((LANES - W_LORA, R_DIM), F32)
    n_groups = PROMPT_GROUPS if b_p % PROMPT_GROUPS == 0 else 1
    xp, xs = jnp.split(x_prompt, n_groups, axis=0), x_sample
    outs = [[] for _ in range(8)]
    for l in range(depth):
        lp = {
            "ln1_g": ln1_g[l], "w_in": w_in[l].astype(BF16), "mu": mu_shift[l], "w0": w0[l], "a0": a0[l],
            "k_k": k_k[l], "k_a": k_a[l], "r_k": r_k[l].reshape(-1), "gn_g": gn_g[l], "gn_b": gn_b[l],
            "w2p": jnp.concatenate([w2[l], zpad], axis=0).astype(BF16),
            "a2p": jnp.concatenate([zpad, a2[l]], axis=0).astype(BF16),
            "g2": g2[l].astype(BF16), "e64": e64,
            "q_gain": jnp.tile(q_norm_g[l], A_HEADS), "k_gain": jnp.tile(k_norm_g[l], A_KV),
            "sink": attn_sink[l].astype(F32), "w_out": w_out[l].astype(BF16), "ln2_g": ln2_g[l],
            "w_pq": w_pq[l].astype(BF16), "sub_keys": sub_keys[l], "peer_u": peer_u[l].reshape((-1,) + SC_ROW_TILE), "peer_v": peer_v[l].reshape((-1,) + SC_ROW_TILE),
        }
        parts = [_layer(xg, jnp.zeros((xg.shape[0], R_PROJ), F32),
                        jnp.zeros((xg.shape[0], R_HEADS, R_HD, R_HD), F32), None, bias_p, lp) for xg in xp]
        xp = [pt[0] for pt in parts]
        wkv_p, sh_p, k_p, v_p = (jnp.concatenate([pt[i] for pt in parts], axis=0) for i in range(1, 5))
        xs, wkv_s, sh_s, k_s, v_s = _layer(
            xs, state_rwkv_shift[l], state_rwkv_wkv[l].astype(F32), (cache_swa_k[l], cache_swa_v[l]),
            bias_s, lp)
        for lst, val in zip(outs, (wkv_p, sh_p, k_p[:, s_p - n_keep:], v_p[:, s_p - n_keep:],
                                   wkv_s, sh_s, k_s, v_s)):
            lst.append(val)
    return (jnp.concatenate(xp, axis=0), xs) + tuple(jnp.stack(o) for o in outs)
```

```python
import functools
import math

import numpy as np
import jax
import jax.numpy as jnp
from jax import lax
from jax.experimental import pallas as pl
from jax.experimental.pallas import tpu as pltpu
from jax.experimental.pallas import tpu_sc as plsc

F32 = jnp.float32
BF16 = jnp.bfloat16
I32 = jnp.int32

D_MODEL = 1024
CHUNK = 64
R_HEADS = 8
R_HD = 64
R_DIM = R_HEADS * R_HD
W_LORA = 64
A_LORA = 64
G_LORA = 128
R_PROJ = 3 * R_DIM + W_LORA + A_LORA + G_LORA
A_HEADS = 8
A_KV = 2
A_GROUP = A_HEADS // A_KV
A_HD = 64
A_DIM = A_HEADS * A_HD
KV_DIM = A_KV * A_HD
IN_COLS = R_PROJ + A_DIM + 2 * KV_DIM
WINDOW = 128
WIN_CHUNKS = WINDOW // CHUNK
NUM_BUCKETS = 32
MAX_DISTANCE = 128
PEER_HEADS = 8
N_KEYS = 128
PK_DIM = 256
PK_HALF = PK_DIM // 2
PEER_TOPK = 16
N_SEL = PEER_HEADS * PEER_TOPK
NORM_EPS = 1e-6
GN_EPS = 64e-5
NEG_INF = -1e30

LANES = 128
N_PAIRS = R_DIM // LANES
VMEM_LIMIT = 48 * 1024 * 1024
N_CAND = PEER_TOPK + 8 * (PEER_TOPK - 1)
SC_CORES = 2
SC_SUBCORES = 16
SC_LANES = 16
SC_WORKERS = SC_CORES * SC_SUBCORES
SC_TOK_BLK = 8
SC_GROUP = SC_LANES
SC_N_GROUPS = N_SEL // SC_GROUP
SC_RSUB = 8
SC_ROW_TILE = (D_MODEL // LANES, LANES)
PROMPT_GROUPS = 8


def _params(sem):
    return pltpu.CompilerParams(dimension_semantics=sem, vmem_limit_bytes=VMEM_LIMIT)


def _mm(a, b):
    return jnp.dot(a, b, preferred_element_type=F32)


def _mm_nt(a, b):
    return lax.dot_general(a, b, (((1,), (1,)), ((), ())), preferred_element_type=F32)


def _mm_tn(a, b):
    return lax.dot_general(a, b, (((0,), (0,)), ((), ())), preferred_element_type=F32)


def _split2(a):
    hi = a.astype(BF16)
    return hi, (a - hi.astype(F32)).astype(BF16)


def _split3(a):
    hi = a.astype(BF16)
    r = a - hi.astype(F32)
    mid = r.astype(BF16)
    return hi, mid, (r - mid.astype(F32)).astype(BF16)


def _x3(mm, a, b):
    ah, al = _split2(a)
    bh, bl = _split2(b)
    return mm(ah, bh) + mm(ah, bl) + mm(al, bh)


def _exact_lhs(mm, a_bf16, b):
    b0, b1, b2 = _split3(b)
    return mm(a_bf16, b0) + mm(a_bf16, b1) + mm(a_bf16, b2)


def _seg_sum(x, e_bf16):
    x0, x1, x2 = _split3(x)
    return _mm(x0, e_bf16) + _mm(x1, e_bf16) + _mm(x2, e_bf16)


def _sigmoid(x):
    return 1.0 / (1.0 + jnp.exp(-x))


def _norm_matmul_kernel(x_ref, g_ref, w_ref, o_ref, h_ref):
    @pl.when(pl.program_id(1) == 0)
    def _():
        x = x_ref[...]
        ms = jnp.mean(x * x, axis=-1, keepdims=True)
        h_ref[...] = (x * lax.rsqrt(ms + NORM_EPS) * g_ref[...]).astype(BF16)

    o_ref[...] = _mm(h_ref[...], w_ref[...])


def _norm_matmul(x, g, w_bf16, tn):
    n, k = x.shape
    m = w_bf16.shape[1]
    tm = min(n, 512)
    return pl.pallas_call(
        _norm_matmul_kernel,
        grid=(n // tm, m // tn),
        in_specs=[pl.BlockSpec((tm, k), lambda i, j: (i, 0)),
                  pl.BlockSpec((1, k), lambda i, j: (0, 0)),
                  pl.BlockSpec((k, tn), lambda i, j: (0, j))],
        out_specs=pl.BlockSpec((tm, tn), lambda i, j: (i, j)),
        out_shape=jax.ShapeDtypeStruct((n, m), F32),
        scratch_shapes=[pltpu.VMEM((tm, k), BF16)],
        compiler_params=_params(("parallel", "arbitrary")),
        name="norm_inproj",
    )(x, g, w_bf16)


def _prep_kernel(z_ref, zp_ref, sh_ref, mu_ref, w0_ref, a0_ref, kk_ref, ka_ref, rk_ref,
                 w2_ref, a2_ref, g2_ref, e_ref, qg_ref, kg_ref,
                 r_o, k_o, v_o, kk_o, b_o, lw_o, g_o, bo_o, qn_o, kn_o):
    i = pl.program_id(1)
    zt = z_ref[0]
    tp = zt.shape[0]
    zr = zt[:, :R_PROJ]
    prev_row = jnp.where(i == 0, sh_ref[0], zp_ref[0][7:8, :R_PROJ])
    row = lax.broadcasted_iota(I32, (tp, 1), 0)
    prev = jnp.where(row == 0, prev_row, pltpu.roll(zr, 1, axis=0))
    zs = zr + (prev - zr) * mu_ref[...]
    r = zs[:, 0:R_DIM]
    k = zs[:, R_DIM:2 * R_DIM]
    v = zs[:, 2 * R_DIM:3 * R_DIM]
    lo = zs[:, 3 * R_DIM:3 * R_DIM + W_LORA + A_LORA]
    g_lo = zs[:, 3 * R_DIM + W_LORA + A_LORA:R_PROJ]
    e = e_ref[...]
    w_in = -(w0_ref[...] + _mm(jnp.tanh(lo).astype(BF16), w2_ref[...]))
    softplus = jnp.maximum(w_in, 0.0) + jnp.log1p(jnp.exp(-jnp.abs(w_in)))
    w_log = -softplus - 0.5
    lw_o[0] = -jnp.exp(w_log)
    a = _sigmoid(a0_ref[...] + _mm(lo.astype(BF16), a2_ref[...]))
    g_o[0] = _mm(_sigmoid(g_lo).astype(BF16), g2_ref[...])
    kk = k * kk_ref[...]
    kk = kk / jnp.maximum(jnp.sqrt(_seg_sum(kk * kk, e)), 1e-12)
    k2 = k * (1.0 + (a - 1.0) * ka_ref[...])
    r_o[0] = r
    k_o[0] = k2
    v_o[0] = v
    kk_o[0] = kk
    b_o[0] = kk * a
    bo_o[0] = _seg_sum(r * k2 * rk_ref[...], e) * v
    q = zt[:, R_PROJ:R_PROJ + A_DIM]
    qn_o[0] = q * lax.rsqrt(_seg_sum(q * q, e) * (1.0 / A_HD) + NORM_EPS) * qg_ref[...]
    kx = zt[:, R_PROJ + A_DIM:R_PROJ + A_DIM + KV_DIM]
    e_kv = e[:KV_DIM, :KV_DIM]
    kn_o[0] = kx * lax.rsqrt(_seg_sum(kx * kx, e_kv) * (1.0 / A_HD) + NORM_EPS) * kg_ref[...]


def _prep(z, shift0, lp):
    b, t, _ = z.shape
    tp = min(t, 256)
    row = lambda a: a.reshape(1, -1)
    vec = lambda n: pl.BlockSpec((1, n), lambda bi, i: (0, 0))
    full = lambda s: pl.BlockSpec(s, lambda bi, i: (0, 0))
    wide = pl.BlockSpec((1, tp, R_DIM), lambda bi, i: (bi, i, 0))
    outs = [jax.ShapeDtypeStruct((b, t, R_DIM), F32)] * 9 + [jax.ShapeDtypeStruct((b, t, KV_DIM), F32)]
    return pl.pallas_call(
        _prep_kernel,
        grid=(b, t // tp),
        in_specs=[pl.BlockSpec((1, tp, IN_COLS), lambda bi, i: (bi, i, 0)),
                  pl.BlockSpec((1, 8, IN_COLS), lambda bi, i: (bi, jnp.maximum(i * (tp // 8) - 1, 0), 0)),
                  pl.BlockSpec((1, 1, R_PROJ), lambda bi, i: (bi, 0, 0)),
                  vec(R_PROJ), vec(R_DIM), vec(R_DIM), vec(R_DIM), vec(R_DIM), vec(R_DIM),
                  full((LANES, R_DIM)), full((LANES, R_DIM)), full((G_LORA, R_DIM)),
                  full((R_DIM, R_DIM)), vec(A_DIM), vec(KV_DIM)],
        out_specs=[wide] * 9 + [pl.BlockSpec((1, tp, KV_DIM), lambda bi, i: (bi, i, 0))],
        out_shape=outs,
        compiler_params=_params(("parallel", "parallel")),
        name="rwkv_prep",
    )(z, z, shift0.reshape(b, 1, R_PROJ), row(lp["mu"]), row(lp["w0"]), row(lp["a0"]), row(lp["k_k"]),
      row(lp["k_a"]), row(lp["r_k"]), lp["w2p"], lp["a2p"], lp["g2"], lp["e64"], row(lp["q_gain"]),
      row(lp["k_gain"]))


def _stack_heads(x, lo_mask):
    return jnp.concatenate([jnp.where(lo_mask, x, 0.0), jnp.where(lo_mask, 0.0, x)], axis=0)


def _chunk_kernel(r_ref, k_ref, v_ref, kk_ref, b_ref, lw_ref, s0_ref, y_ref, sf_ref, s_ref):
    c = pl.program_id(1)
    L = CHUNK

    @pl.when(c == 0)
    def _():
        s_ref[...] = s0_ref[0]

    lane = lax.broadcasted_iota(I32, (1, LANES), 1)
    lo_mask = lane < R_HD
    rr = lax.broadcasted_iota(I32, (L, 2 * L), 0)
    cc = lax.broadcasted_iota(I32, (L, 2 * L), 1)
    cc = jnp.where(cc >= L, cc - L, cc)
    strict = rr > cc
    incl = rr >= cc
    t_r = lax.broadcasted_iota(I32, (L, L), 0)
    t_c = lax.broadcasted_iota(I32, (L, L), 1)
    tri = (t_r >= t_c).astype(BF16)
    col2 = lax.broadcasted_iota(I32, (L, 2 * L), 1) < L
    eye_r = lax.broadcasted_iota(I32, (2 * L, 2 * L), 0)
    eye_c = lax.broadcasted_iota(I32, (2 * L, 2 * L), 1)
    eye = (eye_r == eye_c).astype(F32)

    for p in range(N_PAIRS):
        sl = slice(p * LANES, (p + 1) * LANES)
        r = r_ref[0][:, sl]
        k = k_ref[0][:, sl]
        v = v_ref[0][:, sl]
        kk = kk_ref[0][:, sl]
        bb = b_ref[0][:, sl]
        lw = lw_ref[0][:, sl]
        s0 = s_ref[p]

        cl = _exact_lhs(_mm, tri, lw)
        cl_last = cl[L - 1:L, :]
        e_neg = jnp.exp(-cl)
        e_last = jnp.exp(cl_last - cl)
        rt = r * jnp.exp(cl)
        at = kk * jnp.exp(cl - lw)
        kt = k * e_neg
        bt = bb * e_neg

        lhs = jnp.concatenate([at, rt], axis=0)
        rhs = jnp.concatenate([_stack_heads(kt, lo_mask), _stack_heads(bt, lo_mask)], axis=0)
        gm = _x3(_mm_nt, lhs, rhs)
        mk = jnp.where(strict, gm[:L, :2 * L], 0.0)
        mb = jnp.where(strict, gm[:L, 2 * L:], 0.0)
        hk = jnp.where(incl, gm[L:, :2 * L], 0.0)
        hb = jnp.where(incl, gm[L:, 2 * L:], 0.0)

        nil = -jnp.concatenate([jnp.where(col2, mb, 0.0), jnp.where(col2, 0.0, mb)], axis=0)
        tinv = eye + nil
        for _ in range(5):
            nil = _x3(_mm, nil, nil)
            tinv = tinv + _x3(_mm, nil, tinv)

        vs = _stack_heads(v, lo_mask)
        rhs_u = -(_x3(_mm_nt, at, s0) + _x3(_mm, mk, vs))
        us = _x3(_mm, tinv, _stack_heads(rhs_u, lo_mask))
        y = _x3(_mm_nt, rt, s0) + _x3(_mm, hk, vs) + _x3(_mm, hb, us)
        y_ref[0, :, sl] = y
        s_ref[p] = (s0 * jnp.exp(cl_last)
                    + _x3(_mm_tn, vs, _stack_heads(k * e_last, lo_mask))
                    + _x3(_mm_tn, us, _stack_heads(bb * e_last, lo_mask)))

    @pl.when(c == pl.num_programs(1) - 1)
    def _():
        sf_ref[0] = s_ref[...]


def _chunk_scan(r, k, v, kk, bb, lw, s0_bd):
    b, t, _ = r.shape
    wide = pl.BlockSpec((1, CHUNK, R_DIM), lambda bi, c: (bi, c, 0))
    st = pl.BlockSpec((1, N_PAIRS, LANES, LANES), lambda bi, c: (bi, 0, 0, 0))
    return pl.pallas_call(
        _chunk_kernel,
        grid=(b, t // CHUNK),
        in_specs=[wide] * 6 + [st],
        out_specs=[wide, st],
        out_shape=[jax.ShapeDtypeStruct((b, t, R_DIM), F32),
                   jax.ShapeDtypeStruct((b, N_PAIRS, LANES, LANES), F32)],
        scratch_shapes=[pltpu.VMEM((N_PAIRS, LANES, LANES), F32)],
        compiler_params=_params(("parallel", "arbitrary")),
        name="rwkv_chunk",
    )(r, k, v, kk, bb, lw, s0_bd)


def _swa_kernel(*refs, n_seg, banded):
    q_ref = refs[0]
    k_refs = refs[1:1 + n_seg]
    v_refs = refs[1 + n_seg:1 + 2 * n_seg]
    bias_ref, sink_ref, o_ref = refs[1 + 2 * n_seg:]
    c = pl.program_id(1)
    q = q_ref[0]
    kcat = jnp.concatenate([kr[0] for kr in k_refs], axis=0)
    vcat = jnp.concatenate([vr[0] for vr in v_refs], axis=0)
    n_k = kcat.shape[0]
    lane = lax.broadcasted_iota(I32, (1, LANES), 1)
    lo_mask = lane < A_HD
    k_sw = pltpu.roll(kcat, A_HD, axis=1)
    v_sw = pltpu.roll(vcat, A_HD, axis=1)
    k_dup = [jnp.where(lo_mask, kcat, k_sw).astype(BF16), jnp.where(lo_mask, k_sw, kcat).astype(BF16)]
    v_dup = [jnp.where(lo_mask, vcat, v_sw).astype(BF16), jnp.where(lo_mask, v_sw, vcat).astype(BF16)]
    if banded:
        key_chunk = c - WIN_CHUNKS + lax.broadcasted_iota(I32, (1, n_k), 1) // CHUNK
        valid = key_chunk >= 0
    for pair in range(A_HEADS // 2):
        qp = q[:, pair * LANES:(pair + 1) * LANES]
        outs = []
        for w in range(2):
            hq = 2 * pair + w
            kvh = hq // A_GROUP
            qm = jnp.where(lo_mask if w == 0 else jnp.logical_not(lo_mask), qp, 0.0).astype(BF16)
            s = _mm_nt(qm, k_dup[kvh]) * (A_HD ** -0.5) + bias_ref[hq]
            if banded:
                s = jnp.where(valid, s, NEG_INF)
            sink = sink_ref[0:1, hq:hq + 1]
            m = jnp.maximum(jnp.max(s, axis=-1, keepdims=True), sink)
            pr = jnp.exp(s - m)
            den = jnp.sum(pr, axis=-1, keepdims=True) + jnp.exp(sink - m)
            outs.append(_mm((pr / den).astype(BF16), v_dup[kvh]))
        o_ref[0, :, pair * LANES:(pair + 1) * LANES] = jnp.where(lo_mask, outs[0], outs[1])


def _swa_prompt(qn, kn, z, bias, sink):
    b, t, _ = qn.shape
    v_col = (R_PROJ + A_DIM + KV_DIM) // KV_DIM
    seg = lambda s, col: pl.BlockSpec(
        (1, CHUNK, KV_DIM), lambda bi, c: (bi, jnp.maximum(c - WIN_CHUNKS + s, 0), col))
    n_seg = WIN_CHUNKS + 1
    return pl.pallas_call(
        functools.partial(_swa_kernel, n_seg=n_seg, banded=True),
        grid=(b, t // CHUNK),
        in_specs=[pl.BlockSpec((1, CHUNK, A_DIM), lambda bi, c: (bi, c, 0))]
                 + [seg(s, 0) for s in range(n_seg)] + [seg(s, v_col) for s in range(n_seg)]
                 + [pl.BlockSpec(bias.shape, lambda bi, c: (0, 0, 0)),
                    pl.BlockSpec((1, A_HEADS), lambda bi, c: (0, 0))],
        out_specs=pl.BlockSpec((1, CHUNK, A_DIM), lambda bi, c: (bi, c, 0)),
        out_shape=jax.ShapeDtypeStruct((b, t, A_DIM), F32),
        compiler_params=_params(("parallel", "parallel")),
        name="swa_prompt",
    )(qn, *([kn] * n_seg), *([z] * n_seg), bias, sink.reshape(1, A_HEADS))


def _swa_sample(qn, kn, v_new, k_cache, v_cache, bias, sink):
    b, t, _ = qn.shape
    n_cache = k_cache.shape[1]
    cur = lambda w: pl.BlockSpec((1, t, w), lambda bi, c: (bi, 0, 0))
    old = pl.BlockSpec((1, n_cache, KV_DIM), lambda bi, c: (bi, 0, 0))
    return pl.pallas_call(
        functools.partial(_swa_kernel, n_seg=2, banded=False),
        grid=(b, 1),
        in_specs=[cur(A_DIM), old, cur(KV_DIM), old, cur(KV_DIM),
                  pl.BlockSpec(bias.shape, lambda bi, c: (0, 0, 0)),
                  pl.BlockSpec((1, A_HEADS), lambda bi, c: (0, 0))],
        out_specs=cur(A_DIM),
        out_shape=jax.ShapeDtypeStruct((b, t, A_DIM), F32),
        compiler_params=_params(("parallel", "parallel")),
        name="swa_sample",
    )(qn, k_cache, kn, v_cache, v_new, bias, sink.reshape(1, A_HEADS))


def _mix_out_kernel(y_ref, bo_ref, g_ref, a_ref, x_ref, gng_ref, gnb_ref, e_ref, wo_ref, ln_ref, wq_ref,
                    xo_ref, h_ref, qq_ref):
    e = e_ref[...]
    y = y_ref[...]
    mu = _seg_sum(y, e) * (1.0 / R_HD)
    d = y - mu
    var = _seg_sum(d * d, e) * (1.0 / R_HD)
    yn = d * lax.rsqrt(var + GN_EPS) * gng_ref[...] + gnb_ref[...]
    mix_r = ((yn + bo_ref[...]) * g_ref[...]).astype(BF16)
    x = (x_ref[...] + _mm(mix_r, wo_ref[:R_DIM, :]) + _mm(a_ref[...].astype(BF16), wo_ref[R_DIM:, :]))
    xo_ref[...] = x
    ms = jnp.mean(x * x, axis=-1, keepdims=True)
    h = x * lax.rsqrt(ms + NORM_EPS) * ln_ref[...]
    h_ref[...] = h
    qq_ref[...] = _mm(h.astype(BF16), wq_ref[...])


def _mix_out(y, bonus, g, a_out, x, lp):
    n = x.shape[0]
    tm = min(n, 256)
    nq = PEER_HEADS * PK_DIM
    rows = lambda w: pl.BlockSpec((tm, w), lambda i: (i, 0))
    full = lambda s: pl.BlockSpec(s, lambda i: (0, 0))
    return pl.pallas_call(
        _mix_out_kernel,
        grid=(n // tm,),
        in_specs=[rows(R_DIM), rows(R_DIM), rows(R_DIM), rows(A_DIM), rows(D_MODEL),
                  full((1, R_DIM)), full((1, R_DIM)), full((R_DIM, R_DIM)),
                  full((D_MODEL, D_MODEL)), full((1, D_MODEL)), full((D_MODEL, nq))],
        out_specs=[rows(D_MODEL), rows(D_MODEL), rows(nq)],
        out_shape=[jax.ShapeDtypeStruct((n, D_MODEL), F32), jax.ShapeDtypeStruct((n, D_MODEL), F32),
                   jax.ShapeDtypeStruct((n, nq), F32)],
        compiler_params=_params(("parallel",)),
        name="mix_out_query",
    )(y, bonus, g, a_out, x, lp["gn_g"].reshape(1, -1), lp["gn_b"].reshape(1, -1), lp["e64"],
      lp["w_out"], lp["ln2_g"].reshape(1, -1), lp["w_pq"])


def _pick_rounds(s_ref, n_rows, emit):
    tb = s_ref.shape[1]
    rowid = lax.broadcasted_iota(I32, (n_rows, tb), 0)
    for rnd in range(PEER_TOPK):
        s = s_ref[...]
        m = jnp.max(s, axis=0, keepdims=True)
        idx = jnp.min(jnp.where(s == m, rowid, n_rows), axis=0, keepdims=True)
        hit = rowid == idx
        s_ref[...] = jnp.where(hit, -jnp.inf, s)
        emit(rnd, m, idx, hit)


def _topk_kernel(qq_ref, keys_ref, ei_ref, gate_ref, s_ref, sv_ref, si_ref, c_ref, ci_ref, ts_ref):
    tb = qq_ref.shape[0]
    for half in range(2):
        qh = qq_ref[:, half * PK_HALF:(half + 1) * PK_HALF]
        s_ref[...] = _x3(_mm_nt, keys_ref[half], qh)

        def emit1(rnd, m, idx, hit, half=half):
            sv_ref[half, rnd:rnd + 1, :] = m
            si_ref[half, rnd:rnd + 1, :] = idx

        _pick_rounds(s_ref, N_KEYS, emit1)

    row0 = 0
    for a in range(PEER_TOPK):
        nb = PEER_TOPK if a == 0 else 8
        c_ref[row0:row0 + nb, :] = sv_ref[0, a:a + 1, :] + sv_ref[1, 0:nb, :]
        ci_ref[row0:row0 + nb, :] = si_ref[0, a:a + 1, :] * N_KEYS + si_ref[1, 0:nb, :]
        row0 += nb

    def emit2(rnd, m, idx, hit):
        ts_ref[rnd:rnd + 1, :] = m
        ei_ref[0, rnd:rnd + 1, :] = jnp.max(jnp.where(hit, ci_ref[...], -1), axis=0, keepdims=True)

    _pick_rounds(c_ref, N_CAND, emit2)
    ts = ts_ref[...]
    ex = jnp.exp(ts - ts[0:1, :])
    gate_ref[0] = ex / jnp.sum(ex, axis=0, keepdims=True)


def _topk(qq, sub_keys):
    n = qq.shape[0]
    tb = min(n, 512)
    out = pl.BlockSpec((1, PEER_TOPK, tb), lambda i, h: (h, 0, i))
    return pl.pallas_call(
        _topk_kernel,
        grid=(n // tb, PEER_HEADS),
        in_specs=[pl.BlockSpec((tb, PK_DIM), lambda i, h: (i, h)),
                  pl.BlockSpec((2, N_KEYS, PK_HALF), lambda i, h: (0, 0, 0))],
        out_specs=[out, out],
        out_shape=[jax.ShapeDtypeStruct((PEER_HEADS, PEER_TOPK, n), I32),
                   jax.ShapeDtypeStruct((PEER_HEADS, PEER_TOPK, n), F32)],
        scratch_shapes=[pltpu.VMEM((N_KEYS, tb), F32), pltpu.VMEM((2, PEER_TOPK, tb), F32),
                        pltpu.VMEM((2, PEER_TOPK, tb), I32), pltpu.VMEM((N_CAND, tb), F32),
                        pltpu.VMEM((N_CAND, tb), I32), pltpu.VMEM((PEER_TOPK, tb), F32)],
        compiler_params=_params(("parallel", "parallel")),
        name="peer_topk",
    )(qq, sub_keys)


def _expert_gather(tbl_hbm, idx_v, rows_v, sem, tt, g, buf):
    return pltpu.make_async_copy(tbl_hbm.at[idx_v[pl.ds(tt * N_SEL + g * SC_GROUP, SC_GROUP)]],
                                 rows_v.at[buf], sem.at[buf])


def _sc_token_blocks(n, tbl_hbm, idx_v, rows_v, sem, load_block, compute, store_block):
    wid = lax.axis_index("s") * SC_CORES + lax.axis_index("c")
    npw = n // SC_WORKERS
    steps = SC_TOK_BLK * SC_N_GROUPS

    @pl.loop(0, npw // SC_TOK_BLK)
    def _(blk):
        tok0 = wid * npw + blk * SC_TOK_BLK
        load_block(tok0)
        _expert_gather(tbl_hbm, idx_v, rows_v, sem, 0, 0, 0).start()

        @pl.loop(0, steps, step=2)
        def _(s0):
            for b in range(2):
                s = s0 + b

                @pl.when(s + 1 < steps)
                def _():
                    _expert_gather(tbl_hbm, idx_v, rows_v, sem, (s + 1) // SC_N_GROUPS, (s + 1) % SC_N_GROUPS,
                                   1 - b).start()

                _expert_gather(tbl_hbm, idx_v, rows_v, sem, s // SC_N_GROUPS, s % SC_N_GROUPS, b).wait()
                compute(s // SC_N_GROUPS, s % SC_N_GROUPS, rows_v.at[b])

        store_block(tok0)


def _row_chunk(j):
    per_line = LANES // SC_LANES
    return j // per_line, pl.ds((j % per_line) * SC_LANES, SC_LANES)


def _expert_cost(n):
    elems = n * N_SEL * D_MODEL
    return pl.CostEstimate(flops=2 * elems, transcendentals=0, bytes_accessed=4 * elems)


def _sc_mesh():
    return plsc.VectorSubcoreMesh(core_axis_name="c", subcore_axis_name="s",
                                  num_cores=SC_CORES, num_subcores=SC_SUBCORES)


def _sc_hidden(eidx, h, table):
    n = h.shape[0]

    @functools.partial(
        pl.kernel, out_type=jax.ShapeDtypeStruct((n * N_SEL,), F32), mesh=_sc_mesh(),
        compiler_params=pltpu.CompilerParams(needs_layout_passes=False),
        scratch_types=[pltpu.VMEM((SC_TOK_BLK * N_SEL,), I32), pltpu.VMEM((SC_TOK_BLK * D_MODEL,), F32),
                       pltpu.VMEM((2, SC_GROUP) + SC_ROW_TILE, F32), pltpu.VMEM((SC_TOK_BLK * N_SEL,), F32),
                       pltpu.SemaphoreType.DMA((2,))],
        cost_estimate=_expert_cost(n), name="peer_hidden_sc")
    def run(eidx_hbm, h_hbm, tbl_hbm, hid_hbm, idx_v, h_v, rows_v, hid_v, sem):
        lane = lax.iota(I32, SC_LANES)

        def load_block(tok0):
            pltpu.sync_copy(eidx_hbm.at[pl.ds(tok0 * N_SEL, SC_TOK_BLK * N_SEL)], idx_v)
            pltpu.sync_copy(h_hbm.at[pl.ds(tok0 * D_MODEL, SC_TOK_BLK * D_MODEL)], h_v)

        def compute(tt, g, rows):
            accs = []
            for sub in range(SC_GROUP // SC_RSUB):
                zero = tuple(jnp.zeros((SC_LANES,), F32) for _ in range(SC_RSUB))

                @plsc.parallel_loop(0, D_MODEL // SC_LANES, unroll=4, carry=zero)
                def part(j, acc):
                    hvj = h_v[pl.ds(tt * D_MODEL + j * SC_LANES, SC_LANES)]
                    return tuple(acc[r] + rows[(sub * SC_RSUB + r, *_row_chunk(j))] * hvj
                                 for r in range(SC_RSUB))

                accs.extend(part)
            tot = jnp.zeros((SC_LANES,), F32)
            for r in range(SC_GROUP):
                tot = jnp.where(lane == r, jnp.sum(accs[r]), tot)
            hid_v[pl.ds(tt * N_SEL + g * SC_GROUP, SC_GROUP)] = tot

        def store_block(tok0):
            pltpu.sync_copy(hid_v, hid_hbm.at[pl.ds(tok0 * N_SEL, SC_TOK_BLK * N_SEL)])

        _sc_token_blocks(n, tbl_hbm, idx_v, rows_v, sem, load_block, compute, store_block)

    return run(eidx.reshape(-1), h.reshape(-1), table).reshape(n, N_SEL)


def _sc_combine(eidx, coef, x, table):
    n = x.shape[0]

    @functools.partial(
        pl.kernel, out_type=jax.ShapeDtypeStruct((n * D_MODEL,), F32), mesh=_sc_mesh(),
        compiler_params=pltpu.CompilerParams(needs_layout_passes=False),
        scratch_types=[pltpu.VMEM((SC_TOK_BLK * N_SEL,), I32), pltpu.VMEM((SC_TOK_BLK * N_SEL,), F32),
                       pltpu.VMEM((2, SC_GROUP) + SC_ROW_TILE, F32), pltpu.VMEM((SC_TOK_BLK * D_MODEL,), F32),
                       pltpu.SemaphoreType.DMA((2,))],
        cost_estimate=_expert_cost(n), name="peer_combine_sc")
    def run(eidx_hbm, c_hbm, x_hbm, tbl_hbm, out_hbm, idx_v, c_v, rows_v, out_v, sem):
        lane = lax.iota(I32, SC_LANES)

        def load_block(tok0):
            pltpu.sync_copy(eidx_hbm.at[pl.ds(tok0 * N_SEL, SC_TOK_BLK * N_SEL)], idx_v)
            pltpu.sync_copy(c_hbm.at[pl.ds(tok0 * N_SEL, SC_TOK_BLK * N_SEL)], c_v)
            pltpu.sync_copy(x_hbm.at[pl.ds(tok0 * D_MODEL, SC_TOK_BLK * D_MODEL)], out_v)

        def compute(tt, g, rows):
            cvec = c_v[pl.ds(tt * N_SEL + g * SC_GROUP, SC_GROUP)]
            coefs = [jnp.sum(jnp.where(lane == r, cvec, 0.0)) for r in range(SC_GROUP)]

            @plsc.parallel_loop(0, D_MODEL // SC_LANES, unroll=2)
            def _(j):
                sl = pl.ds(tt * D_MODEL + j * SC_LANES, SC_LANES)
                acc = out_v[sl]
                for r in range(SC_GROUP):
                    acc = acc + rows[(r, *_row_chunk(j))] * coefs[r]
                out_v[sl] = acc

        def store_block(tok0):
            pltpu.sync_copy(out_v, out_hbm.at[pl.ds(tok0 * D_MODEL, SC_TOK_BLK * D_MODEL)])

        _sc_token_blocks(n, tbl_hbm, idx_v, rows_v, sem, load_block, compute, store_block)

    return run(eidx.reshape(-1), coef.reshape(-1), x.reshape(-1), table).reshape(n, D_MODEL)


def _gate_act_kernel(hid_ref, gate_ref, o_ref):
    hid = hid_ref[...]
    o_ref[...] = gate_ref[...] * (0.5 * hid * (1.0 + lax.erf(hid * np.float32(np.sqrt(0.5)))))


def _gate_act(hid, gate):
    n = hid.shape[0]
    tm = min(n, 2048)
    rows = pl.BlockSpec((tm, N_SEL), lambda i: (i, 0))
    return pl.pallas_call(
        _gate_act_kernel, grid=(n // tm,), in_specs=[rows, rows], out_specs=rows,
        out_shape=jax.ShapeDtypeStruct((n, N_SEL), F32), compiler_params=_params(("parallel",)),
        name="peer_gate_act",
    )(hid, gate)


def _t5_bucket(rel):
    nb = NUM_BUCKETS // 2
    max_exact = nb // 2
    ret = jnp.where(rel > 0, nb, 0)
    n = jnp.abs(rel)
    nf = jnp.maximum(n, 1).astype(F32)
    large = max_exact + (jnp.log(nf / max_exact) / math.log(MAX_DISTANCE / max_exact)
                         * (nb - max_exact)).astype(I32)
    large = jnp.minimum(large, nb - 1)
    return ret + jnp.where(n < max_exact, n, large)


def _rel_bias(rel_bias, n_q, n_k, n_before):
    rel = (jnp.arange(n_k)[None, :] - n_before) - jnp.arange(n_q)[:, None]
    return jnp.transpose(rel_bias[_t5_bucket(rel)].astype(F32), (2, 0, 1))


def _state_to_pairs(wkv):
    b = wkv.shape[0]
    s = wkv.reshape(b, N_PAIRS, 2, R_HD, R_HD)
    z = jnp.zeros_like(s[:, :, 0])
    top = jnp.concatenate([s[:, :, 0], z], axis=-1)
    bot = jnp.concatenate([z, s[:, :, 1]], axis=-1)
    return jnp.concatenate([top, bot], axis=-2)


def _pairs_to_state(s_bd):
    b = s_bd.shape[0]
    return jnp.stack([s_bd[:, :, :R_HD, :R_HD], s_bd[:, :, R_HD:, R_HD:]], axis=2).reshape(
        b, R_HEADS, R_HD, R_HD)


def _layer_dense(x, shift0, wkv0, kv_cache, bias, lp):
    b, t, _ = x.shape
    n = b * t
    z = _norm_matmul(x.reshape(n, D_MODEL), lp["ln1_g"].reshape(1, -1), lp["w_in"], 512).reshape(b, t, IN_COLS)
    r, k2, v, kk, bb, lw, g, bonus, qn, kn = _prep(z, shift0, lp)
    v_new = z[:, :, R_PROJ + A_DIM + KV_DIM:]
    if t % CHUNK:
        pad = lambda a: jnp.pad(a, ((0, 0), (0, CHUNK - t % CHUNK), (0, 0)))
        y, s_fin = _chunk_scan(*(pad(a) for a in (r, k2, v, kk, bb, lw)), _state_to_pairs(wkv0))
        y = y[:, :t]
    else:
        y, s_fin = _chunk_scan(r, k2, v, kk, bb, lw, _state_to_pairs(wkv0))
    if kv_cache is None:
        a_out = _swa_prompt(qn, kn, z, bias, lp["sink"])
    else:
        a_out = _swa_sample(qn, kn, v_new, kv_cache[0].reshape(b, -1, KV_DIM),
                            kv_cache[1].reshape(b, -1, KV_DIM), bias, lp["sink"])
    flat = lambda a: a.reshape(n, a.shape[-1])
    x1, h, qq = _mix_out(flat(y), flat(bonus), flat(g), flat(a_out), flat(x), lp)
    eidx, gate = _topk(qq, lp["sub_keys"])
    sel = lambda a: jnp.transpose(a, (2, 0, 1)).reshape(n, N_SEL)
    return ((sel(eidx), sel(gate), h, x1),
            (_pairs_to_state(s_fin), z[:, -1, :R_PROJ],
             kn.reshape(b, t, A_KV, A_HD), v_new.reshape(b, t, A_KV, A_HD)))


def _layer_experts(ops, shape, lp, after=None):
    eidx, gate, h, x1 = ops
    if after is not None:
        h, after = lax.optimization_barrier((h, after))
    coef = _gate_act(_sc_hidden(eidx, h, lp["peer_u"]), gate)
    return _sc_combine(eidx, coef, x1, lp["peer_v"]).reshape(shape), after


def kernel(x_prompt, x_sample, state_rwkv_wkv, state_rwkv_shift, cache_swa_k, cache_swa_v, ln1_g, w_in,
           mu_shift, w0, w2, a0, a2, g2, k_k, k_a, r_k, gn_g, gn_b, q_norm_g, k_norm_g, attn_sink,
           rel_bias, w_out, ln2_g, w_pq, sub_keys, peer_u, peer_v):
    depth = w_in.shape[0]
    b_p, s_p = x_prompt.shape[:2]
    b_s, t_s = x_sample.shape[:2]
    n_cache = cache_swa_k.shape[2]
    n_keep = min(WINDOW, s_p)
    bias_p = _rel_bias(rel_bias, CHUNK, (WIN_CHUNKS + 1) * CHUNK, WIN_CHUNKS * CHUNK)
    bias_s = _rel_bias(rel_bias, t_s, n_cache + t_s, n_cache)
    head_id = jnp.arange(R_DIM) // R_HD
    e64 = (head_id[:, None] == head_id[None, :]).astype(BF16)
    zpad = jnp.zeros((LANES - W_LORA, R_DIM), F32)
    n_groups = PROMPT_GROUPS if b_p % PROMPT_GROUPS == 0 else 1
    xp, xs = jnp.split(x_prompt, n_groups, axis=0), x_sample
    outs = [[] for _ in range(8)]
    for l in range(depth):
        lp = {
            "ln1_g": ln1_g[l], "w_in": w_in[l].astype(BF16), "mu": mu_shift[l], "w0": w0[l], "a0": a0[l],
            "k_k": k_k[l], "k_a": k_a[l], "r_k": r_k[l].reshape(-1), "gn_g": gn_g[l], "gn_b": gn_b[l],
            "w2p": jnp.concatenate([w2[l], zpad], axis=0).astype(BF16),
            "a2p": jnp.concatenate([zpad, a2[l]], axis=0).astype(BF16),
            "g2": g2[l].astype(BF16), "e64": e64,
            "q_gain": jnp.tile(q_norm_g[l], A_HEADS), "k_gain": jnp.tile(k_norm_g[l], A_KV),
            "sink": attn_sink[l].astype(F32), "w_out": w_out[l].astype(BF16), "ln2_g": ln2_g[l],
            "w_pq": w_pq[l].astype(BF16), "sub_keys": sub_keys[l], "peer_u": peer_u[l].reshape((-1,) + SC_ROW_TILE), "peer_v": peer_v[l].reshape((-1,) + SC_ROW_TILE),
        }
        parts = [_layer_dense(xg, jnp.zeros((xg.shape[0], R_PROJ), F32),
                              jnp.zeros((xg.shape[0], R_HEADS, R_HD, R_HD), F32), None, bias_p, lp) for xg in xp]
        wkv_p, sh_p, k_p, v_p = (jnp.concatenate([pt[1][i] for pt in parts], axis=0) for i in range(4))
        new_xp, prev = [], None
        for xg, (ops, _) in zip(xp, parts):
            if prev is None:
                xo, _ = _layer_experts(ops, xg.shape, lp)
            else:
                xo, new_xp[-1] = _layer_experts(ops, xg.shape, lp, after=prev)
            new_xp.append(xo)
            prev = xo
        xp = new_xp
        ops_s, (wkv_s, sh_s, k_s, v_s) = _layer_dense(
            xs, state_rwkv_shift[l], state_rwkv_wkv[l].astype(F32), (cache_swa_k[l], cache_swa_v[l]),
            bias_s, lp)
        xs, _ = _layer_experts(ops_s, xs.shape, lp)
        for lst, val in zip(outs, (wkv_p, sh_p, k_p[:, s_p - n_keep:], v_p[:, s_p - n_keep:],
                                   wkv_s, sh_s, k_s, v_s)):
            lst.append(val)
    return (jnp.concatenate(xp, axis=0), xs) + tuple(jnp.stack(o) for o in outs)
```

```python
import functools
import math

import numpy as np
import jax
import jax.numpy as jnp
from jax import lax
from jax.experimental import pallas as pl
from jax.experimental.pallas import tpu as pltpu
from jax.experimental.pallas import tpu_sc as plsc

F32 = jnp.float32
BF16 = jnp.bfloat16
I32 = jnp.int32

D_MODEL = 1024
CHUNK = 64
R_HEADS = 8
R_HD = 64
R_DIM = R_HEADS * R_HD
W_LORA = 64
A_LORA = 64
G_LORA = 128
R_PROJ = 3 * R_DIM + W_LORA + A_LORA + G_LORA
A_HEADS = 8
A_KV = 2
A_GROUP = A_HEADS // A_KV
A_HD = 64
A_DIM = A_HEADS * A_HD
KV_DIM = A_KV * A_HD
IN_COLS = R_PROJ + A_DIM + 2 * KV_DIM
WINDOW = 128
WIN_CHUNKS = WINDOW // CHUNK
NUM_BUCKETS = 32
MAX_DISTANCE = 128
PEER_HEADS = 8
N_KEYS = 128
PK_DIM = 256
PK_HALF = PK_DIM // 2
PEER_TOPK = 16
N_SEL = PEER_HEADS * PEER_TOPK
NORM_EPS = 1e-6
GN_EPS = 64e-5
NEG_INF = -1e30

LANES = 128
N_PAIRS = R_DIM // LANES
VMEM_LIMIT = 48 * 1024 * 1024
N_CAND = PEER_TOPK + 8 * (PEER_TOPK - 1)
SC_CORES = 2
SC_SUBCORES = 16
SC_LANES = 16
SC_WORKERS = SC_CORES * SC_SUBCORES
SC_TOK_BLK = 8
SC_GROUP = SC_LANES
SC_N_GROUPS = N_SEL // SC_GROUP
SC_RSUB = 8
SC_NBUF = 4
SC_ROW_TILE = (D_MODEL // LANES, LANES)
PROMPT_GROUPS = 8


def _params(sem):
    return pltpu.CompilerParams(dimension_semantics=sem, vmem_limit_bytes=VMEM_LIMIT)


def _mm(a, b):
    return jnp.dot(a, b, preferred_element_type=F32)


def _mm_nt(a, b):
    return lax.dot_general(a, b, (((1,), (1,)), ((), ())), preferred_element_type=F32)


def _mm_tn(a, b):
    return lax.dot_general(a, b, (((0,), (0,)), ((), ())), preferred_element_type=F32)


def _split2(a):
    hi = a.astype(BF16)
    return hi, (a - hi.astype(F32)).astype(BF16)


def _split3(a):
    hi = a.astype(BF16)
    r = a - hi.astype(F32)
    mid = r.astype(BF16)
    return hi, mid, (r - mid.astype(F32)).astype(BF16)


def _x3(mm, a, b):
    ah, al = _split2(a)
    bh, bl = _split2(b)
    return mm(ah, bh) + mm(ah, bl) + mm(al, bh)


def _exact_lhs(mm, a_bf16, b):
    b0, b1, b2 = _split3(b)
    return mm(a_bf16, b0) + mm(a_bf16, b1) + mm(a_bf16, b2)


def _seg_sum(x, e_bf16):
    x0, x1, x2 = _split3(x)
    return _mm(x0, e_bf16) + _mm(x1, e_bf16) + _mm(x2, e_bf16)


def _sigmoid(x):
    return 1.0 / (1.0 + jnp.exp(-x))


def _norm_matmul_kernel(x_ref, g_ref, w_ref, o_ref, h_ref):
    @pl.when(pl.program_id(1) == 0)
    def _():
        x = x_ref[...]
        ms = jnp.mean(x * x, axis=-1, keepdims=True)
        h_ref[...] = (x * lax.rsqrt(ms + NORM_EPS) * g_ref[...]).astype(BF16)

    o_ref[...] = _mm(h_ref[...], w_ref[...])


def _norm_matmul(x, g, w_bf16, tn):
    n, k = x.shape
    m = w_bf16.shape[1]
    tm = min(n, 512)
    return pl.pallas_call(
        _norm_matmul_kernel,
        grid=(n // tm, m // tn),
        in_specs=[pl.BlockSpec((tm, k), lambda i, j: (i, 0)),
                  pl.BlockSpec((1, k), lambda i, j: (0, 0)),
                  pl.BlockSpec((k, tn), lambda i, j: (0, j))],
        out_specs=pl.BlockSpec((tm, tn), lambda i, j: (i, j)),
        out_shape=jax.ShapeDtypeStruct((n, m), F32),
        scratch_shapes=[pltpu.VMEM((tm, k), BF16)],
        compiler_params=_params(("parallel", "arbitrary")),
        name="norm_inproj",
    )(x, g, w_bf16)


def _prep_kernel(z_ref, zp_ref, sh_ref, mu_ref, w0_ref, a0_ref, kk_ref, ka_ref, rk_ref,
                 w2_ref, a2_ref, g2_ref, e_ref, qg_ref, kg_ref,
                 r_o, k_o, v_o, kk_o, b_o, lw_o, g_o, bo_o, qn_o, kn_o):
    i = pl.program_id(1)
    zt = z_ref[0]
    tp = zt.shape[0]
    zr = zt[:, :R_PROJ]
    prev_row = jnp.where(i == 0, sh_ref[0], zp_ref[0][7:8, :R_PROJ])
    row = lax.broadcasted_iota(I32, (tp, 1), 0)
    prev = jnp.where(row == 0, prev_row, pltpu.roll(zr, 1, axis=0))
    zs = zr + (prev - zr) * mu_ref[...]
    r = zs[:, 0:R_DIM]
    k = zs[:, R_DIM:2 * R_DIM]
    v = zs[:, 2 * R_DIM:3 * R_DIM]
    lo = zs[:, 3 * R_DIM:3 * R_DIM + W_LORA + A_LORA]
    g_lo = zs[:, 3 * R_DIM + W_LORA + A_LORA:R_PROJ]
    e = e_ref[...]
    w_in = -(w0_ref[...] + _mm(jnp.tanh(lo).astype(BF16), w2_ref[...]))
    softplus = jnp.maximum(w_in, 0.0) + jnp.log1p(jnp.exp(-jnp.abs(w_in)))
    w_log = -softplus - 0.5
    lw_o[0] = -jnp.exp(w_log)
    a = _sigmoid(a0_ref[...] + _mm(lo.astype(BF16), a2_ref[...]))
    g_o[0] = _mm(_sigmoid(g_lo).astype(BF16), g2_ref[...])
    kk = k * kk_ref[...]
    kk = kk / jnp.maximum(jnp.sqrt(_seg_sum(kk * kk, e)), 1e-12)
    k2 = k * (1.0 + (a - 1.0) * ka_ref[...])
    r_o[0] = r
    k_o[0] = k2
    v_o[0] = v
    kk_o[0] = kk
    b_o[0] = kk * a
    bo_o[0] = _seg_sum(r * k2 * rk_ref[...], e) * v
    q = zt[:, R_PROJ:R_PROJ + A_DIM]
    qn_o[0] = q * lax.rsqrt(_seg_sum(q * q, e) * (1.0 / A_HD) + NORM_EPS) * qg_ref[...]
    kx = zt[:, R_PROJ + A_DIM:R_PROJ + A_DIM + KV_DIM]
    e_kv = e[:KV_DIM, :KV_DIM]
    kn_o[0] = kx * lax.rsqrt(_seg_sum(kx * kx, e_kv) * (1.0 / A_HD) + NORM_EPS) * kg_ref[...]


def _prep(z, shift0, lp):
    b, t, _ = z.shape
    tp = min(t, 256)
    row = lambda a: a.reshape(1, -1)
    vec = lambda n: pl.BlockSpec((1, n), lambda bi, i: (0, 0))
    full = lambda s: pl.BlockSpec(s, lambda bi, i: (0, 0))
    wide = pl.BlockSpec((1, tp, R_DIM), lambda bi, i: (bi, i, 0))
    outs = [jax.ShapeDtypeStruct((b, t, R_DIM), F32)] * 9 + [jax.ShapeDtypeStruct((b, t, KV_DIM), F32)]
    return pl.pallas_call(
        _prep_kernel,
        grid=(b, t // tp),
        in_specs=[pl.BlockSpec((1, tp, IN_COLS), lambda bi, i: (bi, i, 0)),
                  pl.BlockSpec((1, 8, IN_COLS), lambda bi, i: (bi, jnp.maximum(i * (tp // 8) - 1, 0), 0)),
                  pl.BlockSpec((1, 1, R_PROJ), lambda bi, i: (bi, 0, 0)),
                  vec(R_PROJ), vec(R_DIM), vec(R_DIM), vec(R_DIM), vec(R_DIM), vec(R_DIM),
                  full((LANES, R_DIM)), full((LANES, R_DIM)), full((G_LORA, R_DIM)),
                  full((R_DIM, R_DIM)), vec(A_DIM), vec(KV_DIM)],
        out_specs=[wide] * 9 + [pl.BlockSpec((1, tp, KV_DIM), lambda bi, i: (bi, i, 0))],
        out_shape=outs,
        compiler_params=_params(("parallel", "parallel")),
        name="rwkv_prep",
    )(z, z, shift0.reshape(b, 1, R_PROJ), row(lp["mu"]), row(lp["w0"]), row(lp["a0"]), row(lp["k_k"]),
      row(lp["k_a"]), row(lp["r_k"]), lp["w2p"], lp["a2p"], lp["g2"], lp["e64"], row(lp["q_gain"]),
      row(lp["k_gain"]))


def _stack_heads(x, lo_mask):
    return jnp.concatenate([jnp.where(lo_mask, x, 0.0), jnp.where(lo_mask, 0.0, x)], axis=0)


def _chunk_kernel(r_ref, k_ref, v_ref, kk_ref, b_ref, lw_ref, s0_ref, y_ref, sf_ref, s_ref):
    c = pl.program_id(1)
    L = CHUNK

    @pl.when(c == 0)
    def _():
        s_ref[...] = s0_ref[0]

    lane = lax.broadcasted_iota(I32, (1, LANES), 1)
    lo_mask = lane < R_HD
    rr = lax.broadcasted_iota(I32, (L, 2 * L), 0)
    cc = lax.broadcasted_iota(I32, (L, 2 * L), 1)
    cc = jnp.where(cc >= L, cc - L, cc)
    strict = rr > cc
    incl = rr >= cc
    t_r = lax.broadcasted_iota(I32, (L, L), 0)
    t_c = lax.broadcasted_iota(I32, (L, L), 1)
    tri = (t_r >= t_c).astype(BF16)
    col2 = lax.broadcasted_iota(I32, (L, 2 * L), 1) < L
    eye_r = lax.broadcasted_iota(I32, (2 * L, 2 * L), 0)
    eye_c = lax.broadcasted_iota(I32, (2 * L, 2 * L), 1)
    eye = (eye_r == eye_c).astype(F32)

    for p in range(N_PAIRS):
        sl = slice(p * LANES, (p + 1) * LANES)
        r = r_ref[0][:, sl]
        k = k_ref[0][:, sl]
        v = v_ref[0][:, sl]
        kk = kk_ref[0][:, sl]
        bb = b_ref[0][:, sl]
        lw = lw_ref[0][:, sl]
        s0 = s_ref[p]

        cl = _exact_lhs(_mm, tri, lw)
        cl_last = cl[L - 1:L, :]
        e_neg = jnp.exp(-cl)
        e_last = jnp.exp(cl_last - cl)
        rt = r * jnp.exp(cl)
        at = kk * jnp.exp(cl - lw)
        kt = k * e_neg
        bt = bb * e_neg

        lhs = jnp.concatenate([at, rt], axis=0)
        rhs = jnp.concatenate([_stack_heads(kt, lo_mask), _stack_heads(bt, lo_mask)], axis=0)
        gm = _x3(_mm_nt, lhs, rhs)
        mk = jnp.where(strict, gm[:L, :2 * L], 0.0)
        mb = jnp.where(strict, gm[:L, 2 * L:], 0.0)
        hk = jnp.where(incl, gm[L:, :2 * L], 0.0)
        hb = jnp.where(incl, gm[L:, 2 * L:], 0.0)

        nil = -jnp.concatenate([jnp.where(col2, mb, 0.0), jnp.where(col2, 0.0, mb)], axis=0)
        tinv = eye + nil
        for _ in range(5):
            nil = _x3(_mm, nil, nil)
            tinv = tinv + _x3(_mm, nil, tinv)

        vs = _stack_heads(v, lo_mask)
        rhs_u = -(_x3(_mm_nt, at, s0) + _x3(_mm, mk, vs))
        us = _x3(_mm, tinv, _stack_heads(rhs_u, lo_mask))
        y = _x3(_mm_nt, rt, s0) + _x3(_mm, hk, vs) + _x3(_mm, hb, us)
        y_ref[0, :, sl] = y
        s_ref[p] = (s0 * jnp.exp(cl_last)
                    + _x3(_mm_tn, vs, _stack_heads(k * e_last, lo_mask))
                    + _x3(_mm_tn, us, _stack_heads(bb * e_last, lo_mask)))

    @pl.when(c == pl.num_programs(1) - 1)
    def _():
        sf_ref[0] = s_ref[...]


def _chunk_scan(r, k, v, kk, bb, lw, s0_bd):
    b, t, _ = r.shape
    wide = pl.BlockSpec((1, CHUNK, R_DIM), lambda bi, c: (bi, c, 0))
    st = pl.BlockSpec((1, N_PAIRS, LANES, LANES), lambda bi, c: (bi, 0, 0, 0))
    return pl.pallas_call(
        _chunk_kernel,
        grid=(b, t // CHUNK),
        in_specs=[wide] * 6 + [st],
        out_specs=[wide, st],
        out_shape=[jax.ShapeDtypeStruct((b, t, R_DIM), F32),
                   jax.ShapeDtypeStruct((b, N_PAIRS, LANES, LANES), F32)],
        scratch_shapes=[pltpu.VMEM((N_PAIRS, LANES, LANES), F32)],
        compiler_params=_params(("parallel", "arbitrary")),
        name="rwkv_chunk",
    )(r, k, v, kk, bb, lw, s0_bd)


def _swa_kernel(*refs, n_seg, banded):
    q_ref = refs[0]
    k_refs = refs[1:1 + n_seg]
    v_refs = refs[1 + n_seg:1 + 2 * n_seg]
    bias_ref, sink_ref, o_ref = refs[1 + 2 * n_seg:]
    c = pl.program_id(1)
    q = q_ref[0]
    kcat = jnp.concatenate([kr[0] for kr in k_refs], axis=0)
    vcat = jnp.concatenate([vr[0] for vr in v_refs], axis=0)
    n_k = kcat.shape[0]
    lane = lax.broadcasted_iota(I32, (1, LANES), 1)
    lo_mask = lane < A_HD
    k_sw = pltpu.roll(kcat, A_HD, axis=1)
    v_sw = pltpu.roll(vcat, A_HD, axis=1)
    k_dup = [jnp.where(lo_mask, kcat, k_sw).astype(BF16), jnp.where(lo_mask, k_sw, kcat).astype(BF16)]
    v_dup = [jnp.where(lo_mask, vcat, v_sw).astype(BF16), jnp.where(lo_mask, v_sw, vcat).astype(BF16)]
    if banded:
        key_chunk = c - WIN_CHUNKS + lax.broadcasted_iota(I32, (1, n_k), 1) // CHUNK
        valid = key_chunk >= 0
    for pair in range(A_HEADS // 2):
        qp = q[:, pair * LANES:(pair + 1) * LANES]
        outs = []
        for w in range(2):
            hq = 2 * pair + w
            kvh = hq // A_GROUP
            qm = jnp.where(lo_mask if w == 0 else jnp.logical_not(lo_mask), qp, 0.0).astype(BF16)
            s = _mm_nt(qm, k_dup[kvh]) * (A_HD ** -0.5) + bias_ref[hq]
            if banded:
                s = jnp.where(valid, s, NEG_INF)
            sink = sink_ref[0:1, hq:hq + 1]
            m = jnp.maximum(jnp.max(s, axis=-1, keepdims=True), sink)
            pr = jnp.exp(s - m)
            den = jnp.sum(pr, axis=-1, keepdims=True) + jnp.exp(sink - m)
            outs.append(_mm((pr / den).astype(BF16), v_dup[kvh]))
        o_ref[0, :, pair * LANES:(pair + 1) * LANES] = jnp.where(lo_mask, outs[0], outs[1])


def _swa_prompt(qn, kn, z, bias, sink):
    b, t, _ = qn.shape
    v_col = (R_PROJ + A_DIM + KV_DIM) // KV_DIM
    seg = lambda s, col: pl.BlockSpec(
        (1, CHUNK, KV_DIM), lambda bi, c: (bi, jnp.maximum(c - WIN_CHUNKS + s, 0), col))
    n_seg = WIN_CHUNKS + 1
    return pl.pallas_call(
        functools.partial(_swa_kernel, n_seg=n_seg, banded=True),
        grid=(b, t // CHUNK),
        in_specs=[pl.BlockSpec((1, CHUNK, A_DIM), lambda bi, c: (bi, c, 0))]
                 + [seg(s, 0) for s in range(n_seg)] + [seg(s, v_col) for s in range(n_seg)]
                 + [pl.BlockSpec(bias.shape, lambda bi, c: (0, 0, 0)),
                    pl.BlockSpec((1, A_HEADS), lambda bi, c: (0, 0))],
        out_specs=pl.BlockSpec((1, CHUNK, A_DIM), lambda bi, c: (bi, c, 0)),
        out_shape=jax.ShapeDtypeStruct((b, t, A_DIM), F32),
        compiler_params=_params(("parallel", "parallel")),
        name="swa_prompt",
    )(qn, *([kn] * n_seg), *([z] * n_seg), bias, sink.reshape(1, A_HEADS))


def _swa_sample(qn, kn, v_new, k_cache, v_cache, bias, sink):
    b, t, _ = qn.shape
    n_cache = k_cache.shape[1]
    cur = lambda w: pl.BlockSpec((1, t, w), lambda bi, c: (bi, 0, 0))
    old = pl.BlockSpec((1, n_cache, KV_DIM), lambda bi, c: (bi, 0, 0))
    return pl.pallas_call(
        functools.partial(_swa_kernel, n_seg=2, banded=False),
        grid=(b, 1),
        in_specs=[cur(A_DIM), old, cur(KV_DIM), old, cur(KV_DIM),
                  pl.BlockSpec(bias.shape, lambda bi, c: (0, 0, 0)),
                  pl.BlockSpec((1, A_HEADS), lambda bi, c: (0, 0))],
        out_specs=cur(A_DIM),
        out_shape=jax.ShapeDtypeStruct((b, t, A_DIM), F32),
        compiler_params=_params(("parallel", "parallel")),
        name="swa_sample",
    )(qn, k_cache, kn, v_cache, v_new, bias, sink.reshape(1, A_HEADS))


def _mix_out_kernel(y_ref, bo_ref, g_ref, a_ref, x_ref, gng_ref, gnb_ref, e_ref, wo_ref, ln_ref, wq_ref,
                    xo_ref, h_ref, qq_ref):
    e = e_ref[...]
    y = y_ref[...]
    mu = _seg_sum(y, e) * (1.0 / R_HD)
    d = y - mu
    var = _seg_sum(d * d, e) * (1.0 / R_HD)
    yn = d * lax.rsqrt(var + GN_EPS) * gng_ref[...] + gnb_ref[...]
    mix_r = ((yn + bo_ref[...]) * g_ref[...]).astype(BF16)
    x = (x_ref[...] + _mm(mix_r, wo_ref[:R_DIM, :]) + _mm(a_ref[...].astype(BF16), wo_ref[R_DIM:, :]))
    xo_ref[...] = x
    ms = jnp.mean(x * x, axis=-1, keepdims=True)
    h = x * lax.rsqrt(ms + NORM_EPS) * ln_ref[...]
    h_ref[...] = h
    qq_ref[...] = _mm(h.astype(BF16), wq_ref[...])


def _mix_out(y, bonus, g, a_out, x, lp):
    n = x.shape[0]
    tm = min(n, 256)
    nq = PEER_HEADS * PK_DIM
    rows = lambda w: pl.BlockSpec((tm, w), lambda i: (i, 0))
    full = lambda s: pl.BlockSpec(s, lambda i: (0, 0))
    return pl.pallas_call(
        _mix_out_kernel,
        grid=(n // tm,),
        in_specs=[rows(R_DIM), rows(R_DIM), rows(R_DIM), rows(A_DIM), rows(D_MODEL),
                  full((1, R_DIM)), full((1, R_DIM)), full((R_DIM, R_DIM)),
                  full((D_MODEL, D_MODEL)), full((1, D_MODEL)), full((D_MODEL, nq))],
        out_specs=[rows(D_MODEL), rows(D_MODEL), rows(nq)],
        out_shape=[jax.ShapeDtypeStruct((n, D_MODEL), F32), jax.ShapeDtypeStruct((n, D_MODEL), F32),
                   jax.ShapeDtypeStruct((n, nq), F32)],
        compiler_params=_params(("parallel",)),
        name="mix_out_query",
    )(y, bonus, g, a_out, x, lp["gn_g"].reshape(1, -1), lp["gn_b"].reshape(1, -1), lp["e64"],
      lp["w_out"], lp["ln2_g"].reshape(1, -1), lp["w_pq"])


def _pick_rounds(s_ref, n_rows, emit):
    tb = s_ref.shape[1]
    rowid = lax.broadcasted_iota(I32, (n_rows, tb), 0)
    for rnd in range(PEER_TOPK):
        s = s_ref[...]
        m = jnp.max(s, axis=0, keepdims=True)
        idx = jnp.min(jnp.where(s == m, rowid, n_rows), axis=0, keepdims=True)
        hit = rowid == idx
        s_ref[...] = jnp.where(hit, -jnp.inf, s)
        emit(rnd, m, idx, hit)


def _topk_kernel(qq_ref, keys_ref, ei_ref, gate_ref, s_ref, sv_ref, si_ref, c_ref, ci_ref, ts_ref):
    tb = qq_ref.shape[0]
    for half in range(2):
        qh = qq_ref[:, half * PK_HALF:(half + 1) * PK_HALF]
        s_ref[...] = _x3(_mm_nt, keys_ref[half], qh)

        def emit1(rnd, m, idx, hit, half=half):
            sv_ref[half, rnd:rnd + 1, :] = m
            si_ref[half, rnd:rnd + 1, :] = idx

        _pick_rounds(s_ref, N_KEYS, emit1)

    row0 = 0
    for a in range(PEER_TOPK):
        nb = PEER_TOPK if a == 0 else 8
        c_ref[row0:row0 + nb, :] = sv_ref[0, a:a + 1, :] + sv_ref[1, 0:nb, :]
        ci_ref[row0:row0 + nb, :] = si_ref[0, a:a + 1, :] * N_KEYS + si_ref[1, 0:nb, :]
        row0 += nb

    def emit2(rnd, m, idx, hit):
        ts_ref[rnd:rnd + 1, :] = m
        ei_ref[0, rnd:rnd + 1, :] = jnp.max(jnp.where(hit, ci_ref[...], -1), axis=0, keepdims=True)

    _pick_rounds(c_ref, N_CAND, emit2)
    ts = ts_ref[...]
    ex = jnp.exp(ts - ts[0:1, :])
    gate_ref[0] = ex / jnp.sum(ex, axis=0, keepdims=True)


def _topk(qq, sub_keys):
    n = qq.shape[0]
    tb = min(n, 512)
    out = pl.BlockSpec((1, PEER_TOPK, tb), lambda i, h: (h, 0, i))
    return pl.pallas_call(
        _topk_kernel,
        grid=(n // tb, PEER_HEADS),
        in_specs=[pl.BlockSpec((tb, PK_DIM), lambda i, h: (i, h)),
                  pl.BlockSpec((2, N_KEYS, PK_HALF), lambda i, h: (0, 0, 0))],
        out_specs=[out, out],
        out_shape=[jax.ShapeDtypeStruct((PEER_HEADS, PEER_TOPK, n), I32),
                   jax.ShapeDtypeStruct((PEER_HEADS, PEER_TOPK, n), F32)],
        scratch_shapes=[pltpu.VMEM((N_KEYS, tb), F32), pltpu.VMEM((2, PEER_TOPK, tb), F32),
                        pltpu.VMEM((2, PEER_TOPK, tb), I32), pltpu.VMEM((N_CAND, tb), F32),
                        pltpu.VMEM((N_CAND, tb), I32), pltpu.VMEM((PEER_TOPK, tb), F32)],
        compiler_params=_params(("parallel", "parallel")),
        name="peer_topk",
    )(qq, sub_keys)


def _expert_gather(tbl_hbm, idx_v, rows_v, sem, tt, g, buf):
    return pltpu.make_async_copy(tbl_hbm.at[idx_v[pl.ds(tt * N_SEL + g * SC_GROUP, SC_GROUP)]],
                                 rows_v.at[buf], sem.at[buf])


def _sc_token_blocks(n, tbl_hbm, idx_v, rows_v, sem, load_block, compute, store_block):
    wid = lax.axis_index("s") * SC_CORES + lax.axis_index("c")
    npw = n // SC_WORKERS
    steps = SC_TOK_BLK * SC_N_GROUPS

    @pl.loop(0, npw // SC_TOK_BLK)
    def _(blk):
        tok0 = wid * npw + blk * SC_TOK_BLK
        load_block(tok0)
        for s in range(SC_NBUF - 1):
            _expert_gather(tbl_hbm, idx_v, rows_v, sem, s // SC_N_GROUPS, s % SC_N_GROUPS, s).start()

        @pl.loop(0, steps, step=SC_NBUF)
        def _(s0):
            for b in range(SC_NBUF):
                s = s0 + b
                ahead = s + SC_NBUF - 1

                @pl.when(ahead < steps)
                def _():
                    _expert_gather(tbl_hbm, idx_v, rows_v, sem, ahead // SC_N_GROUPS, ahead % SC_N_GROUPS,
                                   (b + SC_NBUF - 1) % SC_NBUF).start()

                _expert_gather(tbl_hbm, idx_v, rows_v, sem, s // SC_N_GROUPS, s % SC_N_GROUPS, b).wait()
                compute(s // SC_N_GROUPS, s % SC_N_GROUPS, rows_v.at[b])

        store_block(tok0)


def _row_chunk(j):
    per_line = LANES // SC_LANES
    return j // per_line, pl.ds((j % per_line) * SC_LANES, SC_LANES)


def _expert_cost(n):
    elems = n * N_SEL * D_MODEL
    return pl.CostEstimate(flops=2 * elems, transcendentals=0, bytes_accessed=4 * elems)


def _sc_mesh():
    return plsc.VectorSubcoreMesh(core_axis_name="c", subcore_axis_name="s",
                                  num_cores=SC_CORES, num_subcores=SC_SUBCORES)


def _sc_hidden(eidx, h, table):
    n = h.shape[0]

    @functools.partial(
        pl.kernel, out_type=jax.ShapeDtypeStruct((n * N_SEL,), F32), mesh=_sc_mesh(),
        compiler_params=pltpu.CompilerParams(needs_layout_passes=False),
        scratch_types=[pltpu.VMEM((SC_TOK_BLK * N_SEL,), I32), pltpu.VMEM((SC_TOK_BLK * D_MODEL,), F32),
                       pltpu.VMEM((SC_NBUF, SC_GROUP) + SC_ROW_TILE, F32), pltpu.VMEM((SC_TOK_BLK * N_SEL,), F32),
                       pltpu.SemaphoreType.DMA((SC_NBUF,))],
        cost_estimate=_expert_cost(n), name="peer_hidden_sc")
    def run(eidx_hbm, h_hbm, tbl_hbm, hid_hbm, idx_v, h_v, rows_v, hid_v, sem):
        lane = lax.iota(I32, SC_LANES)

        def load_block(tok0):
            pltpu.sync_copy(eidx_hbm.at[pl.ds(tok0 * N_SEL, SC_TOK_BLK * N_SEL)], idx_v)
            pltpu.sync_copy(h_hbm.at[pl.ds(tok0 * D_MODEL, SC_TOK_BLK * D_MODEL)], h_v)

        def compute(tt, g, rows):
            accs = []
            for sub in range(SC_GROUP // SC_RSUB):
                zero = tuple(jnp.zeros((SC_LANES,), F32) for _ in range(SC_RSUB))

                @plsc.parallel_loop(0, D_MODEL // SC_LANES, unroll=4, carry=zero)
                def part(j, acc):
                    hvj = h_v[pl.ds(tt * D_MODEL + j * SC_LANES, SC_LANES)]
                    return tuple(acc[r] + rows[(sub * SC_RSUB + r, *_row_chunk(j))] * hvj
                                 for r in range(SC_RSUB))

                accs.extend(part)
            tot = jnp.zeros((SC_LANES,), F32)
            for r in range(SC_GROUP):
                tot = jnp.where(lane == r, jnp.sum(accs[r]), tot)
            hid_v[pl.ds(tt * N_SEL + g * SC_GROUP, SC_GROUP)] = tot

        def store_block(tok0):
            pltpu.sync_copy(hid_v, hid_hbm.at[pl.ds(tok0 * N_SEL, SC_TOK_BLK * N_SEL)])

        _sc_token_blocks(n, tbl_hbm, idx_v, rows_v, sem, load_block, compute, store_block)

    return run(eidx.reshape(-1), h.reshape(-1), table).reshape(n, N_SEL)


def _sc_combine(eidx, coef, x, table):
    n = x.shape[0]

    @functools.partial(
        pl.kernel, out_type=jax.ShapeDtypeStruct((n * D_MODEL,), F32), mesh=_sc_mesh(),
        compiler_params=pltpu.CompilerParams(needs_layout_passes=False),
        scratch_types=[pltpu.VMEM((SC_TOK_BLK * N_SEL,), I32), pltpu.VMEM((SC_TOK_BLK * N_SEL,), F32),
                       pltpu.VMEM((SC_NBUF, SC_GROUP) + SC_ROW_TILE, F32), pltpu.VMEM((SC_TOK_BLK * D_MODEL,), F32),
                       pltpu.SemaphoreType.DMA((SC_NBUF,))],
        cost_estimate=_expert_cost(n), name="peer_combine_sc")
    def run(eidx_hbm, c_hbm, x_hbm, tbl_hbm, out_hbm, idx_v, c_v, rows_v, out_v, sem):
        lane = lax.iota(I32, SC_LANES)

        def load_block(tok0):
            pltpu.sync_copy(eidx_hbm.at[pl.ds(tok0 * N_SEL, SC_TOK_BLK * N_SEL)], idx_v)
            pltpu.sync_copy(c_hbm.at[pl.ds(tok0 * N_SEL, SC_TOK_BLK * N_SEL)], c_v)
            pltpu.sync_copy(x_hbm.at[pl.ds(tok0 * D_MODEL, SC_TOK_BLK * D_MODEL)], out_v)

        def compute(tt, g, rows):
            cvec = c_v[pl.ds(tt * N_SEL + g * SC_GROUP, SC_GROUP)]
            coefs = [jnp.sum(jnp.where(lane == r, cvec, 0.0)) for r in range(SC_GROUP)]

            @plsc.parallel_loop(0, D_MODEL // SC_LANES, unroll=2)
            def _(j):
                sl = pl.ds(tt * D_MODEL + j * SC_LANES, SC_LANES)
                acc = out_v[sl]
                for r in range(SC_GROUP):
                    acc = acc + rows[(r, *_row_chunk(j))] * coefs[r]
                out_v[sl] = acc

        def store_block(tok0):
            pltpu.sync_copy(out_v, out_hbm.at[pl.ds(tok0 * D_MODEL, SC_TOK_BLK * D_MODEL)])

        _sc_token_blocks(n, tbl_hbm, idx_v, rows_v, sem, load_block, compute, store_block)

    return run(eidx.reshape(-1), coef.reshape(-1), x.reshape(-1), table).reshape(n, D_MODEL)


def _gate_act_kernel(hid_ref, gate_ref, o_ref):
    hid = hid_ref[...]
    o_ref[...] = gate_ref[...] * (0.5 * hid * (1.0 + lax.erf(hid * np.float32(np.sqrt(0.5)))))


def _gate_act(hid, gate):
    n = hid.shape[0]
    tm = min(n, 2048)
    rows = pl.BlockSpec((tm, N_SEL), lambda i: (i, 0))
    return pl.pallas_call(
        _gate_act_kernel, grid=(n // tm,), in_specs=[rows, rows], out_specs=rows,
        out_shape=jax.ShapeDtypeStruct((n, N_SEL), F32), compiler_params=_params(("parallel",)),
        name="peer_gate_act",
    )(hid, gate)


def _t5_bucket(rel):
    nb = NUM_BUCKETS // 2
    max_exact = nb // 2
    ret = jnp.where(rel > 0, nb, 0)
    n = jnp.abs(rel)
    nf = jnp.maximum(n, 1).astype(F32)
    large = max_exact + (jnp.log(nf / max_exact) / math.log(MAX_DISTANCE / max_exact)
                         * (nb - max_exact)).astype(I32)
    large = jnp.minimum(large, nb - 1)
    return ret + jnp.where(n < max_exact, n, large)


def _rel_bias(rel_bias, n_q, n_k, n_before):
    rel = (jnp.arange(n_k)[None, :] - n_before) - jnp.arange(n_q)[:, None]
    return jnp.transpose(rel_bias[_t5_bucket(rel)].astype(F32), (2, 0, 1))


def _state_to_pairs(wkv):
    b = wkv.shape[0]
    s = wkv.reshape(b, N_PAIRS, 2, R_HD, R_HD)
    z = jnp.zeros_like(s[:, :, 0])
    top = jnp.concatenate([s[:, :, 0], z], axis=-1)
    bot = jnp.concatenate([z, s[:, :, 1]], axis=-1)
    return jnp.concatenate([top, bot], axis=-2)


def _pairs_to_state(s_bd):
    b = s_bd.shape[0]
    return jnp.stack([s_bd[:, :, :R_HD, :R_HD], s_bd[:, :, R_HD:, R_HD:]], axis=2).reshape(
        b, R_HEADS, R_HD, R_HD)


def _layer_dense(x, shift0, wkv0, kv_cache, bias, lp):
    b, t, _ = x.shape
    n = b * t
    z = _norm_matmul(x.reshape(n, D_MODEL), lp["ln1_g"].reshape(1, -1), lp["w_in"], 512).reshape(b, t, IN_COLS)
    r, k2, v, kk, bb, lw, g, bonus, qn, kn = _prep(z, shift0, lp)
    v_new = z[:, :, R_PROJ + A_DIM + KV_DIM:]
    if t % CHUNK:
        pad = lambda a: jnp.pad(a, ((0, 0), (0, CHUNK - t % CHUNK), (0, 0)))
        y, s_fin = _chunk_scan(*(pad(a) for a in (r, k2, v, kk, bb, lw)), _state_to_pairs(wkv0))
        y = y[:, :t]
    else:
        y, s_fin = _chunk_scan(r, k2, v, kk, bb, lw, _state_to_pairs(wkv0))
    if kv_cache is None:
        a_out = _swa_prompt(qn, kn, z, bias, lp["sink"])
    else:
        a_out = _swa_sample(qn, kn, v_new, kv_cache[0].reshape(b, -1, KV_DIM),
                            kv_cache[1].reshape(b, -1, KV_DIM), bias, lp["sink"])
    flat = lambda a: a.reshape(n, a.shape[-1])
    x1, h, qq = _mix_out(flat(y), flat(bonus), flat(g), flat(a_out), flat(x), lp)
    eidx, gate = _topk(qq, lp["sub_keys"])
    sel = lambda a: jnp.transpose(a, (2, 0, 1)).reshape(n, N_SEL)
    return ((sel(eidx), sel(gate), h, x1),
            (_pairs_to_state(s_fin), z[:, -1, :R_PROJ],
             kn.reshape(b, t, A_KV, A_HD), v_new.reshape(b, t, A_KV, A_HD)))


def _layer_experts(ops, shape, lp, after=None):
    eidx, gate, h, x1 = ops
    if after is not None:
        h, after = lax.optimization_barrier((h, after))
    coef = _gate_act(_sc_hidden(eidx, h, lp["peer_u"]), gate)
    return _sc_combine(eidx, coef, x1, lp["peer_v"]).reshape(shape), after


def kernel(x_prompt, x_sample, state_rwkv_wkv, state_rwkv_shift, cache_swa_k, cache_swa_v, ln1_g, w_in,
           mu_shift, w0, w2, a0, a2, g2, k_k, k_a, r_k, gn_g, gn_b, q_norm_g, k_norm_g, attn_sink,
           rel_bias, w_out, ln2_g, w_pq, sub_keys, peer_u, peer_v):
    depth = w_in.shape[0]
    b_p, s_p = x_prompt.shape[:2]
    b_s, t_s = x_sample.shape[:2]
    n_cache = cache_swa_k.shape[2]
    n_keep = min(WINDOW, s_p)
    bias_p = _rel_bias(rel_bias, CHUNK, (WIN_CHUNKS + 1) * CHUNK, WIN_CHUNKS * CHUNK)
    bias_s = _rel_bias(rel_bias, t_s, n_cache + t_s, n_cache)
    head_id = jnp.arange(R_DIM) // R_HD
    e64 = (head_id[:, None] == head_id[None, :]).astype(BF16)
    zpad = jnp.zeros((LANES - W_LORA, R_DIM), F32)
    n_groups = PROMPT_GROUPS if b_p % PROMPT_GROUPS == 0 else 1
    xp, xs = jnp.split(x_prompt, n_groups, axis=0), x_sample
    outs = [[] for _ in range(8)]
    for l in range(depth):
        lp = {
            "ln1_g": ln1_g[l], "w_in": w_in[l].astype(BF16), "mu": mu_shift[l], "w0": w0[l], "a0": a0[l],
            "k_k": k_k[l], "k_a": k_a[l], "r_k": r_k[l].reshape(-1), "gn_g": gn_g[l], "gn_b": gn_b[l],
            "w2p": jnp.concatenate([w2[l], zpad], axis=0).astype(BF16),
            "a2p": jnp.concatenate([zpad, a2[l]], axis=0).astype(BF16),
            "g2": g2[l].astype(BF16), "e64": e64,
            "q_gain": jnp.tile(q_norm_g[l], A_HEADS), "k_gain": jnp.tile(k_norm_g[l], A_KV),
            "sink": attn_sink[l].astype(F32), "w_out": w_out[l].astype(BF16), "ln2_g": ln2_g[l],
            "w_pq": w_pq[l].astype(BF16), "sub_keys": sub_keys[l], "peer_u": peer_u[l].reshape((-1,) + SC_ROW_TILE), "peer_v": peer_v[l].reshape((-1,) + SC_ROW_TILE),
        }
        parts = [_layer_dense(xg, jnp.zeros((xg.shape[0], R_PROJ), F32),
                              jnp.zeros((xg.shape[0], R_HEADS, R_HD, R_HD), F32), None, bias_p, lp) for xg in xp]
        wkv_p, sh_p, k_p, v_p = (jnp.concatenate([pt[1][i] for pt in parts], axis=0) for i in range(4))
        new_xp, prev = [], None
        for xg, (ops, _) in zip(xp, parts):
            if prev is None:
                xo, _ = _layer_experts(ops, xg.shape, lp)
            else:
                xo, new_xp[-1] = _layer_experts(ops, xg.shape, lp, after=prev)
            new_xp.append(xo)
            prev = xo
        xp = new_xp
        ops_s, (wkv_s, sh_s, k_s, v_s) = _layer_dense(
            xs, state_rwkv_shift[l], state_rwkv_wkv[l].astype(F32), (cache_swa_k[l], cache_swa_v[l]),
            bias_s, lp)
        xs, _ = _layer_experts(ops_s, xs.shape, lp)
        for lst, val in zip(outs, (wkv_p, sh_p, k_p[:, s_p - n_keep:], v_p[:, s_p - n_keep:],
                                   wkv_s, sh_s, k_s, v_s)):
            lst.append(val)
    return (jnp.concatenate(xp, axis=0), xs) + tuple(jnp.stack(o) for o in outs)
```

```python
import functools
import math

import numpy as np
import jax
import jax.numpy as jnp
from jax import lax
from jax.experimental import pallas as pl
from jax.experimental.pallas import tpu as pltpu
from jax.experimental.pallas import tpu_sc as plsc

F32 = jnp.float32
BF16 = jnp.bfloat16
I32 = jnp.int32

D_MODEL = 1024
CHUNK = 64
R_HEADS = 8
R_HD = 64
R_DIM = R_HEADS * R_HD
W_LORA = 64
A_LORA = 64
G_LORA = 128
R_PROJ = 3 * R_DIM + W_LORA + A_LORA + G_LORA
A_HEADS = 8
A_KV = 2
A_GROUP = A_HEADS // A_KV
A_HD = 64
A_DIM = A_HEADS * A_HD
KV_DIM = A_KV * A_HD
IN_COLS = R_PROJ + A_DIM + 2 * KV_DIM
WINDOW = 128
WIN_CHUNKS = WINDOW // CHUNK
NUM_BUCKETS = 32
MAX_DISTANCE = 128
PEER_HEADS = 8
N_KEYS = 128
PK_DIM = 256
PK_HALF = PK_DIM // 2
PEER_TOPK = 16
N_SEL = PEER_HEADS * PEER_TOPK
NORM_EPS = 1e-6
GN_EPS = 64e-5
NEG_INF = -1e30

LANES = 128
N_PAIRS = R_DIM // LANES
VMEM_LIMIT = 48 * 1024 * 1024
CAND_COLS = (16, 8, 8, 4, 4, 4, 4, 4, 1, 1, 1, 1, 1, 1, 1, 1)
assert all(nb >= PEER_TOPK // (a + 1) for a, nb in enumerate(CAND_COLS))
N_CAND = 64
assert sum(CAND_COLS) <= N_CAND
SC_CORES = 2
SC_SUBCORES = 16
SC_LANES = 16
SC_WORKERS = SC_CORES * SC_SUBCORES
SC_TOK_BLK = 8
SC_GROUP = SC_LANES
SC_N_GROUPS = N_SEL // SC_GROUP
SC_RSUB = 16
SC_NBUF = 4
SC_ROW_TILE = (D_MODEL // LANES, LANES)
PROMPT_GROUPS = 8


def _params(sem):
    return pltpu.CompilerParams(dimension_semantics=sem, vmem_limit_bytes=VMEM_LIMIT)


def _mm(a, b):
    return jnp.dot(a, b, preferred_element_type=F32)


def _mm_nt(a, b):
    return lax.dot_general(a, b, (((1,), (1,)), ((), ())), preferred_element_type=F32)


def _mm_tn(a, b):
    return lax.dot_general(a, b, (((0,), (0,)), ((), ())), preferred_element_type=F32)


def _split2(a):
    hi = a.astype(BF16)
    return hi, (a - hi.astype(F32)).astype(BF16)


def _split3(a):
    hi = a.astype(BF16)
    r = a - hi.astype(F32)
    mid = r.astype(BF16)
    return hi, mid, (r - mid.astype(F32)).astype(BF16)


def _x3(mm, a, b):
    ah, al = _split2(a)
    bh, bl = _split2(b)
    return mm(ah, bh) + mm(ah, bl) + mm(al, bh)


def _exact_lhs(mm, a_bf16, b):
    b0, b1, b2 = _split3(b)
    return mm(a_bf16, b0) + mm(a_bf16, b1) + mm(a_bf16, b2)


def _seg_sum(x, e_bf16):
    x0, x1, x2 = _split3(x)
    return _mm(x0, e_bf16) + _mm(x1, e_bf16) + _mm(x2, e_bf16)


def _sigmoid(x):
    return 1.0 / (1.0 + jnp.exp(-x))


def _norm_matmul_kernel(x_ref, g_ref, w_ref, o_ref, h_ref):
    @pl.when(pl.program_id(1) == 0)
    def _():
        x = x_ref[...]
        ms = jnp.mean(x * x, axis=-1, keepdims=True)
        h_ref[...] = (x * lax.rsqrt(ms + NORM_EPS) * g_ref[...]).astype(BF16)

    o_ref[...] = _mm(h_ref[...], w_ref[...])


def _norm_matmul(x, g, w_bf16, tn):
    n, k = x.shape
    m = w_bf16.shape[1]
    tm = min(n, 512)
    return pl.pallas_call(
        _norm_matmul_kernel,
        grid=(n // tm, m // tn),
        in_specs=[pl.BlockSpec((tm, k), lambda i, j: (i, 0)),
                  pl.BlockSpec((1, k), lambda i, j: (0, 0)),
                  pl.BlockSpec((k, tn), lambda i, j: (0, j))],
        out_specs=pl.BlockSpec((tm, tn), lambda i, j: (i, j)),
        out_shape=jax.ShapeDtypeStruct((n, m), F32),
        scratch_shapes=[pltpu.VMEM((tm, k), BF16)],
        compiler_params=_params(("parallel", "arbitrary")),
        name="norm_inproj",
    )(x, g, w_bf16)


def _prep_kernel(z_ref, zp_ref, sh_ref, mu_ref, w0_ref, a0_ref, kk_ref, ka_ref, rk_ref,
                 w2_ref, a2_ref, g2_ref, e_ref, qg_ref, kg_ref,
                 r_o, k_o, v_o, kk_o, b_o, lw_o, g_o, bo_o, qn_o, kn_o):
    i = pl.program_id(1)
    zt = z_ref[0]
    tp = zt.shape[0]
    zr = zt[:, :R_PROJ]
    prev_row = jnp.where(i == 0, sh_ref[0], zp_ref[0][7:8, :R_PROJ])
    row = lax.broadcasted_iota(I32, (tp, 1), 0)
    prev = jnp.where(row == 0, prev_row, pltpu.roll(zr, 1, axis=0))
    zs = zr + (prev - zr) * mu_ref[...]
    r = zs[:, 0:R_DIM]
    k = zs[:, R_DIM:2 * R_DIM]
    v = zs[:, 2 * R_DIM:3 * R_DIM]
    lo = zs[:, 3 * R_DIM:3 * R_DIM + W_LORA + A_LORA]
    g_lo = zs[:, 3 * R_DIM + W_LORA + A_LORA:R_PROJ]
    e = e_ref[...]
    w_in = -(w0_ref[...] + _mm(jnp.tanh(lo).astype(BF16), w2_ref[...]))
    softplus = jnp.maximum(w_in, 0.0) + jnp.log1p(jnp.exp(-jnp.abs(w_in)))
    w_log = -softplus - 0.5
    lw_o[0] = -jnp.exp(w_log)
    a = _sigmoid(a0_ref[...] + _mm(lo.astype(BF16), a2_ref[...]))
    g_o[0] = _mm(_sigmoid(g_lo).astype(BF16), g2_ref[...])
    kk = k * kk_ref[...]
    kk = kk / jnp.maximum(jnp.sqrt(_seg_sum(kk * kk, e)), 1e-12)
    k2 = k * (1.0 + (a - 1.0) * ka_ref[...])
    r_o[0] = r
    k_o[0] = k2
    v_o[0] = v
    kk_o[0] = kk
    b_o[0] = kk * a
    bo_o[0] = _seg_sum(r * k2 * rk_ref[...], e) * v
    q = zt[:, R_PROJ:R_PROJ + A_DIM]
    qn_o[0] = q * lax.rsqrt(_seg_sum(q * q, e) * (1.0 / A_HD) + NORM_EPS) * qg_ref[...]
    kx = zt[:, R_PROJ + A_DIM:R_PROJ + A_DIM + KV_DIM]
    e_kv = e[:KV_DIM, :KV_DIM]
    kn_o[0] = kx * lax.rsqrt(_seg_sum(kx * kx, e_kv) * (1.0 / A_HD) + NORM_EPS) * kg_ref[...]


def _prep(z, shift0, lp):
    b, t, _ = z.shape
    tp = min(t, 256)
    row = lambda a: a.reshape(1, -1)
    vec = lambda n: pl.BlockSpec((1, n), lambda bi, i: (0, 0))
    full = lambda s: pl.BlockSpec(s, lambda bi, i: (0, 0))
    wide = pl.BlockSpec((1, tp, R_DIM), lambda bi, i: (bi, i, 0))
    outs = [jax.ShapeDtypeStruct((b, t, R_DIM), F32)] * 9 + [jax.ShapeDtypeStruct((b, t, KV_DIM), F32)]
    return pl.pallas_call(
        _prep_kernel,
        grid=(b, t // tp),
        in_specs=[pl.BlockSpec((1, tp, IN_COLS), lambda bi, i: (bi, i, 0)),
                  pl.BlockSpec((1, 8, IN_COLS), lambda bi, i: (bi, jnp.maximum(i * (tp // 8) - 1, 0), 0)),
                  pl.BlockSpec((1, 1, R_PROJ), lambda bi, i: (bi, 0, 0)),
                  vec(R_PROJ), vec(R_DIM), vec(R_DIM), vec(R_DIM), vec(R_DIM), vec(R_DIM),
                  full((LANES, R_DIM)), full((LANES, R_DIM)), full((G_LORA, R_DIM)),
                  full((R_DIM, R_DIM)), vec(A_DIM), vec(KV_DIM)],
        out_specs=[wide] * 9 + [pl.BlockSpec((1, tp, KV_DIM), lambda bi, i: (bi, i, 0))],
        out_shape=outs,
        compiler_params=_params(("parallel", "parallel")),
        name="rwkv_prep",
    )(z, z, shift0.reshape(b, 1, R_PROJ), row(lp["mu"]), row(lp["w0"]), row(lp["a0"]), row(lp["k_k"]),
      row(lp["k_a"]), row(lp["r_k"]), lp["w2p"], lp["a2p"], lp["g2"], lp["e64"], row(lp["q_gain"]),
      row(lp["k_gain"]))


def _stack_heads(x, lo_mask):
    return jnp.concatenate([jnp.where(lo_mask, x, 0.0), jnp.where(lo_mask, 0.0, x)], axis=0)


def _chunk_kernel(r_ref, k_ref, v_ref, kk_ref, b_ref, lw_ref, s0_ref, y_ref, sf_ref, s_ref):
    c = pl.program_id(1)
    L = CHUNK

    @pl.when(c == 0)
    def _():
        s_ref[...] = s0_ref[0]

    lane = lax.broadcasted_iota(I32, (1, LANES), 1)
    lo_mask = lane < R_HD
    rr = lax.broadcasted_iota(I32, (L, 2 * L), 0)
    cc = lax.broadcasted_iota(I32, (L, 2 * L), 1)
    cc = jnp.where(cc >= L, cc - L, cc)
    strict = rr > cc
    incl = rr >= cc
    t_r = lax.broadcasted_iota(I32, (L, L), 0)
    t_c = lax.broadcasted_iota(I32, (L, L), 1)
    tri = (t_r >= t_c).astype(BF16)
    col2 = lax.broadcasted_iota(I32, (L, 2 * L), 1) < L
    eye_r = lax.broadcasted_iota(I32, (2 * L, 2 * L), 0)
    eye_c = lax.broadcasted_iota(I32, (2 * L, 2 * L), 1)
    eye = (eye_r == eye_c).astype(F32)

    pairs = range(N_PAIRS)
    sls = [slice(p * LANES, (p + 1) * LANES) for p in pairs]
    r = [r_ref[0][:, sl] for sl in sls]
    k = [k_ref[0][:, sl] for sl in sls]
    v = [v_ref[0][:, sl] for sl in sls]
    kk = [kk_ref[0][:, sl] for sl in sls]
    bb = [b_ref[0][:, sl] for sl in sls]
    lw = [lw_ref[0][:, sl] for sl in sls]
    s0 = [s_ref[p] for p in pairs]

    cl = [_exact_lhs(_mm, tri, lw[p]) for p in pairs]
    cl_last = [cl[p][L - 1:L, :] for p in pairs]
    e_neg = [jnp.exp(-cl[p]) for p in pairs]
    e_last = [jnp.exp(cl_last[p] - cl[p]) for p in pairs]
    rt = [r[p] * jnp.exp(cl[p]) for p in pairs]
    at = [kk[p] * jnp.exp(cl[p] - lw[p]) for p in pairs]

    gm = [_x3(_mm_nt, jnp.concatenate([at[p], rt[p]], axis=0),
              jnp.concatenate([_stack_heads(k[p] * e_neg[p], lo_mask), _stack_heads(bb[p] * e_neg[p], lo_mask)],
                              axis=0)) for p in pairs]
    mk = [jnp.where(strict, gm[p][:L, :2 * L], 0.0) for p in pairs]
    mb = [jnp.where(strict, gm[p][:L, 2 * L:], 0.0) for p in pairs]
    hk = [jnp.where(incl, gm[p][L:, :2 * L], 0.0) for p in pairs]
    hb = [jnp.where(incl, gm[p][L:, 2 * L:], 0.0) for p in pairs]

    nil = [-jnp.concatenate([jnp.where(col2, mb[p], 0.0), jnp.where(col2, 0.0, mb[p])], axis=0) for p in pairs]
    tinv = [eye + nil[p] for p in pairs]
    nil = [_x3(_mm, nil[p], nil[p]) for p in pairs]
    for it in range(5):
        tinv = [tinv[p] + _x3(_mm, nil[p], tinv[p]) for p in pairs]
        if it < 4:
            nil = [_x3(_mm, nil[p], nil[p]) for p in pairs]

    vs = [_stack_heads(v[p], lo_mask) for p in pairs]
    rhs_u = [-(_x3(_mm_nt, at[p], s0[p]) + _x3(_mm, mk[p], vs[p])) for p in pairs]
    us = [_x3(_mm, tinv[p], _stack_heads(rhs_u[p], lo_mask)) for p in pairs]
    for p in pairs:
        y_ref[0, :, sls[p]] = _x3(_mm_nt, rt[p], s0[p]) + _x3(_mm, hk[p], vs[p]) + _x3(_mm, hb[p], us[p])
    for p in pairs:
        s_ref[p] = (s0[p] * jnp.exp(cl_last[p])
                    + _x3(_mm_tn, vs[p], _stack_heads(k[p] * e_last[p], lo_mask))
                    + _x3(_mm_tn, us[p], _stack_heads(bb[p] * e_last[p], lo_mask)))

    @pl.when(c == pl.num_programs(1) - 1)
    def _():
        sf_ref[0] = s_ref[...]


def _chunk_scan(r, k, v, kk, bb, lw, s0_bd):
    b, t, _ = r.shape
    wide = pl.BlockSpec((1, CHUNK, R_DIM), lambda bi, c: (bi, c, 0))
    st = pl.BlockSpec((1, N_PAIRS, LANES, LANES), lambda bi, c: (bi, 0, 0, 0))
    return pl.pallas_call(
        _chunk_kernel,
        grid=(b, t // CHUNK),
        in_specs=[wide] * 6 + [st],
        out_specs=[wide, st],
        out_shape=[jax.ShapeDtypeStruct((b, t, R_DIM), F32),
                   jax.ShapeDtypeStruct((b, N_PAIRS, LANES, LANES), F32)],
        scratch_shapes=[pltpu.VMEM((N_PAIRS, LANES, LANES), F32)],
        compiler_params=_params(("parallel", "arbitrary")),
        name="rwkv_chunk",
    )(r, k, v, kk, bb, lw, s0_bd)


def _swa_kernel(*refs, n_seg, banded):
    q_ref = refs[0]
    k_refs = refs[1:1 + n_seg]
    v_refs = refs[1 + n_seg:1 + 2 * n_seg]
    bias_ref, sink_ref, o_ref = refs[1 + 2 * n_seg:]
    c = pl.program_id(1)
    q = q_ref[0]
    kcat = jnp.concatenate([kr[0] for kr in k_refs], axis=0)
    vcat = jnp.concatenate([vr[0] for vr in v_refs], axis=0)
    n_k = kcat.shape[0]
    lane = lax.broadcasted_iota(I32, (1, LANES), 1)
    lo_mask = lane < A_HD
    k_sw = pltpu.roll(kcat, A_HD, axis=1)
    v_sw = pltpu.roll(vcat, A_HD, axis=1)
    k_dup = [jnp.where(lo_mask, kcat, k_sw).astype(BF16), jnp.where(lo_mask, k_sw, kcat).astype(BF16)]
    v_dup = [jnp.where(lo_mask, vcat, v_sw).astype(BF16), jnp.where(lo_mask, v_sw, vcat).astype(BF16)]
    if banded:
        key_chunk = c - WIN_CHUNKS + lax.broadcasted_iota(I32, (1, n_k), 1) // CHUNK
        valid = key_chunk >= 0
    n_q = q.shape[0]
    hi_mask = jnp.logical_not(lo_mask)
    for kvh in range(A_KV):
        heads = range(kvh * A_GROUP, (kvh + 1) * A_GROUP)
        qs = jnp.concatenate(
            [jnp.where(lo_mask if hq % 2 == 0 else hi_mask, q[:, (hq // 2) * LANES:(hq // 2 + 1) * LANES], 0.0)
             for hq in heads], axis=0).astype(BF16)
        bias = jnp.concatenate([bias_ref[hq] for hq in heads], axis=0)
        sink = jnp.concatenate([jnp.broadcast_to(sink_ref[0:1, hq:hq + 1], (n_q, 1)) for hq in heads], axis=0)
        s = _mm_nt(qs, k_dup[kvh]) * (A_HD ** -0.5) + bias
        if banded:
            s = jnp.where(valid, s, NEG_INF)
        m = jnp.maximum(jnp.max(s, axis=-1, keepdims=True), sink)
        pr = jnp.exp(s - m)
        den = jnp.sum(pr, axis=-1, keepdims=True) + jnp.exp(sink - m)
        o = _mm((pr / den).astype(BF16), v_dup[kvh])
        for i in range(A_GROUP // 2):
            pair = kvh * (A_GROUP // 2) + i
            o_ref[0, :, pair * LANES:(pair + 1) * LANES] = jnp.where(
                lo_mask, o[2 * i * n_q:(2 * i + 1) * n_q], o[(2 * i + 1) * n_q:(2 * i + 2) * n_q])


def _swa_prompt(qn, kn, z, bias, sink):
    b, t, _ = qn.shape
    v_col = (R_PROJ + A_DIM + KV_DIM) // KV_DIM
    seg = lambda s, col: pl.BlockSpec(
        (1, CHUNK, KV_DIM), lambda bi, c: (bi, jnp.maximum(c - WIN_CHUNKS + s, 0), col))
    n_seg = WIN_CHUNKS + 1
    return pl.pallas_call(
        functools.partial(_swa_kernel, n_seg=n_seg, banded=True),
        grid=(b, t // CHUNK),
        in_specs=[pl.BlockSpec((1, CHUNK, A_DIM), lambda bi, c: (bi, c, 0))]
                 + [seg(s, 0) for s in range(n_seg)] + [seg(s, v_col) for s in range(n_seg)]
                 + [pl.BlockSpec(bias.shape, lambda bi, c: (0, 0, 0)),
                    pl.BlockSpec((1, A_HEADS), lambda bi, c: (0, 0))],
        out_specs=pl.BlockSpec((1, CHUNK, A_DIM), lambda bi, c: (bi, c, 0)),
        out_shape=jax.ShapeDtypeStruct((b, t, A_DIM), F32),
        compiler_params=_params(("parallel", "parallel")),
        name="swa_prompt",
    )(qn, *([kn] * n_seg), *([z] * n_seg), bias, sink.reshape(1, A_HEADS))


def _swa_sample(qn, kn, v_new, k_cache, v_cache, bias, sink):
    b, t, _ = qn.shape
    n_cache = k_cache.shape[1]
    cur = lambda w: pl.BlockSpec((1, t, w), lambda bi, c: (bi, 0, 0))
    old = pl.BlockSpec((1, n_cache, KV_DIM), lambda bi, c: (bi, 0, 0))
    return pl.pallas_call(
        functools.partial(_swa_kernel, n_seg=2, banded=False),
        grid=(b, 1),
        in_specs=[cur(A_DIM), old, cur(KV_DIM), old, cur(KV_DIM),
                  pl.BlockSpec(bias.shape, lambda bi, c: (0, 0, 0)),
                  pl.BlockSpec((1, A_HEADS), lambda bi, c: (0, 0))],
        out_specs=cur(A_DIM),
        out_shape=jax.ShapeDtypeStruct((b, t, A_DIM), F32),
        compiler_params=_params(("parallel", "parallel")),
        name="swa_sample",
    )(qn, k_cache, kn, v_cache, v_new, bias, sink.reshape(1, A_HEADS))


def _mix_out_kernel(y_ref, bo_ref, g_ref, a_ref, x_ref, gng_ref, gnb_ref, e_ref, wo_ref, ln_ref, wq_ref,
                    xo_ref, h_ref, qq_ref):
    e = e_ref[...]
    y = y_ref[...]
    mu = _seg_sum(y, e) * (1.0 / R_HD)
    d = y - mu
    var = _seg_sum(d * d, e) * (1.0 / R_HD)
    yn = d * lax.rsqrt(var + GN_EPS) * gng_ref[...] + gnb_ref[...]
    mix_r = ((yn + bo_ref[...]) * g_ref[...]).astype(BF16)
    x = (x_ref[...] + _mm(mix_r, wo_ref[:R_DIM, :]) + _mm(a_ref[...].astype(BF16), wo_ref[R_DIM:, :]))
    xo_ref[...] = x
    ms = jnp.mean(x * x, axis=-1, keepdims=True)
    h = x * lax.rsqrt(ms + NORM_EPS) * ln_ref[...]
    h_ref[...] = h
    qq_ref[...] = _mm(h.astype(BF16), wq_ref[...])


def _mix_out(y, bonus, g, a_out, x, lp):
    n = x.shape[0]
    tm = min(n, 256)
    nq = PEER_HEADS * PK_DIM
    rows = lambda w: pl.BlockSpec((tm, w), lambda i: (i, 0))
    full = lambda s: pl.BlockSpec(s, lambda i: (0, 0))
    return pl.pallas_call(
        _mix_out_kernel,
        grid=(n // tm,),
        in_specs=[rows(R_DIM), rows(R_DIM), rows(R_DIM), rows(A_DIM), rows(D_MODEL),
                  full((1, R_DIM)), full((1, R_DIM)), full((R_DIM, R_DIM)),
                  full((D_MODEL, D_MODEL)), full((1, D_MODEL)), full((D_MODEL, nq))],
        out_specs=[rows(D_MODEL), rows(D_MODEL), rows(nq)],
        out_shape=[jax.ShapeDtypeStruct((n, D_MODEL), F32), jax.ShapeDtypeStruct((n, D_MODEL), F32),
                   jax.ShapeDtypeStruct((n, nq), F32)],
        compiler_params=_params(("parallel",)),
        name="mix_out_query",
    )(y, bonus, g, a_out, x, lp["gn_g"].reshape(1, -1), lp["gn_b"].reshape(1, -1), lp["e64"],
      lp["w_out"], lp["ln2_g"].reshape(1, -1), lp["w_pq"])


def _pick_rounds(s_ref, n_rows, emit):
    tb = s_ref.shape[1]
    rowid = lax.broadcasted_iota(I32, (n_rows, tb), 0)
    for rnd in range(PEER_TOPK):
        s = s_ref[...]
        m = jnp.max(s, axis=0, keepdims=True)
        idx = jnp.min(jnp.where(s == m, rowid, n_rows), axis=0, keepdims=True)
        hit = rowid == idx
        s_ref[...] = jnp.where(hit, -jnp.inf, s)
        emit(rnd, m, idx, hit)


def _topk_kernel(qq_ref, keys_ref, ei_ref, gate_ref, s_ref, sv_ref, si_ref, c_ref, ci_ref, ts_ref):
    tb = qq_ref.shape[0]
    for half in range(2):
        qh = qq_ref[:, half * PK_HALF:(half + 1) * PK_HALF]
        s_ref[...] = _x3(_mm_nt, keys_ref[half], qh)

        def emit1(rnd, m, idx, hit, half=half):
            sv_ref[half, rnd:rnd + 1, :] = m
            si_ref[half, rnd:rnd + 1, :] = idx

        _pick_rounds(s_ref, N_KEYS, emit1)

    row0 = 0
    for a, nb in enumerate(CAND_COLS):
        if nb == 1:
            break
        c_ref[row0:row0 + nb, :] = sv_ref[0, a:a + 1, :] + sv_ref[1, 0:nb, :]
        ci_ref[row0:row0 + nb, :] = si_ref[0, a:a + 1, :] * N_KEYS + si_ref[1, 0:nb, :]
        row0 += nb
    n_one = PEER_TOPK - a
    c_ref[row0:row0 + n_one, :] = sv_ref[0, a:, :] + sv_ref[1, 0:1, :]
    ci_ref[row0:row0 + n_one, :] = si_ref[0, a:, :] * N_KEYS + si_ref[1, 0:1, :]
    row0 += n_one
    c_ref[row0:, :] = jnp.full((N_CAND - row0, tb), -jnp.inf, F32)
    ci_ref[row0:, :] = jnp.zeros((N_CAND - row0, tb), I32)

    def emit2(rnd, m, idx, hit):
        ts_ref[rnd:rnd + 1, :] = m
        ei_ref[0, rnd:rnd + 1, :] = jnp.max(jnp.where(hit, ci_ref[...], -1), axis=0, keepdims=True)

    _pick_rounds(c_ref, N_CAND, emit2)
    ts = ts_ref[...]
    ex = jnp.exp(ts - ts[0:1, :])
    gate_ref[0] = ex / jnp.sum(ex, axis=0, keepdims=True)


def _topk(qq, sub_keys):
    n = qq.shape[0]
    tb = min(n, 512)
    out = pl.BlockSpec((1, PEER_TOPK, tb), lambda i, h: (h, 0, i))
    return pl.pallas_call(
        _topk_kernel,
        grid=(n // tb, PEER_HEADS),
        in_specs=[pl.BlockSpec((tb, PK_DIM), lambda i, h: (i, h)),
                  pl.BlockSpec((2, N_KEYS, PK_HALF), lambda i, h: (0, 0, 0))],
        out_specs=[out, out],
        out_shape=[jax.ShapeDtypeStruct((PEER_HEADS, PEER_TOPK, n), I32),
                   jax.ShapeDtypeStruct((PEER_HEADS, PEER_TOPK, n), F32)],
        scratch_shapes=[pltpu.VMEM((N_KEYS, tb), F32), pltpu.VMEM((2, PEER_TOPK, tb), F32),
                        pltpu.VMEM((2, PEER_TOPK, tb), I32), pltpu.VMEM((N_CAND, tb), F32),
                        pltpu.VMEM((N_CAND, tb), I32), pltpu.VMEM((PEER_TOPK, tb), F32)],
        compiler_params=_params(("parallel", "parallel")),
        name="peer_topk",
    )(qq, sub_keys)


def _expert_gather(tbl_hbm, idx_v, rows_v, sem, tt, g, buf):
    return pltpu.make_async_copy(tbl_hbm.at[idx_v[pl.ds(tt * N_SEL + g * SC_GROUP, SC_GROUP)]],
                                 rows_v.at[buf], sem.at[buf])


def _sc_token_blocks(n, tbl_hbm, idx_v, rows_v, sem, load_block, compute, store_block):
    wid = lax.axis_index("s") * SC_CORES + lax.axis_index("c")
    npw = n // SC_WORKERS
    steps = SC_TOK_BLK * SC_N_GROUPS

    @pl.loop(0, npw // SC_TOK_BLK)
    def _(blk):
        tok0 = wid * npw + blk * SC_TOK_BLK
        load_block(tok0)
        for s in range(SC_NBUF - 1):
            _expert_gather(tbl_hbm, idx_v, rows_v, sem, s // SC_N_GROUPS, s % SC_N_GROUPS, s).start()

        @pl.loop(0, steps, step=SC_NBUF)
        def _(s0):
            for b in range(SC_NBUF):
                s = s0 + b
                ahead = s + SC_NBUF - 1

                @pl.when(ahead < steps)
                def _():
                    _expert_gather(tbl_hbm, idx_v, rows_v, sem, ahead // SC_N_GROUPS, ahead % SC_N_GROUPS,
                                   (b + SC_NBUF - 1) % SC_NBUF).start()

                _expert_gather(tbl_hbm, idx_v, rows_v, sem, s // SC_N_GROUPS, s % SC_N_GROUPS, b).wait()
                compute(s // SC_N_GROUPS, s % SC_N_GROUPS, rows_v.at[b])

        store_block(tok0)


def _row_chunk(j):
    per_line = LANES // SC_LANES
    return j // per_line, pl.ds((j % per_line) * SC_LANES, SC_LANES)


def _expert_cost(n):
    elems = n * N_SEL * D_MODEL
    return pl.CostEstimate(flops=2 * elems, transcendentals=0, bytes_accessed=4 * elems)


def _sc_mesh():
    return plsc.VectorSubcoreMesh(core_axis_name="c", subcore_axis_name="s",
                                  num_cores=SC_CORES, num_subcores=SC_SUBCORES)


def _sc_hidden(eidx, h, table):
    n = h.shape[0]

    @functools.partial(
        pl.kernel, out_type=jax.ShapeDtypeStruct((n * N_SEL,), F32), mesh=_sc_mesh(),
        compiler_params=pltpu.CompilerParams(needs_layout_passes=False),
        scratch_types=[pltpu.VMEM((SC_TOK_BLK * N_SEL,), I32), pltpu.VMEM((SC_TOK_BLK * D_MODEL,), F32),
                       pltpu.VMEM((SC_NBUF, SC_GROUP) + SC_ROW_TILE, F32), pltpu.VMEM((SC_TOK_BLK * N_SEL,), F32),
                       pltpu.SemaphoreType.DMA((SC_NBUF,))],
        cost_estimate=_expert_cost(n), name="peer_hidden_sc")
    def run(eidx_hbm, h_hbm, tbl_hbm, hid_hbm, idx_v, h_v, rows_v, hid_v, sem):
        lane = lax.iota(I32, SC_LANES)

        def load_block(tok0):
            pltpu.sync_copy(eidx_hbm.at[pl.ds(tok0 * N_SEL, SC_TOK_BLK * N_SEL)], idx_v)
            pltpu.sync_copy(h_hbm.at[pl.ds(tok0 * D_MODEL, SC_TOK_BLK * D_MODEL)], h_v)

        def compute(tt, g, rows):
            accs = []
            for sub in range(SC_GROUP // SC_RSUB):
                zero = tuple(jnp.zeros((SC_LANES,), F32) for _ in range(SC_RSUB))

                @plsc.parallel_loop(0, D_MODEL // SC_LANES, unroll=2, carry=zero)
                def part(j, acc):
                    hvj = h_v[pl.ds(tt * D_MODEL + j * SC_LANES, SC_LANES)]
                    return tuple(acc[r] + rows[(sub * SC_RSUB + r, *_row_chunk(j))] * hvj
                                 for r in range(SC_RSUB))

                accs.extend(part)
            tot = jnp.zeros((SC_LANES,), F32)
            for r in range(SC_GROUP):
                tot = jnp.where(lane == r, jnp.sum(accs[r]), tot)
            hid_v[pl.ds(tt * N_SEL + g * SC_GROUP, SC_GROUP)] = tot

        def store_block(tok0):
            pltpu.sync_copy(hid_v, hid_hbm.at[pl.ds(tok0 * N_SEL, SC_TOK_BLK * N_SEL)])

        _sc_token_blocks(n, tbl_hbm, idx_v, rows_v, sem, load_block, compute, store_block)

    return run(eidx.reshape(-1), h.reshape(-1), table).reshape(n, N_SEL)


def _sc_combine(eidx, coef, x, table):
    n = x.shape[0]

    @functools.partial(
        pl.kernel, out_type=jax.ShapeDtypeStruct((n * D_MODEL,), F32), mesh=_sc_mesh(),
        compiler_params=pltpu.CompilerParams(needs_layout_passes=False),
        scratch_types=[pltpu.VMEM((SC_TOK_BLK * N_SEL,), I32), pltpu.VMEM((SC_TOK_BLK * N_SEL,), F32),
                       pltpu.VMEM((SC_NBUF, SC_GROUP) + SC_ROW_TILE, F32), pltpu.VMEM((SC_TOK_BLK * D_MODEL,), F32),
                       pltpu.SemaphoreType.DMA((SC_NBUF,))],
        cost_estimate=_expert_cost(n), name="peer_combine_sc")
    def run(eidx_hbm, c_hbm, x_hbm, tbl_hbm, out_hbm, idx_v, c_v, rows_v, out_v, sem):
        lane = lax.iota(I32, SC_LANES)

        def load_block(tok0):
            pltpu.sync_copy(eidx_hbm.at[pl.ds(tok0 * N_SEL, SC_TOK_BLK * N_SEL)], idx_v)
            pltpu.sync_copy(c_hbm.at[pl.ds(tok0 * N_SEL, SC_TOK_BLK * N_SEL)], c_v)
            pltpu.sync_copy(x_hbm.at[pl.ds(tok0 * D_MODEL, SC_TOK_BLK * D_MODEL)], out_v)

        def compute(tt, g, rows):
            cvec = c_v[pl.ds(tt * N_SEL + g * SC_GROUP, SC_GROUP)]
            coefs = [jnp.sum(jnp.where(lane == r, cvec, 0.0)) for r in range(SC_GROUP)]

            @plsc.parallel_loop(0, D_MODEL // SC_LANES, unroll=2)
            def _(j):
                sl = pl.ds(tt * D_MODEL + j * SC_LANES, SC_LANES)
                acc = out_v[sl]
                for r in range(SC_GROUP):
                    acc = acc + rows[(r, *_row_chunk(j))] * coefs[r]
                out_v[sl] = acc

        def store_block(tok0):
            pltpu.sync_copy(out_v, out_hbm.at[pl.ds(tok0 * D_MODEL, SC_TOK_BLK * D_MODEL)])

        _sc_token_blocks(n, tbl_hbm, idx_v, rows_v, sem, load_block, compute, store_block)

    return run(eidx.reshape(-1), coef.reshape(-1), x.reshape(-1), table).reshape(n, D_MODEL)


def _gate_act_kernel(hid_ref, gate_ref, o_ref):
    hid = hid_ref[...]
    o_ref[...] = gate_ref[...] * (0.5 * hid * (1.0 + lax.erf(hid * np.float32(np.sqrt(0.5)))))


def _gate_act(hid, gate):
    n = hid.shape[0]
    tm = min(n, 2048)
    rows = pl.BlockSpec((tm, N_SEL), lambda i: (i, 0))
    return pl.pallas_call(
        _gate_act_kernel, grid=(n // tm,), in_specs=[rows, rows], out_specs=rows,
        out_shape=jax.ShapeDtypeStruct((n, N_SEL), F32), compiler_params=_params(("parallel",)),
        name="peer_gate_act",
    )(hid, gate)


def _t5_bucket(rel):
    nb = NUM_BUCKETS // 2
    max_exact = nb // 2
    ret = jnp.where(rel > 0, nb, 0)
    n = jnp.abs(rel)
    nf = jnp.maximum(n, 1).astype(F32)
    large = max_exact + (jnp.log(nf / max_exact) / math.log(MAX_DISTANCE / max_exact)
                         * (nb - max_exact)).astype(I32)
    large = jnp.minimum(large, nb - 1)
    return ret + jnp.where(n < max_exact, n, large)


def _rel_bias(rel_bias, n_q, n_k, n_before):
    rel = (jnp.arange(n_k)[None, :] - n_before) - jnp.arange(n_q)[:, None]
    return jnp.transpose(rel_bias[_t5_bucket(rel)].astype(F32), (2, 0, 1))


def _state_to_pairs(wkv):
    b = wkv.shape[0]
    s = wkv.reshape(b, N_PAIRS, 2, R_HD, R_HD)
    z = jnp.zeros_like(s[:, :, 0])
    top = jnp.concatenate([s[:, :, 0], z], axis=-1)
    bot = jnp.concatenate([z, s[:, :, 1]], axis=-1)
    return jnp.concatenate([top, bot], axis=-2)


def _pairs_to_state(s_bd):
    b = s_bd.shape[0]
    return jnp.stack([s_bd[:, :, :R_HD, :R_HD], s_bd[:, :, R_HD:, R_HD:]], axis=2).reshape(
        b, R_HEADS, R_HD, R_HD)


def _layer_dense(x, shift0, wkv0, kv_cache, bias, lp):
    b, t, _ = x.shape
    n = b * t
    z = _norm_matmul(x.reshape(n, D_MODEL), lp["ln1_g"].reshape(1, -1), lp["w_in"], 512).reshape(b, t, IN_COLS)
    r, k2, v, kk, bb, lw, g, bonus, qn, kn = _prep(z, shift0, lp)
    v_new = z[:, :, R_PROJ + A_DIM + KV_DIM:]
    if t % CHUNK:
        pad = lambda a: jnp.pad(a, ((0, 0), (0, CHUNK - t % CHUNK), (0, 0)))
        y, s_fin = _chunk_scan(*(pad(a) for a in (r, k2, v, kk, bb, lw)), _state_to_pairs(wkv0))
        y = y[:, :t]
    else:
        y, s_fin = _chunk_scan(r, k2, v, kk, bb, lw, _state_to_pairs(wkv0))
    if kv_cache is None:
        a_out = _swa_prompt(qn, kn, z, bias, lp["sink"])
    else:
        a_out = _swa_sample(qn, kn, v_new, kv_cache[0].reshape(b, -1, KV_DIM),
                            kv_cache[1].reshape(b, -1, KV_DIM), bias, lp["sink"])
    flat = lambda a: a.reshape(n, a.shape[-1])
    x1, h, qq = _mix_out(flat(y), flat(bonus), flat(g), flat(a_out), flat(x), lp)
    eidx, gate = _topk(qq, lp["sub_keys"])
    sel = lambda a: jnp.transpose(a, (2, 0, 1)).reshape(n, N_SEL)
    return ((sel(eidx), sel(gate), h, x1),
            (_pairs_to_state(s_fin), z[:, -1, :R_PROJ],
             kn.reshape(b, t, A_KV, A_HD), v_new.reshape(b, t, A_KV, A_HD)))


def _layer_experts(ops, shape, lp, after=None):
    eidx, gate, h, x1 = ops
    if after is not None:
        h, after = lax.optimization_barrier((h, after))
    coef = _gate_act(_sc_hidden(eidx, h, lp["peer_u"]), gate)
    return _sc_combine(eidx, coef, x1, lp["peer_v"]).reshape(shape), after


def kernel(x_prompt, x_sample, state_rwkv_wkv, state_rwkv_shift, cache_swa_k, cache_swa_v, ln1_g, w_in,
           mu_shift, w0, w2, a0, a2, g2, k_k, k_a, r_k, gn_g, gn_b, q_norm_g, k_norm_g, attn_sink,
           rel_bias, w_out, ln2_g, w_pq, sub_keys, peer_u, peer_v):
    depth = w_in.shape[0]
    b_p, s_p = x_prompt.shape[:2]
    b_s, t_s = x_sample.shape[:2]
    n_cache = cache_swa_k.shape[2]
    n_keep = min(WINDOW, s_p)
    bias_p = _rel_bias(rel_bias, CHUNK, (WIN_CHUNKS + 1) * CHUNK, WIN_CHUNKS * CHUNK)
    bias_s = _rel_bias(rel_bias, t_s, n_cache + t_s, n_cache)
    head_id = jnp.arange(R_DIM) // R_HD
    e64 = (head_id[:, None] == head_id[None, :]).astype(BF16)
    zpad = jnp.zeros((LANES - W_LORA, R_DIM), F32)
    n_groups = PROMPT_GROUPS if b_p % PROMPT_GROUPS == 0 else 1
    xp, xs = jnp.split(x_prompt, n_groups, axis=0), x_sample
    outs = [[] for _ in range(8)]
    for l in range(depth):
        lp = {
            "ln1_g": ln1_g[l], "w_in": w_in[l].astype(BF16), "mu": mu_shift[l], "w0": w0[l], "a0": a0[l],
            "k_k": k_k[l], "k_a": k_a[l], "r_k": r_k[l].reshape(-1), "gn_g": gn_g[l], "gn_b": gn_b[l],
            "w2p": jnp.concatenate([w2[l], zpad], axis=0).astype(BF16),
            "a2p": jnp.concatenate([zpad, a2[l]], axis=0).astype(BF16),
            "g2": g2[l].astype(BF16), "e64": e64,
            "q_gain": jnp.tile(q_norm_g[l], A_HEADS), "k_gain": jnp.tile(k_norm_g[l], A_KV),
            "sink": attn_sink[l].astype(F32), "w_out": w_out[l].astype(BF16), "ln2_g": ln2_g[l],
            "w_pq": w_pq[l].astype(BF16), "sub_keys": sub_keys[l], "peer_u": peer_u[l].reshape((-1,) + SC_ROW_TILE), "peer_v": peer_v[l].reshape((-1,) + SC_ROW_TILE),
        }
        parts = [_layer_dense(xg, jnp.zeros((xg.shape[0], R_PROJ), F32),
                              jnp.zeros((xg.shape[0], R_HEADS, R_HD, R_HD), F32), None, bias_p, lp) for xg in xp]
        wkv_p, sh_p, k_p, v_p = (jnp.concatenate([pt[1][i] for pt in parts], axis=0) for i in range(4))
        new_xp, prev = [], None
        for xg, (ops, _) in zip(xp, parts):
            if prev is None:
                xo, _ = _layer_experts(ops, xg.shape, lp)
            else:
                xo, new_xp[-1] = _layer_experts(ops, xg.shape, lp, after=prev)
            new_xp.append(xo)
            prev = xo
        xp = new_xp
        ops_s, (wkv_s, sh_s, k_s, v_s) = _layer_dense(
            xs, state_rwkv_shift[l], state_rwkv_wkv[l].astype(F32), (cache_swa_k[l], cache_swa_v[l]),
            bias_s, lp)
        xs, _ = _layer_experts(ops_s, xs.shape, lp)
        for lst, val in zip(outs, (wkv_p, sh_p, k_p[:, s_p - n_keep:], v_p[:, s_p - n_keep:],
                                   wkv_s, sh_s, k_s, v_s)):
            lst.append(val)
    return (jnp.concatenate(xp, axis=0), xs) + tuple(jnp.stack(o) for o in outs)
```

```python
import functools
import math

import numpy as np
import jax
import jax.numpy as jnp
from jax import lax
from jax.experimental import pallas as pl
from jax.experimental.pallas import tpu as pltpu
from jax.experimental.pallas import tpu_sc as plsc

F32 = jnp.float32
BF16 = jnp.bfloat16
I32 = jnp.int32

D_MODEL = 1024
CHUNK = 64
R_HEADS = 8
R_HD = 64
R_DIM = R_HEADS * R_HD
W_LORA = 64
A_LORA = 64
G_LORA = 128
R_PROJ = 3 * R_DIM + W_LORA + A_LORA + G_LORA
A_HEADS = 8
A_KV = 2
A_GROUP = A_HEADS // A_KV
A_HD = 64
A_DIM = A_HEADS * A_HD
KV_DIM = A_KV * A_HD
IN_COLS = R_PROJ + A_DIM + 2 * KV_DIM
WINDOW = 128
WIN_CHUNKS = WINDOW // CHUNK
NUM_BUCKETS = 32
MAX_DISTANCE = 128
PEER_HEADS = 8
N_KEYS = 128
PK_DIM = 256
PK_HALF = PK_DIM // 2
PEER_TOPK = 16
N_SEL = PEER_HEADS * PEER_TOPK
NORM_EPS = 1e-6
GN_EPS = 64e-5
NEG_INF = -1e30

LANES = 128
N_PAIRS = R_DIM // LANES
VMEM_LIMIT = 48 * 1024 * 1024
CAND_COLS = (16, 8, 8, 4, 4, 4, 4, 4, 1, 1, 1, 1, 1, 1, 1, 1)
assert all(nb >= PEER_TOPK // (a + 1) for a, nb in enumerate(CAND_COLS))
N_CAND = 64
assert sum(CAND_COLS) <= N_CAND
SC_CORES = 2
SC_SUBCORES = 16
SC_LANES = 16
SC_WORKERS = SC_CORES * SC_SUBCORES
SC_TOK_BLK = 32
SC_GROUP = SC_LANES
SC_N_GROUPS = N_SEL // SC_GROUP
SC_RSUB = 16
SC_NBUF = 4
SC_ROW_TILE = (D_MODEL // LANES, LANES)
PROMPT_GROUPS = 8


def _params(sem):
    return pltpu.CompilerParams(dimension_semantics=sem, vmem_limit_bytes=VMEM_LIMIT)


def _mm(a, b):
    return jnp.dot(a, b, preferred_element_type=F32)


def _mm_nt(a, b):
    return lax.dot_general(a, b, (((1,), (1,)), ((), ())), preferred_element_type=F32)


def _mm_tn(a, b):
    return lax.dot_general(a, b, (((0,), (0,)), ((), ())), preferred_element_type=F32)


def _split2(a):
    hi = a.astype(BF16)
    return hi, (a - hi.astype(F32)).astype(BF16)


def _split3(a):
    hi = a.astype(BF16)
    r = a - hi.astype(F32)
    mid = r.astype(BF16)
    return hi, mid, (r - mid.astype(F32)).astype(BF16)


def _x3(mm, a, b):
    ah, al = _split2(a)
    bh, bl = _split2(b)
    return mm(ah, bh) + mm(ah, bl) + mm(al, bh)


def _exact_lhs(mm, a_bf16, b):
    b0, b1, b2 = _split3(b)
    return mm(a_bf16, b0) + mm(a_bf16, b1) + mm(a_bf16, b2)


def _seg_sum(x, e_bf16):
    x0, x1, x2 = _split3(x)
    return _mm(x0, e_bf16) + _mm(x1, e_bf16) + _mm(x2, e_bf16)


def _sigmoid(x):
    return 1.0 / (1.0 + jnp.exp(-x))


def _norm_matmul_kernel(x_ref, g_ref, w_ref, o_ref, h_ref):
    @pl.when(pl.program_id(1) == 0)
    def _():
        x = x_ref[...]
        ms = jnp.mean(x * x, axis=-1, keepdims=True)
        h_ref[...] = (x * lax.rsqrt(ms + NORM_EPS) * g_ref[...]).astype(BF16)

    o_ref[...] = _mm(h_ref[...], w_ref[...])


def _norm_matmul(x, g, w_bf16, tn):
    n, k = x.shape
    m = w_bf16.shape[1]
    tm = min(n, 512)
    return pl.pallas_call(
        _norm_matmul_kernel,
        grid=(n // tm, m // tn),
        in_specs=[pl.BlockSpec((tm, k), lambda i, j: (i, 0)),
                  pl.BlockSpec((1, k), lambda i, j: (0, 0)),
                  pl.BlockSpec((k, tn), lambda i, j: (0, j))],
        out_specs=pl.BlockSpec((tm, tn), lambda i, j: (i, j)),
        out_shape=jax.ShapeDtypeStruct((n, m), F32),
        scratch_shapes=[pltpu.VMEM((tm, k), BF16)],
        compiler_params=_params(("parallel", "arbitrary")),
        name="norm_inproj",
    )(x, g, w_bf16)


def _prep_kernel(z_ref, zp_ref, sh_ref, mu_ref, w0_ref, a0_ref, kk_ref, ka_ref, rk_ref,
                 w2_ref, a2_ref, g2_ref, e_ref, qg_ref, kg_ref,
                 r_o, k_o, v_o, kk_o, b_o, lw_o, g_o, bo_o, qn_o, kn_o):
    i = pl.program_id(1)
    zt = z_ref[0]
    tp = zt.shape[0]
    zr = zt[:, :R_PROJ]
    prev_row = jnp.where(i == 0, sh_ref[0], zp_ref[0][7:8, :R_PROJ])
    row = lax.broadcasted_iota(I32, (tp, 1), 0)
    prev = jnp.where(row == 0, prev_row, pltpu.roll(zr, 1, axis=0))
    zs = zr + (prev - zr) * mu_ref[...]
    r = zs[:, 0:R_DIM]
    k = zs[:, R_DIM:2 * R_DIM]
    v = zs[:, 2 * R_DIM:3 * R_DIM]
    lo = zs[:, 3 * R_DIM:3 * R_DIM + W_LORA + A_LORA]
    g_lo = zs[:, 3 * R_DIM + W_LORA + A_LORA:R_PROJ]
    e = e_ref[...]
    w_in = -(w0_ref[...] + _mm(jnp.tanh(lo).astype(BF16), w2_ref[...]))
    softplus = jnp.maximum(w_in, 0.0) + jnp.log1p(jnp.exp(-jnp.abs(w_in)))
    w_log = -softplus - 0.5
    lw_o[0] = -jnp.exp(w_log)
    a = _sigmoid(a0_ref[...] + _mm(lo.astype(BF16), a2_ref[...]))
    g_o[0] = _mm(_sigmoid(g_lo).astype(BF16), g2_ref[...])
    kk = k * kk_ref[...]
    kk = kk / jnp.maximum(jnp.sqrt(_seg_sum(kk * kk, e)), 1e-12)
    k2 = k * (1.0 + (a - 1.0) * ka_ref[...])
    r_o[0] = r
    k_o[0] = k2
    v_o[0] = v
    kk_o[0] = kk
    b_o[0] = kk * a
    bo_o[0] = _seg_sum(r * k2 * rk_ref[...], e) * v
    q = zt[:, R_PROJ:R_PROJ + A_DIM]
    qn_o[0] = q * lax.rsqrt(_seg_sum(q * q, e) * (1.0 / A_HD) + NORM_EPS) * qg_ref[...]
    kx = zt[:, R_PROJ + A_DIM:R_PROJ + A_DIM + KV_DIM]
    e_kv = e[:KV_DIM, :KV_DIM]
    kn_o[0] = kx * lax.rsqrt(_seg_sum(kx * kx, e_kv) * (1.0 / A_HD) + NORM_EPS) * kg_ref[...]


def _prep(z, shift0, lp):
    b, t, _ = z.shape
    tp = min(t, 256)
    row = lambda a: a.reshape(1, -1)
    vec = lambda n: pl.BlockSpec((1, n), lambda bi, i: (0, 0))
    full = lambda s: pl.BlockSpec(s, lambda bi, i: (0, 0))
    wide = pl.BlockSpec((1, tp, R_DIM), lambda bi, i: (bi, i, 0))
    outs = [jax.ShapeDtypeStruct((b, t, R_DIM), F32)] * 9 + [jax.ShapeDtypeStruct((b, t, KV_DIM), F32)]
    return pl.pallas_call(
        _prep_kernel,
        grid=(b, t // tp),
        in_specs=[pl.BlockSpec((1, tp, IN_COLS), lambda bi, i: (bi, i, 0)),
                  pl.BlockSpec((1, 8, IN_COLS), lambda bi, i: (bi, jnp.maximum(i * (tp // 8) - 1, 0), 0)),
                  pl.BlockSpec((1, 1, R_PROJ), lambda bi, i: (bi, 0, 0)),
                  vec(R_PROJ), vec(R_DIM), vec(R_DIM), vec(R_DIM), vec(R_DIM), vec(R_DIM),
                  full((LANES, R_DIM)), full((LANES, R_DIM)), full((G_LORA, R_DIM)),
                  full((R_DIM, R_DIM)), vec(A_DIM), vec(KV_DIM)],
        out_specs=[wide] * 9 + [pl.BlockSpec((1, tp, KV_DIM), lambda bi, i: (bi, i, 0))],
        out_shape=outs,
        compiler_params=_params(("parallel", "parallel")),
        name="rwkv_prep",
    )(z, z, shift0.reshape(b, 1, R_PROJ), row(lp["mu"]), row(lp["w0"]), row(lp["a0"]), row(lp["k_k"]),
      row(lp["k_a"]), row(lp["r_k"]), lp["w2p"], lp["a2p"], lp["g2"], lp["e64"], row(lp["q_gain"]),
      row(lp["k_gain"]))


def _stack_heads(x, lo_mask):
    return jnp.concatenate([jnp.where(lo_mask, x, 0.0), jnp.where(lo_mask, 0.0, x)], axis=0)


def _chunk_kernel(r_ref, k_ref, v_ref, kk_ref, b_ref, lw_ref, s0_ref, y_ref, sf_ref, s_ref):
    c = pl.program_id(1)
    L = CHUNK

    @pl.when(c == 0)
    def _():
        s_ref[...] = s0_ref[0]

    lane = lax.broadcasted_iota(I32, (1, LANES), 1)
    lo_mask = lane < R_HD
    rr = lax.broadcasted_iota(I32, (L, 2 * L), 0)
    cc = lax.broadcasted_iota(I32, (L, 2 * L), 1)
    cc = jnp.where(cc >= L, cc - L, cc)
    strict = rr > cc
    incl = rr >= cc
    t_r = lax.broadcasted_iota(I32, (L, L), 0)
    t_c = lax.broadcasted_iota(I32, (L, L), 1)
    tri = (t_r >= t_c).astype(BF16)
    col2 = lax.broadcasted_iota(I32, (L, 2 * L), 1) < L
    eye_r = lax.broadcasted_iota(I32, (2 * L, 2 * L), 0)
    eye_c = lax.broadcasted_iota(I32, (2 * L, 2 * L), 1)
    eye = (eye_r == eye_c).astype(F32)

    pairs = range(N_PAIRS)
    sls = [slice(p * LANES, (p + 1) * LANES) for p in pairs]
    r = [r_ref[0][:, sl] for sl in sls]
    k = [k_ref[0][:, sl] for sl in sls]
    v = [v_ref[0][:, sl] for sl in sls]
    kk = [kk_ref[0][:, sl] for sl in sls]
    bb = [b_ref[0][:, sl] for sl in sls]
    lw = [lw_ref[0][:, sl] for sl in sls]
    s0 = [s_ref[p] for p in pairs]

    cl = [_exact_lhs(_mm, tri, lw[p]) for p in pairs]
    cl_last = [cl[p][L - 1:L, :] for p in pairs]
    e_neg = [jnp.exp(-cl[p]) for p in pairs]
    e_last = [jnp.exp(cl_last[p] - cl[p]) for p in pairs]
    rt = [r[p] * jnp.exp(cl[p]) for p in pairs]
    at = [kk[p] * jnp.exp(cl[p] - lw[p]) for p in pairs]

    gm = [_x3(_mm_nt, jnp.concatenate([at[p], rt[p]], axis=0),
              jnp.concatenate([_stack_heads(k[p] * e_neg[p], lo_mask), _stack_heads(bb[p] * e_neg[p], lo_mask)],
                              axis=0)) for p in pairs]
    mk = [jnp.where(strict, gm[p][:L, :2 * L], 0.0) for p in pairs]
    mb = [jnp.where(strict, gm[p][:L, 2 * L:], 0.0) for p in pairs]
    hk = [jnp.where(incl, gm[p][L:, :2 * L], 0.0) for p in pairs]
    hb = [jnp.where(incl, gm[p][L:, 2 * L:], 0.0) for p in pairs]

    nil = [-jnp.concatenate([jnp.where(col2, mb[p], 0.0), jnp.where(col2, 0.0, mb[p])], axis=0) for p in pairs]
    tinv = [eye + nil[p] for p in pairs]
    nil = [_x3(_mm, nil[p], nil[p]) for p in pairs]
    for it in range(5):
        tinv = [tinv[p] + _x3(_mm, nil[p], tinv[p]) for p in pairs]
        if it < 4:
            nil = [_x3(_mm, nil[p], nil[p]) for p in pairs]

    vs = [_stack_heads(v[p], lo_mask) for p in pairs]
    rhs_u = [-(_x3(_mm_nt, at[p], s0[p]) + _x3(_mm, mk[p], vs[p])) for p in pairs]
    us = [_x3(_mm, tinv[p], _stack_heads(rhs_u[p], lo_mask)) for p in pairs]
    for p in pairs:
        y_ref[0, :, sls[p]] = _x3(_mm_nt, rt[p], s0[p]) + _x3(_mm, hk[p], vs[p]) + _x3(_mm, hb[p], us[p])
    for p in pairs:
        s_ref[p] = (s0[p] * jnp.exp(cl_last[p])
                    + _x3(_mm_tn, vs[p], _stack_heads(k[p] * e_last[p], lo_mask))
                    + _x3(_mm_tn, us[p], _stack_heads(bb[p] * e_last[p], lo_mask)))

    @pl.when(c == pl.num_programs(1) - 1)
    def _():
        sf_ref[0] = s_ref[...]


def _chunk_scan(r, k, v, kk, bb, lw, s0_bd):
    b, t, _ = r.shape
    wide = pl.BlockSpec((1, CHUNK, R_DIM), lambda bi, c: (bi, c, 0))
    st = pl.BlockSpec((1, N_PAIRS, LANES, LANES), lambda bi, c: (bi, 0, 0, 0))
    return pl.pallas_call(
        _chunk_kernel,
        grid=(b, t // CHUNK),
        in_specs=[wide] * 6 + [st],
        out_specs=[wide, st],
        out_shape=[jax.ShapeDtypeStruct((b, t, R_DIM), F32),
                   jax.ShapeDtypeStruct((b, N_PAIRS, LANES, LANES), F32)],
        scratch_shapes=[pltpu.VMEM((N_PAIRS, LANES, LANES), F32)],
        compiler_params=_params(("parallel", "arbitrary")),
        name="rwkv_chunk",
    )(r, k, v, kk, bb, lw, s0_bd)


def _swa_kernel(*refs, n_seg, banded):
    q_ref = refs[0]
    k_refs = refs[1:1 + n_seg]
    v_refs = refs[1 + n_seg:1 + 2 * n_seg]
    bias_ref, sink_ref, o_ref = refs[1 + 2 * n_seg:]
    c = pl.program_id(1)
    q = q_ref[0]
    kcat = jnp.concatenate([kr[0] for kr in k_refs], axis=0)
    vcat = jnp.concatenate([vr[0] for vr in v_refs], axis=0)
    n_k = kcat.shape[0]
    lane = lax.broadcasted_iota(I32, (1, LANES), 1)
    lo_mask = lane < A_HD
    k_sw = pltpu.roll(kcat, A_HD, axis=1)
    v_sw = pltpu.roll(vcat, A_HD, axis=1)
    k_dup = [jnp.where(lo_mask, kcat, k_sw).astype(BF16), jnp.where(lo_mask, k_sw, kcat).astype(BF16)]
    v_dup = [jnp.where(lo_mask, vcat, v_sw).astype(BF16), jnp.where(lo_mask, v_sw, vcat).astype(BF16)]
    if banded:
        key_chunk = c - WIN_CHUNKS + lax.broadcasted_iota(I32, (1, n_k), 1) // CHUNK
        valid = key_chunk >= 0
    n_q = q.shape[0]
    hi_mask = jnp.logical_not(lo_mask)
    for kvh in range(A_KV):
        heads = range(kvh * A_GROUP, (kvh + 1) * A_GROUP)
        qs = jnp.concatenate(
            [jnp.where(lo_mask if hq % 2 == 0 else hi_mask, q[:, (hq // 2) * LANES:(hq // 2 + 1) * LANES], 0.0)
             for hq in heads], axis=0).astype(BF16)
        bias = jnp.concatenate([bias_ref[hq] for hq in heads], axis=0)
        sink = jnp.concatenate([jnp.broadcast_to(sink_ref[0:1, hq:hq + 1], (n_q, 1)) for hq in heads], axis=0)
        s = _mm_nt(qs, k_dup[kvh]) * (A_HD ** -0.5) + bias
        if banded:
            s = jnp.where(valid, s, NEG_INF)
        m = jnp.maximum(jnp.max(s, axis=-1, keepdims=True), sink)
        pr = jnp.exp(s - m)
        den = jnp.sum(pr, axis=-1, keepdims=True) + jnp.exp(sink - m)
        o = _mm((pr / den).astype(BF16), v_dup[kvh])
        for i in range(A_GROUP // 2):
            pair = kvh * (A_GROUP // 2) + i
            o_ref[0, :, pair * LANES:(pair + 1) * LANES] = jnp.where(
                lo_mask, o[2 * i * n_q:(2 * i + 1) * n_q], o[(2 * i + 1) * n_q:(2 * i + 2) * n_q])


def _swa_prompt(qn, kn, z, bias, sink):
    b, t, _ = qn.shape
    v_col = (R_PROJ + A_DIM + KV_DIM) // KV_DIM
    seg = lambda s, col: pl.BlockSpec(
        (1, CHUNK, KV_DIM), lambda bi, c: (bi, jnp.maximum(c - WIN_CHUNKS + s, 0), col))
    n_seg = WIN_CHUNKS + 1
    return pl.pallas_call(
        functools.partial(_swa_kernel, n_seg=n_seg, banded=True),
        grid=(b, t // CHUNK),
        in_specs=[pl.BlockSpec((1, CHUNK, A_DIM), lambda bi, c: (bi, c, 0))]
                 + [seg(s, 0) for s in range(n_seg)] + [seg(s, v_col) for s in range(n_seg)]
                 + [pl.BlockSpec(bias.shape, lambda bi, c: (0, 0, 0)),
                    pl.BlockSpec((1, A_HEADS), lambda bi, c: (0, 0))],
        out_specs=pl.BlockSpec((1, CHUNK, A_DIM), lambda bi, c: (bi, c, 0)),
        out_shape=jax.ShapeDtypeStruct((b, t, A_DIM), F32),
        compiler_params=_params(("parallel", "parallel")),
        name="swa_prompt",
    )(qn, *([kn] * n_seg), *([z] * n_seg), bias, sink.reshape(1, A_HEADS))


def _swa_sample(qn, kn, v_new, k_cache, v_cache, bias, sink):
    b, t, _ = qn.shape
    n_cache = k_cache.shape[1]
    cur = lambda w: pl.BlockSpec((1, t, w), lambda bi, c: (bi, 0, 0))
    old = pl.BlockSpec((1, n_cache, KV_DIM), lambda bi, c: (bi, 0, 0))
    return pl.pallas_call(
        functools.partial(_swa_kernel, n_seg=2, banded=False),
        grid=(b, 1),
        in_specs=[cur(A_DIM), old, cur(KV_DIM), old, cur(KV_DIM),
                  pl.BlockSpec(bias.shape, lambda bi, c: (0, 0, 0)),
                  pl.BlockSpec((1, A_HEADS), lambda bi, c: (0, 0))],
        out_specs=cur(A_DIM),
        out_shape=jax.ShapeDtypeStruct((b, t, A_DIM), F32),
        compiler_params=_params(("parallel", "parallel")),
        name="swa_sample",
    )(qn, k_cache, kn, v_cache, v_new, bias, sink.reshape(1, A_HEADS))


def _mix_out_kernel(y_ref, bo_ref, g_ref, a_ref, x_ref, gng_ref, gnb_ref, e_ref, wo_ref, ln_ref, wq_ref,
                    xo_ref, h_ref, qq_ref):
    e = e_ref[...]
    y = y_ref[...]
    mu = _seg_sum(y, e) * (1.0 / R_HD)
    d = y - mu
    var = _seg_sum(d * d, e) * (1.0 / R_HD)
    yn = d * lax.rsqrt(var + GN_EPS) * gng_ref[...] + gnb_ref[...]
    mix_r = ((yn + bo_ref[...]) * g_ref[...]).astype(BF16)
    x = (x_ref[...] + _mm(mix_r, wo_ref[:R_DIM, :]) + _mm(a_ref[...].astype(BF16), wo_ref[R_DIM:, :]))
    xo_ref[...] = x
    ms = jnp.mean(x * x, axis=-1, keepdims=True)
    h = x * lax.rsqrt(ms + NORM_EPS) * ln_ref[...]
    h_ref[...] = h
    qq_ref[...] = _mm(h.astype(BF16), wq_ref[...])


def _mix_out(y, bonus, g, a_out, x, lp):
    n = x.shape[0]
    tm = min(n, 256)
    nq = PEER_HEADS * PK_DIM
    rows = lambda w: pl.BlockSpec((tm, w), lambda i: (i, 0))
    full = lambda s: pl.BlockSpec(s, lambda i: (0, 0))
    return pl.pallas_call(
        _mix_out_kernel,
        grid=(n // tm,),
        in_specs=[rows(R_DIM), rows(R_DIM), rows(R_DIM), rows(A_DIM), rows(D_MODEL),
                  full((1, R_DIM)), full((1, R_DIM)), full((R_DIM, R_DIM)),
                  full((D_MODEL, D_MODEL)), full((1, D_MODEL)), full((D_MODEL, nq))],
        out_specs=[rows(D_MODEL), rows(D_MODEL), rows(nq)],
        out_shape=[jax.ShapeDtypeStruct((n, D_MODEL), F32), jax.ShapeDtypeStruct((n, D_MODEL), F32),
                   jax.ShapeDtypeStruct((n, nq), F32)],
        compiler_params=_params(("parallel",)),
        name="mix_out_query",
    )(y, bonus, g, a_out, x, lp["gn_g"].reshape(1, -1), lp["gn_b"].reshape(1, -1), lp["e64"],
      lp["w_out"], lp["ln2_g"].reshape(1, -1), lp["w_pq"])


def _pick_rounds(s_ref, n_rows, emit):
    tb = s_ref.shape[1]
    rowid = lax.broadcasted_iota(I32, (n_rows, tb), 0)
    for rnd in range(PEER_TOPK):
        s = s_ref[...]
        m = jnp.max(s, axis=0, keepdims=True)
        idx = jnp.min(jnp.where(s == m, rowid, n_rows), axis=0, keepdims=True)
        hit = rowid == idx
        s_ref[...] = jnp.where(hit, -jnp.inf, s)
        emit(rnd, m, idx, hit)


def _topk_kernel(qq_ref, keys_ref, ei_ref, gate_ref, s_ref, sv_ref, si_ref, c_ref, ci_ref, ts_ref):
    tb = qq_ref.shape[0]
    for half in range(2):
        qh = qq_ref[:, half * PK_HALF:(half + 1) * PK_HALF]
        s_ref[...] = _x3(_mm_nt, keys_ref[half], qh)

        def emit1(rnd, m, idx, hit, half=half):
            sv_ref[half, rnd:rnd + 1, :] = m
            si_ref[half, rnd:rnd + 1, :] = idx

        _pick_rounds(s_ref, N_KEYS, emit1)

    row0 = 0
    for a, nb in enumerate(CAND_COLS):
        if nb == 1:
            break
        c_ref[row0:row0 + nb, :] = sv_ref[0, a:a + 1, :] + sv_ref[1, 0:nb, :]
        ci_ref[row0:row0 + nb, :] = si_ref[0, a:a + 1, :] * N_KEYS + si_ref[1, 0:nb, :]
        row0 += nb
    n_one = PEER_TOPK - a
    c_ref[row0:row0 + n_one, :] = sv_ref[0, a:, :] + sv_ref[1, 0:1, :]
    ci_ref[row0:row0 + n_one, :] = si_ref[0, a:, :] * N_KEYS + si_ref[1, 0:1, :]
    row0 += n_one
    c_ref[row0:, :] = jnp.full((N_CAND - row0, tb), -jnp.inf, F32)
    ci_ref[row0:, :] = jnp.zeros((N_CAND - row0, tb), I32)

    def emit2(rnd, m, idx, hit):
        ts_ref[rnd:rnd + 1, :] = m
        ei_ref[0, rnd:rnd + 1, :] = jnp.max(jnp.where(hit, ci_ref[...], -1), axis=0, keepdims=True)

    _pick_rounds(c_ref, N_CAND, emit2)
    ts = ts_ref[...]
    ex = jnp.exp(ts - ts[0:1, :])
    gate_ref[0] = ex / jnp.sum(ex, axis=0, keepdims=True)


def _topk(qq, sub_keys):
    n = qq.shape[0]
    tb = min(n, 512)
    out = pl.BlockSpec((1, PEER_TOPK, tb), lambda i, h: (h, 0, i))
    return pl.pallas_call(
        _topk_kernel,
        grid=(n // tb, PEER_HEADS),
        in_specs=[pl.BlockSpec((tb, PK_DIM), lambda i, h: (i, h)),
                  pl.BlockSpec((2, N_KEYS, PK_HALF), lambda i, h: (0, 0, 0))],
        out_specs=[out, out],
        out_shape=[jax.ShapeDtypeStruct((PEER_HEADS, PEER_TOPK, n), I32),
                   jax.ShapeDtypeStruct((PEER_HEADS, PEER_TOPK, n), F32)],
        scratch_shapes=[pltpu.VMEM((N_KEYS, tb), F32), pltpu.VMEM((2, PEER_TOPK, tb), F32),
                        pltpu.VMEM((2, PEER_TOPK, tb), I32), pltpu.VMEM((N_CAND, tb), F32),
                        pltpu.VMEM((N_CAND, tb), I32), pltpu.VMEM((PEER_TOPK, tb), F32)],
        compiler_params=_params(("parallel", "parallel")),
        name="peer_topk",
    )(qq, sub_keys)


def _expert_gather(tbl_hbm, idx_v, rows_v, sem, tt, g, buf):
    return pltpu.make_async_copy(tbl_hbm.at[idx_v[pl.ds(tt * N_SEL + g * SC_GROUP, SC_GROUP)]],
                                 rows_v.at[buf], sem.at[buf])


def _sc_tok_blk(n):
    return min(SC_TOK_BLK, n // SC_WORKERS)


def _sc_token_blocks(n, tbl_hbm, idx_v, rows_v, sem, load_block, compute, store_block):
    wid = lax.axis_index("s") * SC_CORES + lax.axis_index("c")
    npw = n // SC_WORKERS
    tb = _sc_tok_blk(n)
    steps = tb * SC_N_GROUPS

    @pl.loop(0, npw // tb)
    def _(blk):
        tok0 = wid * npw + blk * tb
        load_block(tok0)
        for s in range(SC_NBUF - 1):
            _expert_gather(tbl_hbm, idx_v, rows_v, sem, s // SC_N_GROUPS, s % SC_N_GROUPS, s).start()

        @pl.loop(0, steps, step=SC_NBUF)
        def _(s0):
            for b in range(SC_NBUF):
                s = s0 + b
                ahead = s + SC_NBUF - 1

                @pl.when(ahead < steps)
                def _():
                    _expert_gather(tbl_hbm, idx_v, rows_v, sem, ahead // SC_N_GROUPS, ahead % SC_N_GROUPS,
                                   (b + SC_NBUF - 1) % SC_NBUF).start()

                _expert_gather(tbl_hbm, idx_v, rows_v, sem, s // SC_N_GROUPS, s % SC_N_GROUPS, b).wait()
                compute(s // SC_N_GROUPS, s % SC_N_GROUPS, rows_v.at[b])

        store_block(tok0)


def _row_chunk(j):
    per_line = LANES // SC_LANES
    return j // per_line, pl.ds((j % per_line) * SC_LANES, SC_LANES)


def _expert_cost(n):
    elems = n * N_SEL * D_MODEL
    return pl.CostEstimate(flops=2 * elems, transcendentals=0, bytes_accessed=4 * elems)


def _sc_mesh():
    return plsc.VectorSubcoreMesh(core_axis_name="c", subcore_axis_name="s",
                                  num_cores=SC_CORES, num_subcores=SC_SUBCORES)


def _sc_hidden(eidx, h, table):
    n = h.shape[0]
    tb = _sc_tok_blk(n)

    @functools.partial(
        pl.kernel, out_type=jax.ShapeDtypeStruct((n * N_SEL,), F32), mesh=_sc_mesh(),
        compiler_params=pltpu.CompilerParams(needs_layout_passes=False),
        scratch_types=[pltpu.VMEM((tb * N_SEL,), I32), pltpu.VMEM((tb * D_MODEL,), F32),
                       pltpu.VMEM((SC_NBUF, SC_GROUP) + SC_ROW_TILE, F32), pltpu.VMEM((tb * N_SEL,), F32),
                       pltpu.SemaphoreType.DMA((SC_NBUF,))],
        cost_estimate=_expert_cost(n), name="peer_hidden_sc")
    def run(eidx_hbm, h_hbm, tbl_hbm, hid_hbm, idx_v, h_v, rows_v, hid_v, sem):
        lane = lax.iota(I32, SC_LANES)

        def load_block(tok0):
            pltpu.sync_copy(eidx_hbm.at[pl.ds(tok0 * N_SEL, tb * N_SEL)], idx_v)
            pltpu.sync_copy(h_hbm.at[pl.ds(tok0 * D_MODEL, tb * D_MODEL)], h_v)

        def compute(tt, g, rows):
            accs = []
            for sub in range(SC_GROUP // SC_RSUB):
                zero = tuple(jnp.zeros((SC_LANES,), F32) for _ in range(SC_RSUB))

                @plsc.parallel_loop(0, D_MODEL // SC_LANES, unroll=2, carry=zero)
                def part(j, acc):
                    hvj = h_v[pl.ds(tt * D_MODEL + j * SC_LANES, SC_LANES)]
                    return tuple(acc[r] + rows[(sub * SC_RSUB + r, *_row_chunk(j))] * hvj
                                 for r in range(SC_RSUB))

                accs.extend(part)
            tot = jnp.zeros((SC_LANES,), F32)
            for r in range(SC_GROUP):
                tot = jnp.where(lane == r, jnp.sum(accs[r]), tot)
            hid_v[pl.ds(tt * N_SEL + g * SC_GROUP, SC_GROUP)] = tot

        def store_block(tok0):
            pltpu.sync_copy(hid_v, hid_hbm.at[pl.ds(tok0 * N_SEL, tb * N_SEL)])

        _sc_token_blocks(n, tbl_hbm, idx_v, rows_v, sem, load_block, compute, store_block)

    return run(eidx.reshape(-1), h.reshape(-1), table).reshape(n, N_SEL)


def _sc_combine(eidx, coef, x, table):
    n = x.shape[0]
    tb = _sc_tok_blk(n)

    @functools.partial(
        pl.kernel, out_type=jax.ShapeDtypeStruct((n * D_MODEL,), F32), mesh=_sc_mesh(),
        compiler_params=pltpu.CompilerParams(needs_layout_passes=False),
        scratch_types=[pltpu.VMEM((tb * N_SEL,), I32), pltpu.VMEM((tb * N_SEL,), F32),
                       pltpu.VMEM((SC_NBUF, SC_GROUP) + SC_ROW_TILE, F32), pltpu.VMEM((tb * D_MODEL,), F32),
                       pltpu.SemaphoreType.DMA((SC_NBUF,))],
        cost_estimate=_expert_cost(n), name="peer_combine_sc")
    def run(eidx_hbm, c_hbm, x_hbm, tbl_hbm, out_hbm, idx_v, c_v, rows_v, out_v, sem):
        lane = lax.iota(I32, SC_LANES)

        def load_block(tok0):
            pltpu.sync_copy(eidx_hbm.at[pl.ds(tok0 * N_SEL, tb * N_SEL)], idx_v)
            pltpu.sync_copy(c_hbm.at[pl.ds(tok0 * N_SEL, tb * N_SEL)], c_v)
            pltpu.sync_copy(x_hbm.at[pl.ds(tok0 * D_MODEL, tb * D_MODEL)], out_v)

        def compute(tt, g, rows):
            cvec = c_v[pl.ds(tt * N_SEL + g * SC_GROUP, SC_GROUP)]
            coefs = [jnp.sum(jnp.where(lane == r, cvec, 0.0)) for r in range(SC_GROUP)]

            @plsc.parallel_loop(0, D_MODEL // SC_LANES, unroll=2)
            def _(j):
                sl = pl.ds(tt * D_MODEL + j * SC_LANES, SC_LANES)
                acc = out_v[sl]
                for r in range(SC_GROUP):
                    acc = acc + rows[(r, *_row_chunk(j))] * coefs[r]
                out_v[sl] = acc

        def store_block(tok0):
            pltpu.sync_copy(out_v, out_hbm.at[pl.ds(tok0 * D_MODEL, tb * D_MODEL)])

        _sc_token_blocks(n, tbl_hbm, idx_v, rows_v, sem, load_block, compute, store_block)

    return run(eidx.reshape(-1), coef.reshape(-1), x.reshape(-1), table).reshape(n, D_MODEL)


def _gate_act_kernel(hid_ref, gate_ref, o_ref):
    hid = hid_ref[...]
    o_ref[...] = gate_ref[...] * (0.5 * hid * (1.0 + lax.erf(hid * np.float32(np.sqrt(0.5)))))


def _gate_act(hid, gate):
    n = hid.shape[0]
    tm = min(n, 2048)
    rows = pl.BlockSpec((tm, N_SEL), lambda i: (i, 0))
    return pl.pallas_call(
        _gate_act_kernel, grid=(n // tm,), in_specs=[rows, rows], out_specs=rows,
        out_shape=jax.ShapeDtypeStruct((n, N_SEL), F32), compiler_params=_params(("parallel",)),
        name="peer_gate_act",
    )(hid, gate)


def _t5_bucket(rel):
    nb = NUM_BUCKETS // 2
    max_exact = nb // 2
    ret = jnp.where(rel > 0, nb, 0)
    n = jnp.abs(rel)
    nf = jnp.maximum(n, 1).astype(F32)
    large = max_exact + (jnp.log(nf / max_exact) / math.log(MAX_DISTANCE / max_exact)
                         * (nb - max_exact)).astype(I32)
    large = jnp.minimum(large, nb - 1)
    return ret + jnp.where(n < max_exact, n, large)


def _rel_bias(rel_bias, n_q, n_k, n_before):
    rel = (jnp.arange(n_k)[None, :] - n_before) - jnp.arange(n_q)[:, None]
    return jnp.transpose(rel_bias[_t5_bucket(rel)].astype(F32), (2, 0, 1))


def _state_to_pairs(wkv):
    b = wkv.shape[0]
    s = wkv.reshape(b, N_PAIRS, 2, R_HD, R_HD)
    z = jnp.zeros_like(s[:, :, 0])
    top = jnp.concatenate([s[:, :, 0], z], axis=-1)
    bot = jnp.concatenate([z, s[:, :, 1]], axis=-1)
    return jnp.concatenate([top, bot], axis=-2)


def _pairs_to_state(s_bd):
    b = s_bd.shape[0]
    return jnp.stack([s_bd[:, :, :R_HD, :R_HD], s_bd[:, :, R_HD:, R_HD:]], axis=2).reshape(
        b, R_HEADS, R_HD, R_HD)


def _layer_dense(x, shift0, wkv0, kv_cache, bias, lp):
    b, t, _ = x.shape
    n = b * t
    z = _norm_matmul(x.reshape(n, D_MODEL), lp["ln1_g"].reshape(1, -1), lp["w_in"], 512).reshape(b, t, IN_COLS)
    r, k2, v, kk, bb, lw, g, bonus, qn, kn = _prep(z, shift0, lp)
    v_new = z[:, :, R_PROJ + A_DIM + KV_DIM:]
    if t % CHUNK:
        pad = lambda a: jnp.pad(a, ((0, 0), (0, CHUNK - t % CHUNK), (0, 0)))
        y, s_fin = _chunk_scan(*(pad(a) for a in (r, k2, v, kk, bb, lw)), _state_to_pairs(wkv0))
        y = y[:, :t]
    else:
        y, s_fin = _chunk_scan(r, k2, v, kk, bb, lw, _state_to_pairs(wkv0))
    if kv_cache is None:
        a_out = _swa_prompt(qn, kn, z, bias, lp["sink"])
    else:
        a_out = _swa_sample(qn, kn, v_new, kv_cache[0].reshape(b, -1, KV_DIM),
                            kv_cache[1].reshape(b, -1, KV_DIM), bias, lp["sink"])
    flat = lambda a: a.reshape(n, a.shape[-1])
    x1, h, qq = _mix_out(flat(y), flat(bonus), flat(g), flat(a_out), flat(x), lp)
    eidx, gate = _topk(qq, lp["sub_keys"])
    sel = lambda a: jnp.transpose(a, (2, 0, 1)).reshape(n, N_SEL)
    return ((sel(eidx), sel(gate), h, x1),
            (_pairs_to_state(s_fin), z[:, -1, :R_PROJ],
             kn.reshape(b, t, A_KV, A_HD), v_new.reshape(b, t, A_KV, A_HD)))


def _layer_experts(ops, lp, after=None):
    eidx, gate, h, x1 = ops
    if after is not None:
        h, after = lax.optimization_barrier((h, after))
    coef = _gate_act(_sc_hidden(eidx, h, lp["peer_u"]), gate)
    return _sc_combine(eidx, coef, x1, lp["peer_v"]), after


def kernel(x_prompt, x_sample, state_rwkv_wkv, state_rwkv_shift, cache_swa_k, cache_swa_v, ln1_g, w_in,
           mu_shift, w0, w2, a0, a2, g2, k_k, k_a, r_k, gn_g, gn_b, q_norm_g, k_norm_g, attn_sink,
           rel_bias, w_out, ln2_g, w_pq, sub_keys, peer_u, peer_v):
    depth = w_in.shape[0]
    b_p, s_p = x_prompt.shape[:2]
    b_s, t_s = x_sample.shape[:2]
    n_cache = cache_swa_k.shape[2]
    n_keep = min(WINDOW, s_p)
    bias_p = _rel_bias(rel_bias, CHUNK, (WIN_CHUNKS + 1) * CHUNK, WIN_CHUNKS * CHUNK)
    bias_s = _rel_bias(rel_bias, t_s, n_cache + t_s, n_cache)
    head_id = jnp.arange(R_DIM) // R_HD
    e64 = (head_id[:, None] == head_id[None, :]).astype(BF16)
    zpad = jnp.zeros((LANES - W_LORA, R_DIM), F32)
    n_groups = PROMPT_GROUPS if b_p % PROMPT_GROUPS == 0 else 1
    xp, xs = jnp.split(x_prompt, n_groups, axis=0), x_sample
    outs = [[] for _ in range(8)]
    for l in range(depth):
        lp = {
            "ln1_g": ln1_g[l], "w_in": w_in[l].astype(BF16), "mu": mu_shift[l], "w0": w0[l], "a0": a0[l],
            "k_k": k_k[l], "k_a": k_a[l], "r_k": r_k[l].reshape(-1), "gn_g": gn_g[l], "gn_b": gn_b[l],
            "w2p": jnp.concatenate([w2[l], zpad], axis=0).astype(BF16),
            "a2p": jnp.concatenate([zpad, a2[l]], axis=0).astype(BF16),
            "g2": g2[l].astype(BF16), "e64": e64,
            "q_gain": jnp.tile(q_norm_g[l], A_HEADS), "k_gain": jnp.tile(k_norm_g[l], A_KV),
            "sink": attn_sink[l].astype(F32), "w_out": w_out[l].astype(BF16), "ln2_g": ln2_g[l],
            "w_pq": w_pq[l].astype(BF16), "sub_keys": sub_keys[l], "peer_u": peer_u[l].reshape((-1,) + SC_ROW_TILE), "peer_v": peer_v[l].reshape((-1,) + SC_ROW_TILE),
        }
        parts = [_layer_dense(xg, jnp.zeros((xg.shape[0], R_PROJ), F32),
                              jnp.zeros((xg.shape[0], R_HEADS, R_HD, R_HD), F32), None, bias_p, lp) for xg in xp]
        wkv_p, sh_p, k_p, v_p = (jnp.concatenate([pt[1][i] for pt in parts], axis=0) for i in range(4))
        outs_x = []
        for ops, _ in parts:
            if outs_x:
                xo, outs_x[-1] = _layer_experts(ops, lp, after=outs_x[-1])
            else:
                xo, _ = _layer_experts(ops, lp)
            outs_x.append(xo)
        xp = [xo.reshape(xg.shape) for xo, xg in zip(outs_x, xp)]
        ops_s, (wkv_s, sh_s, k_s, v_s) = _layer_dense(
            xs, state_rwkv_shift[l], state_rwkv_wkv[l].astype(F32), (cache_swa_k[l], cache_swa_v[l]),
            bias_s, lp)
        xs = _layer_experts(ops_s, lp)[0].reshape(xs.shape)
        for lst, val in zip(outs, (wkv_p, sh_p, k_p[:, s_p - n_keep:], v_p[:, s_p - n_keep:],
                                   wkv_s, sh_s, k_s, v_s)):
            lst.append(val)
    return (jnp.concatenate(xp, axis=0), xs) + tuple(jnp.stack(o) for o in outs)
```

```python
import functools
import math

import numpy as np
import jax
import jax.numpy as jnp
from jax import lax
from jax.experimental import pallas as pl
from jax.experimental.pallas import tpu as pltpu
from jax.experimental.pallas import tpu_sc as plsc

F32 = jnp.float32
BF16 = jnp.bfloat16
I32 = jnp.int32

D_MODEL = 1024
CHUNK = 64
R_HEADS = 8
R_HD = 64
R_DIM = R_HEADS * R_HD
W_LORA = 64
A_LORA = 64
G_LORA = 128
R_PROJ = 3 * R_DIM + W_LORA + A_LORA + G_LORA
A_HEADS = 8
A_KV = 2
A_GROUP = A_HEADS // A_KV
A_HD = 64
A_DIM = A_HEADS * A_HD
KV_DIM = A_KV * A_HD
IN_COLS = R_PROJ + A_DIM + 2 * KV_DIM
WINDOW = 128
WIN_CHUNKS = WINDOW // CHUNK
NUM_BUCKETS = 32
MAX_DISTANCE = 128
PEER_HEADS = 8
N_KEYS = 128
PK_DIM = 256
PK_HALF = PK_DIM // 2
PEER_TOPK = 16
N_SEL = PEER_HEADS * PEER_TOPK
NORM_EPS = 1e-6
GN_EPS = 64e-5
NEG_INF = -1e30

LANES = 128
N_PAIRS = R_DIM // LANES
VMEM_LIMIT = 48 * 1024 * 1024
CAND_COLS = (16, 8, 8, 4, 4, 4, 4, 4, 1, 1, 1, 1, 1, 1, 1, 1)
assert all(nb >= PEER_TOPK // (a + 1) for a, nb in enumerate(CAND_COLS))
N_CAND = 64
assert sum(CAND_COLS) <= N_CAND
SC_CORES = 2
SC_SUBCORES = 16
SC_LANES = 16
SC_WORKERS = SC_CORES * SC_SUBCORES
SC_TOK_BLK = 32
SC_MIN_TOK_BLK = 8
SC_GROUP = SC_LANES
SC_N_GROUPS = N_SEL // SC_GROUP
SC_RSUB = 16
SC_NBUF = 4
SC_ROW_TILE = (D_MODEL // LANES, LANES)
TC_EXPERT_BLK = 64
TC_SHARE_16THS = 2
PROMPT_GROUPS = 8


def _params(sem):
    return pltpu.CompilerParams(dimension_semantics=sem, vmem_limit_bytes=VMEM_LIMIT)


def _mm(a, b):
    return jnp.dot(a, b, preferred_element_type=F32)


def _mm_nt(a, b):
    return lax.dot_general(a, b, (((1,), (1,)), ((), ())), preferred_element_type=F32)


def _mm_tn(a, b):
    return lax.dot_general(a, b, (((0,), (0,)), ((), ())), preferred_element_type=F32)


def _split2(a):
    hi = a.astype(BF16)
    return hi, (a - hi.astype(F32)).astype(BF16)


def _split3(a):
    hi = a.astype(BF16)
    r = a - hi.astype(F32)
    mid = r.astype(BF16)
    return hi, mid, (r - mid.astype(F32)).astype(BF16)


def _x3(mm, a, b):
    ah, al = _split2(a)
    bh, bl = _split2(b)
    return mm(ah, bh) + mm(ah, bl) + mm(al, bh)


def _exact_lhs(mm, a_bf16, b):
    b0, b1, b2 = _split3(b)
    return mm(a_bf16, b0) + mm(a_bf16, b1) + mm(a_bf16, b2)


def _seg_sum(x, e_bf16):
    x0, x1, x2 = _split3(x)
    return _mm(x0, e_bf16) + _mm(x1, e_bf16) + _mm(x2, e_bf16)


def _sigmoid(x):
    return 1.0 / (1.0 + jnp.exp(-x))


def _norm_matmul_kernel(x_ref, g_ref, w_ref, o_ref, h_ref):
    @pl.when(pl.program_id(1) == 0)
    def _():
        x = x_ref[...]
        ms = jnp.mean(x * x, axis=-1, keepdims=True)
        h_ref[...] = (x * lax.rsqrt(ms + NORM_EPS) * g_ref[...]).astype(BF16)

    o_ref[...] = _mm(h_ref[...], w_ref[...])


def _norm_matmul(x, g, w_bf16, tn):
    n, k = x.shape
    m = w_bf16.shape[1]
    tm = min(n, 512)
    return pl.pallas_call(
        _norm_matmul_kernel,
        grid=(n // tm, m // tn),
        in_specs=[pl.BlockSpec((tm, k), lambda i, j: (i, 0)),
                  pl.BlockSpec((1, k), lambda i, j: (0, 0)),
                  pl.BlockSpec((k, tn), lambda i, j: (0, j))],
        out_specs=pl.BlockSpec((tm, tn), lambda i, j: (i, j)),
        out_shape=jax.ShapeDtypeStruct((n, m), F32),
        scratch_shapes=[pltpu.VMEM((tm, k), BF16)],
        compiler_params=_params(("parallel", "arbitrary")),
        name="norm_inproj",
    )(x, g, w_bf16)


def _prep_kernel(z_ref, zp_ref, sh_ref, mu_ref, w0_ref, a0_ref, kk_ref, ka_ref, rk_ref,
                 w2_ref, a2_ref, g2_ref, e_ref, qg_ref, kg_ref,
                 r_o, k_o, v_o, kk_o, b_o, lw_o, g_o, bo_o, qn_o, kn_o):
    i = pl.program_id(1)
    zt = z_ref[0]
    tp = zt.shape[0]
    zr = zt[:, :R_PROJ]
    prev_row = jnp.where(i == 0, sh_ref[0], zp_ref[0][7:8, :R_PROJ])
    row = lax.broadcasted_iota(I32, (tp, 1), 0)
    prev = jnp.where(row == 0, prev_row, pltpu.roll(zr, 1, axis=0))
    zs = zr + (prev - zr) * mu_ref[...]
    r = zs[:, 0:R_DIM]
    k = zs[:, R_DIM:2 * R_DIM]
    v = zs[:, 2 * R_DIM:3 * R_DIM]
    lo = zs[:, 3 * R_DIM:3 * R_DIM + W_LORA + A_LORA]
    g_lo = zs[:, 3 * R_DIM + W_LORA + A_LORA:R_PROJ]
    e = e_ref[...]
    w_in = -(w0_ref[...] + _mm(jnp.tanh(lo).astype(BF16), w2_ref[...]))
    softplus = jnp.maximum(w_in, 0.0) + jnp.log1p(jnp.exp(-jnp.abs(w_in)))
    w_log = -softplus - 0.5
    lw_o[0] = -jnp.exp(w_log)
    a = _sigmoid(a0_ref[...] + _mm(lo.astype(BF16), a2_ref[...]))
    g_o[0] = _mm(_sigmoid(g_lo).astype(BF16), g2_ref[...])
    kk = k * kk_ref[...]
    kk = kk / jnp.maximum(jnp.sqrt(_seg_sum(kk * kk, e)), 1e-12)
    k2 = k * (1.0 + (a - 1.0) * ka_ref[...])
    r_o[0] = r
    k_o[0] = k2
    v_o[0] = v
    kk_o[0] = kk
    b_o[0] = kk * a
    bo_o[0] = _seg_sum(r * k2 * rk_ref[...], e) * v
    q = zt[:, R_PROJ:R_PROJ + A_DIM]
    qn_o[0] = q * lax.rsqrt(_seg_sum(q * q, e) * (1.0 / A_HD) + NORM_EPS) * qg_ref[...]
    kx = zt[:, R_PROJ + A_DIM:R_PROJ + A_DIM + KV_DIM]
    e_kv = e[:KV_DIM, :KV_DIM]
    kn_o[0] = kx * lax.rsqrt(_seg_sum(kx * kx, e_kv) * (1.0 / A_HD) + NORM_EPS) * kg_ref[...]


def _prep(z, shift0, lp):
    b, t, _ = z.shape
    tp = min(t, 256)
    row = lambda a: a.reshape(1, -1)
    vec = lambda n: pl.BlockSpec((1, n), lambda bi, i: (0, 0))
    full = lambda s: pl.BlockSpec(s, lambda bi, i: (0, 0))
    wide = pl.BlockSpec((1, tp, R_DIM), lambda bi, i: (bi, i, 0))
    outs = [jax.ShapeDtypeStruct((b, t, R_DIM), F32)] * 9 + [jax.ShapeDtypeStruct((b, t, KV_DIM), F32)]
    return pl.pallas_call(
        _prep_kernel,
        grid=(b, t // tp),
        in_specs=[pl.BlockSpec((1, tp, IN_COLS), lambda bi, i: (bi, i, 0)),
                  pl.BlockSpec((1, 8, IN_COLS), lambda bi, i: (bi, jnp.maximum(i * (tp // 8) - 1, 0), 0)),
                  pl.BlockSpec((1, 1, R_PROJ), lambda bi, i: (bi, 0, 0)),
                  vec(R_PROJ), vec(R_DIM), vec(R_DIM), vec(R_DIM), vec(R_DIM), vec(R_DIM),
                  full((LANES, R_DIM)), full((LANES, R_DIM)), full((G_LORA, R_DIM)),
                  full((R_DIM, R_DIM)), vec(A_DIM), vec(KV_DIM)],
        out_specs=[wide] * 9 + [pl.BlockSpec((1, tp, KV_DIM), lambda bi, i: (bi, i, 0))],
        out_shape=outs,
        compiler_params=_params(("parallel", "parallel")),
        name="rwkv_prep",
    )(z, z, shift0.reshape(b, 1, R_PROJ), row(lp["mu"]), row(lp["w0"]), row(lp["a0"]), row(lp["k_k"]),
      row(lp["k_a"]), row(lp["r_k"]), lp["w2p"], lp["a2p"], lp["g2"], lp["e64"], row(lp["q_gain"]),
      row(lp["k_gain"]))


def _stack_heads(x, lo_mask):
    return jnp.concatenate([jnp.where(lo_mask, x, 0.0), jnp.where(lo_mask, 0.0, x)], axis=0)


def _chunk_kernel(r_ref, k_ref, v_ref, kk_ref, b_ref, lw_ref, s0_ref, y_ref, sf_ref, s_ref):
    c = pl.program_id(1)
    L = CHUNK

    @pl.when(c == 0)
    def _():
        s_ref[...] = s0_ref[0]

    lane = lax.broadcasted_iota(I32, (1, LANES), 1)
    lo_mask = lane < R_HD
    rr = lax.broadcasted_iota(I32, (L, 2 * L), 0)
    cc = lax.broadcasted_iota(I32, (L, 2 * L), 1)
    cc = jnp.where(cc >= L, cc - L, cc)
    strict = rr > cc
    incl = rr >= cc
    t_r = lax.broadcasted_iota(I32, (L, L), 0)
    t_c = lax.broadcasted_iota(I32, (L, L), 1)
    tri = (t_r >= t_c).astype(BF16)
    col2 = lax.broadcasted_iota(I32, (L, 2 * L), 1) < L
    eye_r = lax.broadcasted_iota(I32, (2 * L, 2 * L), 0)
    eye_c = lax.broadcasted_iota(I32, (2 * L, 2 * L), 1)
    eye = (eye_r == eye_c).astype(F32)

    pairs = range(N_PAIRS)
    sls = [slice(p * LANES, (p + 1) * LANES) for p in pairs]
    r = [r_ref[0][:, sl] for sl in sls]
    k = [k_ref[0][:, sl] for sl in sls]
    v = [v_ref[0][:, sl] for sl in sls]
    kk = [kk_ref[0][:, sl] for sl in sls]
    bb = [b_ref[0][:, sl] for sl in sls]
    lw = [lw_ref[0][:, sl] for sl in sls]
    s0 = [s_ref[p] for p in pairs]

    cl = [_exact_lhs(_mm, tri, lw[p]) for p in pairs]
    cl_last = [cl[p][L - 1:L, :] for p in pairs]
    e_neg = [jnp.exp(-cl[p]) for p in pairs]
    e_last = [jnp.exp(cl_last[p] - cl[p]) for p in pairs]
    rt = [r[p] * jnp.exp(cl[p]) for p in pairs]
    at = [kk[p] * jnp.exp(cl[p] - lw[p]) for p in pairs]

    gm = [_x3(_mm_nt, jnp.concatenate([at[p], rt[p]], axis=0),
              jnp.concatenate([_stack_heads(k[p] * e_neg[p], lo_mask), _stack_heads(bb[p] * e_neg[p], lo_mask)],
                              axis=0)) for p in pairs]
    mk = [jnp.where(strict, gm[p][:L, :2 * L], 0.0) for p in pairs]
    mb = [jnp.where(strict, gm[p][:L, 2 * L:], 0.0) for p in pairs]
    hk = [jnp.where(incl, gm[p][L:, :2 * L], 0.0) for p in pairs]
    hb = [jnp.where(incl, gm[p][L:, 2 * L:], 0.0) for p in pairs]

    nil = [-jnp.concatenate([jnp.where(col2, mb[p], 0.0), jnp.where(col2, 0.0, mb[p])], axis=0) for p in pairs]
    tinv = [eye + nil[p] for p in pairs]
    nil = [_x3(_mm, nil[p], nil[p]) for p in pairs]
    for it in range(5):
        tinv = [tinv[p] + _x3(_mm, nil[p], tinv[p]) for p in pairs]
        if it < 4:
            nil = [_x3(_mm, nil[p], nil[p]) for p in pairs]

    vs = [_stack_heads(v[p], lo_mask) for p in pairs]
    rhs_u = [-(_x3(_mm_nt, at[p], s0[p]) + _x3(_mm, mk[p], vs[p])) for p in pairs]
    us = [_x3(_mm, tinv[p], _stack_heads(rhs_u[p], lo_mask)) for p in pairs]
    for p in pairs:
        y_ref[0, :, sls[p]] = _x3(_mm_nt, rt[p], s0[p]) + _x3(_mm, hk[p], vs[p]) + _x3(_mm, hb[p], us[p])
    for p in pairs:
        s_ref[p] = (s0[p] * jnp.exp(cl_last[p])
                    + _x3(_mm_tn, vs[p], _stack_heads(k[p] * e_last[p], lo_mask))
                    + _x3(_mm_tn, us[p], _stack_heads(bb[p] * e_last[p], lo_mask)))

    @pl.when(c == pl.num_programs(1) - 1)
    def _():
        sf_ref[0] = s_ref[...]


def _chunk_scan(r, k, v, kk, bb, lw, s0_bd):
    b, t, _ = r.shape
    wide = pl.BlockSpec((1, CHUNK, R_DIM), lambda bi, c: (bi, c, 0))
    st = pl.BlockSpec((1, N_PAIRS, LANES, LANES), lambda bi, c: (bi, 0, 0, 0))
    return pl.pallas_call(
        _chunk_kernel,
        grid=(b, t // CHUNK),
        in_specs=[wide] * 6 + [st],
        out_specs=[wide, st],
        out_shape=[jax.ShapeDtypeStruct((b, t, R_DIM), F32),
                   jax.ShapeDtypeStruct((b, N_PAIRS, LANES, LANES), F32)],
        scratch_shapes=[pltpu.VMEM((N_PAIRS, LANES, LANES), F32)],
        compiler_params=_params(("parallel", "arbitrary")),
        name="rwkv_chunk",
    )(r, k, v, kk, bb, lw, s0_bd)


def _swa_kernel(*refs, n_seg, banded):
    q_ref = refs[0]
    k_refs = refs[1:1 + n_seg]
    v_refs = refs[1 + n_seg:1 + 2 * n_seg]
    bias_ref, sink_ref, o_ref = refs[1 + 2 * n_seg:]
    c = pl.program_id(1)
    q = q_ref[0]
    kcat = jnp.concatenate([kr[0] for kr in k_refs], axis=0)
    vcat = jnp.concatenate([vr[0] for vr in v_refs], axis=0)
    n_k = kcat.shape[0]
    lane = lax.broadcasted_iota(I32, (1, LANES), 1)
    lo_mask = lane < A_HD
    k_sw = pltpu.roll(kcat, A_HD, axis=1)
    v_sw = pltpu.roll(vcat, A_HD, axis=1)
    k_dup = [jnp.where(lo_mask, kcat, k_sw).astype(BF16), jnp.where(lo_mask, k_sw, kcat).astype(BF16)]
    v_dup = [jnp.where(lo_mask, vcat, v_sw).astype(BF16), jnp.where(lo_mask, v_sw, vcat).astype(BF16)]
    if banded:
        key_chunk = c - WIN_CHUNKS + lax.broadcasted_iota(I32, (1, n_k), 1) // CHUNK
        valid = key_chunk >= 0
    n_q = q.shape[0]
    hi_mask = jnp.logical_not(lo_mask)
    for kvh in range(A_KV):
        heads = range(kvh * A_GROUP, (kvh + 1) * A_GROUP)
        qs = jnp.concatenate(
            [jnp.where(lo_mask if hq % 2 == 0 else hi_mask, q[:, (hq // 2) * LANES:(hq // 2 + 1) * LANES], 0.0)
             for hq in heads], axis=0).astype(BF16)
        bias = jnp.concatenate([bias_ref[hq] for hq in heads], axis=0)
        sink = jnp.concatenate([jnp.broadcast_to(sink_ref[0:1, hq:hq + 1], (n_q, 1)) for hq in heads], axis=0)
        s = _mm_nt(qs, k_dup[kvh]) * (A_HD ** -0.5) + bias
        if banded:
            s = jnp.where(valid, s, NEG_INF)
        m = jnp.maximum(jnp.max(s, axis=-1, keepdims=True), sink)
        pr = jnp.exp(s - m)
        den = jnp.sum(pr, axis=-1, keepdims=True) + jnp.exp(sink - m)
        o = _mm((pr / den).astype(BF16), v_dup[kvh])
        for i in range(A_GROUP // 2):
            pair = kvh * (A_GROUP // 2) + i
            o_ref[0, :, pair * LANES:(pair + 1) * LANES] = jnp.where(
                lo_mask, o[2 * i * n_q:(2 * i + 1) * n_q], o[(2 * i + 1) * n_q:(2 * i + 2) * n_q])


def _swa_prompt(qn, kn, z, bias, sink):
    b, t, _ = qn.shape
    v_col = (R_PROJ + A_DIM + KV_DIM) // KV_DIM
    seg = lambda s, col: pl.BlockSpec(
        (1, CHUNK, KV_DIM), lambda bi, c: (bi, jnp.maximum(c - WIN_CHUNKS + s, 0), col))
    n_seg = WIN_CHUNKS + 1
    return pl.pallas_call(
        functools.partial(_swa_kernel, n_seg=n_seg, banded=True),
        grid=(b, t // CHUNK),
        in_specs=[pl.BlockSpec((1, CHUNK, A_DIM), lambda bi, c: (bi, c, 0))]
                 + [seg(s, 0) for s in range(n_seg)] + [seg(s, v_col) for s in range(n_seg)]
                 + [pl.BlockSpec(bias.shape, lambda bi, c: (0, 0, 0)),
                    pl.BlockSpec((1, A_HEADS), lambda bi, c: (0, 0))],
        out_specs=pl.BlockSpec((1, CHUNK, A_DIM), lambda bi, c: (bi, c, 0)),
        out_shape=jax.ShapeDtypeStruct((b, t, A_DIM), F32),
        compiler_params=_params(("parallel", "parallel")),
        name="swa_prompt",
    )(qn, *([kn] * n_seg), *([z] * n_seg), bias, sink.reshape(1, A_HEADS))


def _swa_sample(qn, kn, v_new, k_cache, v_cache, bias, sink):
    b, t, _ = qn.shape
    n_cache = k_cache.shape[1]
    cur = lambda w: pl.BlockSpec((1, t, w), lambda bi, c: (bi, 0, 0))
    old = pl.BlockSpec((1, n_cache, KV_DIM), lambda bi, c: (bi, 0, 0))
    return pl.pallas_call(
        functools.partial(_swa_kernel, n_seg=2, banded=False),
        grid=(b, 1),
        in_specs=[cur(A_DIM), old, cur(KV_DIM), old, cur(KV_DIM),
                  pl.BlockSpec(bias.shape, lambda bi, c: (0, 0, 0)),
                  pl.BlockSpec((1, A_HEADS), lambda bi, c: (0, 0))],
        out_specs=cur(A_DIM),
        out_shape=jax.ShapeDtypeStruct((b, t, A_DIM), F32),
        compiler_params=_params(("parallel", "parallel")),
        name="swa_sample",
    )(qn, k_cache, kn, v_cache, v_new, bias, sink.reshape(1, A_HEADS))


def _mix_out_kernel(y_ref, bo_ref, g_ref, a_ref, x_ref, gng_ref, gnb_ref, e_ref, wo_ref, ln_ref, wq_ref,
                    xo_ref, h_ref, qq_ref):
    e = e_ref[...]
    y = y_ref[...]
    mu = _seg_sum(y, e) * (1.0 / R_HD)
    d = y - mu
    var = _seg_sum(d * d, e) * (1.0 / R_HD)
    yn = d * lax.rsqrt(var + GN_EPS) * gng_ref[...] + gnb_ref[...]
    mix_r = ((yn + bo_ref[...]) * g_ref[...]).astype(BF16)
    x = (x_ref[...] + _mm(mix_r, wo_ref[:R_DIM, :]) + _mm(a_ref[...].astype(BF16), wo_ref[R_DIM:, :]))
    xo_ref[...] = x
    ms = jnp.mean(x * x, axis=-1, keepdims=True)
    h = x * lax.rsqrt(ms + NORM_EPS) * ln_ref[...]
    h_ref[...] = h
    qq_ref[...] = _mm(h.astype(BF16), wq_ref[...])


def _mix_out(y, bonus, g, a_out, x, lp):
    n = x.shape[0]
    tm = min(n, 256)
    nq = PEER_HEADS * PK_DIM
    rows = lambda w: pl.BlockSpec((tm, w), lambda i: (i, 0))
    full = lambda s: pl.BlockSpec(s, lambda i: (0, 0))
    return pl.pallas_call(
        _mix_out_kernel,
        grid=(n // tm,),
        in_specs=[rows(R_DIM), rows(R_DIM), rows(R_DIM), rows(A_DIM), rows(D_MODEL),
                  full((1, R_DIM)), full((1, R_DIM)), full((R_DIM, R_DIM)),
                  full((D_MODEL, D_MODEL)), full((1, D_MODEL)), full((D_MODEL, nq))],
        out_specs=[rows(D_MODEL), rows(D_MODEL), rows(nq)],
        out_shape=[jax.ShapeDtypeStruct((n, D_MODEL), F32), jax.ShapeDtypeStruct((n, D_MODEL), F32),
                   jax.ShapeDtypeStruct((n, nq), F32)],
        compiler_params=_params(("parallel",)),
        name="mix_out_query",
    )(y, bonus, g, a_out, x, lp["gn_g"].reshape(1, -1), lp["gn_b"].reshape(1, -1), lp["e64"],
      lp["w_out"], lp["ln2_g"].reshape(1, -1), lp["w_pq"])


def _pick_rounds(s_ref, n_rows, emit):
    tb = s_ref.shape[1]
    rowid = lax.broadcasted_iota(I32, (n_rows, tb), 0)
    for rnd in range(PEER_TOPK):
        s = s_ref[...]
        m = jnp.max(s, axis=0, keepdims=True)
        idx = jnp.min(jnp.where(s == m, rowid, n_rows), axis=0, keepdims=True)
        hit = rowid == idx
        s_ref[...] = jnp.where(hit, -jnp.inf, s)
        emit(rnd, m, idx, hit)


def _topk_kernel(qq_ref, keys_ref, ei_ref, gate_ref, s_ref, sv_ref, si_ref, c_ref, ci_ref, ts_ref):
    tb = qq_ref.shape[0]
    for half in range(2):
        qh = qq_ref[:, half * PK_HALF:(half + 1) * PK_HALF]
        s_ref[...] = _x3(_mm_nt, keys_ref[half], qh)

        def emit1(rnd, m, idx, hit, half=half):
            sv_ref[half, rnd:rnd + 1, :] = m
            si_ref[half, rnd:rnd + 1, :] = idx

        _pick_rounds(s_ref, N_KEYS, emit1)

    row0 = 0
    for a, nb in enumerate(CAND_COLS):
        if nb == 1:
            break
        c_ref[row0:row0 + nb, :] = sv_ref[0, a:a + 1, :] + sv_ref[1, 0:nb, :]
        ci_ref[row0:row0 + nb, :] = si_ref[0, a:a + 1, :] * N_KEYS + si_ref[1, 0:nb, :]
        row0 += nb
    n_one = PEER_TOPK - a
    c_ref[row0:row0 + n_one, :] = sv_ref[0, a:, :] + sv_ref[1, 0:1, :]
    ci_ref[row0:row0 + n_one, :] = si_ref[0, a:, :] * N_KEYS + si_ref[1, 0:1, :]
    row0 += n_one
    c_ref[row0:, :] = jnp.full((N_CAND - row0, tb), -jnp.inf, F32)
    ci_ref[row0:, :] = jnp.zeros((N_CAND - row0, tb), I32)

    def emit2(rnd, m, idx, hit):
        ts_ref[rnd:rnd + 1, :] = m
        ei_ref[0, rnd:rnd + 1, :] = jnp.max(jnp.where(hit, ci_ref[...], -1), axis=0, keepdims=True)

    _pick_rounds(c_ref, N_CAND, emit2)
    ts = ts_ref[...]
    ex = jnp.exp(ts - ts[0:1, :])
    gate_ref[0] = ex / jnp.sum(ex, axis=0, keepdims=True)


def _topk(qq, sub_keys):
    n = qq.shape[0]
    tb = min(n, 512)
    out = pl.BlockSpec((1, PEER_TOPK, tb), lambda i, h: (h, 0, i))
    return pl.pallas_call(
        _topk_kernel,
        grid=(n // tb, PEER_HEADS),
        in_specs=[pl.BlockSpec((tb, PK_DIM), lambda i, h: (i, h)),
                  pl.BlockSpec((2, N_KEYS, PK_HALF), lambda i, h: (0, 0, 0))],
        out_specs=[out, out],
        out_shape=[jax.ShapeDtypeStruct((PEER_HEADS, PEER_TOPK, n), I32),
                   jax.ShapeDtypeStruct((PEER_HEADS, PEER_TOPK, n), F32)],
        scratch_shapes=[pltpu.VMEM((N_KEYS, tb), F32), pltpu.VMEM((2, PEER_TOPK, tb), F32),
                        pltpu.VMEM((2, PEER_TOPK, tb), I32), pltpu.VMEM((N_CAND, tb), F32),
                        pltpu.VMEM((N_CAND, tb), I32), pltpu.VMEM((PEER_TOPK, tb), F32)],
        compiler_params=_params(("parallel", "parallel")),
        name="peer_topk",
    )(qq, sub_keys)


def _expert_gather(tbl_hbm, idx_v, rows_v, sem, tt, g, buf):
    return pltpu.make_async_copy(tbl_hbm.at[idx_v[pl.ds(tt * N_SEL + g * SC_GROUP, SC_GROUP)]],
                                 rows_v.at[buf], sem.at[buf])


def _sc_tok_blk(n):
    return math.gcd(n // SC_WORKERS, SC_TOK_BLK)


def _sc_token_blocks(n, tbl_hbm, idx_v, rows_v, sem, load_block, compute, store_block):
    wid = lax.axis_index("s") * SC_CORES + lax.axis_index("c")
    npw = n // SC_WORKERS
    tb = _sc_tok_blk(n)
    steps = tb * SC_N_GROUPS

    @pl.loop(0, npw // tb)
    def _(blk):
        tok0 = wid * npw + blk * tb
        load_block(tok0)
        for s in range(SC_NBUF - 1):
            _expert_gather(tbl_hbm, idx_v, rows_v, sem, s // SC_N_GROUPS, s % SC_N_GROUPS, s).start()

        @pl.loop(0, steps, step=SC_NBUF)
        def _(s0):
            for b in range(SC_NBUF):
                s = s0 + b
                ahead = s + SC_NBUF - 1

                @pl.when(ahead < steps)
                def _():
                    _expert_gather(tbl_hbm, idx_v, rows_v, sem, ahead // SC_N_GROUPS, ahead % SC_N_GROUPS,
                                   (b + SC_NBUF - 1) % SC_NBUF).start()

                _expert_gather(tbl_hbm, idx_v, rows_v, sem, s // SC_N_GROUPS, s % SC_N_GROUPS, b).wait()
                compute(s // SC_N_GROUPS, s % SC_N_GROUPS, rows_v.at[b])

        store_block(tok0)


def _row_chunk(j):
    per_line = LANES // SC_LANES
    return j // per_line, pl.ds((j % per_line) * SC_LANES, SC_LANES)


def _expert_cost(n):
    elems = n * N_SEL * D_MODEL
    return pl.CostEstimate(flops=2 * elems, transcendentals=0, bytes_accessed=4 * elems)


def _sc_mesh():
    return plsc.VectorSubcoreMesh(core_axis_name="c", subcore_axis_name="s",
                                  num_cores=SC_CORES, num_subcores=SC_SUBCORES)


def _sc_hidden(eidx, h, table, n):
    tb = _sc_tok_blk(n)

    @functools.partial(
        pl.kernel, out_type=jax.ShapeDtypeStruct((n * N_SEL,), F32), mesh=_sc_mesh(),
        compiler_params=pltpu.CompilerParams(needs_layout_passes=False),
        scratch_types=[pltpu.VMEM((tb * N_SEL,), I32), pltpu.VMEM((tb * D_MODEL,), F32),
                       pltpu.VMEM((SC_NBUF, SC_GROUP) + SC_ROW_TILE, F32), pltpu.VMEM((tb * N_SEL,), F32),
                       pltpu.SemaphoreType.DMA((SC_NBUF,))],
        cost_estimate=_expert_cost(n), name="peer_hidden_sc")
    def run(eidx_hbm, h_hbm, tbl_hbm, hid_hbm, idx_v, h_v, rows_v, hid_v, sem):
        lane = lax.iota(I32, SC_LANES)

        def load_block(tok0):
            pltpu.sync_copy(eidx_hbm.at[pl.ds(tok0 * N_SEL, tb * N_SEL)], idx_v)
            pltpu.sync_copy(h_hbm.at[pl.ds(tok0 * D_MODEL, tb * D_MODEL)], h_v)

        def compute(tt, g, rows):
            accs = []
            for sub in range(SC_GROUP // SC_RSUB):
                zero = tuple(jnp.zeros((SC_LANES,), F32) for _ in range(SC_RSUB))

                @plsc.parallel_loop(0, D_MODEL // SC_LANES, unroll=2, carry=zero)
                def part(j, acc):
                    hvj = h_v[pl.ds(tt * D_MODEL + j * SC_LANES, SC_LANES)]
                    return tuple(acc[r] + rows[(sub * SC_RSUB + r, *_row_chunk(j))] * hvj
                                 for r in range(SC_RSUB))

                accs.extend(part)
            tot = jnp.zeros((SC_LANES,), F32)
            for r in range(SC_GROUP):
                tot = jnp.where(lane == r, jnp.sum(accs[r]), tot)
            hid_v[pl.ds(tt * N_SEL + g * SC_GROUP, SC_GROUP)] = tot

        def store_block(tok0):
            pltpu.sync_copy(hid_v, hid_hbm.at[pl.ds(tok0 * N_SEL, tb * N_SEL)])

        _sc_token_blocks(n, tbl_hbm, idx_v, rows_v, sem, load_block, compute, store_block)

    return run(eidx.reshape(-1), h.reshape(-1), table).reshape(n, N_SEL)


def _sc_combine(eidx, coef, x, table, n):
    tb = _sc_tok_blk(n)

    @functools.partial(
        pl.kernel, out_type=jax.ShapeDtypeStruct((n * D_MODEL,), F32), mesh=_sc_mesh(),
        compiler_params=pltpu.CompilerParams(needs_layout_passes=False),
        scratch_types=[pltpu.VMEM((tb * N_SEL,), I32), pltpu.VMEM((tb * N_SEL,), F32),
                       pltpu.VMEM((SC_NBUF, SC_GROUP) + SC_ROW_TILE, F32), pltpu.VMEM((tb * D_MODEL,), F32),
                       pltpu.SemaphoreType.DMA((SC_NBUF,))],
        cost_estimate=_expert_cost(n), name="peer_combine_sc")
    def run(eidx_hbm, c_hbm, x_hbm, tbl_hbm, out_hbm, idx_v, c_v, rows_v, out_v, sem):
        lane = lax.iota(I32, SC_LANES)

        def load_block(tok0):
            pltpu.sync_copy(eidx_hbm.at[pl.ds(tok0 * N_SEL, tb * N_SEL)], idx_v)
            pltpu.sync_copy(c_hbm.at[pl.ds(tok0 * N_SEL, tb * N_SEL)], c_v)
            pltpu.sync_copy(x_hbm.at[pl.ds(tok0 * D_MODEL, tb * D_MODEL)], out_v)

        def compute(tt, g, rows):
            cvec = c_v[pl.ds(tt * N_SEL + g * SC_GROUP, SC_GROUP)]
            coefs = [jnp.sum(jnp.where(lane == r, cvec, 0.0)) for r in range(SC_GROUP)]

            @plsc.parallel_loop(0, D_MODEL // SC_LANES, unroll=2)
            def _(j):
                sl = pl.ds(tt * D_MODEL + j * SC_LANES, SC_LANES)
                acc = out_v[sl]
                for r in range(SC_GROUP):
                    acc = acc + rows[(r, *_row_chunk(j))] * coefs[r]
                out_v[sl] = acc

        def store_block(tok0):
            pltpu.sync_copy(out_v, out_hbm.at[pl.ds(tok0 * D_MODEL, tb * D_MODEL)])

        _sc_token_blocks(n, tbl_hbm, idx_v, rows_v, sem, load_block, compute, store_block)

    return run(eidx.reshape(-1), coef.reshape(-1), x.reshape(-1), table).reshape(n, D_MODEL)


def _tc_row_copies(idx_ref, t, slot, u_hbm, v_hbm, ubuf, vbuf, sem, k):
    e = idx_ref[t, k]
    return (pltpu.make_async_copy(u_hbm.at[pl.ds(e, 1)], ubuf.at[slot, pl.ds(k, 1)], sem.at[0, slot]),
            pltpu.make_async_copy(v_hbm.at[pl.ds(e, 1)], vbuf.at[slot, pl.ds(k, 1)], sem.at[1, slot]))


def _tc_experts_kernel(idx_ref, gate_ref, h_ref, x_ref, u_hbm, v_hbm, o_ref, ubuf, vbuf, sem):
    tb = h_ref.shape[0]

    def issue(t, slot):
        def body(k, carry):
            cu, cv = _tc_row_copies(idx_ref, t, slot, u_hbm, v_hbm, ubuf, vbuf, sem, k)
            cu.start()
            cv.start()
            return carry
        lax.fori_loop(0, N_SEL, body, 0, unroll=8)

    def wait(slot):
        pltpu.make_async_copy(u_hbm.at[pl.ds(0, N_SEL)], ubuf.at[slot], sem.at[0, slot]).wait()
        pltpu.make_async_copy(v_hbm.at[pl.ds(0, N_SEL)], vbuf.at[slot], sem.at[1, slot]).wait()

    eye = (lax.broadcasted_iota(I32, (N_SEL, N_SEL), 0) == lax.broadcasted_iota(I32, (N_SEL, N_SEL), 1))
    issue(0, 0)

    def token(t, carry):
        slot = t % 2

        @pl.when(t + 1 < tb)
        def _():
            issue(t + 1, 1 - slot)

        wait(slot)
        hrow = h_ref[pl.ds(t, 1), :]
        hid = jnp.sum(ubuf[slot] * hrow, axis=1, keepdims=True)
        act = 0.5 * hid * (1.0 + lax.erf(hid * np.float32(np.sqrt(0.5))))
        gcol = jnp.sum(jnp.where(eye, gate_ref[pl.ds(t, 1), :], 0.0), axis=1, keepdims=True)
        out = jnp.sum((gcol * act) * vbuf[slot], axis=0, keepdims=True)
        o_ref[pl.ds(t, 1), :] = x_ref[pl.ds(t, 1), :] + out
        return carry

    lax.fori_loop(0, tb, token, 0)


def _tc_experts(eidx, gate, h, x, peer_u, peer_v, first, count):
    tb = TC_EXPERT_BLK
    off = first // tb
    rows = lambda w: pl.BlockSpec((tb, w), lambda i: (i + off, 0))
    return pl.pallas_call(
        _tc_experts_kernel,
        grid=(count // tb,),
        in_specs=[pl.BlockSpec((tb, N_SEL), lambda i: (i + off, 0), memory_space=pltpu.SMEM),
                  rows(N_SEL), rows(D_MODEL), rows(D_MODEL),
                  pl.BlockSpec(memory_space=pl.ANY), pl.BlockSpec(memory_space=pl.ANY)],
        out_specs=pl.BlockSpec((tb, D_MODEL), lambda i: (i, 0)),
        out_shape=jax.ShapeDtypeStruct((count, D_MODEL), F32),
        scratch_shapes=[pltpu.VMEM((2, N_SEL, D_MODEL), F32), pltpu.VMEM((2, N_SEL, D_MODEL), F32),
                        pltpu.SemaphoreType.DMA((2, 2))],
        compiler_params=_params(("parallel",)),
        name="peer_experts_tc",
    )(eidx, gate, h, x, peer_u, peer_v)


def _gate_act_kernel(hid_ref, gate_ref, o_ref):
    hid = hid_ref[...]
    o_ref[...] = gate_ref[...] * (0.5 * hid * (1.0 + lax.erf(hid * np.float32(np.sqrt(0.5)))))


def _gate_act(hid, gate):
    n = hid.shape[0]
    tm = math.gcd(n, 2048)
    rows = pl.BlockSpec((tm, N_SEL), lambda i: (i, 0))
    return pl.pallas_call(
        _gate_act_kernel, grid=(n // tm,), in_specs=[rows, rows], out_specs=rows,
        out_shape=jax.ShapeDtypeStruct((n, N_SEL), F32), compiler_params=_params(("parallel",)),
        name="peer_gate_act",
    )(hid, gate)


def _t5_bucket(rel):
    nb = NUM_BUCKETS // 2
    max_exact = nb // 2
    ret = jnp.where(rel > 0, nb, 0)
    n = jnp.abs(rel)
    nf = jnp.maximum(n, 1).astype(F32)
    large = max_exact + (jnp.log(nf / max_exact) / math.log(MAX_DISTANCE / max_exact)
                         * (nb - max_exact)).astype(I32)
    large = jnp.minimum(large, nb - 1)
    return ret + jnp.where(n < max_exact, n, large)


def _rel_bias(rel_bias, n_q, n_k, n_before):
    rel = (jnp.arange(n_k)[None, :] - n_before) - jnp.arange(n_q)[:, None]
    return jnp.transpose(rel_bias[_t5_bucket(rel)].astype(F32), (2, 0, 1))


def _state_to_pairs(wkv):
    b = wkv.shape[0]
    s = wkv.reshape(b, N_PAIRS, 2, R_HD, R_HD)
    z = jnp.zeros_like(s[:, :, 0])
    top = jnp.concatenate([s[:, :, 0], z], axis=-1)
    bot = jnp.concatenate([z, s[:, :, 1]], axis=-1)
    return jnp.concatenate([top, bot], axis=-2)


def _pairs_to_state(s_bd):
    b = s_bd.shape[0]
    return jnp.stack([s_bd[:, :, :R_HD, :R_HD], s_bd[:, :, R_HD:, R_HD:]], axis=2).reshape(
        b, R_HEADS, R_HD, R_HD)


def _layer_dense(x, shift0, wkv0, kv_cache, bias, lp):
    b, t, _ = x.shape
    n = b * t
    z = _norm_matmul(x.reshape(n, D_MODEL), lp["ln1_g"].reshape(1, -1), lp["w_in"], 512).reshape(b, t, IN_COLS)
    r, k2, v, kk, bb, lw, g, bonus, qn, kn = _prep(z, shift0, lp)
    v_new = z[:, :, R_PROJ + A_DIM + KV_DIM:]
    if t % CHUNK:
        pad = lambda a: jnp.pad(a, ((0, 0), (0, CHUNK - t % CHUNK), (0, 0)))
        y, s_fin = _chunk_scan(*(pad(a) for a in (r, k2, v, kk, bb, lw)), _state_to_pairs(wkv0))
        y = y[:, :t]
    else:
        y, s_fin = _chunk_scan(r, k2, v, kk, bb, lw, _state_to_pairs(wkv0))
    if kv_cache is None:
        a_out = _swa_prompt(qn, kn, z, bias, lp["sink"])
    else:
        a_out = _swa_sample(qn, kn, v_new, kv_cache[0].reshape(b, -1, KV_DIM),
                            kv_cache[1].reshape(b, -1, KV_DIM), bias, lp["sink"])
    flat = lambda a: a.reshape(n, a.shape[-1])
    x1, h, qq = _mix_out(flat(y), flat(bonus), flat(g), flat(a_out), flat(x), lp)
    eidx, gate = _topk(qq, lp["sub_keys"])
    sel = lambda a: jnp.transpose(a, (2, 0, 1)).reshape(n, N_SEL)
    return ((sel(eidx), sel(gate), h, x1),
            (_pairs_to_state(s_fin), z[:, -1, :R_PROJ],
             kn.reshape(b, t, A_KV, A_HD), v_new.reshape(b, t, A_KV, A_HD)))


def _tc_share(n):
    n_tc = n * TC_SHARE_16THS // 16
    if n_tc % TC_EXPERT_BLK or (n - n_tc) % (SC_WORKERS * SC_MIN_TOK_BLK):
        return 0
    return n_tc


def _layer_experts(ops, lp, after=None):
    eidx, gate, h, x1 = ops
    n = eidx.shape[0]
    n_sc = n - _tc_share(n)
    x_tc = None
    if n_sc < n:
        x_tc = _tc_experts(eidx, gate, h, x1, lp["peer_u2d"], lp["peer_v2d"], n_sc, n - n_sc)
    if after is not None:
        h, after = lax.optimization_barrier((h, after))
    hid = _sc_hidden(eidx, h, lp["peer_u"], n_sc)
    if x_tc is not None:
        hid, x_tc = lax.optimization_barrier((hid, x_tc))
    return _sc_combine(eidx, _gate_act(hid, gate), x1, lp["peer_v"], n_sc), x_tc, after


def _join_rows(x_sc, x_tc, shape):
    return (x_sc if x_tc is None else jnp.concatenate([x_sc, x_tc], axis=0)).reshape(shape)


def kernel(x_prompt, x_sample, state_rwkv_wkv, state_rwkv_shift, cache_swa_k, cache_swa_v, ln1_g, w_in,
           mu_shift, w0, w2, a0, a2, g2, k_k, k_a, r_k, gn_g, gn_b, q_norm_g, k_norm_g, attn_sink,
           rel_bias, w_out, ln2_g, w_pq, sub_keys, peer_u, peer_v):
    depth = w_in.shape[0]
    b_p, s_p = x_prompt.shape[:2]
    b_s, t_s = x_sample.shape[:2]
    n_cache = cache_swa_k.shape[2]
    n_keep = min(WINDOW, s_p)
    bias_p = _rel_bias(rel_bias, CHUNK, (WIN_CHUNKS + 1) * CHUNK, WIN_CHUNKS * CHUNK)
    bias_s = _rel_bias(rel_bias, t_s, n_cache + t_s, n_cache)
    head_id = jnp.arange(R_DIM) // R_HD
    e64 = (head_id[:, None] == head_id[None, :]).astype(BF16)
    zpad = jnp.zeros((LANES - W_LORA, R_DIM), F32)
    n_groups = PROMPT_GROUPS if b_p % PROMPT_GROUPS == 0 else 1
    xp, xs = jnp.split(x_prompt, n_groups, axis=0), x_sample
    outs = [[] for _ in range(8)]
    for l in range(depth):
        lp = {
            "ln1_g": ln1_g[l], "w_in": w_in[l].astype(BF16), "mu": mu_shift[l], "w0": w0[l], "a0": a0[l],
            "k_k": k_k[l], "k_a": k_a[l], "r_k": r_k[l].reshape(-1), "gn_g": gn_g[l], "gn_b": gn_b[l],
            "w2p": jnp.concatenate([w2[l], zpad], axis=0).astype(BF16),
            "a2p": jnp.concatenate([zpad, a2[l]], axis=0).astype(BF16),
            "g2": g2[l].astype(BF16), "e64": e64,
            "q_gain": jnp.tile(q_norm_g[l], A_HEADS), "k_gain": jnp.tile(k_norm_g[l], A_KV),
            "sink": attn_sink[l].astype(F32), "w_out": w_out[l].astype(BF16), "ln2_g": ln2_g[l],
            "w_pq": w_pq[l].astype(BF16), "sub_keys": sub_keys[l], "peer_u": peer_u[l].reshape((-1,) + SC_ROW_TILE), "peer_v": peer_v[l].reshape((-1,) + SC_ROW_TILE),
            "peer_u2d": peer_u[l], "peer_v2d": peer_v[l],
        }
        parts = [_layer_dense(xg, jnp.zeros((xg.shape[0], R_PROJ), F32),
                              jnp.zeros((xg.shape[0], R_HEADS, R_HD, R_HD), F32), None, bias_p, lp) for xg in xp]
        wkv_p, sh_p, k_p, v_p = (jnp.concatenate([pt[1][i] for pt in parts], axis=0) for i in range(4))
        sc_rows, tc_rows = [], []
        for ops, _ in parts:
            if sc_rows:
                x_sc, x_tc, sc_rows[-1] = _layer_experts(ops, lp, after=sc_rows[-1])
            else:
                x_sc, x_tc, _ = _layer_experts(ops, lp)
            sc_rows.append(x_sc)
            tc_rows.append(x_tc)
        xp = [_join_rows(a, b, xg.shape) for a, b, xg in zip(sc_rows, tc_rows, xp)]
        ops_s, (wkv_s, sh_s, k_s, v_s) = _layer_dense(
            xs, state_rwkv_shift[l], state_rwkv_wkv[l].astype(F32), (cache_swa_k[l], cache_swa_v[l]),
            bias_s, lp)
        x_sc, x_tc, _ = _layer_experts(ops_s, lp)
        xs = _join_rows(x_sc, x_tc, xs.shape)
        for lst, val in zip(outs, (wkv_p, sh_p, k_p[:, s_p - n_keep:], v_p[:, s_p - n_keep:],
                                   wkv_s, sh_s, k_s, v_s)):
            lst.append(val)
    return (jnp.concatenate(xp, axis=0), xs) + tuple(jnp.stack(o) for o in outs)
```

```python
import functools
import math

import numpy as np
import jax
import jax.numpy as jnp
from jax import lax
from jax.experimental import pallas as pl
from jax.experimental.pallas import tpu as pltpu
from jax.experimental.pallas import tpu_sc as plsc

F32 = jnp.float32
BF16 = jnp.bfloat16
I32 = jnp.int32

D_MODEL = 1024
CHUNK = 64
R_HEADS = 8
R_HD = 64
R_DIM = R_HEADS * R_HD
W_LORA = 64
A_LORA = 64
G_LORA = 128
R_PROJ = 3 * R_DIM + W_LORA + A_LORA + G_LORA
A_HEADS = 8
A_KV = 2
A_GROUP = A_HEADS // A_KV
A_HD = 64
A_DIM = A_HEADS * A_HD
KV_DIM = A_KV * A_HD
IN_COLS = R_PROJ + A_DIM + 2 * KV_DIM
WINDOW = 128
WIN_CHUNKS = WINDOW // CHUNK
NUM_BUCKETS = 32
MAX_DISTANCE = 128
PEER_HEADS = 8
N_KEYS = 128
PK_DIM = 256
PK_HALF = PK_DIM // 2
PEER_TOPK = 16
N_SEL = PEER_HEADS * PEER_TOPK
NORM_EPS = 1e-6
GN_EPS = 64e-5
NEG_INF = -1e30

LANES = 128
N_PAIRS = R_DIM // LANES
VMEM_LIMIT = 48 * 1024 * 1024
CAND_COLS = (16, 8, 8, 4, 4, 4, 4, 4, 1, 1, 1, 1, 1, 1, 1, 1)
assert all(nb >= PEER_TOPK // (a + 1) for a, nb in enumerate(CAND_COLS))
N_CAND = 64
assert sum(CAND_COLS) <= N_CAND
SC_CORES = 2
SC_SUBCORES = 16
SC_LANES = 16
SC_WORKERS = SC_CORES * SC_SUBCORES
SC_TOK_BLK = 32
SC_MIN_TOK_BLK = 8
SC_GROUP = SC_LANES
SC_N_GROUPS = N_SEL // SC_GROUP
SC_RSUB = 16
SC_NBUF = 4
SC_ROW_TILE = (D_MODEL // LANES, LANES)
PROMPT_GROUPS = 8


def _params(sem):
    return pltpu.CompilerParams(dimension_semantics=sem, vmem_limit_bytes=VMEM_LIMIT)


def _mm(a, b):
    return jnp.dot(a, b, preferred_element_type=F32)


def _mm_nt(a, b):
    return lax.dot_general(a, b, (((1,), (1,)), ((), ())), preferred_element_type=F32)


def _mm_tn(a, b):
    return lax.dot_general(a, b, (((0,), (0,)), ((), ())), preferred_element_type=F32)


def _split2(a):
    hi = a.astype(BF16)
    return hi, (a - hi.astype(F32)).astype(BF16)


def _split3(a):
    hi = a.astype(BF16)
    r = a - hi.astype(F32)
    mid = r.astype(BF16)
    return hi, mid, (r - mid.astype(F32)).astype(BF16)


def _x3(mm, a, b):
    ah, al = _split2(a)
    bh, bl = _split2(b)
    return mm(ah, bh) + mm(ah, bl) + mm(al, bh)


def _exact_lhs(mm, a_bf16, b):
    b0, b1, b2 = _split3(b)
    return mm(a_bf16, b0) + mm(a_bf16, b1) + mm(a_bf16, b2)


def _seg_sum(x, e_bf16):
    x0, x1, x2 = _split3(x)
    return _mm(x0, e_bf16) + _mm(x1, e_bf16) + _mm(x2, e_bf16)


def _sigmoid(x):
    return 1.0 / (1.0 + jnp.exp(-x))


def _norm_matmul_kernel(x_ref, g_ref, w_ref, o_ref, h_ref):
    @pl.when(pl.program_id(1) == 0)
    def _():
        x = x_ref[...]
        ms = jnp.mean(x * x, axis=-1, keepdims=True)
        h_ref[...] = (x * lax.rsqrt(ms + NORM_EPS) * g_ref[...]).astype(BF16)

    o_ref[...] = _mm(h_ref[...], w_ref[...])


def _norm_matmul(x, g, w_bf16, tn):
    n, k = x.shape
    m = w_bf16.shape[1]
    tm = min(n, 512)
    return pl.pallas_call(
        _norm_matmul_kernel,
        grid=(n // tm, m // tn),
        in_specs=[pl.BlockSpec((tm, k), lambda i, j: (i, 0)),
                  pl.BlockSpec((1, k), lambda i, j: (0, 0)),
                  pl.BlockSpec((k, tn), lambda i, j: (0, j))],
        out_specs=pl.BlockSpec((tm, tn), lambda i, j: (i, j)),
        out_shape=jax.ShapeDtypeStruct((n, m), F32),
        scratch_shapes=[pltpu.VMEM((tm, k), BF16)],
        compiler_params=_params(("parallel", "arbitrary")),
        name="norm_inproj",
    )(x, g, w_bf16)


def _prep_kernel(z_ref, zp_ref, sh_ref, mu_ref, w0_ref, a0_ref, kk_ref, ka_ref, rk_ref,
                 w2_ref, a2_ref, g2_ref, e_ref, qg_ref, kg_ref,
                 r_o, k_o, v_o, kk_o, b_o, lw_o, g_o, bo_o, qn_o, kn_o):
    i = pl.program_id(1)
    zt = z_ref[0]
    tp = zt.shape[0]
    zr = zt[:, :R_PROJ]
    prev_row = jnp.where(i == 0, sh_ref[0], zp_ref[0][7:8, :R_PROJ])
    row = lax.broadcasted_iota(I32, (tp, 1), 0)
    prev = jnp.where(row == 0, prev_row, pltpu.roll(zr, 1, axis=0))
    zs = zr + (prev - zr) * mu_ref[...]
    r = zs[:, 0:R_DIM]
    k = zs[:, R_DIM:2 * R_DIM]
    v = zs[:, 2 * R_DIM:3 * R_DIM]
    lo = zs[:, 3 * R_DIM:3 * R_DIM + W_LORA + A_LORA]
    g_lo = zs[:, 3 * R_DIM + W_LORA + A_LORA:R_PROJ]
    e = e_ref[...]
    w_in = -(w0_ref[...] + _mm(jnp.tanh(lo).astype(BF16), w2_ref[...]))
    softplus = jnp.maximum(w_in, 0.0) + jnp.log1p(jnp.exp(-jnp.abs(w_in)))
    w_log = -softplus - 0.5
    lw_o[0] = -jnp.exp(w_log)
    a = _sigmoid(a0_ref[...] + _mm(lo.astype(BF16), a2_ref[...]))
    g_o[0] = _mm(_sigmoid(g_lo).astype(BF16), g2_ref[...])
    kk = k * kk_ref[...]
    kk = kk / jnp.maximum(jnp.sqrt(_seg_sum(kk * kk, e)), 1e-12)
    k2 = k * (1.0 + (a - 1.0) * ka_ref[...])
    r_o[0] = r
    k_o[0] = k2
    v_o[0] = v
    kk_o[0] = kk
    b_o[0] = kk * a
    bo_o[0] = _seg_sum(r * k2 * rk_ref[...], e) * v
    q = zt[:, R_PROJ:R_PROJ + A_DIM]
    qn_o[0] = q * lax.rsqrt(_seg_sum(q * q, e) * (1.0 / A_HD) + NORM_EPS) * qg_ref[...]
    kx = zt[:, R_PROJ + A_DIM:R_PROJ + A_DIM + KV_DIM]
    e_kv = e[:KV_DIM, :KV_DIM]
    kn_o[0] = kx * lax.rsqrt(_seg_sum(kx * kx, e_kv) * (1.0 / A_HD) + NORM_EPS) * kg_ref[...]


def _prep(z, shift0, lp):
    b, t, _ = z.shape
    tp = min(t, 256)
    row = lambda a: a.reshape(1, -1)
    vec = lambda n: pl.BlockSpec((1, n), lambda bi, i: (0, 0))
    full = lambda s: pl.BlockSpec(s, lambda bi, i: (0, 0))
    wide = pl.BlockSpec((1, tp, R_DIM), lambda bi, i: (bi, i, 0))
    outs = [jax.ShapeDtypeStruct((b, t, R_DIM), F32)] * 9 + [jax.ShapeDtypeStruct((b, t, KV_DIM), F32)]
    return pl.pallas_call(
        _prep_kernel,
        grid=(b, t // tp),
        in_specs=[pl.BlockSpec((1, tp, IN_COLS), lambda bi, i: (bi, i, 0)),
                  pl.BlockSpec((1, 8, IN_COLS), lambda bi, i: (bi, jnp.maximum(i * (tp // 8) - 1, 0), 0)),
                  pl.BlockSpec((1, 1, R_PROJ), lambda bi, i: (bi, 0, 0)),
                  vec(R_PROJ), vec(R_DIM), vec(R_DIM), vec(R_DIM), vec(R_DIM), vec(R_DIM),
                  full((LANES, R_DIM)), full((LANES, R_DIM)), full((G_LORA, R_DIM)),
                  full((R_DIM, R_DIM)), vec(A_DIM), vec(KV_DIM)],
        out_specs=[wide] * 9 + [pl.BlockSpec((1, tp, KV_DIM), lambda bi, i: (bi, i, 0))],
        out_shape=outs,
        compiler_params=_params(("parallel", "parallel")),
        name="rwkv_prep",
    )(z, z, shift0.reshape(b, 1, R_PROJ), row(lp["mu"]), row(lp["w0"]), row(lp["a0"]), row(lp["k_k"]),
      row(lp["k_a"]), row(lp["r_k"]), lp["w2p"], lp["a2p"], lp["g2"], lp["e64"], row(lp["q_gain"]),
      row(lp["k_gain"]))


def _stack_heads(x, lo_mask):
    return jnp.concatenate([jnp.where(lo_mask, x, 0.0), jnp.where(lo_mask, 0.0, x)], axis=0)


def _chunk_kernel(r_ref, k_ref, v_ref, kk_ref, b_ref, lw_ref, s0_ref, y_ref, sf_ref, s_ref):
    c = pl.program_id(1)
    L = CHUNK

    @pl.when(c == 0)
    def _():
        s_ref[...] = s0_ref[0]

    lane = lax.broadcasted_iota(I32, (1, LANES), 1)
    lo_mask = lane < R_HD
    rr = lax.broadcasted_iota(I32, (L, 2 * L), 0)
    cc = lax.broadcasted_iota(I32, (L, 2 * L), 1)
    cc = jnp.where(cc >= L, cc - L, cc)
    strict = rr > cc
    incl = rr >= cc
    t_r = lax.broadcasted_iota(I32, (L, L), 0)
    t_c = lax.broadcasted_iota(I32, (L, L), 1)
    tri = (t_r >= t_c).astype(BF16)
    col2 = lax.broadcasted_iota(I32, (L, 2 * L), 1) < L
    eye_r = lax.broadcasted_iota(I32, (2 * L, 2 * L), 0)
    eye_c = lax.broadcasted_iota(I32, (2 * L, 2 * L), 1)
    eye = (eye_r == eye_c).astype(F32)

    pairs = range(N_PAIRS)
    sls = [slice(p * LANES, (p + 1) * LANES) for p in pairs]
    r = [r_ref[0][:, sl] for sl in sls]
    k = [k_ref[0][:, sl] for sl in sls]
    v = [v_ref[0][:, sl] for sl in sls]
    kk = [kk_ref[0][:, sl] for sl in sls]
    bb = [b_ref[0][:, sl] for sl in sls]
    lw = [lw_ref[0][:, sl] for sl in sls]
    s0 = [s_ref[p] for p in pairs]

    cl = [_exact_lhs(_mm, tri, lw[p]) for p in pairs]
    cl_last = [cl[p][L - 1:L, :] for p in pairs]
    e_neg = [jnp.exp(-cl[p]) for p in pairs]
    e_last = [jnp.exp(cl_last[p] - cl[p]) for p in pairs]
    rt = [r[p] * jnp.exp(cl[p]) for p in pairs]
    at = [kk[p] * jnp.exp(cl[p] - lw[p]) for p in pairs]

    gm = [_x3(_mm_nt, jnp.concatenate([at[p], rt[p]], axis=0),
              jnp.concatenate([_stack_heads(k[p] * e_neg[p], lo_mask), _stack_heads(bb[p] * e_neg[p], lo_mask)],
                              axis=0)) for p in pairs]
    mk = [jnp.where(strict, gm[p][:L, :2 * L], 0.0) for p in pairs]
    mb = [jnp.where(strict, gm[p][:L, 2 * L:], 0.0) for p in pairs]
    hk = [jnp.where(incl, gm[p][L:, :2 * L], 0.0) for p in pairs]
    hb = [jnp.where(incl, gm[p][L:, 2 * L:], 0.0) for p in pairs]

    nil = [-jnp.concatenate([jnp.where(col2, mb[p], 0.0), jnp.where(col2, 0.0, mb[p])], axis=0) for p in pairs]
    tinv = [eye + nil[p] for p in pairs]
    nil = [_x3(_mm, nil[p], nil[p]) for p in pairs]
    for it in range(5):
        tinv = [tinv[p] + _x3(_mm, nil[p], tinv[p]) for p in pairs]
        if it < 4:
            nil = [_x3(_mm, nil[p], nil[p]) for p in pairs]

    vs = [_stack_heads(v[p], lo_mask) for p in pairs]
    rhs_u = [-(_x3(_mm_nt, at[p], s0[p]) + _x3(_mm, mk[p], vs[p])) for p in pairs]
    us = [_x3(_mm, tinv[p], _stack_heads(rhs_u[p], lo_mask)) for p in pairs]
    for p in pairs:
        y_ref[0, :, sls[p]] = _x3(_mm_nt, rt[p], s0[p]) + _x3(_mm, hk[p], vs[p]) + _x3(_mm, hb[p], us[p])
    for p in pairs:
        s_ref[p] = (s0[p] * jnp.exp(cl_last[p])
                    + _x3(_mm_tn, vs[p], _stack_heads(k[p] * e_last[p], lo_mask))
                    + _x3(_mm_tn, us[p], _stack_heads(bb[p] * e_last[p], lo_mask)))

    @pl.when(c == pl.num_programs(1) - 1)
    def _():
        sf_ref[0] = s_ref[...]


def _chunk_scan(r, k, v, kk, bb, lw, s0_bd):
    b, t, _ = r.shape
    wide = pl.BlockSpec((1, CHUNK, R_DIM), lambda bi, c: (bi, c, 0))
    st = pl.BlockSpec((1, N_PAIRS, LANES, LANES), lambda bi, c: (bi, 0, 0, 0))
    return pl.pallas_call(
        _chunk_kernel,
        grid=(b, t // CHUNK),
        in_specs=[wide] * 6 + [st],
        out_specs=[wide, st],
        out_shape=[jax.ShapeDtypeStruct((b, t, R_DIM), F32),
                   jax.ShapeDtypeStruct((b, N_PAIRS, LANES, LANES), F32)],
        scratch_shapes=[pltpu.VMEM((N_PAIRS, LANES, LANES), F32)],
        compiler_params=_params(("parallel", "arbitrary")),
        name="rwkv_chunk",
    )(r, k, v, kk, bb, lw, s0_bd)


def _swa_kernel(*refs, n_seg, banded):
    q_ref = refs[0]
    k_refs = refs[1:1 + n_seg]
    v_refs = refs[1 + n_seg:1 + 2 * n_seg]
    bias_ref, sink_ref, o_ref = refs[1 + 2 * n_seg:]
    c = pl.program_id(1)
    q = q_ref[0]
    kcat = jnp.concatenate([kr[0] for kr in k_refs], axis=0)
    vcat = jnp.concatenate([vr[0] for vr in v_refs], axis=0)
    n_k = kcat.shape[0]
    lane = lax.broadcasted_iota(I32, (1, LANES), 1)
    lo_mask = lane < A_HD
    k_sw = pltpu.roll(kcat, A_HD, axis=1)
    v_sw = pltpu.roll(vcat, A_HD, axis=1)
    k_dup = [jnp.where(lo_mask, kcat, k_sw).astype(BF16), jnp.where(lo_mask, k_sw, kcat).astype(BF16)]
    v_dup = [jnp.where(lo_mask, vcat, v_sw).astype(BF16), jnp.where(lo_mask, v_sw, vcat).astype(BF16)]
    if banded:
        key_chunk = c - WIN_CHUNKS + lax.broadcasted_iota(I32, (1, n_k), 1) // CHUNK
        valid = key_chunk >= 0
    n_q = q.shape[0]
    hi_mask = jnp.logical_not(lo_mask)
    for kvh in range(A_KV):
        heads = range(kvh * A_GROUP, (kvh + 1) * A_GROUP)
        qs = jnp.concatenate(
            [jnp.where(lo_mask if hq % 2 == 0 else hi_mask, q[:, (hq // 2) * LANES:(hq // 2 + 1) * LANES], 0.0)
             for hq in heads], axis=0).astype(BF16)
        bias = jnp.concatenate([bias_ref[hq] for hq in heads], axis=0)
        sink = jnp.concatenate([jnp.broadcast_to(sink_ref[0:1, hq:hq + 1], (n_q, 1)) for hq in heads], axis=0)
        s = _mm_nt(qs, k_dup[kvh]) * (A_HD ** -0.5) + bias
        if banded:
            s = jnp.where(valid, s, NEG_INF)
        m = jnp.maximum(jnp.max(s, axis=-1, keepdims=True), sink)
        pr = jnp.exp(s - m)
        den = jnp.sum(pr, axis=-1, keepdims=True) + jnp.exp(sink - m)
        o = _mm((pr / den).astype(BF16), v_dup[kvh])
        for i in range(A_GROUP // 2):
            pair = kvh * (A_GROUP // 2) + i
            o_ref[0, :, pair * LANES:(pair + 1) * LANES] = jnp.where(
                lo_mask, o[2 * i * n_q:(2 * i + 1) * n_q], o[(2 * i + 1) * n_q:(2 * i + 2) * n_q])


def _swa_prompt(qn, kn, z, bias, sink):
    b, t, _ = qn.shape
    v_col = (R_PROJ + A_DIM + KV_DIM) // KV_DIM
    seg = lambda s, col: pl.BlockSpec(
        (1, CHUNK, KV_DIM), lambda bi, c: (bi, jnp.maximum(c - WIN_CHUNKS + s, 0), col))
    n_seg = WIN_CHUNKS + 1
    return pl.pallas_call(
        functools.partial(_swa_kernel, n_seg=n_seg, banded=True),
        grid=(b, t // CHUNK),
        in_specs=[pl.BlockSpec((1, CHUNK, A_DIM), lambda bi, c: (bi, c, 0))]
                 + [seg(s, 0) for s in range(n_seg)] + [seg(s, v_col) for s in range(n_seg)]
                 + [pl.BlockSpec(bias.shape, lambda bi, c: (0, 0, 0)),
                    pl.BlockSpec((1, A_HEADS), lambda bi, c: (0, 0))],
        out_specs=pl.BlockSpec((1, CHUNK, A_DIM), lambda bi, c: (bi, c, 0)),
        out_shape=jax.ShapeDtypeStruct((b, t, A_DIM), F32),
        compiler_params=_params(("parallel", "parallel")),
        name="swa_prompt",
    )(qn, *([kn] * n_seg), *([z] * n_seg), bias, sink.reshape(1, A_HEADS))


def _swa_sample(qn, kn, v_new, k_cache, v_cache, bias, sink):
    b, t, _ = qn.shape
    n_cache = k_cache.shape[1]
    cur = lambda w: pl.BlockSpec((1, t, w), lambda bi, c: (bi, 0, 0))
    old = pl.BlockSpec((1, n_cache, KV_DIM), lambda bi, c: (bi, 0, 0))
    return pl.pallas_call(
        functools.partial(_swa_kernel, n_seg=2, banded=False),
        grid=(b, 1),
        in_specs=[cur(A_DIM), old, cur(KV_DIM), old, cur(KV_DIM),
                  pl.BlockSpec(bias.shape, lambda bi, c: (0, 0, 0)),
                  pl.BlockSpec((1, A_HEADS), lambda bi, c: (0, 0))],
        out_specs=cur(A_DIM),
        out_shape=jax.ShapeDtypeStruct((b, t, A_DIM), F32),
        compiler_params=_params(("parallel", "parallel")),
        name="swa_sample",
    )(qn, k_cache, kn, v_cache, v_new, bias, sink.reshape(1, A_HEADS))


def _mix_out_kernel(y_ref, bo_ref, g_ref, a_ref, x_ref, gng_ref, gnb_ref, e_ref, wo_ref, ln_ref, wq_ref,
                    xo_ref, h_ref, qq_ref):
    e = e_ref[...]
    y = y_ref[...]
    mu = _seg_sum(y, e) * (1.0 / R_HD)
    d = y - mu
    var = _seg_sum(d * d, e) * (1.0 / R_HD)
    yn = d * lax.rsqrt(var + GN_EPS) * gng_ref[...] + gnb_ref[...]
    mix_r = ((yn + bo_ref[...]) * g_ref[...]).astype(BF16)
    x = (x_ref[...] + _mm(mix_r, wo_ref[:R_DIM, :]) + _mm(a_ref[...].astype(BF16), wo_ref[R_DIM:, :]))
    xo_ref[...] = x
    ms = jnp.mean(x * x, axis=-1, keepdims=True)
    h = x * lax.rsqrt(ms + NORM_EPS) * ln_ref[...]
    h_ref[...] = h
    qq_ref[...] = _mm(h.astype(BF16), wq_ref[...])


def _mix_out(y, bonus, g, a_out, x, lp):
    n = x.shape[0]
    tm = min(n, 256)
    nq = PEER_HEADS * PK_DIM
    rows = lambda w: pl.BlockSpec((tm, w), lambda i: (i, 0))
    full = lambda s: pl.BlockSpec(s, lambda i: (0, 0))
    return pl.pallas_call(
        _mix_out_kernel,
        grid=(n // tm,),
        in_specs=[rows(R_DIM), rows(R_DIM), rows(R_DIM), rows(A_DIM), rows(D_MODEL),
                  full((1, R_DIM)), full((1, R_DIM)), full((R_DIM, R_DIM)),
                  full((D_MODEL, D_MODEL)), full((1, D_MODEL)), full((D_MODEL, nq))],
        out_specs=[rows(D_MODEL), rows(D_MODEL), rows(nq)],
        out_shape=[jax.ShapeDtypeStruct((n, D_MODEL), F32), jax.ShapeDtypeStruct((n, D_MODEL), F32),
                   jax.ShapeDtypeStruct((n, nq), F32)],
        compiler_params=_params(("parallel",)),
        name="mix_out_query",
    )(y, bonus, g, a_out, x, lp["gn_g"].reshape(1, -1), lp["gn_b"].reshape(1, -1), lp["e64"],
      lp["w_out"], lp["ln2_g"].reshape(1, -1), lp["w_pq"])


def _pick_rounds(s_ref, n_rows, emit):
    tb = s_ref.shape[1]
    rowid = lax.broadcasted_iota(I32, (n_rows, tb), 0)
    for rnd in range(PEER_TOPK):
        s = s_ref[...]
        m = jnp.max(s, axis=0, keepdims=True)
        idx = jnp.min(jnp.where(s == m, rowid, n_rows), axis=0, keepdims=True)
        hit = rowid == idx
        s_ref[...] = jnp.where(hit, -jnp.inf, s)
        emit(rnd, m, idx, hit)


def _topk_kernel(qq_ref, keys_ref, ei_ref, gate_ref, s_ref, sv_ref, si_ref, c_ref, ci_ref, ts_ref):
    tb = qq_ref.shape[0]
    for half in range(2):
        qh = qq_ref[:, half * PK_HALF:(half + 1) * PK_HALF]
        s_ref[...] = _x3(_mm_nt, keys_ref[half], qh)

        def emit1(rnd, m, idx, hit, half=half):
            sv_ref[half, rnd:rnd + 1, :] = m
            si_ref[half, rnd:rnd + 1, :] = idx

        _pick_rounds(s_ref, N_KEYS, emit1)

    row0 = 0
    for a, nb in enumerate(CAND_COLS):
        if nb == 1:
            break
        c_ref[row0:row0 + nb, :] = sv_ref[0, a:a + 1, :] + sv_ref[1, 0:nb, :]
        ci_ref[row0:row0 + nb, :] = si_ref[0, a:a + 1, :] * N_KEYS + si_ref[1, 0:nb, :]
        row0 += nb
    n_one = PEER_TOPK - a
    c_ref[row0:row0 + n_one, :] = sv_ref[0, a:, :] + sv_ref[1, 0:1, :]
    ci_ref[row0:row0 + n_one, :] = si_ref[0, a:, :] * N_KEYS + si_ref[1, 0:1, :]
    row0 += n_one
    c_ref[row0:, :] = jnp.full((N_CAND - row0, tb), -jnp.inf, F32)
    ci_ref[row0:, :] = jnp.zeros((N_CAND - row0, tb), I32)

    def emit2(rnd, m, idx, hit):
        ts_ref[rnd:rnd + 1, :] = m
        ei_ref[0, rnd:rnd + 1, :] = jnp.max(jnp.where(hit, ci_ref[...], -1), axis=0, keepdims=True)

    _pick_rounds(c_ref, N_CAND, emit2)
    ts = ts_ref[...]
    ex = jnp.exp(ts - ts[0:1, :])
    gate_ref[0] = ex / jnp.sum(ex, axis=0, keepdims=True)


def _topk(qq, sub_keys):
    n = qq.shape[0]
    tb = min(n, 512)
    out = pl.BlockSpec((1, PEER_TOPK, tb), lambda i, h: (h, 0, i))
    return pl.pallas_call(
        _topk_kernel,
        grid=(n // tb, PEER_HEADS),
        in_specs=[pl.BlockSpec((tb, PK_DIM), lambda i, h: (i, h)),
                  pl.BlockSpec((2, N_KEYS, PK_HALF), lambda i, h: (0, 0, 0))],
        out_specs=[out, out],
        out_shape=[jax.ShapeDtypeStruct((PEER_HEADS, PEER_TOPK, n), I32),
                   jax.ShapeDtypeStruct((PEER_HEADS, PEER_TOPK, n), F32)],
        scratch_shapes=[pltpu.VMEM((N_KEYS, tb), F32), pltpu.VMEM((2, PEER_TOPK, tb), F32),
                        pltpu.VMEM((2, PEER_TOPK, tb), I32), pltpu.VMEM((N_CAND, tb), F32),
                        pltpu.VMEM((N_CAND, tb), I32), pltpu.VMEM((PEER_TOPK, tb), F32)],
        compiler_params=_params(("parallel", "parallel")),
        name="peer_topk",
    )(qq, sub_keys)


def _expert_gather(tbl_hbm, idx_v, rows_v, sem, tt, g, buf):
    return pltpu.make_async_copy(tbl_hbm.at[idx_v[pl.ds(tt * N_SEL + g * SC_GROUP, SC_GROUP)]],
                                 rows_v.at[buf], sem.at[buf])


def _sc_tok_blk(n):
    return math.gcd(n // SC_WORKERS, SC_TOK_BLK)


def _sc_token_blocks(n, tbl_hbm, idx_v, rows_v, sem, load_block, compute, store_block):
    wid = lax.axis_index("s") * SC_CORES + lax.axis_index("c")
    npw = n // SC_WORKERS
    tb = _sc_tok_blk(n)
    steps = tb * SC_N_GROUPS

    @pl.loop(0, npw // tb)
    def _(blk):
        tok0 = wid * npw + blk * tb
        load_block(tok0)
        for s in range(SC_NBUF - 1):
            _expert_gather(tbl_hbm, idx_v, rows_v, sem, s // SC_N_GROUPS, s % SC_N_GROUPS, s).start()

        @pl.loop(0, steps, step=SC_NBUF)
        def _(s0):
            for b in range(SC_NBUF):
                s = s0 + b
                ahead = s + SC_NBUF - 1

                @pl.when(ahead < steps)
                def _():
                    _expert_gather(tbl_hbm, idx_v, rows_v, sem, ahead // SC_N_GROUPS, ahead % SC_N_GROUPS,
                                   (b + SC_NBUF - 1) % SC_NBUF).start()

                _expert_gather(tbl_hbm, idx_v, rows_v, sem, s // SC_N_GROUPS, s % SC_N_GROUPS, b).wait()
                compute(s // SC_N_GROUPS, s % SC_N_GROUPS, rows_v.at[b])

        store_block(tok0)


def _row_chunk(j):
    per_line = LANES // SC_LANES
    return j // per_line, pl.ds((j % per_line) * SC_LANES, SC_LANES)


def _expert_cost(n):
    elems = n * N_SEL * D_MODEL
    return pl.CostEstimate(flops=2 * elems, transcendentals=0, bytes_accessed=4 * elems)


def _sc_mesh():
    return plsc.VectorSubcoreMesh(core_axis_name="c", subcore_axis_name="s",
                                  num_cores=SC_CORES, num_subcores=SC_SUBCORES)


def _sc_hidden(eidx, h, table, n):
    tb = _sc_tok_blk(n)

    @functools.partial(
        pl.kernel, out_type=jax.ShapeDtypeStruct((n * N_SEL,), F32), mesh=_sc_mesh(),
        compiler_params=pltpu.CompilerParams(needs_layout_passes=False),
        scratch_types=[pltpu.VMEM((tb * N_SEL,), I32), pltpu.VMEM((tb * D_MODEL,), F32),
                       pltpu.VMEM((SC_NBUF, SC_GROUP) + SC_ROW_TILE, F32), pltpu.VMEM((tb * N_SEL,), F32),
                       pltpu.SemaphoreType.DMA((SC_NBUF,))],
        cost_estimate=_expert_cost(n), name="peer_hidden_sc")
    def run(eidx_hbm, h_hbm, tbl_hbm, hid_hbm, idx_v, h_v, rows_v, hid_v, sem):
        lane = lax.iota(I32, SC_LANES)

        def load_block(tok0):
            pltpu.sync_copy(eidx_hbm.at[pl.ds(tok0 * N_SEL, tb * N_SEL)], idx_v)
            pltpu.sync_copy(h_hbm.at[pl.ds(tok0 * D_MODEL, tb * D_MODEL)], h_v)

        def compute(tt, g, rows):
            accs = []
            for sub in range(SC_GROUP // SC_RSUB):
                zero = tuple(jnp.zeros((SC_LANES,), F32) for _ in range(SC_RSUB))

                @plsc.parallel_loop(0, D_MODEL // SC_LANES, unroll=2, carry=zero)
                def part(j, acc):
                    hvj = h_v[pl.ds(tt * D_MODEL + j * SC_LANES, SC_LANES)]
                    return tuple(acc[r] + rows[(sub * SC_RSUB + r, *_row_chunk(j))] * hvj
                                 for r in range(SC_RSUB))

                accs.extend(part)
            tot = jnp.zeros((SC_LANES,), F32)
            for r in range(SC_GROUP):
                tot = jnp.where(lane == r, jnp.sum(accs[r]), tot)
            hid_v[pl.ds(tt * N_SEL + g * SC_GROUP, SC_GROUP)] = tot

        def store_block(tok0):
            pltpu.sync_copy(hid_v, hid_hbm.at[pl.ds(tok0 * N_SEL, tb * N_SEL)])

        _sc_token_blocks(n, tbl_hbm, idx_v, rows_v, sem, load_block, compute, store_block)

    return run(eidx.reshape(-1), h.reshape(-1), table).reshape(n, N_SEL)


def _sc_combine(eidx, coef, x, table, n):
    tb = _sc_tok_blk(n)

    @functools.partial(
        pl.kernel, out_type=jax.ShapeDtypeStruct((n * D_MODEL,), F32), mesh=_sc_mesh(),
        compiler_params=pltpu.CompilerParams(needs_layout_passes=False),
        scratch_types=[pltpu.VMEM((tb * N_SEL,), I32), pltpu.VMEM((tb * N_SEL,), F32),
                       pltpu.VMEM((SC_NBUF, SC_GROUP) + SC_ROW_TILE, F32), pltpu.VMEM((tb * D_MODEL,), F32),
                       pltpu.SemaphoreType.DMA((SC_NBUF,))],
        cost_estimate=_expert_cost(n), name="peer_combine_sc")
    def run(eidx_hbm, c_hbm, x_hbm, tbl_hbm, out_hbm, idx_v, c_v, rows_v, out_v, sem):
        lane = lax.iota(I32, SC_LANES)

        def load_block(tok0):
            pltpu.sync_copy(eidx_hbm.at[pl.ds(tok0 * N_SEL, tb * N_SEL)], idx_v)
            pltpu.sync_copy(c_hbm.at[pl.ds(tok0 * N_SEL, tb * N_SEL)], c_v)
            pltpu.sync_copy(x_hbm.at[pl.ds(tok0 * D_MODEL, tb * D_MODEL)], out_v)

        def compute(tt, g, rows):
            cvec = c_v[pl.ds(tt * N_SEL + g * SC_GROUP, SC_GROUP)]
            coefs = [jnp.sum(jnp.where(lane == r, cvec, 0.0)) for r in range(SC_GROUP)]

            @plsc.parallel_loop(0, D_MODEL // SC_LANES, unroll=2)
            def _(j):
                sl = pl.ds(tt * D_MODEL + j * SC_LANES, SC_LANES)
                acc = out_v[sl]
                for r in range(SC_GROUP):
                    acc = acc + rows[(r, *_row_chunk(j))] * coefs[r]
                out_v[sl] = acc

        def store_block(tok0):
            pltpu.sync_copy(out_v, out_hbm.at[pl.ds(tok0 * D_MODEL, tb * D_MODEL)])

        _sc_token_blocks(n, tbl_hbm, idx_v, rows_v, sem, load_block, compute, store_block)

    return run(eidx.reshape(-1), coef.reshape(-1), x.reshape(-1), table).reshape(n, D_MODEL)


def _gate_act_kernel(hid_ref, gate_ref, o_ref):
    hid = hid_ref[...]
    o_ref[...] = gate_ref[...] * (0.5 * hid * (1.0 + lax.erf(hid * np.float32(np.sqrt(0.5)))))


def _gate_act(hid, gate):
    n = hid.shape[0]
    tm = math.gcd(n, 2048)
    rows = pl.BlockSpec((tm, N_SEL), lambda i: (i, 0))
    return pl.pallas_call(
        _gate_act_kernel, grid=(n // tm,), in_specs=[rows, rows], out_specs=rows,
        out_shape=jax.ShapeDtypeStruct((n, N_SEL), F32), compiler_params=_params(("parallel",)),
        name="peer_gate_act",
    )(hid, gate)


def _t5_bucket(rel):
    nb = NUM_BUCKETS // 2
    max_exact = nb // 2
    ret = jnp.where(rel > 0, nb, 0)
    n = jnp.abs(rel)
    nf = jnp.maximum(n, 1).astype(F32)
    large = max_exact + (jnp.log(nf / max_exact) / math.log(MAX_DISTANCE / max_exact)
                         * (nb - max_exact)).astype(I32)
    large = jnp.minimum(large, nb - 1)
    return ret + jnp.where(n < max_exact, n, large)


def _rel_bias(rel_bias, n_q, n_k, n_before):
    rel = (jnp.arange(n_k)[None, :] - n_before) - jnp.arange(n_q)[:, None]
    return jnp.transpose(rel_bias[_t5_bucket(rel)].astype(F32), (2, 0, 1))


def _state_to_pairs(wkv):
    b = wkv.shape[0]
    s = wkv.reshape(b, N_PAIRS, 2, R_HD, R_HD)
    z = jnp.zeros_like(s[:, :, 0])
    top = jnp.concatenate([s[:, :, 0], z], axis=-1)
    bot = jnp.concatenate([z, s[:, :, 1]], axis=-1)
    return jnp.concatenate([top, bot], axis=-2)


def _pairs_to_state(s_bd):
    b = s_bd.shape[0]
    return jnp.stack([s_bd[:, :, :R_HD, :R_HD], s_bd[:, :, R_HD:, R_HD:]], axis=2).reshape(
        b, R_HEADS, R_HD, R_HD)


def _layer_dense(x, shift0, wkv0, kv_cache, bias, lp):
    b, t, _ = x.shape
    n = b * t
    z = _norm_matmul(x.reshape(n, D_MODEL), lp["ln1_g"].reshape(1, -1), lp["w_in"], 512).reshape(b, t, IN_COLS)
    r, k2, v, kk, bb, lw, g, bonus, qn, kn = _prep(z, shift0, lp)
    v_new = z[:, :, R_PROJ + A_DIM + KV_DIM:]
    if t % CHUNK:
        pad = lambda a: jnp.pad(a, ((0, 0), (0, CHUNK - t % CHUNK), (0, 0)))
        y, s_fin = _chunk_scan(*(pad(a) for a in (r, k2, v, kk, bb, lw)), _state_to_pairs(wkv0))
        y = y[:, :t]
    else:
        y, s_fin = _chunk_scan(r, k2, v, kk, bb, lw, _state_to_pairs(wkv0))
    if kv_cache is None:
        a_out = _swa_prompt(qn, kn, z, bias, lp["sink"])
    else:
        a_out = _swa_sample(qn, kn, v_new, kv_cache[0].reshape(b, -1, KV_DIM),
                            kv_cache[1].reshape(b, -1, KV_DIM), bias, lp["sink"])
    flat = lambda a: a.reshape(n, a.shape[-1])
    x1, h, qq = _mix_out(flat(y), flat(bonus), flat(g), flat(a_out), flat(x), lp)
    eidx, gate = _topk(qq, lp["sub_keys"])
    sel = lambda a: jnp.transpose(a, (2, 0, 1)).reshape(n, N_SEL)
    return ((sel(eidx), sel(gate), h, x1),
            (_pairs_to_state(s_fin), z[:, -1, :R_PROJ],
             kn.reshape(b, t, A_KV, A_HD), v_new.reshape(b, t, A_KV, A_HD)))


def _layer_experts(ops, lp, after=None):
    eidx, gate, h, x1 = ops
    n = eidx.shape[0]
    if after is not None:
        h, after = lax.optimization_barrier((h, after))
    coef = _gate_act(_sc_hidden(eidx, h, lp["peer_u"], n), gate)
    return _sc_combine(eidx, coef, x1, lp["peer_v"], n), after


def kernel(x_prompt, x_sample, state_rwkv_wkv, state_rwkv_shift, cache_swa_k, cache_swa_v, ln1_g, w_in,
           mu_shift, w0, w2, a0, a2, g2, k_k, k_a, r_k, gn_g, gn_b, q_norm_g, k_norm_g, attn_sink,
           rel_bias, w_out, ln2_g, w_pq, sub_keys, peer_u, peer_v):
    depth = w_in.shape[0]
    b_p, s_p = x_prompt.shape[:2]
    b_s, t_s = x_sample.shape[:2]
    n_cache = cache_swa_k.shape[2]
    n_keep = min(WINDOW, s_p)
    bias_p = _rel_bias(rel_bias, CHUNK, (WIN_CHUNKS + 1) * CHUNK, WIN_CHUNKS * CHUNK)
    bias_s = _rel_bias(rel_bias, t_s, n_cache + t_s, n_cache)
    head_id = jnp.arange(R_DIM) // R_HD
    e64 = (head_id[:, None] == head_id[None, :]).astype(BF16)
    zpad = jnp.zeros((LANES - W_LORA, R_DIM), F32)
    n_groups = PROMPT_GROUPS if b_p % PROMPT_GROUPS == 0 else 1
    xp, xs = jnp.split(x_prompt, n_groups, axis=0), x_sample
    outs = [[] for _ in range(8)]
    for l in range(depth):
        lp = {
            "ln1_g": ln1_g[l], "w_in": w_in[l].astype(BF16), "mu": mu_shift[l], "w0": w0[l], "a0": a0[l],
            "k_k": k_k[l], "k_a": k_a[l], "r_k": r_k[l].reshape(-1), "gn_g": gn_g[l], "gn_b": gn_b[l],
            "w2p": jnp.concatenate([w2[l], zpad], axis=0).astype(BF16),
            "a2p": jnp.concatenate([zpad, a2[l]], axis=0).astype(BF16),
            "g2": g2[l].astype(BF16), "e64": e64,
            "q_gain": jnp.tile(q_norm_g[l], A_HEADS), "k_gain": jnp.tile(k_norm_g[l], A_KV),
            "sink": attn_sink[l].astype(F32), "w_out": w_out[l].astype(BF16), "ln2_g": ln2_g[l],
            "w_pq": w_pq[l].astype(BF16), "sub_keys": sub_keys[l], "peer_u": peer_u[l].reshape((-1,) + SC_ROW_TILE), "peer_v": peer_v[l].reshape((-1,) + SC_ROW_TILE),
        }
        parts = []
        for xg in xp:
            if parts:
                (eidx, *rest), states = parts[-1]
                xg, eidx = lax.optimization_barrier((xg, eidx))
                parts[-1] = ((eidx, *rest), states)
            parts.append(_layer_dense(xg, jnp.zeros((xg.shape[0], R_PROJ), F32),
                                      jnp.zeros((xg.shape[0], R_HEADS, R_HD, R_HD), F32), None, bias_p, lp))
        wkv_p, sh_p, k_p, v_p = (jnp.concatenate([pt[1][i] for pt in parts], axis=0) for i in range(4))
        outs_x = []
        for ops, _ in parts:
            if outs_x:
                xo, outs_x[-1] = _layer_experts(ops, lp, after=outs_x[-1])
            else:
                xo, _ = _layer_experts(ops, lp)
            outs_x.append(xo)
        xp = [xo.reshape(xg.shape) for xo, xg in zip(outs_x, xp)]
        ops_s, (wkv_s, sh_s, k_s, v_s) = _layer_dense(
            xs, state_rwkv_shift[l], state_rwkv_wkv[l].astype(F32), (cache_swa_k[l], cache_swa_v[l]),
            bias_s, lp)
        xs = _layer_experts(ops_s, lp)[0].reshape(xs.shape)
        for lst, val in zip(outs, (wkv_p, sh_p, k_p[:, s_p - n_keep:], v_p[:, s_p - n_keep:],
                                   wkv_s, sh_s, k_s, v_s)):
            lst.append(val)
    return (jnp.concatenate(xp, axis=0), xs) + tuple(jnp.stack(o) for o in outs)
```

```python
import functools
import math

import numpy as np
import jax
import jax.numpy as jnp
from jax import lax
from jax.experimental import pallas as pl
from jax.experimental.pallas import tpu as pltpu
from jax.experimental.pallas import tpu_sc as plsc

F32 = jnp.float32
BF16 = jnp.bfloat16
I32 = jnp.int32

D_MODEL = 1024
CHUNK = 64
R_HEADS = 8
R_HD = 64
R_DIM = R_HEADS * R_HD
W_LORA = 64
A_LORA = 64
G_LORA = 128
R_PROJ = 3 * R_DIM + W_LORA + A_LORA + G_LORA
A_HEADS = 8
A_KV = 2
A_GROUP = A_HEADS // A_KV
A_HD = 64
A_DIM = A_HEADS * A_HD
KV_DIM = A_KV * A_HD
IN_COLS = R_PROJ + A_DIM + 2 * KV_DIM
WINDOW = 128
WIN_CHUNKS = WINDOW // CHUNK
NUM_BUCKETS = 32
MAX_DISTANCE = 128
PEER_HEADS = 8
N_KEYS = 128
PK_DIM = 256
PK_HALF = PK_DIM // 2
PEER_TOPK = 16
N_SEL = PEER_HEADS * PEER_TOPK
NORM_EPS = 1e-6
GN_EPS = 64e-5
NEG_INF = -1e30

LANES = 128
N_PAIRS = R_DIM // LANES
VMEM_LIMIT = 48 * 1024 * 1024
CAND_COLS = (16, 8, 8, 4, 4, 4, 4, 4, 1, 1, 1, 1, 1, 1, 1, 1)
assert all(nb >= PEER_TOPK // (a + 1) for a, nb in enumerate(CAND_COLS))
N_CAND = 64
assert sum(CAND_COLS) <= N_CAND
SC_CORES = 2
SC_SUBCORES = 16
SC_LANES = 16
SC_WORKERS = SC_CORES * SC_SUBCORES
SC_TOK_BLK = 32
SC_GROUP = SC_LANES
SC_N_GROUPS = N_SEL // SC_GROUP
SC_RSUB = 16
SC_NBUF = 4
SC_ROW_TILE = (D_MODEL // LANES, LANES)
PROMPT_GROUPS = 8


def _params(sem):
    return pltpu.CompilerParams(dimension_semantics=sem, vmem_limit_bytes=VMEM_LIMIT)


def _mm(a, b):
    return jnp.dot(a, b, preferred_element_type=F32)


def _mm_nt(a, b):
    return lax.dot_general(a, b, (((1,), (1,)), ((), ())), preferred_element_type=F32)


def _mm_tn(a, b):
    return lax.dot_general(a, b, (((0,), (0,)), ((), ())), preferred_element_type=F32)


def _split2(a):
    hi = a.astype(BF16)
    return hi, (a - hi.astype(F32)).astype(BF16)


def _split3(a):
    hi = a.astype(BF16)
    r = a - hi.astype(F32)
    mid = r.astype(BF16)
    return hi, mid, (r - mid.astype(F32)).astype(BF16)


def _x3(mm, a, b):
    ah, al = _split2(a)
    bh, bl = _split2(b)
    return mm(ah, bh) + mm(ah, bl) + mm(al, bh)


def _exact_lhs(mm, a_bf16, b):
    b0, b1, b2 = _split3(b)
    return mm(a_bf16, b0) + mm(a_bf16, b1) + mm(a_bf16, b2)


def _seg_sum(x, e_bf16):
    x0, x1, x2 = _split3(x)
    return _mm(x0, e_bf16) + _mm(x1, e_bf16) + _mm(x2, e_bf16)


def _sigmoid(x):
    return 1.0 / (1.0 + jnp.exp(-x))


def _norm_matmul_kernel(x_ref, g_ref, w_ref, o_ref, h_ref):
    @pl.when(pl.program_id(1) == 0)
    def _():
        x = x_ref[...]
        ms = jnp.mean(x * x, axis=-1, keepdims=True)
        h_ref[...] = (x * lax.rsqrt(ms + NORM_EPS) * g_ref[...]).astype(BF16)

    o_ref[...] = _mm(h_ref[...], w_ref[...])


def _norm_matmul(x, g, w_bf16, tn):
    n, k = x.shape
    m = w_bf16.shape[1]
    tm = min(n, 512)
    return pl.pallas_call(
        _norm_matmul_kernel,
        grid=(n // tm, m // tn),
        in_specs=[pl.BlockSpec((tm, k), lambda i, j: (i, 0)),
                  pl.BlockSpec((1, k), lambda i, j: (0, 0)),
                  pl.BlockSpec((k, tn), lambda i, j: (0, j))],
        out_specs=pl.BlockSpec((tm, tn), lambda i, j: (i, j)),
        out_shape=jax.ShapeDtypeStruct((n, m), F32),
        scratch_shapes=[pltpu.VMEM((tm, k), BF16)],
        compiler_params=_params(("parallel", "arbitrary")),
        name="norm_inproj",
    )(x, g, w_bf16)


def _prep_kernel(z_ref, zp_ref, sh_ref, mu_ref, w0_ref, a0_ref, kk_ref, ka_ref, rk_ref,
                 w2_ref, a2_ref, g2_ref, e_ref, qg_ref, kg_ref,
                 r_o, k_o, v_o, kk_o, b_o, lw_o, g_o, bo_o, qn_o, kn_o):
    i = pl.program_id(1)
    zt = z_ref[0]
    tp = zt.shape[0]
    zr = zt[:, :R_PROJ]
    prev_row = jnp.where(i == 0, sh_ref[0], zp_ref[0][7:8, :R_PROJ])
    row = lax.broadcasted_iota(I32, (tp, 1), 0)
    prev = jnp.where(row == 0, prev_row, pltpu.roll(zr, 1, axis=0))
    zs = zr + (prev - zr) * mu_ref[...]
    r = zs[:, 0:R_DIM]
    k = zs[:, R_DIM:2 * R_DIM]
    v = zs[:, 2 * R_DIM:3 * R_DIM]
    lo = zs[:, 3 * R_DIM:3 * R_DIM + W_LORA + A_LORA]
    g_lo = zs[:, 3 * R_DIM + W_LORA + A_LORA:R_PROJ]
    e = e_ref[...]
    w_in = -(w0_ref[...] + _mm(jnp.tanh(lo).astype(BF16), w2_ref[...]))
    softplus = jnp.maximum(w_in, 0.0) + jnp.log1p(jnp.exp(-jnp.abs(w_in)))
    w_log = -softplus - 0.5
    lw_o[0] = -jnp.exp(w_log)
    a = _sigmoid(a0_ref[...] + _mm(lo.astype(BF16), a2_ref[...]))
    g_o[0] = _mm(_sigmoid(g_lo).astype(BF16), g2_ref[...])
    kk = k * kk_ref[...]
    kk = kk / jnp.maximum(jnp.sqrt(_seg_sum(kk * kk, e)), 1e-12)
    k2 = k * (1.0 + (a - 1.0) * ka_ref[...])
    r_o[0] = r
    k_o[0] = k2
    v_o[0] = v
    kk_o[0] = kk
    b_o[0] = kk * a
    bo_o[0] = _seg_sum(r * k2 * rk_ref[...], e) * v
    q = zt[:, R_PROJ:R_PROJ + A_DIM]
    qn_o[0] = q * lax.rsqrt(_seg_sum(q * q, e) * (1.0 / A_HD) + NORM_EPS) * qg_ref[...]
    kx = zt[:, R_PROJ + A_DIM:R_PROJ + A_DIM + KV_DIM]
    e_kv = e[:KV_DIM, :KV_DIM]
    kn_o[0] = kx * lax.rsqrt(_seg_sum(kx * kx, e_kv) * (1.0 / A_HD) + NORM_EPS) * kg_ref[...]


def _prep(z, shift0, lp):
    b, t, _ = z.shape
    tp = min(t, 256)
    row = lambda a: a.reshape(1, -1)
    vec = lambda n: pl.BlockSpec((1, n), lambda bi, i: (0, 0))
    full = lambda s: pl.BlockSpec(s, lambda bi, i: (0, 0))
    wide = pl.BlockSpec((1, tp, R_DIM), lambda bi, i: (bi, i, 0))
    outs = [jax.ShapeDtypeStruct((b, t, R_DIM), F32)] * 9 + [jax.ShapeDtypeStruct((b, t, KV_DIM), F32)]
    return pl.pallas_call(
        _prep_kernel,
        grid=(b, t // tp),
        in_specs=[pl.BlockSpec((1, tp, IN_COLS), lambda bi, i: (bi, i, 0)),
                  pl.BlockSpec((1, 8, IN_COLS), lambda bi, i: (bi, jnp.maximum(i * (tp // 8) - 1, 0), 0)),
                  pl.BlockSpec((1, 1, R_PROJ), lambda bi, i: (bi, 0, 0)),
                  vec(R_PROJ), vec(R_DIM), vec(R_DIM), vec(R_DIM), vec(R_DIM), vec(R_DIM),
                  full((LANES, R_DIM)), full((LANES, R_DIM)), full((G_LORA, R_DIM)),
                  full((R_DIM, R_DIM)), vec(A_DIM), vec(KV_DIM)],
        out_specs=[wide] * 9 + [pl.BlockSpec((1, tp, KV_DIM), lambda bi, i: (bi, i, 0))],
        out_shape=outs,
        compiler_params=_params(("parallel", "parallel")),
        name="rwkv_prep",
    )(z, z, shift0.reshape(b, 1, R_PROJ), row(lp["mu"]), row(lp["w0"]), row(lp["a0"]), row(lp["k_k"]),
      row(lp["k_a"]), row(lp["r_k"]), lp["w2p"], lp["a2p"], lp["g2"], lp["e64"], row(lp["q_gain"]),
      row(lp["k_gain"]))


def _stack_heads(x, lo_mask):
    return jnp.concatenate([jnp.where(lo_mask, x, 0.0), jnp.where(lo_mask, 0.0, x)], axis=0)


def _chunk_kernel(r_ref, k_ref, v_ref, kk_ref, b_ref, lw_ref, s0_ref, y_ref, sf_ref, s_ref):
    c = pl.program_id(1)
    L = CHUNK

    @pl.when(c == 0)
    def _():
        s_ref[...] = s0_ref[0]

    lane = lax.broadcasted_iota(I32, (1, LANES), 1)
    lo_mask = lane < R_HD
    rr = lax.broadcasted_iota(I32, (L, 2 * L), 0)
    cc = lax.broadcasted_iota(I32, (L, 2 * L), 1)
    cc = jnp.where(cc >= L, cc - L, cc)
    strict = rr > cc
    incl = rr >= cc
    t_r = lax.broadcasted_iota(I32, (L, L), 0)
    t_c = lax.broadcasted_iota(I32, (L, L), 1)
    tri = (t_r >= t_c).astype(BF16)
    col2 = lax.broadcasted_iota(I32, (L, 2 * L), 1) < L
    eye_r = lax.broadcasted_iota(I32, (2 * L, 2 * L), 0)
    eye_c = lax.broadcasted_iota(I32, (2 * L, 2 * L), 1)
    eye = (eye_r == eye_c).astype(F32)

    pairs = range(N_PAIRS)
    sls = [slice(p * LANES, (p + 1) * LANES) for p in pairs]
    r = [r_ref[0][:, sl] for sl in sls]
    k = [k_ref[0][:, sl] for sl in sls]
    v = [v_ref[0][:, sl] for sl in sls]
    kk = [kk_ref[0][:, sl] for sl in sls]
    bb = [b_ref[0][:, sl] for sl in sls]
    lw = [lw_ref[0][:, sl] for sl in sls]
    s0 = [s_ref[p] for p in pairs]

    cl = [_exact_lhs(_mm, tri, lw[p]) for p in pairs]
    cl_last = [cl[p][L - 1:L, :] for p in pairs]
    e_neg = [jnp.exp(-cl[p]) for p in pairs]
    e_last = [jnp.exp(cl_last[p] - cl[p]) for p in pairs]
    rt = [r[p] * jnp.exp(cl[p]) for p in pairs]
    at = [kk[p] * jnp.exp(cl[p] - lw[p]) for p in pairs]

    gm = [_x3(_mm_nt, jnp.concatenate([at[p], rt[p]], axis=0),
              jnp.concatenate([_stack_heads(k[p] * e_neg[p], lo_mask), _stack_heads(bb[p] * e_neg[p], lo_mask)],
                              axis=0)) for p in pairs]
    mk = [jnp.where(strict, gm[p][:L, :2 * L], 0.0) for p in pairs]
    mb = [jnp.where(strict, gm[p][:L, 2 * L:], 0.0) for p in pairs]
    hk = [jnp.where(incl, gm[p][L:, :2 * L], 0.0) for p in pairs]
    hb = [jnp.where(incl, gm[p][L:, 2 * L:], 0.0) for p in pairs]

    nil = [-jnp.concatenate([jnp.where(col2, mb[p], 0.0), jnp.where(col2, 0.0, mb[p])], axis=0) for p in pairs]
    tinv = [eye + nil[p] for p in pairs]
    nil = [_x3(_mm, nil[p], nil[p]) for p in pairs]
    for it in range(5):
        tinv = [tinv[p] + _x3(_mm, nil[p], tinv[p]) for p in pairs]
        if it < 4:
            nil = [_x3(_mm, nil[p], nil[p]) for p in pairs]

    vs = [_stack_heads(v[p], lo_mask) for p in pairs]
    rhs_u = [-(_x3(_mm_nt, at[p], s0[p]) + _x3(_mm, mk[p], vs[p])) for p in pairs]
    us = [_x3(_mm, tinv[p], _stack_heads(rhs_u[p], lo_mask)) for p in pairs]
    for p in pairs:
        y_ref[0, :, sls[p]] = _x3(_mm_nt, rt[p], s0[p]) + _x3(_mm, hk[p], vs[p]) + _x3(_mm, hb[p], us[p])
    for p in pairs:
        s_ref[p] = (s0[p] * jnp.exp(cl_last[p])
                    + _x3(_mm_tn, vs[p], _stack_heads(k[p] * e_last[p], lo_mask))
                    + _x3(_mm_tn, us[p], _stack_heads(bb[p] * e_last[p], lo_mask)))

    @pl.when(c == pl.num_programs(1) - 1)
    def _():
        sf_ref[0] = s_ref[...]


def _chunk_scan(r, k, v, kk, bb, lw, s0_bd):
    b, t, _ = r.shape
    wide = pl.BlockSpec((1, CHUNK, R_DIM), lambda bi, c: (bi, c, 0))
    st = pl.BlockSpec((1, N_PAIRS, LANES, LANES), lambda bi, c: (bi, 0, 0, 0))
    return pl.pallas_call(
        _chunk_kernel,
        grid=(b, t // CHUNK),
        in_specs=[wide] * 6 + [st],
        out_specs=[wide, st],
        out_shape=[jax.ShapeDtypeStruct((b, t, R_DIM), F32),
                   jax.ShapeDtypeStruct((b, N_PAIRS, LANES, LANES), F32)],
        scratch_shapes=[pltpu.VMEM((N_PAIRS, LANES, LANES), F32)],
        compiler_params=_params(("parallel", "arbitrary")),
        name="rwkv_chunk",
    )(r, k, v, kk, bb, lw, s0_bd)


def _swa_kernel(*refs, n_seg, banded):
    q_ref = refs[0]
    k_refs = refs[1:1 + n_seg]
    v_refs = refs[1 + n_seg:1 + 2 * n_seg]
    bias_ref, sink_ref, o_ref = refs[1 + 2 * n_seg:]
    c = pl.program_id(1)
    q = q_ref[0]
    kcat = jnp.concatenate([kr[0] for kr in k_refs], axis=0)
    vcat = jnp.concatenate([vr[0] for vr in v_refs], axis=0)
    n_k = kcat.shape[0]
    lane = lax.broadcasted_iota(I32, (1, LANES), 1)
    lo_mask = lane < A_HD
    k_sw = pltpu.roll(kcat, A_HD, axis=1)
    v_sw = pltpu.roll(vcat, A_HD, axis=1)
    k_dup = [jnp.where(lo_mask, kcat, k_sw).astype(BF16), jnp.where(lo_mask, k_sw, kcat).astype(BF16)]
    v_dup = [jnp.where(lo_mask, vcat, v_sw).astype(BF16), jnp.where(lo_mask, v_sw, vcat).astype(BF16)]
    if banded:
        key_chunk = c - WIN_CHUNKS + lax.broadcasted_iota(I32, (1, n_k), 1) // CHUNK
        valid = key_chunk >= 0
    n_q = q.shape[0]
    hi_mask = jnp.logical_not(lo_mask)
    for kvh in range(A_KV):
        heads = range(kvh * A_GROUP, (kvh + 1) * A_GROUP)
        qs = jnp.concatenate(
            [jnp.where(lo_mask if hq % 2 == 0 else hi_mask, q[:, (hq // 2) * LANES:(hq // 2 + 1) * LANES], 0.0)
             for hq in heads], axis=0).astype(BF16)
        bias = jnp.concatenate([bias_ref[hq] for hq in heads], axis=0)
        sink = jnp.concatenate([jnp.broadcast_to(sink_ref[0:1, hq:hq + 1], (n_q, 1)) for hq in heads], axis=0)
        s = _mm_nt(qs, k_dup[kvh]) * (A_HD ** -0.5) + bias
        if banded:
            s = jnp.where(valid, s, NEG_INF)
        m = jnp.maximum(jnp.max(s, axis=-1, keepdims=True), sink)
        pr = jnp.exp(s - m)
        den = jnp.sum(pr, axis=-1, keepdims=True) + jnp.exp(sink - m)
        o = _mm((pr / den).astype(BF16), v_dup[kvh])
        for i in range(A_GROUP // 2):
            pair = kvh * (A_GROUP // 2) + i
            o_ref[0, :, pair * LANES:(pair + 1) * LANES] = jnp.where(
                lo_mask, o[2 * i * n_q:(2 * i + 1) * n_q], o[(2 * i + 1) * n_q:(2 * i + 2) * n_q])


def _swa_prompt(qn, kn, z, bias, sink):
    b, t, _ = qn.shape
    v_col = (R_PROJ + A_DIM + KV_DIM) // KV_DIM
    seg = lambda s, col: pl.BlockSpec(
        (1, CHUNK, KV_DIM), lambda bi, c: (bi, jnp.maximum(c - WIN_CHUNKS + s, 0), col))
    n_seg = WIN_CHUNKS + 1
    return pl.pallas_call(
        functools.partial(_swa_kernel, n_seg=n_seg, banded=True),
        grid=(b, t // CHUNK),
        in_specs=[pl.BlockSpec((1, CHUNK, A_DIM), lambda bi, c: (bi, c, 0))]
                 + [seg(s, 0) for s in range(n_seg)] + [seg(s, v_col) for s in range(n_seg)]
                 + [pl.BlockSpec(bias.shape, lambda bi, c: (0, 0, 0)),
                    pl.BlockSpec((1, A_HEADS), lambda bi, c: (0, 0))],
        out_specs=pl.BlockSpec((1, CHUNK, A_DIM), lambda bi, c: (bi, c, 0)),
        out_shape=jax.ShapeDtypeStruct((b, t, A_DIM), F32),
        compiler_params=_params(("parallel", "parallel")),
        name="swa_prompt",
    )(qn, *([kn] * n_seg), *([z] * n_seg), bias, sink.reshape(1, A_HEADS))


def _swa_sample(qn, kn, v_new, k_cache, v_cache, bias, sink):
    b, t, _ = qn.shape
    n_cache = k_cache.shape[1]
    cur = lambda w: pl.BlockSpec((1, t, w), lambda bi, c: (bi, 0, 0))
    old = pl.BlockSpec((1, n_cache, KV_DIM), lambda bi, c: (bi, 0, 0))
    return pl.pallas_call(
        functools.partial(_swa_kernel, n_seg=2, banded=False),
        grid=(b, 1),
        in_specs=[cur(A_DIM), old, cur(KV_DIM), old, cur(KV_DIM),
                  pl.BlockSpec(bias.shape, lambda bi, c: (0, 0, 0)),
                  pl.BlockSpec((1, A_HEADS), lambda bi, c: (0, 0))],
        out_specs=cur(A_DIM),
        out_shape=jax.ShapeDtypeStruct((b, t, A_DIM), F32),
        compiler_params=_params(("parallel", "parallel")),
        name="swa_sample",
    )(qn, k_cache, kn, v_cache, v_new, bias, sink.reshape(1, A_HEADS))


def _mix_out_kernel(y_ref, bo_ref, g_ref, a_ref, x_ref, gng_ref, gnb_ref, e_ref, wo_ref, ln_ref, wq_ref,
                    xo_ref, h_ref, qq_ref):
    e = e_ref[...]
    y = y_ref[...]
    mu = _seg_sum(y, e) * (1.0 / R_HD)
    d = y - mu
    var = _seg_sum(d * d, e) * (1.0 / R_HD)
    yn = d * lax.rsqrt(var + GN_EPS) * gng_ref[...] + gnb_ref[...]
    mix_r = ((yn + bo_ref[...]) * g_ref[...]).astype(BF16)
    x = (x_ref[...] + _mm(mix_r, wo_ref[:R_DIM, :]) + _mm(a_ref[...].astype(BF16), wo_ref[R_DIM:, :]))
    xo_ref[...] = x
    ms = jnp.mean(x * x, axis=-1, keepdims=True)
    h = x * lax.rsqrt(ms + NORM_EPS) * ln_ref[...]
    h_ref[...] = h
    qq_ref[...] = _mm(h.astype(BF16), wq_ref[...])


def _mix_out(y, bonus, g, a_out, x, lp):
    n = x.shape[0]
    tm = min(n, 256)
    nq = PEER_HEADS * PK_DIM
    rows = lambda w: pl.BlockSpec((tm, w), lambda i: (i, 0))
    full = lambda s: pl.BlockSpec(s, lambda i: (0, 0))
    return pl.pallas_call(
        _mix_out_kernel,
        grid=(n // tm,),
        in_specs=[rows(R_DIM), rows(R_DIM), rows(R_DIM), rows(A_DIM), rows(D_MODEL),
                  full((1, R_DIM)), full((1, R_DIM)), full((R_DIM, R_DIM)),
                  full((D_MODEL, D_MODEL)), full((1, D_MODEL)), full((D_MODEL, nq))],
        out_specs=[rows(D_MODEL), rows(D_MODEL), rows(nq)],
        out_shape=[jax.ShapeDtypeStruct((n, D_MODEL), F32), jax.ShapeDtypeStruct((n, D_MODEL), F32),
                   jax.ShapeDtypeStruct((n, nq), F32)],
        compiler_params=_params(("parallel",)),
        name="mix_out_query",
    )(y, bonus, g, a_out, x, lp["gn_g"].reshape(1, -1), lp["gn_b"].reshape(1, -1), lp["e64"],
      lp["w_out"], lp["ln2_g"].reshape(1, -1), lp["w_pq"])


def _pick_rounds(s_ref, n_rows, emit):
    tb = s_ref.shape[1]
    rowid = lax.broadcasted_iota(I32, (n_rows, tb), 0)
    for rnd in range(PEER_TOPK):
        s = s_ref[...]
        m = jnp.max(s, axis=0, keepdims=True)
        idx = jnp.min(jnp.where(s == m, rowid, n_rows), axis=0, keepdims=True)
        hit = rowid == idx
        s_ref[...] = jnp.where(hit, -jnp.inf, s)
        emit(rnd, m, idx, hit)


def _topk_kernel(qq_ref, keys_ref, ei_ref, gate_ref, s_ref, sv_ref, si_ref, c_ref, ci_ref, ts_ref):
    tb = qq_ref.shape[0]
    for half in range(2):
        qh = qq_ref[:, half * PK_HALF:(half + 1) * PK_HALF]
        s_ref[...] = _x3(_mm_nt, keys_ref[half], qh)

        def emit1(rnd, m, idx, hit, half=half):
            sv_ref[half, rnd:rnd + 1, :] = m
            si_ref[half, rnd:rnd + 1, :] = idx

        _pick_rounds(s_ref, N_KEYS, emit1)

    row0 = 0
    for a, nb in enumerate(CAND_COLS):
        if nb == 1:
            break
        c_ref[row0:row0 + nb, :] = sv_ref[0, a:a + 1, :] + sv_ref[1, 0:nb, :]
        ci_ref[row0:row0 + nb, :] = si_ref[0, a:a + 1, :] * N_KEYS + si_ref[1, 0:nb, :]
        row0 += nb
    n_one = PEER_TOPK - a
    c_ref[row0:row0 + n_one, :] = sv_ref[0, a:, :] + sv_ref[1, 0:1, :]
    ci_ref[row0:row0 + n_one, :] = si_ref[0, a:, :] * N_KEYS + si_ref[1, 0:1, :]
    row0 += n_one
    c_ref[row0:, :] = jnp.full((N_CAND - row0, tb), -jnp.inf, F32)
    ci_ref[row0:, :] = jnp.zeros((N_CAND - row0, tb), I32)

    def emit2(rnd, m, idx, hit):
        ts_ref[rnd:rnd + 1, :] = m
        ei_ref[0, rnd:rnd + 1, :] = jnp.max(jnp.where(hit, ci_ref[...], -1), axis=0, keepdims=True)

    _pick_rounds(c_ref, N_CAND, emit2)
    ts = ts_ref[...]
    ex = jnp.exp(ts - ts[0:1, :])
    gate_ref[0] = ex / jnp.sum(ex, axis=0, keepdims=True)


def _topk(qq, sub_keys):
    n = qq.shape[0]
    tb = min(n, 512)
    out = pl.BlockSpec((1, PEER_TOPK, tb), lambda i, h: (h, 0, i))
    return pl.pallas_call(
        _topk_kernel,
        grid=(n // tb, PEER_HEADS),
        in_specs=[pl.BlockSpec((tb, PK_DIM), lambda i, h: (i, h)),
                  pl.BlockSpec((2, N_KEYS, PK_HALF), lambda i, h: (0, 0, 0))],
        out_specs=[out, out],
        out_shape=[jax.ShapeDtypeStruct((PEER_HEADS, PEER_TOPK, n), I32),
                   jax.ShapeDtypeStruct((PEER_HEADS, PEER_TOPK, n), F32)],
        scratch_shapes=[pltpu.VMEM((N_KEYS, tb), F32), pltpu.VMEM((2, PEER_TOPK, tb), F32),
                        pltpu.VMEM((2, PEER_TOPK, tb), I32), pltpu.VMEM((N_CAND, tb), F32),
                        pltpu.VMEM((N_CAND, tb), I32), pltpu.VMEM((PEER_TOPK, tb), F32)],
        compiler_params=_params(("parallel", "parallel")),
        name="peer_topk",
    )(qq, sub_keys)


def _expert_gather(tbl_hbm, idx_v, rows_v, sem, tt, g, buf):
    return pltpu.make_async_copy(tbl_hbm.at[idx_v[pl.ds(tt * N_SEL + g * SC_GROUP, SC_GROUP)]],
                                 rows_v.at[buf], sem.at[buf])


def _sc_tok_blk(n):
    return math.gcd(n // SC_WORKERS, SC_TOK_BLK)


def _sc_token_blocks(n, tbl_hbm, idx_v, rows_v, sem, load_block, compute, store_block):
    wid = lax.axis_index("s") * SC_CORES + lax.axis_index("c")
    npw = n // SC_WORKERS
    tb = _sc_tok_blk(n)
    steps = tb * SC_N_GROUPS

    @pl.loop(0, npw // tb)
    def _(blk):
        tok0 = wid * npw + blk * tb
        load_block(tok0)
        for s in range(SC_NBUF - 1):
            _expert_gather(tbl_hbm, idx_v, rows_v, sem, s // SC_N_GROUPS, s % SC_N_GROUPS, s).start()

        @pl.loop(0, steps, step=SC_NBUF)
        def _(s0):
            for b in range(SC_NBUF):
                s = s0 + b
                ahead = s + SC_NBUF - 1

                @pl.when(ahead < steps)
                def _():
                    _expert_gather(tbl_hbm, idx_v, rows_v, sem, ahead // SC_N_GROUPS, ahead % SC_N_GROUPS,
                                   (b + SC_NBUF - 1) % SC_NBUF).start()

                _expert_gather(tbl_hbm, idx_v, rows_v, sem, s // SC_N_GROUPS, s % SC_N_GROUPS, b).wait()
                compute(s // SC_N_GROUPS, s % SC_N_GROUPS, rows_v.at[b])

        store_block(tok0)


def _row_chunk(j):
    per_line = LANES // SC_LANES
    return j // per_line, pl.ds((j % per_line) * SC_LANES, SC_LANES)


def _expert_cost(n):
    elems = n * N_SEL * D_MODEL
    return pl.CostEstimate(flops=2 * elems, transcendentals=0, bytes_accessed=4 * elems)


def _sc_mesh():
    return plsc.VectorSubcoreMesh(core_axis_name="c", subcore_axis_name="s",
                                  num_cores=SC_CORES, num_subcores=SC_SUBCORES)


def _sc_hidden(eidx, h, table):
    n = h.shape[0]
    tb = _sc_tok_blk(n)

    @functools.partial(
        pl.kernel, out_type=jax.ShapeDtypeStruct((n * N_SEL,), F32), mesh=_sc_mesh(),
        compiler_params=pltpu.CompilerParams(needs_layout_passes=False),
        scratch_types=[pltpu.VMEM((tb * N_SEL,), I32), pltpu.VMEM((tb * D_MODEL,), F32),
                       pltpu.VMEM((SC_NBUF, SC_GROUP) + SC_ROW_TILE, F32), pltpu.VMEM((tb * N_SEL,), F32),
                       pltpu.SemaphoreType.DMA((SC_NBUF,))],
        cost_estimate=_expert_cost(n), name="peer_hidden_sc")
    def run(eidx_hbm, h_hbm, tbl_hbm, hid_hbm, idx_v, h_v, rows_v, hid_v, sem):
        lane = lax.iota(I32, SC_LANES)

        def load_block(tok0):
            pltpu.sync_copy(eidx_hbm.at[pl.ds(tok0 * N_SEL, tb * N_SEL)], idx_v)
            pltpu.sync_copy(h_hbm.at[pl.ds(tok0 * D_MODEL, tb * D_MODEL)], h_v)

        def compute(tt, g, rows):
            accs = []
            for sub in range(SC_GROUP // SC_RSUB):
                zero = tuple(jnp.zeros((SC_LANES,), F32) for _ in range(SC_RSUB))

                @plsc.parallel_loop(0, D_MODEL // SC_LANES, unroll=2, carry=zero)
                def part(j, acc):
                    hvj = h_v[pl.ds(tt * D_MODEL + j * SC_LANES, SC_LANES)]
                    return tuple(acc[r] + rows[(sub * SC_RSUB + r, *_row_chunk(j))] * hvj
                                 for r in range(SC_RSUB))

                accs.extend(part)
            tot = jnp.zeros((SC_LANES,), F32)
            for r in range(SC_GROUP):
                tot = jnp.where(lane == r, jnp.sum(accs[r]), tot)
            hid_v[pl.ds(tt * N_SEL + g * SC_GROUP, SC_GROUP)] = tot

        def store_block(tok0):
            pltpu.sync_copy(hid_v, hid_hbm.at[pl.ds(tok0 * N_SEL, tb * N_SEL)])

        _sc_token_blocks(n, tbl_hbm, idx_v, rows_v, sem, load_block, compute, store_block)

    return run(eidx.reshape(-1), h.reshape(-1), table).reshape(n, N_SEL)


def _sc_combine(eidx, coef, x, table):
    n = x.shape[0]
    tb = _sc_tok_blk(n)

    @functools.partial(
        pl.kernel, out_type=jax.ShapeDtypeStruct((n * D_MODEL,), F32), mesh=_sc_mesh(),
        compiler_params=pltpu.CompilerParams(needs_layout_passes=False),
        scratch_types=[pltpu.VMEM((tb * N_SEL,), I32), pltpu.VMEM((tb * N_SEL,), F32),
                       pltpu.VMEM((SC_NBUF, SC_GROUP) + SC_ROW_TILE, F32), pltpu.VMEM((tb * D_MODEL,), F32),
                       pltpu.SemaphoreType.DMA((SC_NBUF,))],
        cost_estimate=_expert_cost(n), name="peer_combine_sc")
    def run(eidx_hbm, c_hbm, x_hbm, tbl_hbm, out_hbm, idx_v, c_v, rows_v, out_v, sem):
        lane = lax.iota(I32, SC_LANES)

        def load_block(tok0):
            pltpu.sync_copy(eidx_hbm.at[pl.ds(tok0 * N_SEL, tb * N_SEL)], idx_v)
            pltpu.sync_copy(c_hbm.at[pl.ds(tok0 * N_SEL, tb * N_SEL)], c_v)
            pltpu.sync_copy(x_hbm.at[pl.ds(tok0 * D_MODEL, tb * D_MODEL)], out_v)

        def compute(tt, g, rows):
            cvec = c_v[pl.ds(tt * N_SEL + g * SC_GROUP, SC_GROUP)]
            coefs = [jnp.sum(jnp.where(lane == r, cvec, 0.0)) for r in range(SC_GROUP)]

            @plsc.parallel_loop(0, D_MODEL // SC_LANES, unroll=2)
            def _(j):
                sl = pl.ds(tt * D_MODEL + j * SC_LANES, SC_LANES)
                acc = out_v[sl]
                for r in range(SC_GROUP):
                    acc = acc + rows[(r, *_row_chunk(j))] * coefs[r]
                out_v[sl] = acc

        def store_block(tok0):
            pltpu.sync_copy(out_v, out_hbm.at[pl.ds(tok0 * D_MODEL, tb * D_MODEL)])

        _sc_token_blocks(n, tbl_hbm, idx_v, rows_v, sem, load_block, compute, store_block)

    return run(eidx.reshape(-1), coef.reshape(-1), x.reshape(-1), table).reshape(n, D_MODEL)


def _gate_act_kernel(hid_ref, gate_ref, o_ref):
    hid = hid_ref[...]
    o_ref[...] = gate_ref[...] * (0.5 * hid * (1.0 + lax.erf(hid * np.float32(np.sqrt(0.5)))))


def _gate_act(hid, gate):
    n = hid.shape[0]
    tm = math.gcd(n, 2048)
    rows = pl.BlockSpec((tm, N_SEL), lambda i: (i, 0))
    return pl.pallas_call(
        _gate_act_kernel, grid=(n // tm,), in_specs=[rows, rows], out_specs=rows,
        out_shape=jax.ShapeDtypeStruct((n, N_SEL), F32), compiler_params=_params(("parallel",)),
        name="peer_gate_act",
    )(hid, gate)


def _t5_bucket(rel):
    nb = NUM_BUCKETS // 2
    max_exact = nb // 2
    ret = jnp.where(rel > 0, nb, 0)
    n = jnp.abs(rel)
    nf = jnp.maximum(n, 1).astype(F32)
    large = max_exact + (jnp.log(nf / max_exact) / math.log(MAX_DISTANCE / max_exact)
                         * (nb - max_exact)).astype(I32)
    large = jnp.minimum(large, nb - 1)
    return ret + jnp.where(n < max_exact, n, large)


def _rel_bias(rel_bias, n_q, n_k, n_before):
    rel = (jnp.arange(n_k)[None, :] - n_before) - jnp.arange(n_q)[:, None]
    return jnp.transpose(rel_bias[_t5_bucket(rel)].astype(F32), (2, 0, 1))


def _state_to_pairs(wkv):
    b = wkv.shape[0]
    s = wkv.reshape(b, N_PAIRS, 2, R_HD, R_HD)
    z = jnp.zeros_like(s[:, :, 0])
    top = jnp.concatenate([s[:, :, 0], z], axis=-1)
    bot = jnp.concatenate([z, s[:, :, 1]], axis=-1)
    return jnp.concatenate([top, bot], axis=-2)


def _pairs_to_state(s_bd):
    b = s_bd.shape[0]
    return jnp.stack([s_bd[:, :, :R_HD, :R_HD], s_bd[:, :, R_HD:, R_HD:]], axis=2).reshape(
        b, R_HEADS, R_HD, R_HD)


def _layer_dense(x, shift0, wkv0, kv_cache, bias, lp):
    b, t, _ = x.shape
    n = b * t
    z = _norm_matmul(x.reshape(n, D_MODEL), lp["ln1_g"].reshape(1, -1), lp["w_in"], 512).reshape(b, t, IN_COLS)
    r, k2, v, kk, bb, lw, g, bonus, qn, kn = _prep(z, shift0, lp)
    v_new = z[:, :, R_PROJ + A_DIM + KV_DIM:]
    if t % CHUNK:
        pad = lambda a: jnp.pad(a, ((0, 0), (0, CHUNK - t % CHUNK), (0, 0)))
        y, s_fin = _chunk_scan(*(pad(a) for a in (r, k2, v, kk, bb, lw)), _state_to_pairs(wkv0))
        y = y[:, :t]
    else:
        y, s_fin = _chunk_scan(r, k2, v, kk, bb, lw, _state_to_pairs(wkv0))
    if kv_cache is None:
        a_out = _swa_prompt(qn, kn, z, bias, lp["sink"])
    else:
        a_out = _swa_sample(qn, kn, v_new, kv_cache[0].reshape(b, -1, KV_DIM),
                            kv_cache[1].reshape(b, -1, KV_DIM), bias, lp["sink"])
    flat = lambda a: a.reshape(n, a.shape[-1])
    x1, h, qq = _mix_out(flat(y), flat(bonus), flat(g), flat(a_out), flat(x), lp)
    eidx, gate = _topk(qq, lp["sub_keys"])
    sel = lambda a: jnp.transpose(a, (2, 0, 1)).reshape(n, N_SEL)
    return ((sel(eidx), sel(gate), h, x1),
            (_pairs_to_state(s_fin), z[:, -1, :R_PROJ],
             kn.reshape(b, t, A_KV, A_HD), v_new.reshape(b, t, A_KV, A_HD)))


def _layer_experts(ops, lp, after=None):
    eidx, gate, h, x1 = ops
    if after is not None:
        h, after = lax.optimization_barrier((h, after))
    coef = _gate_act(_sc_hidden(eidx, h, lp["peer_u"]), gate)
    return _sc_combine(eidx, coef, x1, lp["peer_v"]), after


def kernel(x_prompt, x_sample, state_rwkv_wkv, state_rwkv_shift, cache_swa_k, cache_swa_v, ln1_g, w_in,
           mu_shift, w0, w2, a0, a2, g2, k_k, k_a, r_k, gn_g, gn_b, q_norm_g, k_norm_g, attn_sink,
           rel_bias, w_out, ln2_g, w_pq, sub_keys, peer_u, peer_v):
    depth = w_in.shape[0]
    b_p, s_p = x_prompt.shape[:2]
    b_s, t_s = x_sample.shape[:2]
    n_cache = cache_swa_k.shape[2]
    n_keep = min(WINDOW, s_p)
    bias_p = _rel_bias(rel_bias, CHUNK, (WIN_CHUNKS + 1) * CHUNK, WIN_CHUNKS * CHUNK)
    bias_s = _rel_bias(rel_bias, t_s, n_cache + t_s, n_cache)
    head_id = jnp.arange(R_DIM) // R_HD
    e64 = (head_id[:, None] == head_id[None, :]).astype(BF16)
    zpad = jnp.zeros((LANES - W_LORA, R_DIM), F32)
    n_groups = PROMPT_GROUPS if b_p % PROMPT_GROUPS == 0 else 1
    xp, xs = jnp.split(x_prompt, n_groups, axis=0), x_sample
    outs = [[] for _ in range(8)]
    for l in range(depth):
        lp = {
            "ln1_g": ln1_g[l], "w_in": w_in[l].astype(BF16), "mu": mu_shift[l], "w0": w0[l], "a0": a0[l],
            "k_k": k_k[l], "k_a": k_a[l], "r_k": r_k[l].reshape(-1), "gn_g": gn_g[l], "gn_b": gn_b[l],
            "w2p": jnp.concatenate([w2[l], zpad], axis=0).astype(BF16),
            "a2p": jnp.concatenate([zpad, a2[l]], axis=0).astype(BF16),
            "g2": g2[l].astype(BF16), "e64": e64,
            "q_gain": jnp.tile(q_norm_g[l], A_HEADS), "k_gain": jnp.tile(k_norm_g[l], A_KV),
            "sink": attn_sink[l].astype(F32), "w_out": w_out[l].astype(BF16), "ln2_g": ln2_g[l],
            "w_pq": w_pq[l].astype(BF16), "sub_keys": sub_keys[l], "peer_u": peer_u[l].reshape((-1,) + SC_ROW_TILE), "peer_v": peer_v[l].reshape((-1,) + SC_ROW_TILE),
        }
        parts = []
        for xg in xp:
            if parts:
                (eidx, *rest), states = parts[-1]
                xg, eidx = lax.optimization_barrier((xg, eidx))
                parts[-1] = ((eidx, *rest), states)
            parts.append(_layer_dense(xg, jnp.zeros((xg.shape[0], R_PROJ), F32),
                                      jnp.zeros((xg.shape[0], R_HEADS, R_HD, R_HD), F32), None, bias_p, lp))
        wkv_p, sh_p, k_p, v_p = (jnp.concatenate([pt[1][i] for pt in parts], axis=0) for i in range(4))
        outs_x = []
        for ops, _ in parts:
            if outs_x:
                xo, outs_x[-1] = _layer_experts(ops, lp, after=outs_x[-1])
            else:
                xo, _ = _layer_experts(ops, lp)
            outs_x.append(xo)
        xp = [xo.reshape(xg.shape) for xo, xg in zip(outs_x, xp)]
        ops_s, (wkv_s, sh_s, k_s, v_s) = _layer_dense(
            xs, state_rwkv_shift[l], state_rwkv_wkv[l].astype(F32), (cache_swa_k[l], cache_swa_v[l]),
            bias_s, lp)
        xs = _layer_experts(ops_s, lp)[0].reshape(xs.shape)
        for lst, val in zip(outs, (wkv_p, sh_p, k_p[:, s_p - n_keep:], v_p[:, s_p - n_keep:],
                                   wkv_s, sh_s, k_s, v_s)):
            lst.append(val)
    return (jnp.concatenate(xp, axis=0), xs) + tuple(jnp.stack(o) for o in outs)
```

```python
import functools
import math

import numpy as np
import jax
import jax.numpy as jnp
from jax import lax
from jax.experimental import pallas as pl
from jax.experimental.pallas import tpu as pltpu
from jax.experimental.pallas import tpu_sc as plsc

F32 = jnp.float32
BF16 = jnp.bfloat16
I32 = jnp.int32

D_MODEL = 1024
CHUNK = 64
R_HEADS = 8
R_HD = 64
R_DIM = R_HEADS * R_HD
W_LORA = 64
A_LORA = 64
G_LORA = 128
R_PROJ = 3 * R_DIM + W_LORA + A_LORA + G_LORA
A_HEADS = 8
A_KV = 2
A_GROUP = A_HEADS // A_KV
A_HD = 64
A_DIM = A_HEADS * A_HD
KV_DIM = A_KV * A_HD
IN_COLS = R_PROJ + A_DIM + 2 * KV_DIM
WINDOW = 128
WIN_CHUNKS = WINDOW // CHUNK
NUM_BUCKETS = 32
MAX_DISTANCE = 128
PEER_HEADS = 8
N_KEYS = 128
PK_DIM = 256
PK_HALF = PK_DIM // 2
PEER_TOPK = 16
N_SEL = PEER_HEADS * PEER_TOPK
NORM_EPS = 1e-6
GN_EPS = 64e-5
NEG_INF = -1e30

LANES = 128
N_PAIRS = R_DIM // LANES
VMEM_LIMIT = 48 * 1024 * 1024
CAND_COLS = (16, 8, 8, 4, 4, 4, 4, 4, 1, 1, 1, 1, 1, 1, 1, 1)
assert all(nb >= PEER_TOPK // (a + 1) for a, nb in enumerate(CAND_COLS))
N_CAND = 64
assert sum(CAND_COLS) <= N_CAND
SC_CORES = 2
SC_SUBCORES = 16
SC_LANES = 16
SC_WORKERS = SC_CORES * SC_SUBCORES
SC_TOK_BLK = 32
SC_GROUP = SC_LANES
SC_N_GROUPS = N_SEL // SC_GROUP
SC_NBUF = 4
SC_ROW_WORDS = D_MODEL // 2
HI_MASK = -65536
PROMPT_GROUPS = 8


def _params(sem):
    return pltpu.CompilerParams(dimension_semantics=sem, vmem_limit_bytes=VMEM_LIMIT)


def _mm(a, b):
    return jnp.dot(a, b, preferred_element_type=F32)


def _mm_nt(a, b):
    return lax.dot_general(a, b, (((1,), (1,)), ((), ())), preferred_element_type=F32)


def _mm_tn(a, b):
    return lax.dot_general(a, b, (((0,), (0,)), ((), ())), preferred_element_type=F32)


def _split2(a):
    hi = a.astype(BF16)
    return hi, (a - hi.astype(F32)).astype(BF16)


def _split3(a):
    hi = a.astype(BF16)
    r = a - hi.astype(F32)
    mid = r.astype(BF16)
    return hi, mid, (r - mid.astype(F32)).astype(BF16)


def _x3(mm, a, b):
    ah, al = _split2(a)
    bh, bl = _split2(b)
    return mm(ah, bh) + mm(ah, bl) + mm(al, bh)


def _exact_lhs(mm, a_bf16, b):
    b0, b1, b2 = _split3(b)
    return mm(a_bf16, b0) + mm(a_bf16, b1) + mm(a_bf16, b2)


def _seg_sum(x, e_bf16):
    x0, x1, x2 = _split3(x)
    return _mm(x0, e_bf16) + _mm(x1, e_bf16) + _mm(x2, e_bf16)


def _sigmoid(x):
    return 1.0 / (1.0 + jnp.exp(-x))


def _norm_matmul_kernel(x_ref, g_ref, w_ref, o_ref, h_ref):
    @pl.when(pl.program_id(1) == 0)
    def _():
        x = x_ref[...]
        ms = jnp.mean(x * x, axis=-1, keepdims=True)
        h_ref[...] = (x * lax.rsqrt(ms + NORM_EPS) * g_ref[...]).astype(BF16)

    o_ref[...] = _mm(h_ref[...], w_ref[...])


def _norm_matmul(x, g, w_bf16, tn):
    n, k = x.shape
    m = w_bf16.shape[1]
    tm = min(n, 512)
    return pl.pallas_call(
        _norm_matmul_kernel,
        grid=(n // tm, m // tn),
        in_specs=[pl.BlockSpec((tm, k), lambda i, j: (i, 0)),
                  pl.BlockSpec((1, k), lambda i, j: (0, 0)),
                  pl.BlockSpec((k, tn), lambda i, j: (0, j))],
        out_specs=pl.BlockSpec((tm, tn), lambda i, j: (i, j)),
        out_shape=jax.ShapeDtypeStruct((n, m), F32),
        scratch_shapes=[pltpu.VMEM((tm, k), BF16)],
        compiler_params=_params(("parallel", "arbitrary")),
        name="norm_inproj",
    )(x, g, w_bf16)


def _prep_kernel(z_ref, zp_ref, sh_ref, mu_ref, w0_ref, a0_ref, kk_ref, ka_ref, rk_ref,
                 w2_ref, a2_ref, g2_ref, e_ref, qg_ref, kg_ref,
                 r_o, k_o, v_o, kk_o, b_o, lw_o, g_o, bo_o, qn_o, kn_o):
    i = pl.program_id(1)
    zt = z_ref[0]
    tp = zt.shape[0]
    zr = zt[:, :R_PROJ]
    prev_row = jnp.where(i == 0, sh_ref[0], zp_ref[0][7:8, :R_PROJ])
    row = lax.broadcasted_iota(I32, (tp, 1), 0)
    prev = jnp.where(row == 0, prev_row, pltpu.roll(zr, 1, axis=0))
    zs = zr + (prev - zr) * mu_ref[...]
    r = zs[:, 0:R_DIM]
    k = zs[:, R_DIM:2 * R_DIM]
    v = zs[:, 2 * R_DIM:3 * R_DIM]
    lo = zs[:, 3 * R_DIM:3 * R_DIM + W_LORA + A_LORA]
    g_lo = zs[:, 3 * R_DIM + W_LORA + A_LORA:R_PROJ]
    e = e_ref[...]
    w_in = -(w0_ref[...] + _mm(jnp.tanh(lo).astype(BF16), w2_ref[...]))
    softplus = jnp.maximum(w_in, 0.0) + jnp.log1p(jnp.exp(-jnp.abs(w_in)))
    w_log = -softplus - 0.5
    lw_o[0] = -jnp.exp(w_log)
    a = _sigmoid(a0_ref[...] + _mm(lo.astype(BF16), a2_ref[...]))
    g_o[0] = _mm(_sigmoid(g_lo).astype(BF16), g2_ref[...])
    kk = k * kk_ref[...]
    kk = kk / jnp.maximum(jnp.sqrt(_seg_sum(kk * kk, e)), 1e-12)
    k2 = k * (1.0 + (a - 1.0) * ka_ref[...])
    r_o[0] = r
    k_o[0] = k2
    v_o[0] = v
    kk_o[0] = kk
    b_o[0] = kk * a
    bo_o[0] = _seg_sum(r * k2 * rk_ref[...], e) * v
    q = zt[:, R_PROJ:R_PROJ + A_DIM]
    qn_o[0] = q * lax.rsqrt(_seg_sum(q * q, e) * (1.0 / A_HD) + NORM_EPS) * qg_ref[...]
    kx = zt[:, R_PROJ + A_DIM:R_PROJ + A_DIM + KV_DIM]
    e_kv = e[:KV_DIM, :KV_DIM]
    kn_o[0] = kx * lax.rsqrt(_seg_sum(kx * kx, e_kv) * (1.0 / A_HD) + NORM_EPS) * kg_ref[...]


def _prep(z, shift0, lp):
    b, t, _ = z.shape
    tp = min(t, 256)
    row = lambda a: a.reshape(1, -1)
    vec = lambda n: pl.BlockSpec((1, n), lambda bi, i: (0, 0))
    full = lambda s: pl.BlockSpec(s, lambda bi, i: (0, 0))
    wide = pl.BlockSpec((1, tp, R_DIM), lambda bi, i: (bi, i, 0))
    outs = [jax.ShapeDtypeStruct((b, t, R_DIM), F32)] * 9 + [jax.ShapeDtypeStruct((b, t, KV_DIM), F32)]
    return pl.pallas_call(
        _prep_kernel,
        grid=(b, t // tp),
        in_specs=[pl.BlockSpec((1, tp, IN_COLS), lambda bi, i: (bi, i, 0)),
                  pl.BlockSpec((1, 8, IN_COLS), lambda bi, i: (bi, jnp.maximum(i * (tp // 8) - 1, 0), 0)),
                  pl.BlockSpec((1, 1, R_PROJ), lambda bi, i: (bi, 0, 0)),
                  vec(R_PROJ), vec(R_DIM), vec(R_DIM), vec(R_DIM), vec(R_DIM), vec(R_DIM),
                  full((LANES, R_DIM)), full((LANES, R_DIM)), full((G_LORA, R_DIM)),
                  full((R_DIM, R_DIM)), vec(A_DIM), vec(KV_DIM)],
        out_specs=[wide] * 9 + [pl.BlockSpec((1, tp, KV_DIM), lambda bi, i: (bi, i, 0))],
        out_shape=outs,
        compiler_params=_params(("parallel", "parallel")),
        name="rwkv_prep",
    )(z, z, shift0.reshape(b, 1, R_PROJ), row(lp["mu"]), row(lp["w0"]), row(lp["a0"]), row(lp["k_k"]),
      row(lp["k_a"]), row(lp["r_k"]), lp["w2p"], lp["a2p"], lp["g2"], lp["e64"], row(lp["q_gain"]),
      row(lp["k_gain"]))


def _stack_heads(x, lo_mask):
    return jnp.concatenate([jnp.where(lo_mask, x, 0.0), jnp.where(lo_mask, 0.0, x)], axis=0)


def _chunk_kernel(r_ref, k_ref, v_ref, kk_ref, b_ref, lw_ref, s0_ref, y_ref, sf_ref, s_ref):
    c = pl.program_id(1)
    L = CHUNK

    @pl.when(c == 0)
    def _():
        s_ref[...] = s0_ref[0]

    lane = lax.broadcasted_iota(I32, (1, LANES), 1)
    lo_mask = lane < R_HD
    rr = lax.broadcasted_iota(I32, (L, 2 * L), 0)
    cc = lax.broadcasted_iota(I32, (L, 2 * L), 1)
    cc = jnp.where(cc >= L, cc - L, cc)
    strict = rr > cc
    incl = rr >= cc
    t_r = lax.broadcasted_iota(I32, (L, L), 0)
    t_c = lax.broadcasted_iota(I32, (L, L), 1)
    tri = (t_r >= t_c).astype(BF16)
    col2 = lax.broadcasted_iota(I32, (L, 2 * L), 1) < L
    eye_r = lax.broadcasted_iota(I32, (2 * L, 2 * L), 0)
    eye_c = lax.broadcasted_iota(I32, (2 * L, 2 * L), 1)
    eye = (eye_r == eye_c).astype(F32)

    pairs = range(N_PAIRS)
    sls = [slice(p * LANES, (p + 1) * LANES) for p in pairs]
    r = [r_ref[0][:, sl] for sl in sls]
    k = [k_ref[0][:, sl] for sl in sls]
    v = [v_ref[0][:, sl] for sl in sls]
    kk = [kk_ref[0][:, sl] for sl in sls]
    bb = [b_ref[0][:, sl] for sl in sls]
    lw = [lw_ref[0][:, sl] for sl in sls]
    s0 = [s_ref[p] for p in pairs]

    cl = [_exact_lhs(_mm, tri, lw[p]) for p in pairs]
    cl_last = [cl[p][L - 1:L, :] for p in pairs]
    e_neg = [jnp.exp(-cl[p]) for p in pairs]
    e_last = [jnp.exp(cl_last[p] - cl[p]) for p in pairs]
    rt = [r[p] * jnp.exp(cl[p]) for p in pairs]
    at = [kk[p] * jnp.exp(cl[p] - lw[p]) for p in pairs]

    gm = [_x3(_mm_nt, jnp.concatenate([at[p], rt[p]], axis=0),
              jnp.concatenate([_stack_heads(k[p] * e_neg[p], lo_mask), _stack_heads(bb[p] * e_neg[p], lo_mask)],
                              axis=0)) for p in pairs]
    mk = [jnp.where(strict, gm[p][:L, :2 * L], 0.0) for p in pairs]
    mb = [jnp.where(strict, gm[p][:L, 2 * L:], 0.0) for p in pairs]
    hk = [jnp.where(incl, gm[p][L:, :2 * L], 0.0) for p in pairs]
    hb = [jnp.where(incl, gm[p][L:, 2 * L:], 0.0) for p in pairs]

    nil = [-jnp.concatenate([jnp.where(col2, mb[p], 0.0), jnp.where(col2, 0.0, mb[p])], axis=0) for p in pairs]
    tinv = [eye + nil[p] for p in pairs]
    nil = [_x3(_mm, nil[p], nil[p]) for p in pairs]
    for it in range(5):
        tinv = [tinv[p] + _x3(_mm, nil[p], tinv[p]) for p in pairs]
        if it < 4:
            nil = [_x3(_mm, nil[p], nil[p]) for p in pairs]

    vs = [_stack_heads(v[p], lo_mask) for p in pairs]
    rhs_u = [-(_x3(_mm_nt, at[p], s0[p]) + _x3(_mm, mk[p], vs[p])) for p in pairs]
    us = [_x3(_mm, tinv[p], _stack_heads(rhs_u[p], lo_mask)) for p in pairs]
    for p in pairs:
        y_ref[0, :, sls[p]] = _x3(_mm_nt, rt[p], s0[p]) + _x3(_mm, hk[p], vs[p]) + _x3(_mm, hb[p], us[p])
    for p in pairs:
        s_ref[p] = (s0[p] * jnp.exp(cl_last[p])
                    + _x3(_mm_tn, vs[p], _stack_heads(k[p] * e_last[p], lo_mask))
                    + _x3(_mm_tn, us[p], _stack_heads(bb[p] * e_last[p], lo_mask)))

    @pl.when(c == pl.num_programs(1) - 1)
    def _():
        sf_ref[0] = s_ref[...]


def _chunk_scan(r, k, v, kk, bb, lw, s0_bd):
    b, t, _ = r.shape
    wide = pl.BlockSpec((1, CHUNK, R_DIM), lambda bi, c: (bi, c, 0))
    st = pl.BlockSpec((1, N_PAIRS, LANES, LANES), lambda bi, c: (bi, 0, 0, 0))
    return pl.pallas_call(
        _chunk_kernel,
        grid=(b, t // CHUNK),
        in_specs=[wide] * 6 + [st],
        out_specs=[wide, st],
        out_shape=[jax.ShapeDtypeStruct((b, t, R_DIM), F32),
                   jax.ShapeDtypeStruct((b, N_PAIRS, LANES, LANES), F32)],
        scratch_shapes=[pltpu.VMEM((N_PAIRS, LANES, LANES), F32)],
        compiler_params=_params(("parallel", "arbitrary")),
        name="rwkv_chunk",
    )(r, k, v, kk, bb, lw, s0_bd)


def _swa_kernel(*refs, n_seg, banded):
    q_ref = refs[0]
    k_refs = refs[1:1 + n_seg]
    v_refs = refs[1 + n_seg:1 + 2 * n_seg]
    bias_ref, sink_ref, o_ref = refs[1 + 2 * n_seg:]
    c = pl.program_id(1)
    q = q_ref[0]
    kcat = jnp.concatenate([kr[0] for kr in k_refs], axis=0)
    vcat = jnp.concatenate([vr[0] for vr in v_refs], axis=0)
    n_k = kcat.shape[0]
    lane = lax.broadcasted_iota(I32, (1, LANES), 1)
    lo_mask = lane < A_HD
    k_sw = pltpu.roll(kcat, A_HD, axis=1)
    v_sw = pltpu.roll(vcat, A_HD, axis=1)
    k_dup = [jnp.where(lo_mask, kcat, k_sw).astype(BF16), jnp.where(lo_mask, k_sw, kcat).astype(BF16)]
    v_dup = [jnp.where(lo_mask, vcat, v_sw).astype(BF16), jnp.where(lo_mask, v_sw, vcat).astype(BF16)]
    if banded:
        key_chunk = c - WIN_CHUNKS + lax.broadcasted_iota(I32, (1, n_k), 1) // CHUNK
        valid = key_chunk >= 0
    n_q = q.shape[0]
    hi_mask = jnp.logical_not(lo_mask)
    for kvh in range(A_KV):
        heads = range(kvh * A_GROUP, (kvh + 1) * A_GROUP)
        qs = jnp.concatenate(
            [jnp.where(lo_mask if hq % 2 == 0 else hi_mask, q[:, (hq // 2) * LANES:(hq // 2 + 1) * LANES], 0.0)
             for hq in heads], axis=0).astype(BF16)
        bias = jnp.concatenate([bias_ref[hq] for hq in heads], axis=0)
        sink = jnp.concatenate([jnp.broadcast_to(sink_ref[0:1, hq:hq + 1], (n_q, 1)) for hq in heads], axis=0)
        s = _mm_nt(qs, k_dup[kvh]) * (A_HD ** -0.5) + bias
        if banded:
            s = jnp.where(valid, s, NEG_INF)
        m = jnp.maximum(jnp.max(s, axis=-1, keepdims=True), sink)
        pr = jnp.exp(s - m)
        den = jnp.sum(pr, axis=-1, keepdims=True) + jnp.exp(sink - m)
        o = _mm((pr / den).astype(BF16), v_dup[kvh])
        for i in range(A_GROUP // 2):
            pair = kvh * (A_GROUP // 2) + i
            o_ref[0, :, pair * LANES:(pair + 1) * LANES] = jnp.where(
                lo_mask, o[2 * i * n_q:(2 * i + 1) * n_q], o[(2 * i + 1) * n_q:(2 * i + 2) * n_q])


def _swa_prompt(qn, kn, z, bias, sink):
    b, t, _ = qn.shape
    v_col = (R_PROJ + A_DIM + KV_DIM) // KV_DIM
    seg = lambda s, col: pl.BlockSpec(
        (1, CHUNK, KV_DIM), lambda bi, c: (bi, jnp.maximum(c - WIN_CHUNKS + s, 0), col))
    n_seg = WIN_CHUNKS + 1
    return pl.pallas_call(
        functools.partial(_swa_kernel, n_seg=n_seg, banded=True),
        grid=(b, t // CHUNK),
        in_specs=[pl.BlockSpec((1, CHUNK, A_DIM), lambda bi, c: (bi, c, 0))]
                 + [seg(s, 0) for s in range(n_seg)] + [seg(s, v_col) for s in range(n_seg)]
                 + [pl.BlockSpec(bias.shape, lambda bi, c: (0, 0, 0)),
                    pl.BlockSpec((1, A_HEADS), lambda bi, c: (0, 0))],
        out_specs=pl.BlockSpec((1, CHUNK, A_DIM), lambda bi, c: (bi, c, 0)),
        out_shape=jax.ShapeDtypeStruct((b, t, A_DIM), F32),
        compiler_params=_params(("parallel", "parallel")),
        name="swa_prompt",
    )(qn, *([kn] * n_seg), *([z] * n_seg), bias, sink.reshape(1, A_HEADS))


def _swa_sample(qn, kn, v_new, k_cache, v_cache, bias, sink):
    b, t, _ = qn.shape
    n_cache = k_cache.shape[1]
    cur = lambda w: pl.BlockSpec((1, t, w), lambda bi, c: (bi, 0, 0))
    old = pl.BlockSpec((1, n_cache, KV_DIM), lambda bi, c: (bi, 0, 0))
    return pl.pallas_call(
        functools.partial(_swa_kernel, n_seg=2, banded=False),
        grid=(b, 1),
        in_specs=[cur(A_DIM), old, cur(KV_DIM), old, cur(KV_DIM),
                  pl.BlockSpec(bias.shape, lambda bi, c: (0, 0, 0)),
                  pl.BlockSpec((1, A_HEADS), lambda bi, c: (0, 0))],
        out_specs=cur(A_DIM),
        out_shape=jax.ShapeDtypeStruct((b, t, A_DIM), F32),
        compiler_params=_params(("parallel", "parallel")),
        name="swa_sample",
    )(qn, k_cache, kn, v_cache, v_new, bias, sink.reshape(1, A_HEADS))


def _mix_out_kernel(y_ref, bo_ref, g_ref, a_ref, x_ref, gng_ref, gnb_ref, e_ref, wo_ref, ln_ref, wq_ref,
                    xo_ref, h_ref, qq_ref):
    e = e_ref[...]
    y = y_ref[...]
    mu = _seg_sum(y, e) * (1.0 / R_HD)
    d = y - mu
    var = _seg_sum(d * d, e) * (1.0 / R_HD)
    yn = d * lax.rsqrt(var + GN_EPS) * gng_ref[...] + gnb_ref[...]
    mix_r = ((yn + bo_ref[...]) * g_ref[...]).astype(BF16)
    x = (x_ref[...] + _mm(mix_r, wo_ref[:R_DIM, :]) + _mm(a_ref[...].astype(BF16), wo_ref[R_DIM:, :]))
    xo_ref[...] = x
    ms = jnp.mean(x * x, axis=-1, keepdims=True)
    h = x * lax.rsqrt(ms + NORM_EPS) * ln_ref[...]
    h_ref[...] = h
    qq_ref[...] = _mm(h.astype(BF16), wq_ref[...])


def _mix_out(y, bonus, g, a_out, x, lp):
    n = x.shape[0]
    tm = min(n, 256)
    nq = PEER_HEADS * PK_DIM
    rows = lambda w: pl.BlockSpec((tm, w), lambda i: (i, 0))
    full = lambda s: pl.BlockSpec(s, lambda i: (0, 0))
    return pl.pallas_call(
        _mix_out_kernel,
        grid=(n // tm,),
        in_specs=[rows(R_DIM), rows(R_DIM), rows(R_DIM), rows(A_DIM), rows(D_MODEL),
                  full((1, R_DIM)), full((1, R_DIM)), full((R_DIM, R_DIM)),
                  full((D_MODEL, D_MODEL)), full((1, D_MODEL)), full((D_MODEL, nq))],
        out_specs=[rows(D_MODEL), rows(D_MODEL), rows(nq)],
        out_shape=[jax.ShapeDtypeStruct((n, D_MODEL), F32), jax.ShapeDtypeStruct((n, D_MODEL), F32),
                   jax.ShapeDtypeStruct((n, nq), F32)],
        compiler_params=_params(("parallel",)),
        name="mix_out_query",
    )(y, bonus, g, a_out, x, lp["gn_g"].reshape(1, -1), lp["gn_b"].reshape(1, -1), lp["e64"],
      lp["w_out"], lp["ln2_g"].reshape(1, -1), lp["w_pq"])


def _pick_rounds(s_ref, n_rows, emit):
    tb = s_ref.shape[1]
    rowid = lax.broadcasted_iota(I32, (n_rows, tb), 0)
    for rnd in range(PEER_TOPK):
        s = s_ref[...]
        m = jnp.max(s, axis=0, keepdims=True)
        idx = jnp.min(jnp.where(s == m, rowid, n_rows), axis=0, keepdims=True)
        hit = rowid == idx
        s_ref[...] = jnp.where(hit, -jnp.inf, s)
        emit(rnd, m, idx, hit)


def _topk_kernel(qq_ref, keys_ref, ei_ref, gate_ref, s_ref, sv_ref, si_ref, c_ref, ci_ref, ts_ref):
    tb = qq_ref.shape[0]
    for half in range(2):
        qh = qq_ref[:, half * PK_HALF:(half + 1) * PK_HALF]
        s_ref[...] = _x3(_mm_nt, keys_ref[half], qh)

        def emit1(rnd, m, idx, hit, half=half):
            sv_ref[half, rnd:rnd + 1, :] = m
            si_ref[half, rnd:rnd + 1, :] = idx

        _pick_rounds(s_ref, N_KEYS, emit1)

    row0 = 0
    for a, nb in enumerate(CAND_COLS):
        if nb == 1:
            break
        c_ref[row0:row0 + nb, :] = sv_ref[0, a:a + 1, :] + sv_ref[1, 0:nb, :]
        ci_ref[row0:row0 + nb, :] = si_ref[0, a:a + 1, :] * N_KEYS + si_ref[1, 0:nb, :]
        row0 += nb
    n_one = PEER_TOPK - a
    c_ref[row0:row0 + n_one, :] = sv_ref[0, a:, :] + sv_ref[1, 0:1, :]
    ci_ref[row0:row0 + n_one, :] = si_ref[0, a:, :] * N_KEYS + si_ref[1, 0:1, :]
    row0 += n_one
    c_ref[row0:, :] = jnp.full((N_CAND - row0, tb), -jnp.inf, F32)
    ci_ref[row0:, :] = jnp.zeros((N_CAND - row0, tb), I32)

    def emit2(rnd, m, idx, hit):
        ts_ref[rnd:rnd + 1, :] = m
        ei_ref[0, rnd:rnd + 1, :] = jnp.max(jnp.where(hit, ci_ref[...], -1), axis=0, keepdims=True)

    _pick_rounds(c_ref, N_CAND, emit2)
    ts = ts_ref[...]
    ex = jnp.exp(ts - ts[0:1, :])
    gate_ref[0] = ex / jnp.sum(ex, axis=0, keepdims=True)


def _topk(qq, sub_keys):
    n = qq.shape[0]
    tb = min(n, 512)
    out = pl.BlockSpec((1, PEER_TOPK, tb), lambda i, h: (h, 0, i))
    return pl.pallas_call(
        _topk_kernel,
        grid=(n // tb, PEER_HEADS),
        in_specs=[pl.BlockSpec((tb, PK_DIM), lambda i, h: (i, h)),
                  pl.BlockSpec((2, N_KEYS, PK_HALF), lambda i, h: (0, 0, 0))],
        out_specs=[out, out],
        out_shape=[jax.ShapeDtypeStruct((PEER_HEADS, PEER_TOPK, n), I32),
                   jax.ShapeDtypeStruct((PEER_HEADS, PEER_TOPK, n), F32)],
        scratch_shapes=[pltpu.VMEM((N_KEYS, tb), F32), pltpu.VMEM((2, PEER_TOPK, tb), F32),
                        pltpu.VMEM((2, PEER_TOPK, tb), I32), pltpu.VMEM((N_CAND, tb), F32),
                        pltpu.VMEM((N_CAND, tb), I32), pltpu.VMEM((PEER_TOPK, tb), F32)],
        compiler_params=_params(("parallel", "parallel")),
        name="peer_topk",
    )(qq, sub_keys)


def _expert_gather(tbl_hbm, idx_v, rows_v, sem, tt, g, buf):
    return pltpu.make_async_copy(tbl_hbm.at[idx_v[pl.ds(tt * N_SEL + g * SC_GROUP, SC_GROUP)]],
                                 rows_v.at[buf], sem.at[buf])


def _sc_tok_blk(n):
    return math.gcd(n // SC_WORKERS, SC_TOK_BLK)


def _sc_token_blocks(n, tbl_hbm, idx_v, rows_v, sem, load_block, compute, store_block):
    wid = lax.axis_index("s") * SC_CORES + lax.axis_index("c")
    npw = n // SC_WORKERS
    tb = _sc_tok_blk(n)
    steps = tb * SC_N_GROUPS

    @pl.loop(0, npw // tb)
    def _(blk):
        tok0 = wid * npw + blk * tb
        load_block(tok0)
        for s in range(SC_NBUF - 1):
            _expert_gather(tbl_hbm, idx_v, rows_v, sem, s // SC_N_GROUPS, s % SC_N_GROUPS, s).start()

        @pl.loop(0, steps, step=SC_NBUF)
        def _(s0):
            for b in range(SC_NBUF):
                s = s0 + b
                ahead = s + SC_NBUF - 1

                @pl.when(ahead < steps)
                def _():
                    _expert_gather(tbl_hbm, idx_v, rows_v, sem, ahead // SC_N_GROUPS, ahead % SC_N_GROUPS,
                                   (b + SC_NBUF - 1) % SC_NBUF).start()

                _expert_gather(tbl_hbm, idx_v, rows_v, sem, s // SC_N_GROUPS, s % SC_N_GROUPS, b).wait()
                compute(s // SC_N_GROUPS, s % SC_N_GROUPS, rows_v.at[b])

        store_block(tok0)


def _pair_products(words, other):
    p = plsc.bitcast(plsc.bitcast(words, BF16) * other, I32)
    return plsc.bitcast(p << 16, F32), plsc.bitcast(p & HI_MASK, F32)


def _expert_cost(n):
    elems = n * N_SEL * D_MODEL
    return pl.CostEstimate(flops=2 * elems, transcendentals=0, bytes_accessed=2 * elems)


def _sc_mesh():
    return plsc.VectorSubcoreMesh(core_axis_name="c", subcore_axis_name="s",
                                  num_cores=SC_CORES, num_subcores=SC_SUBCORES)


def _sc_hidden(eidx, h_words, table):
    n = h_words.shape[0]
    tb = _sc_tok_blk(n)

    @functools.partial(
        pl.kernel, out_type=jax.ShapeDtypeStruct((n * N_SEL,), F32), mesh=_sc_mesh(),
        compiler_params=pltpu.CompilerParams(needs_layout_passes=False),
        scratch_types=[pltpu.VMEM((tb * N_SEL,), I32), pltpu.VMEM((tb * SC_ROW_WORDS,), I32),
                       pltpu.VMEM((SC_NBUF, SC_GROUP, SC_ROW_WORDS), I32), pltpu.VMEM((tb * N_SEL,), F32),
                       pltpu.SemaphoreType.DMA((SC_NBUF,))],
        cost_estimate=_expert_cost(n), name="peer_hidden_sc")
    def run(eidx_hbm, h_hbm, tbl_hbm, hid_hbm, idx_v, h_v, rows_v, hid_v, sem):
        lane = lax.iota(I32, SC_LANES)

        def load_block(tok0):
            pltpu.sync_copy(eidx_hbm.at[pl.ds(tok0 * N_SEL, tb * N_SEL)], idx_v)
            pltpu.sync_copy(h_hbm.at[pl.ds(tok0 * SC_ROW_WORDS, tb * SC_ROW_WORDS)], h_v)

        def compute(tt, g, rows):
            zero = tuple(jnp.zeros((SC_LANES,), F32) for _ in range(SC_GROUP))

            @plsc.parallel_loop(0, SC_ROW_WORDS // SC_LANES, unroll=2, carry=zero)
            def accs(j, acc):
                hw = plsc.bitcast(h_v[pl.ds(tt * SC_ROW_WORDS + j * SC_LANES, SC_LANES)], BF16)
                out = []
                for r in range(SC_GROUP):
                    lo, hi = _pair_products(rows[r, pl.ds(j * SC_LANES, SC_LANES)], hw)
                    out.append(acc[r] + (lo + hi))
                return tuple(out)

            tot = jnp.zeros((SC_LANES,), F32)
            for r in range(SC_GROUP):
                tot = jnp.where(lane == r, jnp.sum(accs[r]), tot)
            hid_v[pl.ds(tt * N_SEL + g * SC_GROUP, SC_GROUP)] = tot

        def store_block(tok0):
            pltpu.sync_copy(hid_v, hid_hbm.at[pl.ds(tok0 * N_SEL, tb * N_SEL)])

        _sc_token_blocks(n, tbl_hbm, idx_v, rows_v, sem, load_block, compute, store_block)

    return run(eidx.reshape(-1), h_words.reshape(-1), table).reshape(n, N_SEL)


def _sc_combine(eidx, coef_words, x_planar, table):
    n = x_planar.shape[0]
    tb = _sc_tok_blk(n)

    @functools.partial(
        pl.kernel, out_type=jax.ShapeDtypeStruct((n * D_MODEL,), F32), mesh=_sc_mesh(),
        compiler_params=pltpu.CompilerParams(needs_layout_passes=False),
        scratch_types=[pltpu.VMEM((tb * N_SEL,), I32), pltpu.VMEM((tb * N_SEL,), I32),
                       pltpu.VMEM((SC_NBUF, SC_GROUP, SC_ROW_WORDS), I32), pltpu.VMEM((tb * D_MODEL,), F32),
                       pltpu.SemaphoreType.DMA((SC_NBUF,))],
        cost_estimate=_expert_cost(n), name="peer_combine_sc")
    def run(eidx_hbm, c_hbm, x_hbm, tbl_hbm, out_hbm, idx_v, c_v, rows_v, out_v, sem):
        def load_block(tok0):
            pltpu.sync_copy(eidx_hbm.at[pl.ds(tok0 * N_SEL, tb * N_SEL)], idx_v)
            pltpu.sync_copy(c_hbm.at[pl.ds(tok0 * N_SEL, tb * N_SEL)], c_v)
            pltpu.sync_copy(x_hbm.at[pl.ds(tok0 * D_MODEL, tb * D_MODEL)], out_v)

        def compute(tt, g, rows):
            base = tt * N_SEL + g * SC_GROUP
            cs = [plsc.bitcast(plsc.load_gather(c_v, [jnp.full((SC_LANES,), r, I32) + base]), BF16)
                  for r in range(SC_GROUP)]

            @plsc.parallel_loop(0, SC_ROW_WORDS // SC_LANES, unroll=2)
            def _(j):
                even = pl.ds(tt * D_MODEL + j * SC_LANES, SC_LANES)
                odd = pl.ds(tt * D_MODEL + SC_ROW_WORDS + j * SC_LANES, SC_LANES)
                acc_e = out_v[even]
                acc_o = out_v[odd]
                for r in range(SC_GROUP):
                    lo, hi = _pair_products(rows[r, pl.ds(j * SC_LANES, SC_LANES)], cs[r])
                    acc_e = acc_e + lo
                    acc_o = acc_o + hi
                out_v[even] = acc_e
                out_v[odd] = acc_o

        def store_block(tok0):
            pltpu.sync_copy(out_v, out_hbm.at[pl.ds(tok0 * D_MODEL, tb * D_MODEL)])

        _sc_token_blocks(n, tbl_hbm, idx_v, rows_v, sem, load_block, compute, store_block)

    return run(eidx.reshape(-1), coef_words.reshape(-1), x_planar.reshape(-1), table).reshape(n, D_MODEL)


def _gate_act_kernel(hid_ref, gate_ref, o_ref):
    hid = hid_ref[...]
    o_ref[...] = gate_ref[...] * (0.5 * hid * (1.0 + lax.erf(hid * np.float32(np.sqrt(0.5)))))


def _gate_act(hid, gate):
    n = hid.shape[0]
    tm = math.gcd(n, 2048)
    rows = pl.BlockSpec((tm, N_SEL), lambda i: (i, 0))
    return pl.pallas_call(
        _gate_act_kernel, grid=(n // tm,), in_specs=[rows, rows], out_specs=rows,
        out_shape=jax.ShapeDtypeStruct((n, N_SEL), F32), compiler_params=_params(("parallel",)),
        name="peer_gate_act",
    )(hid, gate)


def _t5_bucket(rel):
    nb = NUM_BUCKETS // 2
    max_exact = nb // 2
    ret = jnp.where(rel > 0, nb, 0)
    n = jnp.abs(rel)
    nf = jnp.maximum(n, 1).astype(F32)
    large = max_exact + (jnp.log(nf / max_exact) / math.log(MAX_DISTANCE / max_exact)
                         * (nb - max_exact)).astype(I32)
    large = jnp.minimum(large, nb - 1)
    return ret + jnp.where(n < max_exact, n, large)


def _rel_bias(rel_bias, n_q, n_k, n_before):
    rel = (jnp.arange(n_k)[None, :] - n_before) - jnp.arange(n_q)[:, None]
    return jnp.transpose(rel_bias[_t5_bucket(rel)].astype(F32), (2, 0, 1))


def _state_to_pairs(wkv):
    b = wkv.shape[0]
    s = wkv.reshape(b, N_PAIRS, 2, R_HD, R_HD)
    z = jnp.zeros_like(s[:, :, 0])
    top = jnp.concatenate([s[:, :, 0], z], axis=-1)
    bot = jnp.concatenate([z, s[:, :, 1]], axis=-1)
    return jnp.concatenate([top, bot], axis=-2)


def _pairs_to_state(s_bd):
    b = s_bd.shape[0]
    return jnp.stack([s_bd[:, :, :R_HD, :R_HD], s_bd[:, :, R_HD:, R_HD:]], axis=2).reshape(
        b, R_HEADS, R_HD, R_HD)


def _layer_dense(x, shift0, wkv0, kv_cache, bias, lp):
    b, t, _ = x.shape
    n = b * t
    z = _norm_matmul(x.reshape(n, D_MODEL), lp["ln1_g"].reshape(1, -1), lp["w_in"], 512).reshape(b, t, IN_COLS)
    r, k2, v, kk, bb, lw, g, bonus, qn, kn = _prep(z, shift0, lp)
    v_new = z[:, :, R_PROJ + A_DIM + KV_DIM:]
    if t % CHUNK:
        pad = lambda a: jnp.pad(a, ((0, 0), (0, CHUNK - t % CHUNK), (0, 0)))
        y, s_fin = _chunk_scan(*(pad(a) for a in (r, k2, v, kk, bb, lw)), _state_to_pairs(wkv0))
        y = y[:, :t]
    else:
        y, s_fin = _chunk_scan(r, k2, v, kk, bb, lw, _state_to_pairs(wkv0))
    if kv_cache is None:
        a_out = _swa_prompt(qn, kn, z, bias, lp["sink"])
    else:
        a_out = _swa_sample(qn, kn, v_new, kv_cache[0].reshape(b, -1, KV_DIM),
                            kv_cache[1].reshape(b, -1, KV_DIM), bias, lp["sink"])
    flat = lambda a: a.reshape(n, a.shape[-1])
    x1, h, qq = _mix_out(flat(y), flat(bonus), flat(g), flat(a_out), flat(x), lp)
    eidx, gate = _topk(qq, lp["sub_keys"])
    sel = lambda a: jnp.transpose(a, (2, 0, 1)).reshape(n, N_SEL)
    return ((sel(eidx), sel(gate), h, x1),
            (_pairs_to_state(s_fin), z[:, -1, :R_PROJ],
             kn.reshape(b, t, A_KV, A_HD), v_new.reshape(b, t, A_KV, A_HD)))


def _bf16_pair_words(a):
    bits = lax.bitcast_convert_type(a.astype(BF16), jnp.uint16).astype(jnp.uint32)
    return lax.bitcast_convert_type(bits[:, 0::2] | (bits[:, 1::2] << 16), I32)


def _layer_experts(ops, lp, after=None):
    eidx, gate, h, x1 = ops
    h_words = _bf16_pair_words(h)
    if after is not None:
        h_words, after = lax.optimization_barrier((h_words, after))
    coef = _gate_act(_sc_hidden(eidx, h_words, lp["peer_u"]), gate)
    x_planar = jnp.concatenate([x1[:, 0::2], x1[:, 1::2]], axis=1)
    out = _sc_combine(eidx, _bf16_pair_words(jnp.repeat(coef, 2, axis=1)), x_planar, lp["peer_v"])
    return jnp.stack([out[:, :SC_ROW_WORDS], out[:, SC_ROW_WORDS:]], axis=-1).reshape(x1.shape), after


def kernel(x_prompt, x_sample, state_rwkv_wkv, state_rwkv_shift, cache_swa_k, cache_swa_v, ln1_g, w_in,
           mu_shift, w0, w2, a0, a2, g2, k_k, k_a, r_k, gn_g, gn_b, q_norm_g, k_norm_g, attn_sink,
           rel_bias, w_out, ln2_g, w_pq, sub_keys, peer_u, peer_v):
    depth = w_in.shape[0]
    b_p, s_p = x_prompt.shape[:2]
    b_s, t_s = x_sample.shape[:2]
    n_cache = cache_swa_k.shape[2]
    n_keep = min(WINDOW, s_p)
    bias_p = _rel_bias(rel_bias, CHUNK, (WIN_CHUNKS + 1) * CHUNK, WIN_CHUNKS * CHUNK)
    bias_s = _rel_bias(rel_bias, t_s, n_cache + t_s, n_cache)
    head_id = jnp.arange(R_DIM) // R_HD
    e64 = (head_id[:, None] == head_id[None, :]).astype(BF16)
    zpad = jnp.zeros((LANES - W_LORA, R_DIM), F32)
    n_groups = PROMPT_GROUPS if b_p % PROMPT_GROUPS == 0 else 1
    xp, xs = jnp.split(x_prompt, n_groups, axis=0), x_sample
    outs = [[] for _ in range(8)]
    for l in range(depth):
        lp = {
            "ln1_g": ln1_g[l], "w_in": w_in[l].astype(BF16), "mu": mu_shift[l], "w0": w0[l], "a0": a0[l],
            "k_k": k_k[l], "k_a": k_a[l], "r_k": r_k[l].reshape(-1), "gn_g": gn_g[l], "gn_b": gn_b[l],
            "w2p": jnp.concatenate([w2[l], zpad], axis=0).astype(BF16),
            "a2p": jnp.concatenate([zpad, a2[l]], axis=0).astype(BF16),
            "g2": g2[l].astype(BF16), "e64": e64,
            "q_gain": jnp.tile(q_norm_g[l], A_HEADS), "k_gain": jnp.tile(k_norm_g[l], A_KV),
            "sink": attn_sink[l].astype(F32), "w_out": w_out[l].astype(BF16), "ln2_g": ln2_g[l],
            "w_pq": w_pq[l].astype(BF16), "sub_keys": sub_keys[l], "peer_u": _bf16_pair_words(peer_u[l]), "peer_v": _bf16_pair_words(peer_v[l]),
        }
        parts = []
        for xg in xp:
            if parts:
                (eidx, *rest), states = parts[-1]
                xg, eidx = lax.optimization_barrier((xg, eidx))
                parts[-1] = ((eidx, *rest), states)
            parts.append(_layer_dense(xg, jnp.zeros((xg.shape[0], R_PROJ), F32),
                                      jnp.zeros((xg.shape[0], R_HEADS, R_HD, R_HD), F32), None, bias_p, lp))
        wkv_p, sh_p, k_p, v_p = (jnp.concatenate([pt[1][i] for pt in parts], axis=0) for i in range(4))
        outs_x = []
        for ops, _ in parts:
            if outs_x:
                xo, outs_x[-1] = _layer_experts(ops, lp, after=outs_x[-1])
            else:
                xo, _ = _layer_experts(ops, lp)
            outs_x.append(xo)
        xp = [xo.reshape(xg.shape) for xo, xg in zip(outs_x, xp)]
        ops_s, (wkv_s, sh_s, k_s, v_s) = _layer_dense(
            xs, state_rwkv_shift[l], state_rwkv_wkv[l].astype(F32), (cache_swa_k[l], cache_swa_v[l]),
            bias_s, lp)
        xs = _layer_experts(ops_s, lp)[0].reshape(xs.shape)
        for lst, val in zip(outs, (wkv_p, sh_p, k_p[:, s_p - n_keep:], v_p[:, s_p - n_keep:],
                                   wkv_s, sh_s, k_s, v_s)):
            lst.append(val)
    return (jnp.concatenate(xp, axis=0), xs) + tuple(jnp.stack(o) for o in outs)
```

```python
import functools
import math

import numpy as np
import jax
import jax.numpy as jnp
from jax import lax
from jax.experimental import pallas as pl
from jax.experimental.pallas import tpu as pltpu
from jax.experimental.pallas import tpu_sc as plsc

F32 = jnp.float32
BF16 = jnp.bfloat16
I32 = jnp.int32

D_MODEL = 1024
CHUNK = 64
R_HEADS = 8
R_HD = 64
R_DIM = R_HEADS * R_HD
W_LORA = 64
A_LORA = 64
G_LORA = 128
R_PROJ = 3 * R_DIM + W_LORA + A_LORA + G_LORA
A_HEADS = 8
A_KV = 2
A_GROUP = A_HEADS // A_KV
A_HD = 64
A_DIM = A_HEADS * A_HD
KV_DIM = A_KV * A_HD
IN_COLS = R_PROJ + A_DIM + 2 * KV_DIM
WINDOW = 128
WIN_CHUNKS = WINDOW // CHUNK
NUM_BUCKETS = 32
MAX_DISTANCE = 128
PEER_HEADS = 8
N_KEYS = 128
PK_DIM = 256
PK_HALF = PK_DIM // 2
PEER_TOPK = 16
N_SEL = PEER_HEADS * PEER_TOPK
NORM_EPS = 1e-6
GN_EPS = 64e-5
NEG_INF = -1e30

LANES = 128
N_PAIRS = R_DIM // LANES
VMEM_LIMIT = 48 * 1024 * 1024
CAND_COLS = (16, 8, 8, 4, 4, 4, 4, 4, 1, 1, 1, 1, 1, 1, 1, 1)
assert all(nb >= PEER_TOPK // (a + 1) for a, nb in enumerate(CAND_COLS))
N_CAND = 64
assert sum(CAND_COLS) <= N_CAND
SC_CORES = 2
SC_SUBCORES = 16
SC_LANES = 16
SC_WORKERS = SC_CORES * SC_SUBCORES
SC_TOK_BLK = 32
SC_GROUP = SC_LANES
SC_N_GROUPS = N_SEL // SC_GROUP
SC_NBUF = 4
SC_ROW_WORDS = D_MODEL // 2
HI_MASK = -65536
PROMPT_GROUPS = 8


def _params(sem):
    return pltpu.CompilerParams(dimension_semantics=sem, vmem_limit_bytes=VMEM_LIMIT)


def _mm(a, b):
    return jnp.dot(a, b, preferred_element_type=F32)


def _mm_nt(a, b):
    return lax.dot_general(a, b, (((1,), (1,)), ((), ())), preferred_element_type=F32)


def _mm_tn(a, b):
    return lax.dot_general(a, b, (((0,), (0,)), ((), ())), preferred_element_type=F32)


def _split2(a):
    hi = a.astype(BF16)
    return hi, (a - hi.astype(F32)).astype(BF16)


def _split3(a):
    hi = a.astype(BF16)
    r = a - hi.astype(F32)
    mid = r.astype(BF16)
    return hi, mid, (r - mid.astype(F32)).astype(BF16)


def _x3(mm, a, b):
    ah, al = _split2(a)
    bh, bl = _split2(b)
    return mm(ah, bh) + mm(ah, bl) + mm(al, bh)


def _exact_lhs(mm, a_bf16, b):
    b0, b1, b2 = _split3(b)
    return mm(a_bf16, b0) + mm(a_bf16, b1) + mm(a_bf16, b2)


def _seg_sum(x, e_bf16):
    x0, x1, x2 = _split3(x)
    return _mm(x0, e_bf16) + _mm(x1, e_bf16) + _mm(x2, e_bf16)


def _sigmoid(x):
    return 1.0 / (1.0 + jnp.exp(-x))


def _norm_matmul_kernel(x_ref, g_ref, w_ref, o_ref, h_ref):
    @pl.when(pl.program_id(1) == 0)
    def _():
        x = x_ref[...]
        ms = jnp.mean(x * x, axis=-1, keepdims=True)
        h_ref[...] = (x * lax.rsqrt(ms + NORM_EPS) * g_ref[...]).astype(BF16)

    o_ref[...] = _mm(h_ref[...], w_ref[...])


def _norm_matmul(x, g, w_bf16, tn):
    n, k = x.shape
    m = w_bf16.shape[1]
    tm = min(n, 512)
    return pl.pallas_call(
        _norm_matmul_kernel,
        grid=(n // tm, m // tn),
        in_specs=[pl.BlockSpec((tm, k), lambda i, j: (i, 0)),
                  pl.BlockSpec((1, k), lambda i, j: (0, 0)),
                  pl.BlockSpec((k, tn), lambda i, j: (0, j))],
        out_specs=pl.BlockSpec((tm, tn), lambda i, j: (i, j)),
        out_shape=jax.ShapeDtypeStruct((n, m), F32),
        scratch_shapes=[pltpu.VMEM((tm, k), BF16)],
        compiler_params=_params(("parallel", "arbitrary")),
        name="norm_inproj",
    )(x, g, w_bf16)


def _prep_kernel(z_ref, zp_ref, sh_ref, mu_ref, w0_ref, a0_ref, kk_ref, ka_ref, rk_ref,
                 w2_ref, a2_ref, g2_ref, e_ref, qg_ref, kg_ref,
                 r_o, k_o, v_o, kk_o, b_o, lw_o, g_o, bo_o, qn_o, kn_o):
    i = pl.program_id(1)
    zt = z_ref[0]
    tp = zt.shape[0]
    zr = zt[:, :R_PROJ]
    prev_row = jnp.where(i == 0, sh_ref[0], zp_ref[0][7:8, :R_PROJ])
    row = lax.broadcasted_iota(I32, (tp, 1), 0)
    prev = jnp.where(row == 0, prev_row, pltpu.roll(zr, 1, axis=0))
    zs = zr + (prev - zr) * mu_ref[...]
    r = zs[:, 0:R_DIM]
    k = zs[:, R_DIM:2 * R_DIM]
    v = zs[:, 2 * R_DIM:3 * R_DIM]
    lo = zs[:, 3 * R_DIM:3 * R_DIM + W_LORA + A_LORA]
    g_lo = zs[:, 3 * R_DIM + W_LORA + A_LORA:R_PROJ]
    e = e_ref[...]
    w_in = -(w0_ref[...] + _mm(jnp.tanh(lo).astype(BF16), w2_ref[...]))
    softplus = jnp.maximum(w_in, 0.0) + jnp.log1p(jnp.exp(-jnp.abs(w_in)))
    w_log = -softplus - 0.5
    lw_o[0] = -jnp.exp(w_log)
    a = _sigmoid(a0_ref[...] + _mm(lo.astype(BF16), a2_ref[...]))
    g_o[0] = _mm(_sigmoid(g_lo).astype(BF16), g2_ref[...])
    kk = k * kk_ref[...]
    kk = kk / jnp.maximum(jnp.sqrt(_seg_sum(kk * kk, e)), 1e-12)
    k2 = k * (1.0 + (a - 1.0) * ka_ref[...])
    r_o[0] = r
    k_o[0] = k2
    v_o[0] = v
    kk_o[0] = kk
    b_o[0] = kk * a
    bo_o[0] = _seg_sum(r * k2 * rk_ref[...], e) * v
    q = zt[:, R_PROJ:R_PROJ + A_DIM]
    qn_o[0] = q * lax.rsqrt(_seg_sum(q * q, e) * (1.0 / A_HD) + NORM_EPS) * qg_ref[...]
    kx = zt[:, R_PROJ + A_DIM:R_PROJ + A_DIM + KV_DIM]
    e_kv = e[:KV_DIM, :KV_DIM]
    kn_o[0] = kx * lax.rsqrt(_seg_sum(kx * kx, e_kv) * (1.0 / A_HD) + NORM_EPS) * kg_ref[...]


def _prep(z, shift0, lp):
    b, t, _ = z.shape
    tp = min(t, 256)
    row = lambda a: a.reshape(1, -1)
    vec = lambda n: pl.BlockSpec((1, n), lambda bi, i: (0, 0))
    full = lambda s: pl.BlockSpec(s, lambda bi, i: (0, 0))
    wide = pl.BlockSpec((1, tp, R_DIM), lambda bi, i: (bi, i, 0))
    outs = [jax.ShapeDtypeStruct((b, t, R_DIM), F32)] * 9 + [jax.ShapeDtypeStruct((b, t, KV_DIM), F32)]
    return pl.pallas_call(
        _prep_kernel,
        grid=(b, t // tp),
        in_specs=[pl.BlockSpec((1, tp, IN_COLS), lambda bi, i: (bi, i, 0)),
                  pl.BlockSpec((1, 8, IN_COLS), lambda bi, i: (bi, jnp.maximum(i * (tp // 8) - 1, 0), 0)),
                  pl.BlockSpec((1, 1, R_PROJ), lambda bi, i: (bi, 0, 0)),
                  vec(R_PROJ), vec(R_DIM), vec(R_DIM), vec(R_DIM), vec(R_DIM), vec(R_DIM),
                  full((LANES, R_DIM)), full((LANES, R_DIM)), full((G_LORA, R_DIM)),
                  full((R_DIM, R_DIM)), vec(A_DIM), vec(KV_DIM)],
        out_specs=[wide] * 9 + [pl.BlockSpec((1, tp, KV_DIM), lambda bi, i: (bi, i, 0))],
        out_shape=outs,
        compiler_params=_params(("parallel", "parallel")),
        name="rwkv_prep",
    )(z, z, shift0.reshape(b, 1, R_PROJ), row(lp["mu"]), row(lp["w0"]), row(lp["a0"]), row(lp["k_k"]),
      row(lp["k_a"]), row(lp["r_k"]), lp["w2p"], lp["a2p"], lp["g2"], lp["e64"], row(lp["q_gain"]),
      row(lp["k_gain"]))


def _stack_heads(x, lo_mask):
    return jnp.concatenate([jnp.where(lo_mask, x, 0.0), jnp.where(lo_mask, 0.0, x)], axis=0)


def _chunk_kernel(r_ref, k_ref, v_ref, kk_ref, b_ref, lw_ref, s0_ref, y_ref, sf_ref, s_ref):
    c = pl.program_id(1)
    L = CHUNK

    @pl.when(c == 0)
    def _():
        s_ref[...] = s0_ref[0]

    lane = lax.broadcasted_iota(I32, (1, LANES), 1)
    lo_mask = lane < R_HD
    rr = lax.broadcasted_iota(I32, (L, 2 * L), 0)
    cc = lax.broadcasted_iota(I32, (L, 2 * L), 1)
    cc = jnp.where(cc >= L, cc - L, cc)
    strict = rr > cc
    incl = rr >= cc
    t_r = lax.broadcasted_iota(I32, (L, L), 0)
    t_c = lax.broadcasted_iota(I32, (L, L), 1)
    tri = (t_r >= t_c).astype(BF16)
    col2 = lax.broadcasted_iota(I32, (L, 2 * L), 1) < L
    eye_r = lax.broadcasted_iota(I32, (2 * L, 2 * L), 0)
    eye_c = lax.broadcasted_iota(I32, (2 * L, 2 * L), 1)
    eye = (eye_r == eye_c).astype(F32)

    pairs = range(N_PAIRS)
    sls = [slice(p * LANES, (p + 1) * LANES) for p in pairs]
    r = [r_ref[0][:, sl] for sl in sls]
    k = [k_ref[0][:, sl] for sl in sls]
    v = [v_ref[0][:, sl] for sl in sls]
    kk = [kk_ref[0][:, sl] for sl in sls]
    bb = [b_ref[0][:, sl] for sl in sls]
    lw = [lw_ref[0][:, sl] for sl in sls]
    s0 = [s_ref[p] for p in pairs]

    cl = [_exact_lhs(_mm, tri, lw[p]) for p in pairs]
    cl_last = [cl[p][L - 1:L, :] for p in pairs]
    e_neg = [jnp.exp(-cl[p]) for p in pairs]
    e_last = [jnp.exp(cl_last[p] - cl[p]) for p in pairs]
    rt = [r[p] * jnp.exp(cl[p]) for p in pairs]
    at = [kk[p] * jnp.exp(cl[p] - lw[p]) for p in pairs]

    gm = [_x3(_mm_nt, jnp.concatenate([at[p], rt[p]], axis=0),
              jnp.concatenate([_stack_heads(k[p] * e_neg[p], lo_mask), _stack_heads(bb[p] * e_neg[p], lo_mask)],
                              axis=0)) for p in pairs]
    mk = [jnp.where(strict, gm[p][:L, :2 * L], 0.0) for p in pairs]
    mb = [jnp.where(strict, gm[p][:L, 2 * L:], 0.0) for p in pairs]
    hk = [jnp.where(incl, gm[p][L:, :2 * L], 0.0) for p in pairs]
    hb = [jnp.where(incl, gm[p][L:, 2 * L:], 0.0) for p in pairs]

    nil = [-jnp.concatenate([jnp.where(col2, mb[p], 0.0), jnp.where(col2, 0.0, mb[p])], axis=0) for p in pairs]
    tinv = [eye + nil[p] for p in pairs]
    nil = [_x3(_mm, nil[p], nil[p]) for p in pairs]
    for it in range(5):
        tinv = [tinv[p] + _x3(_mm, nil[p], tinv[p]) for p in pairs]
        if it < 4:
            nil = [_x3(_mm, nil[p], nil[p]) for p in pairs]

    vs = [_stack_heads(v[p], lo_mask) for p in pairs]
    rhs_u = [-(_x3(_mm_nt, at[p], s0[p]) + _x3(_mm, mk[p], vs[p])) for p in pairs]
    us = [_x3(_mm, tinv[p], _stack_heads(rhs_u[p], lo_mask)) for p in pairs]
    for p in pairs:
        y_ref[0, :, sls[p]] = _x3(_mm_nt, rt[p], s0[p]) + _x3(_mm, hk[p], vs[p]) + _x3(_mm, hb[p], us[p])
    for p in pairs:
        s_ref[p] = (s0[p] * jnp.exp(cl_last[p])
                    + _x3(_mm_tn, vs[p], _stack_heads(k[p] * e_last[p], lo_mask))
                    + _x3(_mm_tn, us[p], _stack_heads(bb[p] * e_last[p], lo_mask)))

    @pl.when(c == pl.num_programs(1) - 1)
    def _():
        sf_ref[0] = s_ref[...]


def _chunk_scan(r, k, v, kk, bb, lw, s0_bd):
    b, t, _ = r.shape
    wide = pl.BlockSpec((1, CHUNK, R_DIM), lambda bi, c: (bi, c, 0))
    st = pl.BlockSpec((1, N_PAIRS, LANES, LANES), lambda bi, c: (bi, 0, 0, 0))
    return pl.pallas_call(
        _chunk_kernel,
        grid=(b, t // CHUNK),
        in_specs=[wide] * 6 + [st],
        out_specs=[wide, st],
        out_shape=[jax.ShapeDtypeStruct((b, t, R_DIM), F32),
                   jax.ShapeDtypeStruct((b, N_PAIRS, LANES, LANES), F32)],
        scratch_shapes=[pltpu.VMEM((N_PAIRS, LANES, LANES), F32)],
        compiler_params=_params(("parallel", "arbitrary")),
        name="rwkv_chunk",
    )(r, k, v, kk, bb, lw, s0_bd)


def _swa_kernel(*refs, n_seg, banded):
    q_ref = refs[0]
    k_refs = refs[1:1 + n_seg]
    v_refs = refs[1 + n_seg:1 + 2 * n_seg]
    bias_ref, sink_ref, o_ref = refs[1 + 2 * n_seg:]
    c = pl.program_id(1)
    q = q_ref[0]
    kcat = jnp.concatenate([kr[0] for kr in k_refs], axis=0)
    vcat = jnp.concatenate([vr[0] for vr in v_refs], axis=0)
    n_k = kcat.shape[0]
    lane = lax.broadcasted_iota(I32, (1, LANES), 1)
    lo_mask = lane < A_HD
    k_sw = pltpu.roll(kcat, A_HD, axis=1)
    v_sw = pltpu.roll(vcat, A_HD, axis=1)
    k_dup = [jnp.where(lo_mask, kcat, k_sw).astype(BF16), jnp.where(lo_mask, k_sw, kcat).astype(BF16)]
    v_dup = [jnp.where(lo_mask, vcat, v_sw).astype(BF16), jnp.where(lo_mask, v_sw, vcat).astype(BF16)]
    if banded:
        key_chunk = c - WIN_CHUNKS + lax.broadcasted_iota(I32, (1, n_k), 1) // CHUNK
        valid = key_chunk >= 0
    n_q = q.shape[0]
    hi_mask = jnp.logical_not(lo_mask)
    for kvh in range(A_KV):
        heads = range(kvh * A_GROUP, (kvh + 1) * A_GROUP)
        qs = jnp.concatenate(
            [jnp.where(lo_mask if hq % 2 == 0 else hi_mask, q[:, (hq // 2) * LANES:(hq // 2 + 1) * LANES], 0.0)
             for hq in heads], axis=0).astype(BF16)
        bias = jnp.concatenate([bias_ref[hq] for hq in heads], axis=0)
        sink = jnp.concatenate([jnp.broadcast_to(sink_ref[0:1, hq:hq + 1], (n_q, 1)) for hq in heads], axis=0)
        s = _mm_nt(qs, k_dup[kvh]) * (A_HD ** -0.5) + bias
        if banded:
            s = jnp.where(valid, s, NEG_INF)
        m = jnp.maximum(jnp.max(s, axis=-1, keepdims=True), sink)
        pr = jnp.exp(s - m)
        den = jnp.sum(pr, axis=-1, keepdims=True) + jnp.exp(sink - m)
        o = _mm((pr / den).astype(BF16), v_dup[kvh])
        for i in range(A_GROUP // 2):
            pair = kvh * (A_GROUP // 2) + i
            o_ref[0, :, pair * LANES:(pair + 1) * LANES] = jnp.where(
                lo_mask, o[2 * i * n_q:(2 * i + 1) * n_q], o[(2 * i + 1) * n_q:(2 * i + 2) * n_q])


def _swa_prompt(qn, kn, z, bias, sink):
    b, t, _ = qn.shape
    v_col = (R_PROJ + A_DIM + KV_DIM) // KV_DIM
    seg = lambda s, col: pl.BlockSpec(
        (1, CHUNK, KV_DIM), lambda bi, c: (bi, jnp.maximum(c - WIN_CHUNKS + s, 0), col))
    n_seg = WIN_CHUNKS + 1
    return pl.pallas_call(
        functools.partial(_swa_kernel, n_seg=n_seg, banded=True),
        grid=(b, t // CHUNK),
        in_specs=[pl.BlockSpec((1, CHUNK, A_DIM), lambda bi, c: (bi, c, 0))]
                 + [seg(s, 0) for s in range(n_seg)] + [seg(s, v_col) for s in range(n_seg)]
                 + [pl.BlockSpec(bias.shape, lambda bi, c: (0, 0, 0)),
                    pl.BlockSpec((1, A_HEADS), lambda bi, c: (0, 0))],
        out_specs=pl.BlockSpec((1, CHUNK, A_DIM), lambda bi, c: (bi, c, 0)),
        out_shape=jax.ShapeDtypeStruct((b, t, A_DIM), F32),
        compiler_params=_params(("parallel", "parallel")),
        name="swa_prompt",
    )(qn, *([kn] * n_seg), *([z] * n_seg), bias, sink.reshape(1, A_HEADS))


def _swa_sample(qn, kn, v_new, k_cache, v_cache, bias, sink):
    b, t, _ = qn.shape
    n_cache = k_cache.shape[1]
    cur = lambda w: pl.BlockSpec((1, t, w), lambda bi, c: (bi, 0, 0))
    old = pl.BlockSpec((1, n_cache, KV_DIM), lambda bi, c: (bi, 0, 0))
    return pl.pallas_call(
        functools.partial(_swa_kernel, n_seg=2, banded=False),
        grid=(b, 1),
        in_specs=[cur(A_DIM), old, cur(KV_DIM), old, cur(KV_DIM),
                  pl.BlockSpec(bias.shape, lambda bi, c: (0, 0, 0)),
                  pl.BlockSpec((1, A_HEADS), lambda bi, c: (0, 0))],
        out_specs=cur(A_DIM),
        out_shape=jax.ShapeDtypeStruct((b, t, A_DIM), F32),
        compiler_params=_params(("parallel", "parallel")),
        name="swa_sample",
    )(qn, k_cache, kn, v_cache, v_new, bias, sink.reshape(1, A_HEADS))


def _mix_out_kernel(y_ref, bo_ref, g_ref, a_ref, x_ref, gng_ref, gnb_ref, e_ref, wo_ref, ln_ref, wq_ref,
                    xo_ref, h_ref, qq_ref):
    e = e_ref[...]
    y = y_ref[...]
    mu = _seg_sum(y, e) * (1.0 / R_HD)
    d = y - mu
    var = _seg_sum(d * d, e) * (1.0 / R_HD)
    yn = d * lax.rsqrt(var + GN_EPS) * gng_ref[...] + gnb_ref[...]
    mix_r = ((yn + bo_ref[...]) * g_ref[...]).astype(BF16)
    x = (x_ref[...] + _mm(mix_r, wo_ref[:R_DIM, :]) + _mm(a_ref[...].astype(BF16), wo_ref[R_DIM:, :]))
    xo_ref[...] = x
    ms = jnp.mean(x * x, axis=-1, keepdims=True)
    h = x * lax.rsqrt(ms + NORM_EPS) * ln_ref[...]
    h_ref[...] = h
    qq_ref[...] = _mm(h.astype(BF16), wq_ref[...])


def _mix_out(y, bonus, g, a_out, x, lp):
    n = x.shape[0]
    tm = min(n, 256)
    nq = PEER_HEADS * PK_DIM
    rows = lambda w: pl.BlockSpec((tm, w), lambda i: (i, 0))
    full = lambda s: pl.BlockSpec(s, lambda i: (0, 0))
    return pl.pallas_call(
        _mix_out_kernel,
        grid=(n // tm,),
        in_specs=[rows(R_DIM), rows(R_DIM), rows(R_DIM), rows(A_DIM), rows(D_MODEL),
                  full((1, R_DIM)), full((1, R_DIM)), full((R_DIM, R_DIM)),
                  full((D_MODEL, D_MODEL)), full((1, D_MODEL)), full((D_MODEL, nq))],
        out_specs=[rows(D_MODEL), rows(D_MODEL), rows(nq)],
        out_shape=[jax.ShapeDtypeStruct((n, D_MODEL), F32), jax.ShapeDtypeStruct((n, D_MODEL), F32),
                   jax.ShapeDtypeStruct((n, nq), F32)],
        compiler_params=_params(("parallel",)),
        name="mix_out_query",
    )(y, bonus, g, a_out, x, lp["gn_g"].reshape(1, -1), lp["gn_b"].reshape(1, -1), lp["e64"],
      lp["w_out"], lp["ln2_g"].reshape(1, -1), lp["w_pq"])


def _pick_rounds(s_ref, n_rows, emit):
    tb = s_ref.shape[1]
    rowid = lax.broadcasted_iota(I32, (n_rows, tb), 0)
    for rnd in range(PEER_TOPK):
        s = s_ref[...]
        m = jnp.max(s, axis=0, keepdims=True)
        idx = jnp.min(jnp.where(s == m, rowid, n_rows), axis=0, keepdims=True)
        hit = rowid == idx
        s_ref[...] = jnp.where(hit, -jnp.inf, s)
        emit(rnd, m, idx, hit)


def _topk_kernel(qq_ref, keys_ref, ei_ref, gate_ref, s_ref, sv_ref, si_ref, c_ref, ci_ref, ts_ref):
    tb = qq_ref.shape[0]
    for half in range(2):
        qh = qq_ref[:, half * PK_HALF:(half + 1) * PK_HALF]
        s_ref[...] = _x3(_mm_nt, keys_ref[half], qh)

        def emit1(rnd, m, idx, hit, half=half):
            sv_ref[half, rnd:rnd + 1, :] = m
            si_ref[half, rnd:rnd + 1, :] = idx

        _pick_rounds(s_ref, N_KEYS, emit1)

    row0 = 0
    for a, nb in enumerate(CAND_COLS):
        if nb == 1:
            break
        c_ref[row0:row0 + nb, :] = sv_ref[0, a:a + 1, :] + sv_ref[1, 0:nb, :]
        ci_ref[row0:row0 + nb, :] = si_ref[0, a:a + 1, :] * N_KEYS + si_ref[1, 0:nb, :]
        row0 += nb
    n_one = PEER_TOPK - a
    c_ref[row0:row0 + n_one, :] = sv_ref[0, a:, :] + sv_ref[1, 0:1, :]
    ci_ref[row0:row0 + n_one, :] = si_ref[0, a:, :] * N_KEYS + si_ref[1, 0:1, :]
    row0 += n_one
    c_ref[row0:, :] = jnp.full((N_CAND - row0, tb), -jnp.inf, F32)
    ci_ref[row0:, :] = jnp.zeros((N_CAND - row0, tb), I32)

    def emit2(rnd, m, idx, hit):
        ts_ref[rnd:rnd + 1, :] = m
        ei_ref[0, rnd:rnd + 1, :] = jnp.max(jnp.where(hit, ci_ref[...], -1), axis=0, keepdims=True)

    _pick_rounds(c_ref, N_CAND, emit2)
    ts = ts_ref[...]
    ex = jnp.exp(ts - ts[0:1, :])
    gate_ref[0] = ex / jnp.sum(ex, axis=0, keepdims=True)


def _topk(qq, sub_keys):
    n = qq.shape[0]
    tb = min(n, 512)
    out = pl.BlockSpec((1, PEER_TOPK, tb), lambda i, h: (h, 0, i))
    return pl.pallas_call(
        _topk_kernel,
        grid=(n // tb, PEER_HEADS),
        in_specs=[pl.BlockSpec((tb, PK_DIM), lambda i, h: (i, h)),
                  pl.BlockSpec((2, N_KEYS, PK_HALF), lambda i, h: (0, 0, 0))],
        out_specs=[out, out],
        out_shape=[jax.ShapeDtypeStruct((PEER_HEADS, PEER_TOPK, n), I32),
                   jax.ShapeDtypeStruct((PEER_HEADS, PEER_TOPK, n), F32)],
        scratch_shapes=[pltpu.VMEM((N_KEYS, tb), F32), pltpu.VMEM((2, PEER_TOPK, tb), F32),
                        pltpu.VMEM((2, PEER_TOPK, tb), I32), pltpu.VMEM((N_CAND, tb), F32),
                        pltpu.VMEM((N_CAND, tb), I32), pltpu.VMEM((PEER_TOPK, tb), F32)],
        compiler_params=_params(("parallel", "parallel")),
        name="peer_topk",
    )(qq, sub_keys)


def _expert_gather(tbl_hbm, idx_v, rows_v, sem, tt, g, buf):
    return pltpu.make_async_copy(tbl_hbm.at[idx_v[pl.ds(tt * N_SEL + g * SC_GROUP, SC_GROUP)]],
                                 rows_v.at[buf], sem.at[buf])


def _sc_tok_blk(n):
    return math.gcd(n // SC_WORKERS, SC_TOK_BLK)


def _sc_token_blocks(n, tbl_hbm, idx_v, rows_v, sem, load_block, compute, store_block):
    wid = lax.axis_index("s") * SC_CORES + lax.axis_index("c")
    npw = n // SC_WORKERS
    tb = _sc_tok_blk(n)
    steps = tb * SC_N_GROUPS

    @pl.loop(0, npw // tb)
    def _(blk):
        tok0 = wid * npw + blk * tb
        load_block(tok0)
        for s in range(SC_NBUF - 1):
            _expert_gather(tbl_hbm, idx_v, rows_v, sem, s // SC_N_GROUPS, s % SC_N_GROUPS, s).start()

        @pl.loop(0, steps, step=SC_NBUF)
        def _(s0):
            for b in range(SC_NBUF):
                s = s0 + b
                ahead = s + SC_NBUF - 1

                @pl.when(ahead < steps)
                def _():
                    _expert_gather(tbl_hbm, idx_v, rows_v, sem, ahead // SC_N_GROUPS, ahead % SC_N_GROUPS,
                                   (b + SC_NBUF - 1) % SC_NBUF).start()

                _expert_gather(tbl_hbm, idx_v, rows_v, sem, s // SC_N_GROUPS, s % SC_N_GROUPS, b).wait()
                compute(s // SC_N_GROUPS, s % SC_N_GROUPS, rows_v.at[b])

        store_block(tok0)


def _pair_products(words, other):
    p = plsc.bitcast(plsc.bitcast(words, BF16) * other, I32)
    return plsc.bitcast(p << 16, F32), plsc.bitcast(p & HI_MASK, F32)


def _expert_cost(n):
    elems = n * N_SEL * D_MODEL
    return pl.CostEstimate(flops=2 * elems, transcendentals=0, bytes_accessed=2 * elems)


def _sc_mesh():
    return plsc.VectorSubcoreMesh(core_axis_name="c", subcore_axis_name="s",
                                  num_cores=SC_CORES, num_subcores=SC_SUBCORES)


def _sc_hidden(eidx, h_words, table):
    n = h_words.shape[0]
    tb = _sc_tok_blk(n)

    @functools.partial(
        pl.kernel, out_type=jax.ShapeDtypeStruct((n * N_SEL,), F32), mesh=_sc_mesh(),
        compiler_params=pltpu.CompilerParams(needs_layout_passes=False),
        scratch_types=[pltpu.VMEM((tb * N_SEL,), I32), pltpu.VMEM((tb * SC_ROW_WORDS,), I32),
                       pltpu.VMEM((SC_NBUF, SC_GROUP, SC_ROW_WORDS), I32), pltpu.VMEM((tb * N_SEL,), F32),
                       pltpu.SemaphoreType.DMA((SC_NBUF,))],
        cost_estimate=_expert_cost(n), name="peer_hidden_sc")
    def run(eidx_hbm, h_hbm, tbl_hbm, hid_hbm, idx_v, h_v, rows_v, hid_v, sem):
        lane = lax.iota(I32, SC_LANES)

        def load_block(tok0):
            pltpu.sync_copy(eidx_hbm.at[pl.ds(tok0 * N_SEL, tb * N_SEL)], idx_v)
            pltpu.sync_copy(h_hbm.at[pl.ds(tok0 * SC_ROW_WORDS, tb * SC_ROW_WORDS)], h_v)

        def compute(tt, g, rows):
            zero = tuple(jnp.zeros((SC_LANES,), F32) for _ in range(SC_GROUP))

            @plsc.parallel_loop(0, SC_ROW_WORDS // SC_LANES, unroll=2, carry=zero)
            def accs(j, acc):
                hw = plsc.bitcast(h_v[pl.ds(tt * SC_ROW_WORDS + j * SC_LANES, SC_LANES)], BF16)
                out = []
                for r in range(SC_GROUP):
                    lo, hi = _pair_products(rows[r, pl.ds(j * SC_LANES, SC_LANES)], hw)
                    out.append(acc[r] + (lo + hi))
                return tuple(out)

            tot = jnp.zeros((SC_LANES,), F32)
            for r in range(SC_GROUP):
                tot = jnp.where(lane == r, jnp.sum(accs[r]), tot)
            hid_v[pl.ds(tt * N_SEL + g * SC_GROUP, SC_GROUP)] = tot

        def store_block(tok0):
            pltpu.sync_copy(hid_v, hid_hbm.at[pl.ds(tok0 * N_SEL, tb * N_SEL)])

        _sc_token_blocks(n, tbl_hbm, idx_v, rows_v, sem, load_block, compute, store_block)

    return run(eidx.reshape(-1), h_words.reshape(-1), table).reshape(n, N_SEL)


def _sc_combine(eidx, coef_words, x, table):
    n = x.shape[0]
    tb = _sc_tok_blk(n)

    @functools.partial(
        pl.kernel, out_type=jax.ShapeDtypeStruct((n * D_MODEL,), F32), mesh=_sc_mesh(),
        compiler_params=pltpu.CompilerParams(needs_layout_passes=False),
        scratch_types=[pltpu.VMEM((tb * N_SEL,), I32), pltpu.VMEM((tb * N_SEL,), I32),
                       pltpu.VMEM((SC_NBUF, SC_GROUP, SC_ROW_WORDS), I32), pltpu.VMEM((tb * D_MODEL,), F32),
                       pltpu.SemaphoreType.DMA((SC_NBUF,))],
        cost_estimate=_expert_cost(n), name="peer_combine_sc")
    def run(eidx_hbm, c_hbm, x_hbm, tbl_hbm, out_hbm, idx_v, c_v, rows_v, out_v, sem):
        def load_block(tok0):
            pltpu.sync_copy(eidx_hbm.at[pl.ds(tok0 * N_SEL, tb * N_SEL)], idx_v)
            pltpu.sync_copy(c_hbm.at[pl.ds(tok0 * N_SEL, tb * N_SEL)], c_v)
            pltpu.sync_copy(x_hbm.at[pl.ds(tok0 * D_MODEL, tb * D_MODEL)], out_v)

        def compute(tt, g, rows):
            base = tt * N_SEL + g * SC_GROUP
            cs = [plsc.bitcast(plsc.load_gather(c_v, [jnp.full((SC_LANES,), r, I32) + base]), BF16)
                  for r in range(SC_GROUP)]

            @plsc.parallel_loop(0, SC_ROW_WORDS // SC_LANES, unroll=2)
            def _(j):
                first = pl.ds(tt * D_MODEL + j * SC_LANES, SC_LANES)
                second = pl.ds(tt * D_MODEL + SC_ROW_WORDS + j * SC_LANES, SC_LANES)
                acc_lo = out_v[first]
                acc_hi = out_v[second]
                for r in range(SC_GROUP):
                    lo, hi = _pair_products(rows[r, pl.ds(j * SC_LANES, SC_LANES)], cs[r])
                    acc_lo = acc_lo + lo
                    acc_hi = acc_hi + hi
                out_v[first] = acc_lo
                out_v[second] = acc_hi

        def store_block(tok0):
            pltpu.sync_copy(out_v, out_hbm.at[pl.ds(tok0 * D_MODEL, tb * D_MODEL)])

        _sc_token_blocks(n, tbl_hbm, idx_v, rows_v, sem, load_block, compute, store_block)

    return run(eidx.reshape(-1), coef_words.reshape(-1), x.reshape(-1), table).reshape(n, D_MODEL)


def _gate_act_kernel(hid_ref, gate_ref, o_ref):
    hid = hid_ref[...]
    o_ref[...] = gate_ref[...] * (0.5 * hid * (1.0 + lax.erf(hid * np.float32(np.sqrt(0.5)))))


def _gate_act(hid, gate):
    n = hid.shape[0]
    tm = math.gcd(n, 2048)
    rows = pl.BlockSpec((tm, N_SEL), lambda i: (i, 0))
    return pl.pallas_call(
        _gate_act_kernel, grid=(n // tm,), in_specs=[rows, rows], out_specs=rows,
        out_shape=jax.ShapeDtypeStruct((n, N_SEL), F32), compiler_params=_params(("parallel",)),
        name="peer_gate_act",
    )(hid, gate)


def _t5_bucket(rel):
    nb = NUM_BUCKETS // 2
    max_exact = nb // 2
    ret = jnp.where(rel > 0, nb, 0)
    n = jnp.abs(rel)
    nf = jnp.maximum(n, 1).astype(F32)
    large = max_exact + (jnp.log(nf / max_exact) / math.log(MAX_DISTANCE / max_exact)
                         * (nb - max_exact)).astype(I32)
    large = jnp.minimum(large, nb - 1)
    return ret + jnp.where(n < max_exact, n, large)


def _rel_bias(rel_bias, n_q, n_k, n_before):
    rel = (jnp.arange(n_k)[None, :] - n_before) - jnp.arange(n_q)[:, None]
    return jnp.transpose(rel_bias[_t5_bucket(rel)].astype(F32), (2, 0, 1))


def _state_to_pairs(wkv):
    b = wkv.shape[0]
    s = wkv.reshape(b, N_PAIRS, 2, R_HD, R_HD)
    z = jnp.zeros_like(s[:, :, 0])
    top = jnp.concatenate([s[:, :, 0], z], axis=-1)
    bot = jnp.concatenate([z, s[:, :, 1]], axis=-1)
    return jnp.concatenate([top, bot], axis=-2)


def _pairs_to_state(s_bd):
    b = s_bd.shape[0]
    return jnp.stack([s_bd[:, :, :R_HD, :R_HD], s_bd[:, :, R_HD:, R_HD:]], axis=2).reshape(
        b, R_HEADS, R_HD, R_HD)


def _layer_dense(x, shift0, wkv0, kv_cache, bias, lp):
    b, t, _ = x.shape
    n = b * t
    z = _norm_matmul(x.reshape(n, D_MODEL), lp["ln1_g"].reshape(1, -1), lp["w_in"], 512).reshape(b, t, IN_COLS)
    r, k2, v, kk, bb, lw, g, bonus, qn, kn = _prep(z, shift0, lp)
    v_new = z[:, :, R_PROJ + A_DIM + KV_DIM:]
    if t % CHUNK:
        pad = lambda a: jnp.pad(a, ((0, 0), (0, CHUNK - t % CHUNK), (0, 0)))
        y, s_fin = _chunk_scan(*(pad(a) for a in (r, k2, v, kk, bb, lw)), _state_to_pairs(wkv0))
        y = y[:, :t]
    else:
        y, s_fin = _chunk_scan(r, k2, v, kk, bb, lw, _state_to_pairs(wkv0))
    if kv_cache is None:
        a_out = _swa_prompt(qn, kn, z, bias, lp["sink"])
    else:
        a_out = _swa_sample(qn, kn, v_new, kv_cache[0].reshape(b, -1, KV_DIM),
                            kv_cache[1].reshape(b, -1, KV_DIM), bias, lp["sink"])
    flat = lambda a: a.reshape(n, a.shape[-1])
    x1, h, qq = _mix_out(flat(y), flat(bonus), flat(g), flat(a_out), flat(x), lp)
    eidx, gate = _topk(qq, lp["sub_keys"])
    sel = lambda a: jnp.transpose(a, (2, 0, 1)).reshape(n, N_SEL)
    return ((sel(eidx), sel(gate), h, x1),
            (_pairs_to_state(s_fin), z[:, -1, :R_PROJ],
             kn.reshape(b, t, A_KV, A_HD), v_new.reshape(b, t, A_KV, A_HD)))


def _bf16_pair_words(a):
    k = a.shape[1] // 2
    bits = lax.bitcast_convert_type(a.astype(BF16), jnp.uint16).astype(jnp.uint32)
    return lax.bitcast_convert_type(bits[:, :k] | (bits[:, k:] << 16), I32)


def _layer_experts(ops, lp, after=None):
    eidx, gate, h, x1 = ops
    h_words = _bf16_pair_words(h)
    if after is not None:
        h_words, after = lax.optimization_barrier((h_words, after))
    coef = _gate_act(_sc_hidden(eidx, h_words, lp["peer_u"]), gate)
    coef_words = _bf16_pair_words(jnp.concatenate([coef, coef], axis=1))
    return _sc_combine(eidx, coef_words, x1, lp["peer_v"]), after


def kernel(x_prompt, x_sample, state_rwkv_wkv, state_rwkv_shift, cache_swa_k, cache_swa_v, ln1_g, w_in,
           mu_shift, w0, w2, a0, a2, g2, k_k, k_a, r_k, gn_g, gn_b, q_norm_g, k_norm_g, attn_sink,
           rel_bias, w_out, ln2_g, w_pq, sub_keys, peer_u, peer_v):
    depth = w_in.shape[0]
    b_p, s_p = x_prompt.shape[:2]
    b_s, t_s = x_sample.shape[:2]
    n_cache = cache_swa_k.shape[2]
    n_keep = min(WINDOW, s_p)
    bias_p = _rel_bias(rel_bias, CHUNK, (WIN_CHUNKS + 1) * CHUNK, WIN_CHUNKS * CHUNK)
    bias_s = _rel_bias(rel_bias, t_s, n_cache + t_s, n_cache)
    head_id = jnp.arange(R_DIM) // R_HD
    e64 = (head_id[:, None] == head_id[None, :]).astype(BF16)
    zpad = jnp.zeros((LANES - W_LORA, R_DIM), F32)
    n_groups = PROMPT_GROUPS if b_p % PROMPT_GROUPS == 0 else 1
    xp, xs = jnp.split(x_prompt, n_groups, axis=0), x_sample
    outs = [[] for _ in range(8)]
    for l in range(depth):
        lp = {
            "ln1_g": ln1_g[l], "w_in": w_in[l].astype(BF16), "mu": mu_shift[l], "w0": w0[l], "a0": a0[l],
            "k_k": k_k[l], "k_a": k_a[l], "r_k": r_k[l].reshape(-1), "gn_g": gn_g[l], "gn_b": gn_b[l],
            "w2p": jnp.concatenate([w2[l], zpad], axis=0).astype(BF16),
            "a2p": jnp.concatenate([zpad, a2[l]], axis=0).astype(BF16),
            "g2": g2[l].astype(BF16), "e64": e64,
            "q_gain": jnp.tile(q_norm_g[l], A_HEADS), "k_gain": jnp.tile(k_norm_g[l], A_KV),
            "sink": attn_sink[l].astype(F32), "w_out": w_out[l].astype(BF16), "ln2_g": ln2_g[l],
            "w_pq": w_pq[l].astype(BF16), "sub_keys": sub_keys[l], "peer_u": _bf16_pair_words(peer_u[l]), "peer_v": _bf16_pair_words(peer_v[l]),
        }
        parts = []
        for xg in xp:
            if parts:
                (eidx, *rest), states = parts[-1]
                xg, eidx = lax.optimization_barrier((xg, eidx))
                parts[-1] = ((eidx, *rest), states)
            parts.append(_layer_dense(xg, jnp.zeros((xg.shape[0], R_PROJ), F32),
                                      jnp.zeros((xg.shape[0], R_HEADS, R_HD, R_HD), F32), None, bias_p, lp))
        wkv_p, sh_p, k_p, v_p = (jnp.concatenate([pt[1][i] for pt in parts], axis=0) for i in range(4))
        outs_x = []
        for ops, _ in parts:
            if outs_x:
                xo, outs_x[-1] = _layer_experts(ops, lp, after=outs_x[-1])
            else:
                xo, _ = _layer_experts(ops, lp)
            outs_x.append(xo)
        xp = [xo.reshape(xg.shape) for xo, xg in zip(outs_x, xp)]
        ops_s, (wkv_s, sh_s, k_s, v_s) = _layer_dense(
            xs, state_rwkv_shift[l], state_rwkv_wkv[l].astype(F32), (cache_swa_k[l], cache_swa_v[l]),
            bias_s, lp)
        xs = _layer_experts(ops_s, lp)[0].reshape(xs.shape)
        for lst, val in zip(outs, (wkv_p, sh_p, k_p[:, s_p - n_keep:], v_p[:, s_p - n_keep:],
                                   wkv_s, sh_s, k_s, v_s)):
            lst.append(val)
    return (jnp.concatenate(xp, axis=0), xs) + tuple(jnp.stack(o) for o in outs)
```

```python
import functools
import math

import numpy as np
import jax
import jax.numpy as jnp
from jax import lax
from jax.experimental import pallas as pl
from jax.experimental.pallas import tpu as pltpu
from jax.experimental.pallas import tpu_sc as plsc

F32 = jnp.float32
BF16 = jnp.bfloat16
I32 = jnp.int32

D_MODEL = 1024
CHUNK = 64
R_HEADS = 8
R_HD = 64
R_DIM = R_HEADS * R_HD
W_LORA = 64
A_LORA = 64
G_LORA = 128
R_PROJ = 3 * R_DIM + W_LORA + A_LORA + G_LORA
A_HEADS = 8
A_KV = 2
A_GROUP = A_HEADS // A_KV
A_HD = 64
A_DIM = A_HEADS * A_HD
KV_DIM = A_KV * A_HD
IN_COLS = R_PROJ + A_DIM + 2 * KV_DIM
WINDOW = 128
WIN_CHUNKS = WINDOW // CHUNK
NUM_BUCKETS = 32
MAX_DISTANCE = 128
PEER_HEADS = 8
N_KEYS = 128
PK_DIM = 256
PK_HALF = PK_DIM // 2
PEER_TOPK = 16
N_SEL = PEER_HEADS * PEER_TOPK
NORM_EPS = 1e-6
GN_EPS = 64e-5
NEG_INF = -1e30

LANES = 128
N_PAIRS = R_DIM // LANES
VMEM_LIMIT = 48 * 1024 * 1024
CAND_COLS = (16, 8, 8, 4, 4, 4, 4, 4, 1, 1, 1, 1, 1, 1, 1, 1)
assert all(nb >= PEER_TOPK // (a + 1) for a, nb in enumerate(CAND_COLS))
N_CAND = 64
assert sum(CAND_COLS) <= N_CAND
SC_CORES = 2
SC_SUBCORES = 16
SC_LANES = 16
SC_WORKERS = SC_CORES * SC_SUBCORES
SC_TOK_BLK = 32
SC_GROUP = SC_LANES
SC_N_GROUPS = N_SEL // SC_GROUP
SC_NBUF = 8
SC_ROW_WORDS = D_MODEL // 2
HI_MASK = -65536
PROMPT_GROUPS = 8


def _params(sem):
    return pltpu.CompilerParams(dimension_semantics=sem, vmem_limit_bytes=VMEM_LIMIT)


def _mm(a, b):
    return jnp.dot(a, b, preferred_element_type=F32)


def _mm_nt(a, b):
    return lax.dot_general(a, b, (((1,), (1,)), ((), ())), preferred_element_type=F32)


def _mm_tn(a, b):
    return lax.dot_general(a, b, (((0,), (0,)), ((), ())), preferred_element_type=F32)


def _split2(a):
    hi = a.astype(BF16)
    return hi, (a - hi.astype(F32)).astype(BF16)


def _split3(a):
    hi = a.astype(BF16)
    r = a - hi.astype(F32)
    mid = r.astype(BF16)
    return hi, mid, (r - mid.astype(F32)).astype(BF16)


def _x3(mm, a, b):
    ah, al = _split2(a)
    bh, bl = _split2(b)
    return mm(ah, bh) + mm(ah, bl) + mm(al, bh)


def _exact_lhs(mm, a_bf16, b):
    b0, b1, b2 = _split3(b)
    return mm(a_bf16, b0) + mm(a_bf16, b1) + mm(a_bf16, b2)


def _seg_sum(x, e_bf16):
    x0, x1, x2 = _split3(x)
    return _mm(x0, e_bf16) + _mm(x1, e_bf16) + _mm(x2, e_bf16)


def _sigmoid(x):
    return 1.0 / (1.0 + jnp.exp(-x))


def _norm_matmul_kernel(x_ref, g_ref, w_ref, o_ref, h_ref):
    @pl.when(pl.program_id(1) == 0)
    def _():
        x = x_ref[...]
        ms = jnp.mean(x * x, axis=-1, keepdims=True)
        h_ref[...] = (x * lax.rsqrt(ms + NORM_EPS) * g_ref[...]).astype(BF16)

    o_ref[...] = _mm(h_ref[...], w_ref[...])


def _norm_matmul(x, g, w_bf16, tn):
    n, k = x.shape
    m = w_bf16.shape[1]
    tm = min(n, 512)
    return pl.pallas_call(
        _norm_matmul_kernel,
        grid=(n // tm, m // tn),
        in_specs=[pl.BlockSpec((tm, k), lambda i, j: (i, 0)),
                  pl.BlockSpec((1, k), lambda i, j: (0, 0)),
                  pl.BlockSpec((k, tn), lambda i, j: (0, j))],
        out_specs=pl.BlockSpec((tm, tn), lambda i, j: (i, j)),
        out_shape=jax.ShapeDtypeStruct((n, m), F32),
        scratch_shapes=[pltpu.VMEM((tm, k), BF16)],
        compiler_params=_params(("parallel", "arbitrary")),
        name="norm_inproj",
    )(x, g, w_bf16)


def _prep_kernel(z_ref, zp_ref, sh_ref, mu_ref, w0_ref, a0_ref, kk_ref, ka_ref, rk_ref,
                 w2_ref, a2_ref, g2_ref, e_ref, qg_ref, kg_ref,
                 r_o, k_o, v_o, kk_o, b_o, lw_o, g_o, bo_o, qn_o, kn_o):
    i = pl.program_id(1)
    zt = z_ref[0]
    tp = zt.shape[0]
    zr = zt[:, :R_PROJ]
    prev_row = jnp.where(i == 0, sh_ref[0], zp_ref[0][7:8, :R_PROJ])
    row = lax.broadcasted_iota(I32, (tp, 1), 0)
    prev = jnp.where(row == 0, prev_row, pltpu.roll(zr, 1, axis=0))
    zs = zr + (prev - zr) * mu_ref[...]
    r = zs[:, 0:R_DIM]
    k = zs[:, R_DIM:2 * R_DIM]
    v = zs[:, 2 * R_DIM:3 * R_DIM]
    lo = zs[:, 3 * R_DIM:3 * R_DIM + W_LORA + A_LORA]
    g_lo = zs[:, 3 * R_DIM + W_LORA + A_LORA:R_PROJ]
    e = e_ref[...]
    w_in = -(w0_ref[...] + _mm(jnp.tanh(lo).astype(BF16), w2_ref[...]))
    softplus = jnp.maximum(w_in, 0.0) + jnp.log1p(jnp.exp(-jnp.abs(w_in)))
    w_log = -softplus - 0.5
    lw_o[0] = -jnp.exp(w_log)
    a = _sigmoid(a0_ref[...] + _mm(lo.astype(BF16), a2_ref[...]))
    g_o[0] = _mm(_sigmoid(g_lo).astype(BF16), g2_ref[...])
    kk = k * kk_ref[...]
    kk = kk / jnp.maximum(jnp.sqrt(_seg_sum(kk * kk, e)), 1e-12)
    k2 = k * (1.0 + (a - 1.0) * ka_ref[...])
    r_o[0] = r
    k_o[0] = k2
    v_o[0] = v
    kk_o[0] = kk
    b_o[0] = kk * a
    bo_o[0] = _seg_sum(r * k2 * rk_ref[...], e) * v
    q = zt[:, R_PROJ:R_PROJ + A_DIM]
    qn_o[0] = q * lax.rsqrt(_seg_sum(q * q, e) * (1.0 / A_HD) + NORM_EPS) * qg_ref[...]
    kx = zt[:, R_PROJ + A_DIM:R_PROJ + A_DIM + KV_DIM]
    e_kv = e[:KV_DIM, :KV_DIM]
    kn_o[0] = kx * lax.rsqrt(_seg_sum(kx * kx, e_kv) * (1.0 / A_HD) + NORM_EPS) * kg_ref[...]


def _prep(z, shift0, lp):
    b, t, _ = z.shape
    tp = min(t, 256)
    row = lambda a: a.reshape(1, -1)
    vec = lambda n: pl.BlockSpec((1, n), lambda bi, i: (0, 0))
    full = lambda s: pl.BlockSpec(s, lambda bi, i: (0, 0))
    wide = pl.BlockSpec((1, tp, R_DIM), lambda bi, i: (bi, i, 0))
    outs = [jax.ShapeDtypeStruct((b, t, R_DIM), F32)] * 9 + [jax.ShapeDtypeStruct((b, t, KV_DIM), F32)]
    return pl.pallas_call(
        _prep_kernel,
        grid=(b, t // tp),
        in_specs=[pl.BlockSpec((1, tp, IN_COLS), lambda bi, i: (bi, i, 0)),
                  pl.BlockSpec((1, 8, IN_COLS), lambda bi, i: (bi, jnp.maximum(i * (tp // 8) - 1, 0), 0)),
                  pl.BlockSpec((1, 1, R_PROJ), lambda bi, i: (bi, 0, 0)),
                  vec(R_PROJ), vec(R_DIM), vec(R_DIM), vec(R_DIM), vec(R_DIM), vec(R_DIM),
                  full((LANES, R_DIM)), full((LANES, R_DIM)), full((G_LORA, R_DIM)),
                  full((R_DIM, R_DIM)), vec(A_DIM), vec(KV_DIM)],
        out_specs=[wide] * 9 + [pl.BlockSpec((1, tp, KV_DIM), lambda bi, i: (bi, i, 0))],
        out_shape=outs,
        compiler_params=_params(("parallel", "parallel")),
        name="rwkv_prep",
    )(z, z, shift0.reshape(b, 1, R_PROJ), row(lp["mu"]), row(lp["w0"]), row(lp["a0"]), row(lp["k_k"]),
      row(lp["k_a"]), row(lp["r_k"]), lp["w2p"], lp["a2p"], lp["g2"], lp["e64"], row(lp["q_gain"]),
      row(lp["k_gain"]))


def _stack_heads(x, lo_mask):
    return jnp.concatenate([jnp.where(lo_mask, x, 0.0), jnp.where(lo_mask, 0.0, x)], axis=0)


def _chunk_kernel(r_ref, k_ref, v_ref, kk_ref, b_ref, lw_ref, s0_ref, y_ref, sf_ref, s_ref):
    c = pl.program_id(1)
    L = CHUNK

    @pl.when(c == 0)
    def _():
        s_ref[...] = s0_ref[0]

    lane = lax.broadcasted_iota(I32, (1, LANES), 1)
    lo_mask = lane < R_HD
    rr = lax.broadcasted_iota(I32, (L, 2 * L), 0)
    cc = lax.broadcasted_iota(I32, (L, 2 * L), 1)
    cc = jnp.where(cc >= L, cc - L, cc)
    strict = rr > cc
    incl = rr >= cc
    t_r = lax.broadcasted_iota(I32, (L, L), 0)
    t_c = lax.broadcasted_iota(I32, (L, L), 1)
    tri = (t_r >= t_c).astype(BF16)
    col2 = lax.broadcasted_iota(I32, (L, 2 * L), 1) < L
    eye_r = lax.broadcasted_iota(I32, (2 * L, 2 * L), 0)
    eye_c = lax.broadcasted_iota(I32, (2 * L, 2 * L), 1)
    eye = (eye_r == eye_c).astype(F32)

    pairs = range(N_PAIRS)
    sls = [slice(p * LANES, (p + 1) * LANES) for p in pairs]
    r = [r_ref[0][:, sl] for sl in sls]
    k = [k_ref[0][:, sl] for sl in sls]
    v = [v_ref[0][:, sl] for sl in sls]
    kk = [kk_ref[0][:, sl] for sl in sls]
    bb = [b_ref[0][:, sl] for sl in sls]
    lw = [lw_ref[0][:, sl] for sl in sls]
    s0 = [s_ref[p] for p in pairs]

    cl = [_exact_lhs(_mm, tri, lw[p]) for p in pairs]
    cl_last = [cl[p][L - 1:L, :] for p in pairs]
    e_neg = [jnp.exp(-cl[p]) for p in pairs]
    e_last = [jnp.exp(cl_last[p] - cl[p]) for p in pairs]
    rt = [r[p] * jnp.exp(cl[p]) for p in pairs]
    at = [kk[p] * jnp.exp(cl[p] - lw[p]) for p in pairs]

    gm = [_x3(_mm_nt, jnp.concatenate([at[p], rt[p]], axis=0),
              jnp.concatenate([_stack_heads(k[p] * e_neg[p], lo_mask), _stack_heads(bb[p] * e_neg[p], lo_mask)],
                              axis=0)) for p in pairs]
    mk = [jnp.where(strict, gm[p][:L, :2 * L], 0.0) for p in pairs]
    mb = [jnp.where(strict, gm[p][:L, 2 * L:], 0.0) for p in pairs]
    hk = [jnp.where(incl, gm[p][L:, :2 * L], 0.0) for p in pairs]
    hb = [jnp.where(incl, gm[p][L:, 2 * L:], 0.0) for p in pairs]

    nil = [-jnp.concatenate([jnp.where(col2, mb[p], 0.0), jnp.where(col2, 0.0, mb[p])], axis=0) for p in pairs]
    tinv = [eye + nil[p] for p in pairs]
    nil = [_x3(_mm, nil[p], nil[p]) for p in pairs]
    for it in range(5):
        tinv = [tinv[p] + _x3(_mm, nil[p], tinv[p]) for p in pairs]
        if it < 4:
            nil = [_x3(_mm, nil[p], nil[p]) for p in pairs]

    vs = [_stack_heads(v[p], lo_mask) for p in pairs]
    rhs_u = [-(_x3(_mm_nt, at[p], s0[p]) + _x3(_mm, mk[p], vs[p])) for p in pairs]
    us = [_x3(_mm, tinv[p], _stack_heads(rhs_u[p], lo_mask)) for p in pairs]
    for p in pairs:
        y_ref[0, :, sls[p]] = _x3(_mm_nt, rt[p], s0[p]) + _x3(_mm, hk[p], vs[p]) + _x3(_mm, hb[p], us[p])
    for p in pairs:
        s_ref[p] = (s0[p] * jnp.exp(cl_last[p])
                    + _x3(_mm_tn, vs[p], _stack_heads(k[p] * e_last[p], lo_mask))
                    + _x3(_mm_tn, us[p], _stack_heads(bb[p] * e_last[p], lo_mask)))

    @pl.when(c == pl.num_programs(1) - 1)
    def _():
        sf_ref[0] = s_ref[...]


def _chunk_scan(r, k, v, kk, bb, lw, s0_bd):
    b, t, _ = r.shape
    wide = pl.BlockSpec((1, CHUNK, R_DIM), lambda bi, c: (bi, c, 0))
    st = pl.BlockSpec((1, N_PAIRS, LANES, LANES), lambda bi, c: (bi, 0, 0, 0))
    return pl.pallas_call(
        _chunk_kernel,
        grid=(b, t // CHUNK),
        in_specs=[wide] * 6 + [st],
        out_specs=[wide, st],
        out_shape=[jax.ShapeDtypeStruct((b, t, R_DIM), F32),
                   jax.ShapeDtypeStruct((b, N_PAIRS, LANES, LANES), F32)],
        scratch_shapes=[pltpu.VMEM((N_PAIRS, LANES, LANES), F32)],
        compiler_params=_params(("parallel", "arbitrary")),
        name="rwkv_chunk",
    )(r, k, v, kk, bb, lw, s0_bd)


def _swa_kernel(*refs, n_seg, banded):
    q_ref = refs[0]
    k_refs = refs[1:1 + n_seg]
    v_refs = refs[1 + n_seg:1 + 2 * n_seg]
    bias_ref, sink_ref, o_ref = refs[1 + 2 * n_seg:]
    c = pl.program_id(1)
    q = q_ref[0]
    kcat = jnp.concatenate([kr[0] for kr in k_refs], axis=0)
    vcat = jnp.concatenate([vr[0] for vr in v_refs], axis=0)
    n_k = kcat.shape[0]
    lane = lax.broadcasted_iota(I32, (1, LANES), 1)
    lo_mask = lane < A_HD
    k_sw = pltpu.roll(kcat, A_HD, axis=1)
    v_sw = pltpu.roll(vcat, A_HD, axis=1)
    k_dup = [jnp.where(lo_mask, kcat, k_sw).astype(BF16), jnp.where(lo_mask, k_sw, kcat).astype(BF16)]
    v_dup = [jnp.where(lo_mask, vcat, v_sw).astype(BF16), jnp.where(lo_mask, v_sw, vcat).astype(BF16)]
    if banded:
        key_chunk = c - WIN_CHUNKS + lax.broadcasted_iota(I32, (1, n_k), 1) // CHUNK
        valid = key_chunk >= 0
    n_q = q.shape[0]
    hi_mask = jnp.logical_not(lo_mask)
    for kvh in range(A_KV):
        heads = range(kvh * A_GROUP, (kvh + 1) * A_GROUP)
        qs = jnp.concatenate(
            [jnp.where(lo_mask if hq % 2 == 0 else hi_mask, q[:, (hq // 2) * LANES:(hq // 2 + 1) * LANES], 0.0)
             for hq in heads], axis=0).astype(BF16)
        bias = jnp.concatenate([bias_ref[hq] for hq in heads], axis=0)
        sink = jnp.concatenate([jnp.broadcast_to(sink_ref[0:1, hq:hq + 1], (n_q, 1)) for hq in heads], axis=0)
        s = _mm_nt(qs, k_dup[kvh]) * (A_HD ** -0.5) + bias
        if banded:
            s = jnp.where(valid, s, NEG_INF)
        m = jnp.maximum(jnp.max(s, axis=-1, keepdims=True), sink)
        pr = jnp.exp(s - m)
        den = jnp.sum(pr, axis=-1, keepdims=True) + jnp.exp(sink - m)
        o = _mm((pr / den).astype(BF16), v_dup[kvh])
        for i in range(A_GROUP // 2):
            pair = kvh * (A_GROUP // 2) + i
            o_ref[0, :, pair * LANES:(pair + 1) * LANES] = jnp.where(
                lo_mask, o[2 * i * n_q:(2 * i + 1) * n_q], o[(2 * i + 1) * n_q:(2 * i + 2) * n_q])


def _swa_prompt(qn, kn, z, bias, sink):
    b, t, _ = qn.shape
    v_col = (R_PROJ + A_DIM + KV_DIM) // KV_DIM
    seg = lambda s, col: pl.BlockSpec(
        (1, CHUNK, KV_DIM), lambda bi, c: (bi, jnp.maximum(c - WIN_CHUNKS + s, 0), col))
    n_seg = WIN_CHUNKS + 1
    return pl.pallas_call(
        functools.partial(_swa_kernel, n_seg=n_seg, banded=True),
        grid=(b, t // CHUNK),
        in_specs=[pl.BlockSpec((1, CHUNK, A_DIM), lambda bi, c: (bi, c, 0))]
                 + [seg(s, 0) for s in range(n_seg)] + [seg(s, v_col) for s in range(n_seg)]
                 + [pl.BlockSpec(bias.shape, lambda bi, c: (0, 0, 0)),
                    pl.BlockSpec((1, A_HEADS), lambda bi, c: (0, 0))],
        out_specs=pl.BlockSpec((1, CHUNK, A_DIM), lambda bi, c: (bi, c, 0)),
        out_shape=jax.ShapeDtypeStruct((b, t, A_DIM), F32),
        compiler_params=_params(("parallel", "parallel")),
        name="swa_prompt",
    )(qn, *([kn] * n_seg), *([z] * n_seg), bias, sink.reshape(1, A_HEADS))


def _swa_sample(qn, kn, v_new, k_cache, v_cache, bias, sink):
    b, t, _ = qn.shape
    n_cache = k_cache.shape[1]
    cur = lambda w: pl.BlockSpec((1, t, w), lambda bi, c: (bi, 0, 0))
    old = pl.BlockSpec((1, n_cache, KV_DIM), lambda bi, c: (bi, 0, 0))
    return pl.pallas_call(
        functools.partial(_swa_kernel, n_seg=2, banded=False),
        grid=(b, 1),
        in_specs=[cur(A_DIM), old, cur(KV_DIM), old, cur(KV_DIM),
                  pl.BlockSpec(bias.shape, lambda bi, c: (0, 0, 0)),
                  pl.BlockSpec((1, A_HEADS), lambda bi, c: (0, 0))],
        out_specs=cur(A_DIM),
        out_shape=jax.ShapeDtypeStruct((b, t, A_DIM), F32),
        compiler_params=_params(("parallel", "parallel")),
        name="swa_sample",
    )(qn, k_cache, kn, v_cache, v_new, bias, sink.reshape(1, A_HEADS))


def _mix_out_kernel(y_ref, bo_ref, g_ref, a_ref, x_ref, gng_ref, gnb_ref, e_ref, wo_ref, ln_ref, wq_ref,
                    xo_ref, h_ref, qq_ref):
    e = e_ref[...]
    y = y_ref[...]
    mu = _seg_sum(y, e) * (1.0 / R_HD)
    d = y - mu
    var = _seg_sum(d * d, e) * (1.0 / R_HD)
    yn = d * lax.rsqrt(var + GN_EPS) * gng_ref[...] + gnb_ref[...]
    mix_r = ((yn + bo_ref[...]) * g_ref[...]).astype(BF16)
    x = (x_ref[...] + _mm(mix_r, wo_ref[:R_DIM, :]) + _mm(a_ref[...].astype(BF16), wo_ref[R_DIM:, :]))
    xo_ref[...] = x
    ms = jnp.mean(x * x, axis=-1, keepdims=True)
    h = x * lax.rsqrt(ms + NORM_EPS) * ln_ref[...]
    h_ref[...] = h
    qq_ref[...] = _mm(h.astype(BF16), wq_ref[...])


def _mix_out(y, bonus, g, a_out, x, lp):
    n = x.shape[0]
    tm = min(n, 256)
    nq = PEER_HEADS * PK_DIM
    rows = lambda w: pl.BlockSpec((tm, w), lambda i: (i, 0))
    full = lambda s: pl.BlockSpec(s, lambda i: (0, 0))
    return pl.pallas_call(
        _mix_out_kernel,
        grid=(n // tm,),
        in_specs=[rows(R_DIM), rows(R_DIM), rows(R_DIM), rows(A_DIM), rows(D_MODEL),
                  full((1, R_DIM)), full((1, R_DIM)), full((R_DIM, R_DIM)),
                  full((D_MODEL, D_MODEL)), full((1, D_MODEL)), full((D_MODEL, nq))],
        out_specs=[rows(D_MODEL), rows(D_MODEL), rows(nq)],
        out_shape=[jax.ShapeDtypeStruct((n, D_MODEL), F32), jax.ShapeDtypeStruct((n, D_MODEL), F32),
                   jax.ShapeDtypeStruct((n, nq), F32)],
        compiler_params=_params(("parallel",)),
        name="mix_out_query",
    )(y, bonus, g, a_out, x, lp["gn_g"].reshape(1, -1), lp["gn_b"].reshape(1, -1), lp["e64"],
      lp["w_out"], lp["ln2_g"].reshape(1, -1), lp["w_pq"])


def _pick_rounds(s_ref, n_rows, emit):
    tb = s_ref.shape[1]
    rowid = lax.broadcasted_iota(I32, (n_rows, tb), 0)
    for rnd in range(PEER_TOPK):
        s = s_ref[...]
        m = jnp.max(s, axis=0, keepdims=True)
        idx = jnp.min(jnp.where(s == m, rowid, n_rows), axis=0, keepdims=True)
        hit = rowid == idx
        s_ref[...] = jnp.where(hit, -jnp.inf, s)
        emit(rnd, m, idx, hit)


def _topk_kernel(qq_ref, keys_ref, ei_ref, gate_ref, s_ref, sv_ref, si_ref, c_ref, ci_ref, ts_ref):
    tb = qq_ref.shape[0]
    for half in range(2):
        qh = qq_ref[:, half * PK_HALF:(half + 1) * PK_HALF]
        s_ref[...] = _x3(_mm_nt, keys_ref[half], qh)

        def emit1(rnd, m, idx, hit, half=half):
            sv_ref[half, rnd:rnd + 1, :] = m
            si_ref[half, rnd:rnd + 1, :] = idx

        _pick_rounds(s_ref, N_KEYS, emit1)

    row0 = 0
    for a, nb in enumerate(CAND_COLS):
        if nb == 1:
            break
        c_ref[row0:row0 + nb, :] = sv_ref[0, a:a + 1, :] + sv_ref[1, 0:nb, :]
        ci_ref[row0:row0 + nb, :] = si_ref[0, a:a + 1, :] * N_KEYS + si_ref[1, 0:nb, :]
        row0 += nb
    n_one = PEER_TOPK - a
    c_ref[row0:row0 + n_one, :] = sv_ref[0, a:, :] + sv_ref[1, 0:1, :]
    ci_ref[row0:row0 + n_one, :] = si_ref[0, a:, :] * N_KEYS + si_ref[1, 0:1, :]
    row0 += n_one
    c_ref[row0:, :] = jnp.full((N_CAND - row0, tb), -jnp.inf, F32)
    ci_ref[row0:, :] = jnp.zeros((N_CAND - row0, tb), I32)

    def emit2(rnd, m, idx, hit):
        ts_ref[rnd:rnd + 1, :] = m
        ei_ref[0, rnd:rnd + 1, :] = jnp.max(jnp.where(hit, ci_ref[...], -1), axis=0, keepdims=True)

    _pick_rounds(c_ref, N_CAND, emit2)
    ts = ts_ref[...]
    ex = jnp.exp(ts - ts[0:1, :])
    gate_ref[0] = ex / jnp.sum(ex, axis=0, keepdims=True)


def _topk(qq, sub_keys):
    n = qq.shape[0]
    tb = min(n, 512)
    out = pl.BlockSpec((1, PEER_TOPK, tb), lambda i, h: (h, 0, i))
    return pl.pallas_call(
        _topk_kernel,
        grid=(n // tb, PEER_HEADS),
        in_specs=[pl.BlockSpec((tb, PK_DIM), lambda i, h: (i, h)),
                  pl.BlockSpec((2, N_KEYS, PK_HALF), lambda i, h: (0, 0, 0))],
        out_specs=[out, out],
        out_shape=[jax.ShapeDtypeStruct((PEER_HEADS, PEER_TOPK, n), I32),
                   jax.ShapeDtypeStruct((PEER_HEADS, PEER_TOPK, n), F32)],
        scratch_shapes=[pltpu.VMEM((N_KEYS, tb), F32), pltpu.VMEM((2, PEER_TOPK, tb), F32),
                        pltpu.VMEM((2, PEER_TOPK, tb), I32), pltpu.VMEM((N_CAND, tb), F32),
                        pltpu.VMEM((N_CAND, tb), I32), pltpu.VMEM((PEER_TOPK, tb), F32)],
        compiler_params=_params(("parallel", "parallel")),
        name="peer_topk",
    )(qq, sub_keys)


def _expert_gather(tbl_hbm, idx_v, rows_v, sem, tt, g, buf):
    return pltpu.make_async_copy(tbl_hbm.at[idx_v[pl.ds(tt * N_SEL + g * SC_GROUP, SC_GROUP)]],
                                 rows_v.at[buf], sem.at[buf])


def _sc_tok_blk(n):
    return math.gcd(n // SC_WORKERS, SC_TOK_BLK)


def _sc_token_blocks(n, tbl_hbm, idx_v, rows_v, sem, load_block, compute, store_block):
    wid = lax.axis_index("s") * SC_CORES + lax.axis_index("c")
    npw = n // SC_WORKERS
    tb = _sc_tok_blk(n)
    steps = tb * SC_N_GROUPS

    @pl.loop(0, npw // tb)
    def _(blk):
        tok0 = wid * npw + blk * tb
        load_block(tok0)
        for s in range(SC_NBUF - 1):
            _expert_gather(tbl_hbm, idx_v, rows_v, sem, s // SC_N_GROUPS, s % SC_N_GROUPS, s).start()

        @pl.loop(0, steps, step=SC_NBUF)
        def _(s0):
            for b in range(SC_NBUF):
                s = s0 + b
                ahead = s + SC_NBUF - 1

                @pl.when(ahead < steps)
                def _():
                    _expert_gather(tbl_hbm, idx_v, rows_v, sem, ahead // SC_N_GROUPS, ahead % SC_N_GROUPS,
                                   (b + SC_NBUF - 1) % SC_NBUF).start()

                _expert_gather(tbl_hbm, idx_v, rows_v, sem, s // SC_N_GROUPS, s % SC_N_GROUPS, b).wait()
                compute(s // SC_N_GROUPS, s % SC_N_GROUPS, rows_v.at[b])

        store_block(tok0)


def _pair_products(words, other):
    p = plsc.bitcast(plsc.bitcast(words, BF16) * other, I32)
    return plsc.bitcast(p << 16, F32), plsc.bitcast(p & HI_MASK, F32)


def _expert_cost(n):
    elems = n * N_SEL * D_MODEL
    return pl.CostEstimate(flops=2 * elems, transcendentals=0, bytes_accessed=2 * elems)


def _sc_mesh():
    return plsc.VectorSubcoreMesh(core_axis_name="c", subcore_axis_name="s",
                                  num_cores=SC_CORES, num_subcores=SC_SUBCORES)


def _sc_hidden(eidx, h_words, table):
    n = h_words.shape[0]
    tb = _sc_tok_blk(n)

    @functools.partial(
        pl.kernel, out_type=jax.ShapeDtypeStruct((n * N_SEL,), F32), mesh=_sc_mesh(),
        compiler_params=pltpu.CompilerParams(needs_layout_passes=False),
        scratch_types=[pltpu.VMEM((tb * N_SEL,), I32), pltpu.VMEM((tb * SC_ROW_WORDS,), I32),
                       pltpu.VMEM((SC_NBUF, SC_GROUP, SC_ROW_WORDS), I32), pltpu.VMEM((tb * N_SEL,), F32),
                       pltpu.SemaphoreType.DMA((SC_NBUF,))],
        cost_estimate=_expert_cost(n), name="peer_hidden_sc")
    def run(eidx_hbm, h_hbm, tbl_hbm, hid_hbm, idx_v, h_v, rows_v, hid_v, sem):
        lane = lax.iota(I32, SC_LANES)

        def load_block(tok0):
            pltpu.sync_copy(eidx_hbm.at[pl.ds(tok0 * N_SEL, tb * N_SEL)], idx_v)
            pltpu.sync_copy(h_hbm.at[pl.ds(tok0 * SC_ROW_WORDS, tb * SC_ROW_WORDS)], h_v)

        def compute(tt, g, rows):
            zero = tuple(jnp.zeros((SC_LANES,), F32) for _ in range(SC_GROUP))

            @plsc.parallel_loop(0, SC_ROW_WORDS // SC_LANES, unroll=2, carry=zero)
            def accs(j, acc):
                hw = plsc.bitcast(h_v[pl.ds(tt * SC_ROW_WORDS + j * SC_LANES, SC_LANES)], BF16)
                out = []
                for r in range(SC_GROUP):
                    lo, hi = _pair_products(rows[r, pl.ds(j * SC_LANES, SC_LANES)], hw)
                    out.append(acc[r] + (lo + hi))
                return tuple(out)

            tot = jnp.zeros((SC_LANES,), F32)
            for r in range(SC_GROUP):
                tot = jnp.where(lane == r, jnp.sum(accs[r]), tot)
            hid_v[pl.ds(tt * N_SEL + g * SC_GROUP, SC_GROUP)] = tot

        def store_block(tok0):
            pltpu.sync_copy(hid_v, hid_hbm.at[pl.ds(tok0 * N_SEL, tb * N_SEL)])

        _sc_token_blocks(n, tbl_hbm, idx_v, rows_v, sem, load_block, compute, store_block)

    return run(eidx.reshape(-1), h_words.reshape(-1), table).reshape(n, N_SEL)


def _sc_combine(eidx, coef_words, x, table):
    n = x.shape[0]
    tb = _sc_tok_blk(n)

    @functools.partial(
        pl.kernel, out_type=jax.ShapeDtypeStruct((n * D_MODEL,), F32), mesh=_sc_mesh(),
        compiler_params=pltpu.CompilerParams(needs_layout_passes=False),
        scratch_types=[pltpu.VMEM((tb * N_SEL,), I32), pltpu.VMEM((tb * N_SEL,), I32),
                       pltpu.VMEM((SC_NBUF, SC_GROUP, SC_ROW_WORDS), I32), pltpu.VMEM((tb * D_MODEL,), F32),
                       pltpu.SemaphoreType.DMA((SC_NBUF,))],
        cost_estimate=_expert_cost(n), name="peer_combine_sc")
    def run(eidx_hbm, c_hbm, x_hbm, tbl_hbm, out_hbm, idx_v, c_v, rows_v, out_v, sem):
        def load_block(tok0):
            pltpu.sync_copy(eidx_hbm.at[pl.ds(tok0 * N_SEL, tb * N_SEL)], idx_v)
            pltpu.sync_copy(c_hbm.at[pl.ds(tok0 * N_SEL, tb * N_SEL)], c_v)
            pltpu.sync_copy(x_hbm.at[pl.ds(tok0 * D_MODEL, tb * D_MODEL)], out_v)

        def compute(tt, g, rows):
            base = tt * N_SEL + g * SC_GROUP
            cs = [plsc.bitcast(plsc.load_gather(c_v, [jnp.full((SC_LANES,), r, I32) + base]), BF16)
                  for r in range(SC_GROUP)]

            @plsc.parallel_loop(0, SC_ROW_WORDS // SC_LANES, unroll=2)
            def _(j):
                first = pl.ds(tt * D_MODEL + j * SC_LANES, SC_LANES)
                second = pl.ds(tt * D_MODEL + SC_ROW_WORDS + j * SC_LANES, SC_LANES)
                acc_lo = out_v[first]
                acc_hi = out_v[second]
                for r in range(SC_GROUP):
                    lo, hi = _pair_products(rows[r, pl.ds(j * SC_LANES, SC_LANES)], cs[r])
                    acc_lo = acc_lo + lo
                    acc_hi = acc_hi + hi
                out_v[first] = acc_lo
                out_v[second] = acc_hi

        def store_block(tok0):
            pltpu.sync_copy(out_v, out_hbm.at[pl.ds(tok0 * D_MODEL, tb * D_MODEL)])

        _sc_token_blocks(n, tbl_hbm, idx_v, rows_v, sem, load_block, compute, store_block)

    return run(eidx.reshape(-1), coef_words.reshape(-1), x.reshape(-1), table).reshape(n, D_MODEL)


def _gate_act_kernel(hid_ref, gate_ref, o_ref):
    hid = hid_ref[...]
    o_ref[...] = gate_ref[...] * (0.5 * hid * (1.0 + lax.erf(hid * np.float32(np.sqrt(0.5)))))


def _gate_act(hid, gate):
    n = hid.shape[0]
    tm = math.gcd(n, 2048)
    rows = pl.BlockSpec((tm, N_SEL), lambda i: (i, 0))
    return pl.pallas_call(
        _gate_act_kernel, grid=(n // tm,), in_specs=[rows, rows], out_specs=rows,
        out_shape=jax.ShapeDtypeStruct((n, N_SEL), F32), compiler_params=_params(("parallel",)),
        name="peer_gate_act",
    )(hid, gate)


def _t5_bucket(rel):
    nb = NUM_BUCKETS // 2
    max_exact = nb // 2
    ret = jnp.where(rel > 0, nb, 0)
    n = jnp.abs(rel)
    nf = jnp.maximum(n, 1).astype(F32)
    large = max_exact + (jnp.log(nf / max_exact) / math.log(MAX_DISTANCE / max_exact)
                         * (nb - max_exact)).astype(I32)
    large = jnp.minimum(large, nb - 1)
    return ret + jnp.where(n < max_exact, n, large)


def _rel_bias(rel_bias, n_q, n_k, n_before):
    rel = (jnp.arange(n_k)[None, :] - n_before) - jnp.arange(n_q)[:, None]
    return jnp.transpose(rel_bias[_t5_bucket(rel)].astype(F32), (2, 0, 1))


def _state_to_pairs(wkv):
    b = wkv.shape[0]
    s = wkv.reshape(b, N_PAIRS, 2, R_HD, R_HD)
    z = jnp.zeros_like(s[:, :, 0])
    top = jnp.concatenate([s[:, :, 0], z], axis=-1)
    bot = jnp.concatenate([z, s[:, :, 1]], axis=-1)
    return jnp.concatenate([top, bot], axis=-2)


def _pairs_to_state(s_bd):
    b = s_bd.shape[0]
    return jnp.stack([s_bd[:, :, :R_HD, :R_HD], s_bd[:, :, R_HD:, R_HD:]], axis=2).reshape(
        b, R_HEADS, R_HD, R_HD)


def _layer_dense(x, shift0, wkv0, kv_cache, bias, lp):
    b, t, _ = x.shape
    n = b * t
    z = _norm_matmul(x.reshape(n, D_MODEL), lp["ln1_g"].reshape(1, -1), lp["w_in"], 512).reshape(b, t, IN_COLS)
    r, k2, v, kk, bb, lw, g, bonus, qn, kn = _prep(z, shift0, lp)
    v_new = z[:, :, R_PROJ + A_DIM + KV_DIM:]
    if t % CHUNK:
        pad = lambda a: jnp.pad(a, ((0, 0), (0, CHUNK - t % CHUNK), (0, 0)))
        y, s_fin = _chunk_scan(*(pad(a) for a in (r, k2, v, kk, bb, lw)), _state_to_pairs(wkv0))
        y = y[:, :t]
    else:
        y, s_fin = _chunk_scan(r, k2, v, kk, bb, lw, _state_to_pairs(wkv0))
    if kv_cache is None:
        a_out = _swa_prompt(qn, kn, z, bias, lp["sink"])
    else:
        a_out = _swa_sample(qn, kn, v_new, kv_cache[0].reshape(b, -1, KV_DIM),
                            kv_cache[1].reshape(b, -1, KV_DIM), bias, lp["sink"])
    flat = lambda a: a.reshape(n, a.shape[-1])
    x1, h, qq = _mix_out(flat(y), flat(bonus), flat(g), flat(a_out), flat(x), lp)
    eidx, gate = _topk(qq, lp["sub_keys"])
    sel = lambda a: jnp.transpose(a, (2, 0, 1)).reshape(n, N_SEL)
    return ((sel(eidx), sel(gate), h, x1),
            (_pairs_to_state(s_fin), z[:, -1, :R_PROJ],
             kn.reshape(b, t, A_KV, A_HD), v_new.reshape(b, t, A_KV, A_HD)))


def _bf16_pair_words(a):
    k = a.shape[1] // 2
    bits = lax.bitcast_convert_type(a.astype(BF16), jnp.uint16).astype(jnp.uint32)
    return lax.bitcast_convert_type(bits[:, :k] | (bits[:, k:] << 16), I32)


def _layer_experts(ops, lp, after=None):
    eidx, gate, h, x1 = ops
    h_words = _bf16_pair_words(h)
    if after is not None:
        h_words, after = lax.optimization_barrier((h_words, after))
    coef = _gate_act(_sc_hidden(eidx, h_words, lp["peer_u"]), gate)
    coef_words = _bf16_pair_words(jnp.concatenate([coef, coef], axis=1))
    return _sc_combine(eidx, coef_words, x1, lp["peer_v"]), after


def kernel(x_prompt, x_sample, state_rwkv_wkv, state_rwkv_shift, cache_swa_k, cache_swa_v, ln1_g, w_in,
           mu_shift, w0, w2, a0, a2, g2, k_k, k_a, r_k, gn_g, gn_b, q_norm_g, k_norm_g, attn_sink,
           rel_bias, w_out, ln2_g, w_pq, sub_keys, peer_u, peer_v):
    depth = w_in.shape[0]
    b_p, s_p = x_prompt.shape[:2]
    b_s, t_s = x_sample.shape[:2]
    n_cache = cache_swa_k.shape[2]
    n_keep = min(WINDOW, s_p)
    bias_p = _rel_bias(rel_bias, CHUNK, (WIN_CHUNKS + 1) * CHUNK, WIN_CHUNKS * CHUNK)
    bias_s = _rel_bias(rel_bias, t_s, n_cache + t_s, n_cache)
    head_id = jnp.arange(R_DIM) // R_HD
    e64 = (head_id[:, None] == head_id[None, :]).astype(BF16)
    zpad = jnp.zeros((LANES - W_LORA, R_DIM), F32)
    n_groups = PROMPT_GROUPS if b_p % PROMPT_GROUPS == 0 else 1
    xp, xs = jnp.split(x_prompt, n_groups, axis=0), x_sample
    outs = [[] for _ in range(8)]
    for l in range(depth):
        lp = {
            "ln1_g": ln1_g[l], "w_in": w_in[l].astype(BF16), "mu": mu_shift[l], "w0": w0[l], "a0": a0[l],
            "k_k": k_k[l], "k_a": k_a[l], "r_k": r_k[l].reshape(-1), "gn_g": gn_g[l], "gn_b": gn_b[l],
            "w2p": jnp.concatenate([w2[l], zpad], axis=0).astype(BF16),
            "a2p": jnp.concatenate([zpad, a2[l]], axis=0).astype(BF16),
            "g2": g2[l].astype(BF16), "e64": e64,
            "q_gain": jnp.tile(q_norm_g[l], A_HEADS), "k_gain": jnp.tile(k_norm_g[l], A_KV),
            "sink": attn_sink[l].astype(F32), "w_out": w_out[l].astype(BF16), "ln2_g": ln2_g[l],
            "w_pq": w_pq[l].astype(BF16), "sub_keys": sub_keys[l], "peer_u": _bf16_pair_words(peer_u[l]), "peer_v": _bf16_pair_words(peer_v[l]),
        }
        parts = []
        for xg in xp:
            if parts:
                (eidx, *rest), states = parts[-1]
                xg, eidx = lax.optimization_barrier((xg, eidx))
                parts[-1] = ((eidx, *rest), states)
            parts.append(_layer_dense(xg, jnp.zeros((xg.shape[0], R_PROJ), F32),
                                      jnp.zeros((xg.shape[0], R_HEADS, R_HD, R_HD), F32), None, bias_p, lp))
        wkv_p, sh_p, k_p, v_p = (jnp.concatenate([pt[1][i] for pt in parts], axis=0) for i in range(4))
        outs_x = []
        for ops, _ in parts:
            if outs_x:
                xo, outs_x[-1] = _layer_experts(ops, lp, after=outs_x[-1])
            else:
                xo, _ = _layer_experts(ops, lp)
            outs_x.append(xo)
        xp = [xo.reshape(xg.shape) for xo, xg in zip(outs_x, xp)]
        ops_s, (wkv_s, sh_s, k_s, v_s) = _layer_dense(
            xs, state_rwkv_shift[l], state_rwkv_wkv[l].astype(F32), (cache_swa_k[l], cache_swa_v[l]),
            bias_s, lp)
        xs = _layer_experts(ops_s, lp)[0].reshape(xs.shape)
        for lst, val in zip(outs, (wkv_p, sh_p, k_p[:, s_p - n_keep:], v_p[:, s_p - n_keep:],
                                   wkv_s, sh_s, k_s, v_s)):
            lst.append(val)
    return (jnp.concatenate(xp, axis=0), xs) + tuple(jnp.stack(o) for o in outs)
```

```python
import functools
import math

import numpy as np
import jax
import jax.numpy as jnp
from jax import lax
from jax.experimental import pallas as pl
from jax.experimental.pallas import tpu as pltpu
from jax.experimental.pallas import tpu_sc as plsc

F32 = jnp.float32
BF16 = jnp.bfloat16
I32 = jnp.int32

D_MODEL = 1024
CHUNK = 64
R_HEADS = 8
R_HD = 64
R_DIM = R_HEADS * R_HD
W_LORA = 64
A_LORA = 64
G_LORA = 128
R_PROJ = 3 * R_DIM + W_LORA + A_LORA + G_LORA
A_HEADS = 8
A_KV = 2
A_GROUP = A_HEADS // A_KV
A_HD = 64
A_DIM = A_HEADS * A_HD
KV_DIM = A_KV * A_HD
IN_COLS = R_PROJ + A_DIM + 2 * KV_DIM
WINDOW = 128
WIN_CHUNKS = WINDOW // CHUNK
NUM_BUCKETS = 32
MAX_DISTANCE = 128
PEER_HEADS = 8
N_KEYS = 128
PK_DIM = 256
PK_HALF = PK_DIM // 2
PEER_TOPK = 16
N_SEL = PEER_HEADS * PEER_TOPK
NORM_EPS = 1e-6
GN_EPS = 64e-5
NEG_INF = -1e30

LANES = 128
N_PAIRS = R_DIM // LANES
VMEM_LIMIT = 48 * 1024 * 1024
CAND_COLS = (16, 8, 8, 4, 4, 4, 4, 4, 1, 1, 1, 1, 1, 1, 1, 1)
assert all(nb >= PEER_TOPK // (a + 1) for a, nb in enumerate(CAND_COLS))
N_CAND = 64
assert sum(CAND_COLS) <= N_CAND
SC_CORES = 2
SC_SUBCORES = 16
SC_LANES = 16
SC_WORKERS = SC_CORES * SC_SUBCORES
SC_TOK_BLK = 64
SC_GROUP = SC_LANES
SC_N_GROUPS = N_SEL // SC_GROUP
SC_NBUF = 4
SC_ROW_WORDS = D_MODEL // 2
HI_MASK = -65536
PROMPT_GROUPS = 8


def _params(sem):
    return pltpu.CompilerParams(dimension_semantics=sem, vmem_limit_bytes=VMEM_LIMIT)


def _mm(a, b):
    return jnp.dot(a, b, preferred_element_type=F32)


def _mm_nt(a, b):
    return lax.dot_general(a, b, (((1,), (1,)), ((), ())), preferred_element_type=F32)


def _mm_tn(a, b):
    return lax.dot_general(a, b, (((0,), (0,)), ((), ())), preferred_element_type=F32)


def _split2(a):
    hi = a.astype(BF16)
    return hi, (a - hi.astype(F32)).astype(BF16)


def _split3(a):
    hi = a.astype(BF16)
    r = a - hi.astype(F32)
    mid = r.astype(BF16)
    return hi, mid, (r - mid.astype(F32)).astype(BF16)


def _x3(mm, a, b):
    ah, al = _split2(a)
    bh, bl = _split2(b)
    return mm(ah, bh) + mm(ah, bl) + mm(al, bh)


def _exact_lhs(mm, a_bf16, b):
    b0, b1, b2 = _split3(b)
    return mm(a_bf16, b0) + mm(a_bf16, b1) + mm(a_bf16, b2)


def _seg_sum(x, e_bf16):
    x0, x1, x2 = _split3(x)
    return _mm(x0, e_bf16) + _mm(x1, e_bf16) + _mm(x2, e_bf16)


def _sigmoid(x):
    return 1.0 / (1.0 + jnp.exp(-x))


def _norm_matmul_kernel(x_ref, g_ref, w_ref, o_ref, h_ref):
    @pl.when(pl.program_id(1) == 0)
    def _():
        x = x_ref[...]
        ms = jnp.mean(x * x, axis=-1, keepdims=True)
        h_ref[...] = (x * lax.rsqrt(ms + NORM_EPS) * g_ref[...]).astype(BF16)

    o_ref[...] = _mm(h_ref[...], w_ref[...])


def _norm_matmul(x, g, w_bf16, tn):
    n, k = x.shape
    m = w_bf16.shape[1]
    tm = min(n, 512)
    return pl.pallas_call(
        _norm_matmul_kernel,
        grid=(n // tm, m // tn),
        in_specs=[pl.BlockSpec((tm, k), lambda i, j: (i, 0)),
                  pl.BlockSpec((1, k), lambda i, j: (0, 0)),
                  pl.BlockSpec((k, tn), lambda i, j: (0, j))],
        out_specs=pl.BlockSpec((tm, tn), lambda i, j: (i, j)),
        out_shape=jax.ShapeDtypeStruct((n, m), F32),
        scratch_shapes=[pltpu.VMEM((tm, k), BF16)],
        compiler_params=_params(("parallel", "arbitrary")),
        name="norm_inproj",
    )(x, g, w_bf16)


def _prep_kernel(z_ref, zp_ref, sh_ref, mu_ref, w0_ref, a0_ref, kk_ref, ka_ref, rk_ref,
                 w2_ref, a2_ref, g2_ref, e_ref, qg_ref, kg_ref,
                 r_o, k_o, v_o, kk_o, b_o, lw_o, g_o, bo_o, qn_o, kn_o):
    i = pl.program_id(1)
    zt = z_ref[0]
    tp = zt.shape[0]
    zr = zt[:, :R_PROJ]
    prev_row = jnp.where(i == 0, sh_ref[0], zp_ref[0][7:8, :R_PROJ])
    row = lax.broadcasted_iota(I32, (tp, 1), 0)
    prev = jnp.where(row == 0, prev_row, pltpu.roll(zr, 1, axis=0))
    zs = zr + (prev - zr) * mu_ref[...]
    r = zs[:, 0:R_DIM]
    k = zs[:, R_DIM:2 * R_DIM]
    v = zs[:, 2 * R_DIM:3 * R_DIM]
    lo = zs[:, 3 * R_DIM:3 * R_DIM + W_LORA + A_LORA]
    g_lo = zs[:, 3 * R_DIM + W_LORA + A_LORA:R_PROJ]
    e = e_ref[...]
    w_in = -(w0_ref[...] + _mm(jnp.tanh(lo).astype(BF16), w2_ref[...]))
    softplus = jnp.maximum(w_in, 0.0) + jnp.log1p(jnp.exp(-jnp.abs(w_in)))
    w_log = -softplus - 0.5
    lw_o[0] = -jnp.exp(w_log)
    a = _sigmoid(a0_ref[...] + _mm(lo.astype(BF16), a2_ref[...]))
    g_o[0] = _mm(_sigmoid(g_lo).astype(BF16), g2_ref[...])
    kk = k * kk_ref[...]
    kk = kk / jnp.maximum(jnp.sqrt(_seg_sum(kk * kk, e)), 1e-12)
    k2 = k * (1.0 + (a - 1.0) * ka_ref[...])
    r_o[0] = r
    k_o[0] = k2
    v_o[0] = v
    kk_o[0] = kk
    b_o[0] = kk * a
    bo_o[0] = _seg_sum(r * k2 * rk_ref[...], e) * v
    q = zt[:, R_PROJ:R_PROJ + A_DIM]
    qn_o[0] = q * lax.rsqrt(_seg_sum(q * q, e) * (1.0 / A_HD) + NORM_EPS) * qg_ref[...]
    kx = zt[:, R_PROJ + A_DIM:R_PROJ + A_DIM + KV_DIM]
    e_kv = e[:KV_DIM, :KV_DIM]
    kn_o[0] = kx * lax.rsqrt(_seg_sum(kx * kx, e_kv) * (1.0 / A_HD) + NORM_EPS) * kg_ref[...]


def _prep(z, shift0, lp):
    b, t, _ = z.shape
    tp = min(t, 256)
    row = lambda a: a.reshape(1, -1)
    vec = lambda n: pl.BlockSpec((1, n), lambda bi, i: (0, 0))
    full = lambda s: pl.BlockSpec(s, lambda bi, i: (0, 0))
    wide = pl.BlockSpec((1, tp, R_DIM), lambda bi, i: (bi, i, 0))
    outs = [jax.ShapeDtypeStruct((b, t, R_DIM), F32)] * 9 + [jax.ShapeDtypeStruct((b, t, KV_DIM), F32)]
    return pl.pallas_call(
        _prep_kernel,
        grid=(b, t // tp),
        in_specs=[pl.BlockSpec((1, tp, IN_COLS), lambda bi, i: (bi, i, 0)),
                  pl.BlockSpec((1, 8, IN_COLS), lambda bi, i: (bi, jnp.maximum(i * (tp // 8) - 1, 0), 0)),
                  pl.BlockSpec((1, 1, R_PROJ), lambda bi, i: (bi, 0, 0)),
                  vec(R_PROJ), vec(R_DIM), vec(R_DIM), vec(R_DIM), vec(R_DIM), vec(R_DIM),
                  full((LANES, R_DIM)), full((LANES, R_DIM)), full((G_LORA, R_DIM)),
                  full((R_DIM, R_DIM)), vec(A_DIM), vec(KV_DIM)],
        out_specs=[wide] * 9 + [pl.BlockSpec((1, tp, KV_DIM), lambda bi, i: (bi, i, 0))],
        out_shape=outs,
        compiler_params=_params(("parallel", "parallel")),
        name="rwkv_prep",
    )(z, z, shift0.reshape(b, 1, R_PROJ), row(lp["mu"]), row(lp["w0"]), row(lp["a0"]), row(lp["k_k"]),
      row(lp["k_a"]), row(lp["r_k"]), lp["w2p"], lp["a2p"], lp["g2"], lp["e64"], row(lp["q_gain"]),
      row(lp["k_gain"]))


def _stack_heads(x, lo_mask):
    return jnp.concatenate([jnp.where(lo_mask, x, 0.0), jnp.where(lo_mask, 0.0, x)], axis=0)


def _chunk_kernel(r_ref, k_ref, v_ref, kk_ref, b_ref, lw_ref, s0_ref, y_ref, sf_ref, s_ref):
    c = pl.program_id(1)
    L = CHUNK

    @pl.when(c == 0)
    def _():
        s_ref[...] = s0_ref[0]

    lane = lax.broadcasted_iota(I32, (1, LANES), 1)
    lo_mask = lane < R_HD
    rr = lax.broadcasted_iota(I32, (L, 2 * L), 0)
    cc = lax.broadcasted_iota(I32, (L, 2 * L), 1)
    cc = jnp.where(cc >= L, cc - L, cc)
    strict = rr > cc
    incl = rr >= cc
    t_r = lax.broadcasted_iota(I32, (L, L), 0)
    t_c = lax.broadcasted_iota(I32, (L, L), 1)
    tri = (t_r >= t_c).astype(BF16)
    col2 = lax.broadcasted_iota(I32, (L, 2 * L), 1) < L
    eye_r = lax.broadcasted_iota(I32, (2 * L, 2 * L), 0)
    eye_c = lax.broadcasted_iota(I32, (2 * L, 2 * L), 1)
    eye = (eye_r == eye_c).astype(F32)

    pairs = range(N_PAIRS)
    sls = [slice(p * LANES, (p + 1) * LANES) for p in pairs]
    r = [r_ref[0][:, sl] for sl in sls]
    k = [k_ref[0][:, sl] for sl in sls]
    v = [v_ref[0][:, sl] for sl in sls]
    kk = [kk_ref[0][:, sl] for sl in sls]
    bb = [b_ref[0][:, sl] for sl in sls]
    lw = [lw_ref[0][:, sl] for sl in sls]
    s0 = [s_ref[p] for p in pairs]

    cl = [_exact_lhs(_mm, tri, lw[p]) for p in pairs]
    cl_last = [cl[p][L - 1:L, :] for p in pairs]
    e_neg = [jnp.exp(-cl[p]) for p in pairs]
    e_last = [jnp.exp(cl_last[p] - cl[p]) for p in pairs]
    rt = [r[p] * jnp.exp(cl[p]) for p in pairs]
    at = [kk[p] * jnp.exp(cl[p] - lw[p]) for p in pairs]

    gm = [_x3(_mm_nt, jnp.concatenate([at[p], rt[p]], axis=0),
              jnp.concatenate([_stack_heads(k[p] * e_neg[p], lo_mask), _stack_heads(bb[p] * e_neg[p], lo_mask)],
                              axis=0)) for p in pairs]
    mk = [jnp.where(strict, gm[p][:L, :2 * L], 0.0) for p in pairs]
    mb = [jnp.where(strict, gm[p][:L, 2 * L:], 0.0) for p in pairs]
    hk = [jnp.where(incl, gm[p][L:, :2 * L], 0.0) for p in pairs]
    hb = [jnp.where(incl, gm[p][L:, 2 * L:], 0.0) for p in pairs]

    nil = [-jnp.concatenate([jnp.where(col2, mb[p], 0.0), jnp.where(col2, 0.0, mb[p])], axis=0) for p in pairs]
    tinv = [eye + nil[p] for p in pairs]
    nil = [_x3(_mm, nil[p], nil[p]) for p in pairs]
    for it in range(5):
        tinv = [tinv[p] + _x3(_mm, nil[p], tinv[p]) for p in pairs]
        if it < 4:
            nil = [_x3(_mm, nil[p], nil[p]) for p in pairs]

    vs = [_stack_heads(v[p], lo_mask) for p in pairs]
    rhs_u = [-(_x3(_mm_nt, at[p], s0[p]) + _x3(_mm, mk[p], vs[p])) for p in pairs]
    us = [_x3(_mm, tinv[p], _stack_heads(rhs_u[p], lo_mask)) for p in pairs]
    for p in pairs:
        y_ref[0, :, sls[p]] = _x3(_mm_nt, rt[p], s0[p]) + _x3(_mm, hk[p], vs[p]) + _x3(_mm, hb[p], us[p])
    for p in pairs:
        s_ref[p] = (s0[p] * jnp.exp(cl_last[p])
                    + _x3(_mm_tn, vs[p], _stack_heads(k[p] * e_last[p], lo_mask))
                    + _x3(_mm_tn, us[p], _stack_heads(bb[p] * e_last[p], lo_mask)))

    @pl.when(c == pl.num_programs(1) - 1)
    def _():
        sf_ref[0] = s_ref[...]


def _chunk_scan(r, k, v, kk, bb, lw, s0_bd):
    b, t, _ = r.shape
    wide = pl.BlockSpec((1, CHUNK, R_DIM), lambda bi, c: (bi, c, 0))
    st = pl.BlockSpec((1, N_PAIRS, LANES, LANES), lambda bi, c: (bi, 0, 0, 0))
    return pl.pallas_call(
        _chunk_kernel,
        grid=(b, t // CHUNK),
        in_specs=[wide] * 6 + [st],
        out_specs=[wide, st],
        out_shape=[jax.ShapeDtypeStruct((b, t, R_DIM), F32),
                   jax.ShapeDtypeStruct((b, N_PAIRS, LANES, LANES), F32)],
        scratch_shapes=[pltpu.VMEM((N_PAIRS, LANES, LANES), F32)],
        compiler_params=_params(("parallel", "arbitrary")),
        name="rwkv_chunk",
    )(r, k, v, kk, bb, lw, s0_bd)


def _swa_kernel(*refs, n_seg, banded):
    q_ref = refs[0]
    k_refs = refs[1:1 + n_seg]
    v_refs = refs[1 + n_seg:1 + 2 * n_seg]
    bias_ref, sink_ref, o_ref = refs[1 + 2 * n_seg:]
    c = pl.program_id(1)
    q = q_ref[0]
    kcat = jnp.concatenate([kr[0] for kr in k_refs], axis=0)
    vcat = jnp.concatenate([vr[0] for vr in v_refs], axis=0)
    n_k = kcat.shape[0]
    lane = lax.broadcasted_iota(I32, (1, LANES), 1)
    lo_mask = lane < A_HD
    k_sw = pltpu.roll(kcat, A_HD, axis=1)
    v_sw = pltpu.roll(vcat, A_HD, axis=1)
    k_dup = [jnp.where(lo_mask, kcat, k_sw).astype(BF16), jnp.where(lo_mask, k_sw, kcat).astype(BF16)]
    v_dup = [jnp.where(lo_mask, vcat, v_sw).astype(BF16), jnp.where(lo_mask, v_sw, vcat).astype(BF16)]
    if banded:
        key_chunk = c - WIN_CHUNKS + lax.broadcasted_iota(I32, (1, n_k), 1) // CHUNK
        valid = key_chunk >= 0
    n_q = q.shape[0]
    hi_mask = jnp.logical_not(lo_mask)
    for kvh in range(A_KV):
        heads = range(kvh * A_GROUP, (kvh + 1) * A_GROUP)
        qs = jnp.concatenate(
            [jnp.where(lo_mask if hq % 2 == 0 else hi_mask, q[:, (hq // 2) * LANES:(hq // 2 + 1) * LANES], 0.0)
             for hq in heads], axis=0).astype(BF16)
        bias = jnp.concatenate([bias_ref[hq] for hq in heads], axis=0)
        sink = jnp.concatenate([jnp.broadcast_to(sink_ref[0:1, hq:hq + 1], (n_q, 1)) for hq in heads], axis=0)
        s = _mm_nt(qs, k_dup[kvh]) * (A_HD ** -0.5) + bias
        if banded:
            s = jnp.where(valid, s, NEG_INF)
        m = jnp.maximum(jnp.max(s, axis=-1, keepdims=True), sink)
        pr = jnp.exp(s - m)
        den = jnp.sum(pr, axis=-1, keepdims=True) + jnp.exp(sink - m)
        o = _mm((pr / den).astype(BF16), v_dup[kvh])
        for i in range(A_GROUP // 2):
            pair = kvh * (A_GROUP // 2) + i
            o_ref[0, :, pair * LANES:(pair + 1) * LANES] = jnp.where(
                lo_mask, o[2 * i * n_q:(2 * i + 1) * n_q], o[(2 * i + 1) * n_q:(2 * i + 2) * n_q])


def _swa_prompt(qn, kn, z, bias, sink):
    b, t, _ = qn.shape
    v_col = (R_PROJ + A_DIM + KV_DIM) // KV_DIM
    seg = lambda s, col: pl.BlockSpec(
        (1, CHUNK, KV_DIM), lambda bi, c: (bi, jnp.maximum(c - WIN_CHUNKS + s, 0), col))
    n_seg = WIN_CHUNKS + 1
    return pl.pallas_call(
        functools.partial(_swa_kernel, n_seg=n_seg, banded=True),
        grid=(b, t // CHUNK),
        in_specs=[pl.BlockSpec((1, CHUNK, A_DIM), lambda bi, c: (bi, c, 0))]
                 + [seg(s, 0) for s in range(n_seg)] + [seg(s, v_col) for s in range(n_seg)]
                 + [pl.BlockSpec(bias.shape, lambda bi, c: (0, 0, 0)),
                    pl.BlockSpec((1, A_HEADS), lambda bi, c: (0, 0))],
        out_specs=pl.BlockSpec((1, CHUNK, A_DIM), lambda bi, c: (bi, c, 0)),
        out_shape=jax.ShapeDtypeStruct((b, t, A_DIM), F32),
        compiler_params=_params(("parallel", "parallel")),
        name="swa_prompt",
    )(qn, *([kn] * n_seg), *([z] * n_seg), bias, sink.reshape(1, A_HEADS))


def _swa_sample(qn, kn, v_new, k_cache, v_cache, bias, sink):
    b, t, _ = qn.shape
    n_cache = k_cache.shape[1]
    cur = lambda w: pl.BlockSpec((1, t, w), lambda bi, c: (bi, 0, 0))
    old = pl.BlockSpec((1, n_cache, KV_DIM), lambda bi, c: (bi, 0, 0))
    return pl.pallas_call(
        functools.partial(_swa_kernel, n_seg=2, banded=False),
        grid=(b, 1),
        in_specs=[cur(A_DIM), old, cur(KV_DIM), old, cur(KV_DIM),
                  pl.BlockSpec(bias.shape, lambda bi, c: (0, 0, 0)),
                  pl.BlockSpec((1, A_HEADS), lambda bi, c: (0, 0))],
        out_specs=cur(A_DIM),
        out_shape=jax.ShapeDtypeStruct((b, t, A_DIM), F32),
        compiler_params=_params(("parallel", "parallel")),
        name="swa_sample",
    )(qn, k_cache, kn, v_cache, v_new, bias, sink.reshape(1, A_HEADS))


def _mix_out_kernel(y_ref, bo_ref, g_ref, a_ref, x_ref, gng_ref, gnb_ref, e_ref, wo_ref, ln_ref, wq_ref,
                    xo_ref, h_ref, qq_ref):
    e = e_ref[...]
    y = y_ref[...]
    mu = _seg_sum(y, e) * (1.0 / R_HD)
    d = y - mu
    var = _seg_sum(d * d, e) * (1.0 / R_HD)
    yn = d * lax.rsqrt(var + GN_EPS) * gng_ref[...] + gnb_ref[...]
    mix_r = ((yn + bo_ref[...]) * g_ref[...]).astype(BF16)
    x = (x_ref[...] + _mm(mix_r, wo_ref[:R_DIM, :]) + _mm(a_ref[...].astype(BF16), wo_ref[R_DIM:, :]))
    xo_ref[...] = x
    ms = jnp.mean(x * x, axis=-1, keepdims=True)
    h = x * lax.rsqrt(ms + NORM_EPS) * ln_ref[...]
    h_ref[...] = h
    qq_ref[...] = _mm(h.astype(BF16), wq_ref[...])


def _mix_out(y, bonus, g, a_out, x, lp):
    n = x.shape[0]
    tm = min(n, 256)
    nq = PEER_HEADS * PK_DIM
    rows = lambda w: pl.BlockSpec((tm, w), lambda i: (i, 0))
    full = lambda s: pl.BlockSpec(s, lambda i: (0, 0))
    return pl.pallas_call(
        _mix_out_kernel,
        grid=(n // tm,),
        in_specs=[rows(R_DIM), rows(R_DIM), rows(R_DIM), rows(A_DIM), rows(D_MODEL),
                  full((1, R_DIM)), full((1, R_DIM)), full((R_DIM, R_DIM)),
                  full((D_MODEL, D_MODEL)), full((1, D_MODEL)), full((D_MODEL, nq))],
        out_specs=[rows(D_MODEL), rows(D_MODEL), rows(nq)],
        out_shape=[jax.ShapeDtypeStruct((n, D_MODEL), F32), jax.ShapeDtypeStruct((n, D_MODEL), F32),
                   jax.ShapeDtypeStruct((n, nq), F32)],
        compiler_params=_params(("parallel",)),
        name="mix_out_query",
    )(y, bonus, g, a_out, x, lp["gn_g"].reshape(1, -1), lp["gn_b"].reshape(1, -1), lp["e64"],
      lp["w_out"], lp["ln2_g"].reshape(1, -1), lp["w_pq"])


def _pick_rounds(s_ref, n_rows, emit):
    tb = s_ref.shape[1]
    rowid = lax.broadcasted_iota(I32, (n_rows, tb), 0)
    for rnd in range(PEER_TOPK):
        s = s_ref[...]
        m = jnp.max(s, axis=0, keepdims=True)
        idx = jnp.min(jnp.where(s == m, rowid, n_rows), axis=0, keepdims=True)
        hit = rowid == idx
        s_ref[...] = jnp.where(hit, -jnp.inf, s)
        emit(rnd, m, idx, hit)


def _topk_kernel(qq_ref, keys_ref, ei_ref, gate_ref, s_ref, sv_ref, si_ref, c_ref, ci_ref, ts_ref):
    tb = qq_ref.shape[0]
    for half in range(2):
        qh = qq_ref[:, half * PK_HALF:(half + 1) * PK_HALF]
        s_ref[...] = _x3(_mm_nt, keys_ref[half], qh)

        def emit1(rnd, m, idx, hit, half=half):
            sv_ref[half, rnd:rnd + 1, :] = m
            si_ref[half, rnd:rnd + 1, :] = idx

        _pick_rounds(s_ref, N_KEYS, emit1)

    row0 = 0
    for a, nb in enumerate(CAND_COLS):
        if nb == 1:
            break
        c_ref[row0:row0 + nb, :] = sv_ref[0, a:a + 1, :] + sv_ref[1, 0:nb, :]
        ci_ref[row0:row0 + nb, :] = si_ref[0, a:a + 1, :] * N_KEYS + si_ref[1, 0:nb, :]
        row0 += nb
    n_one = PEER_TOPK - a
    c_ref[row0:row0 + n_one, :] = sv_ref[0, a:, :] + sv_ref[1, 0:1, :]
    ci_ref[row0:row0 + n_one, :] = si_ref[0, a:, :] * N_KEYS + si_ref[1, 0:1, :]
    row0 += n_one
    c_ref[row0:, :] = jnp.full((N_CAND - row0, tb), -jnp.inf, F32)
    ci_ref[row0:, :] = jnp.zeros((N_CAND - row0, tb), I32)

    def emit2(rnd, m, idx, hit):
        ts_ref[rnd:rnd + 1, :] = m
        ei_ref[0, rnd:rnd + 1, :] = jnp.max(jnp.where(hit, ci_ref[...], -1), axis=0, keepdims=True)

    _pick_rounds(c_ref, N_CAND, emit2)
    ts = ts_ref[...]
    ex = jnp.exp(ts - ts[0:1, :])
    gate_ref[0] = ex / jnp.sum(ex, axis=0, keepdims=True)


def _topk(qq, sub_keys):
    n = qq.shape[0]
    tb = min(n, 512)
    out = pl.BlockSpec((1, PEER_TOPK, tb), lambda i, h: (h, 0, i))
    return pl.pallas_call(
        _topk_kernel,
        grid=(n // tb, PEER_HEADS),
        in_specs=[pl.BlockSpec((tb, PK_DIM), lambda i, h: (i, h)),
                  pl.BlockSpec((2, N_KEYS, PK_HALF), lambda i, h: (0, 0, 0))],
        out_specs=[out, out],
        out_shape=[jax.ShapeDtypeStruct((PEER_HEADS, PEER_TOPK, n), I32),
                   jax.ShapeDtypeStruct((PEER_HEADS, PEER_TOPK, n), F32)],
        scratch_shapes=[pltpu.VMEM((N_KEYS, tb), F32), pltpu.VMEM((2, PEER_TOPK, tb), F32),
                        pltpu.VMEM((2, PEER_TOPK, tb), I32), pltpu.VMEM((N_CAND, tb), F32),
                        pltpu.VMEM((N_CAND, tb), I32), pltpu.VMEM((PEER_TOPK, tb), F32)],
        compiler_params=_params(("parallel", "parallel")),
        name="peer_topk",
    )(qq, sub_keys)


def _expert_gather(tbl_hbm, idx_v, rows_v, sem, tt, g, buf):
    return pltpu.make_async_copy(tbl_hbm.at[idx_v[pl.ds(tt * N_SEL + g * SC_GROUP, SC_GROUP)]],
                                 rows_v.at[buf], sem.at[buf])


def _sc_tok_blk(n):
    return math.gcd(n // SC_WORKERS, SC_TOK_BLK)


def _sc_token_blocks(n, tbl_hbm, idx_v, rows_v, sem, load_block, compute, store_block):
    wid = lax.axis_index("s") * SC_CORES + lax.axis_index("c")
    npw = n // SC_WORKERS
    tb = _sc_tok_blk(n)
    steps = tb * SC_N_GROUPS

    @pl.loop(0, npw // tb)
    def _(blk):
        tok0 = wid * npw + blk * tb
        load_block(tok0)
        for s in range(SC_NBUF - 1):
            _expert_gather(tbl_hbm, idx_v, rows_v, sem, s // SC_N_GROUPS, s % SC_N_GROUPS, s).start()

        @pl.loop(0, steps, step=SC_NBUF)
        def _(s0):
            for b in range(SC_NBUF):
                s = s0 + b
                ahead = s + SC_NBUF - 1

                @pl.when(ahead < steps)
                def _():
                    _expert_gather(tbl_hbm, idx_v, rows_v, sem, ahead // SC_N_GROUPS, ahead % SC_N_GROUPS,
                                   (b + SC_NBUF - 1) % SC_NBUF).start()

                _expert_gather(tbl_hbm, idx_v, rows_v, sem, s // SC_N_GROUPS, s % SC_N_GROUPS, b).wait()
                compute(s // SC_N_GROUPS, s % SC_N_GROUPS, rows_v.at[b])

        store_block(tok0)


def _pair_products(words, other):
    p = plsc.bitcast(plsc.bitcast(words, BF16) * other, I32)
    return plsc.bitcast(p << 16, F32), plsc.bitcast(p & HI_MASK, F32)


def _expert_cost(n):
    elems = n * N_SEL * D_MODEL
    return pl.CostEstimate(flops=2 * elems, transcendentals=0, bytes_accessed=2 * elems)


def _sc_mesh():
    return plsc.VectorSubcoreMesh(core_axis_name="c", subcore_axis_name="s",
                                  num_cores=SC_CORES, num_subcores=SC_SUBCORES)


def _sc_hidden(eidx, h_words, table):
    n = h_words.shape[0]
    tb = _sc_tok_blk(n)

    @functools.partial(
        pl.kernel, out_type=jax.ShapeDtypeStruct((n * N_SEL,), F32), mesh=_sc_mesh(),
        compiler_params=pltpu.CompilerParams(needs_layout_passes=False),
        scratch_types=[pltpu.VMEM((tb * N_SEL,), I32), pltpu.VMEM((tb * SC_ROW_WORDS,), I32),
                       pltpu.VMEM((SC_NBUF, SC_GROUP, SC_ROW_WORDS), I32), pltpu.VMEM((tb * N_SEL,), F32),
                       pltpu.SemaphoreType.DMA((SC_NBUF,))],
        cost_estimate=_expert_cost(n), name="peer_hidden_sc")
    def run(eidx_hbm, h_hbm, tbl_hbm, hid_hbm, idx_v, h_v, rows_v, hid_v, sem):
        lane = lax.iota(I32, SC_LANES)

        def load_block(tok0):
            pltpu.sync_copy(eidx_hbm.at[pl.ds(tok0 * N_SEL, tb * N_SEL)], idx_v)
            pltpu.sync_copy(h_hbm.at[pl.ds(tok0 * SC_ROW_WORDS, tb * SC_ROW_WORDS)], h_v)

        def compute(tt, g, rows):
            zero = tuple(jnp.zeros((SC_LANES,), F32) for _ in range(SC_GROUP))

            @plsc.parallel_loop(0, SC_ROW_WORDS // SC_LANES, unroll=2, carry=zero)
            def accs(j, acc):
                hw = plsc.bitcast(h_v[pl.ds(tt * SC_ROW_WORDS + j * SC_LANES, SC_LANES)], BF16)
                out = []
                for r in range(SC_GROUP):
                    lo, hi = _pair_products(rows[r, pl.ds(j * SC_LANES, SC_LANES)], hw)
                    out.append(acc[r] + (lo + hi))
                return tuple(out)

            tot = jnp.zeros((SC_LANES,), F32)
            for r in range(SC_GROUP):
                tot = jnp.where(lane == r, jnp.sum(accs[r]), tot)
            hid_v[pl.ds(tt * N_SEL + g * SC_GROUP, SC_GROUP)] = tot

        def store_block(tok0):
            pltpu.sync_copy(hid_v, hid_hbm.at[pl.ds(tok0 * N_SEL, tb * N_SEL)])

        _sc_token_blocks(n, tbl_hbm, idx_v, rows_v, sem, load_block, compute, store_block)

    return run(eidx.reshape(-1), h_words.reshape(-1), table).reshape(n, N_SEL)


def _sc_combine(eidx, coef_words, x, table):
    n = x.shape[0]
    tb = _sc_tok_blk(n)

    @functools.partial(
        pl.kernel, out_type=jax.ShapeDtypeStruct((n * D_MODEL,), F32), mesh=_sc_mesh(),
        compiler_params=pltpu.CompilerParams(needs_layout_passes=False),
        scratch_types=[pltpu.VMEM((tb * N_SEL,), I32), pltpu.VMEM((tb * N_SEL,), I32),
                       pltpu.VMEM((SC_NBUF, SC_GROUP, SC_ROW_WORDS), I32), pltpu.VMEM((tb * D_MODEL,), F32),
                       pltpu.SemaphoreType.DMA((SC_NBUF,))],
        cost_estimate=_expert_cost(n), name="peer_combine_sc")
    def run(eidx_hbm, c_hbm, x_hbm, tbl_hbm, out_hbm, idx_v, c_v, rows_v, out_v, sem):
        def load_block(tok0):
            pltpu.sync_copy(eidx_hbm.at[pl.ds(tok0 * N_SEL, tb * N_SEL)], idx_v)
            pltpu.sync_copy(c_hbm.at[pl.ds(tok0 * N_SEL, tb * N_SEL)], c_v)
            pltpu.sync_copy(x_hbm.at[pl.ds(tok0 * D_MODEL, tb * D_MODEL)], out_v)

        def compute(tt, g, rows):
            base = tt * N_SEL + g * SC_GROUP
            cs = [plsc.bitcast(plsc.load_gather(c_v, [jnp.full((SC_LANES,), r, I32) + base]), BF16)
                  for r in range(SC_GROUP)]

            @plsc.parallel_loop(0, SC_ROW_WORDS // SC_LANES, unroll=2)
            def _(j):
                first = pl.ds(tt * D_MODEL + j * SC_LANES, SC_LANES)
                second = pl.ds(tt * D_MODEL + SC_ROW_WORDS + j * SC_LANES, SC_LANES)
                acc_lo = out_v[first]
                acc_hi = out_v[second]
                for r in range(SC_GROUP):
                    lo, hi = _pair_products(rows[r, pl.ds(j * SC_LANES, SC_LANES)], cs[r])
                    acc_lo = acc_lo + lo
                    acc_hi = acc_hi + hi
                out_v[first] = acc_lo
                out_v[second] = acc_hi

        def store_block(tok0):
            pltpu.sync_copy(out_v, out_hbm.at[pl.ds(tok0 * D_MODEL, tb * D_MODEL)])

        _sc_token_blocks(n, tbl_hbm, idx_v, rows_v, sem, load_block, compute, store_block)

    return run(eidx.reshape(-1), coef_words.reshape(-1), x.reshape(-1), table).reshape(n, D_MODEL)


def _gate_act_kernel(hid_ref, gate_ref, o_ref):
    hid = hid_ref[...]
    o_ref[...] = gate_ref[...] * (0.5 * hid * (1.0 + lax.erf(hid * np.float32(np.sqrt(0.5)))))


def _gate_act(hid, gate):
    n = hid.shape[0]
    tm = math.gcd(n, 2048)
    rows = pl.BlockSpec((tm, N_SEL), lambda i: (i, 0))
    return pl.pallas_call(
        _gate_act_kernel, grid=(n // tm,), in_specs=[rows, rows], out_specs=rows,
        out_shape=jax.ShapeDtypeStruct((n, N_SEL), F32), compiler_params=_params(("parallel",)),
        name="peer_gate_act",
    )(hid, gate)


def _t5_bucket(rel):
    nb = NUM_BUCKETS // 2
    max_exact = nb // 2
    ret = jnp.where(rel > 0, nb, 0)
    n = jnp.abs(rel)
    nf = jnp.maximum(n, 1).astype(F32)
    large = max_exact + (jnp.log(nf / max_exact) / math.log(MAX_DISTANCE / max_exact)
                         * (nb - max_exact)).astype(I32)
    large = jnp.minimum(large, nb - 1)
    return ret + jnp.where(n < max_exact, n, large)


def _rel_bias(rel_bias, n_q, n_k, n_before):
    rel = (jnp.arange(n_k)[None, :] - n_before) - jnp.arange(n_q)[:, None]
    return jnp.transpose(rel_bias[_t5_bucket(rel)].astype(F32), (2, 0, 1))


def _state_to_pairs(wkv):
    b = wkv.shape[0]
    s = wkv.reshape(b, N_PAIRS, 2, R_HD, R_HD)
    z = jnp.zeros_like(s[:, :, 0])
    top = jnp.concatenate([s[:, :, 0], z], axis=-1)
    bot = jnp.concatenate([z, s[:, :, 1]], axis=-1)
    return jnp.concatenate([top, bot], axis=-2)


def _pairs_to_state(s_bd):
    b = s_bd.shape[0]
    return jnp.stack([s_bd[:, :, :R_HD, :R_HD], s_bd[:, :, R_HD:, R_HD:]], axis=2).reshape(
        b, R_HEADS, R_HD, R_HD)


def _layer_dense(x, shift0, wkv0, kv_cache, bias, lp):
    b, t, _ = x.shape
    n = b * t
    z = _norm_matmul(x.reshape(n, D_MODEL), lp["ln1_g"].reshape(1, -1), lp["w_in"], 512).reshape(b, t, IN_COLS)
    r, k2, v, kk, bb, lw, g, bonus, qn, kn = _prep(z, shift0, lp)
    v_new = z[:, :, R_PROJ + A_DIM + KV_DIM:]
    if t % CHUNK:
        pad = lambda a: jnp.pad(a, ((0, 0), (0, CHUNK - t % CHUNK), (0, 0)))
        y, s_fin = _chunk_scan(*(pad(a) for a in (r, k2, v, kk, bb, lw)), _state_to_pairs(wkv0))
        y = y[:, :t]
    else:
        y, s_fin = _chunk_scan(r, k2, v, kk, bb, lw, _state_to_pairs(wkv0))
    if kv_cache is None:
        a_out = _swa_prompt(qn, kn, z, bias, lp["sink"])
    else:
        a_out = _swa_sample(qn, kn, v_new, kv_cache[0].reshape(b, -1, KV_DIM),
                            kv_cache[1].reshape(b, -1, KV_DIM), bias, lp["sink"])
    flat = lambda a: a.reshape(n, a.shape[-1])
    x1, h, qq = _mix_out(flat(y), flat(bonus), flat(g), flat(a_out), flat(x), lp)
    eidx, gate = _topk(qq, lp["sub_keys"])
    sel = lambda a: jnp.transpose(a, (2, 0, 1)).reshape(n, N_SEL)
    return ((sel(eidx), sel(gate), h, x1),
            (_pairs_to_state(s_fin), z[:, -1, :R_PROJ],
             kn.reshape(b, t, A_KV, A_HD), v_new.reshape(b, t, A_KV, A_HD)))


def _bf16_pair_words(a):
    k = a.shape[1] // 2
    bits = lax.bitcast_convert_type(a.astype(BF16), jnp.uint16).astype(jnp.uint32)
    return lax.bitcast_convert_type(bits[:, :k] | (bits[:, k:] << 16), I32)


def _layer_experts(ops, lp, after=None):
    eidx, gate, h, x1 = ops
    h_words = _bf16_pair_words(h)
    if after is not None:
        h_words, after = lax.optimization_barrier((h_words, after))
    coef = _gate_act(_sc_hidden(eidx, h_words, lp["peer_u"]), gate)
    coef_words = _bf16_pair_words(jnp.concatenate([coef, coef], axis=1))
    return _sc_combine(eidx, coef_words, x1, lp["peer_v"]), after


def kernel(x_prompt, x_sample, state_rwkv_wkv, state_rwkv_shift, cache_swa_k, cache_swa_v, ln1_g, w_in,
           mu_shift, w0, w2, a0, a2, g2, k_k, k_a, r_k, gn_g, gn_b, q_norm_g, k_norm_g, attn_sink,
           rel_bias, w_out, ln2_g, w_pq, sub_keys, peer_u, peer_v):
    depth = w_in.shape[0]
    b_p, s_p = x_prompt.shape[:2]
    b_s, t_s = x_sample.shape[:2]
    n_cache = cache_swa_k.shape[2]
    n_keep = min(WINDOW, s_p)
    bias_p = _rel_bias(rel_bias, CHUNK, (WIN_CHUNKS + 1) * CHUNK, WIN_CHUNKS * CHUNK)
    bias_s = _rel_bias(rel_bias, t_s, n_cache + t_s, n_cache)
    head_id = jnp.arange(R_DIM) // R_HD
    e64 = (head_id[:, None] == head_id[None, :]).astype(BF16)
    zpad = jnp.zeros((LANES - W_LORA, R_DIM), F32)
    n_groups = PROMPT_GROUPS if b_p % PROMPT_GROUPS == 0 else 1
    xp, xs = jnp.split(x_prompt, n_groups, axis=0), x_sample
    outs = [[] for _ in range(8)]
    for l in range(depth):
        lp = {
            "ln1_g": ln1_g[l], "w_in": w_in[l].astype(BF16), "mu": mu_shift[l], "w0": w0[l], "a0": a0[l],
            "k_k": k_k[l], "k_a": k_a[l], "r_k": r_k[l].reshape(-1), "gn_g": gn_g[l], "gn_b": gn_b[l],
            "w2p": jnp.concatenate([w2[l], zpad], axis=0).astype(BF16),
            "a2p": jnp.concatenate([zpad, a2[l]], axis=0).astype(BF16),
            "g2": g2[l].astype(BF16), "e64": e64,
            "q_gain": jnp.tile(q_norm_g[l], A_HEADS), "k_gain": jnp.tile(k_norm_g[l], A_KV),
            "sink": attn_sink[l].astype(F32), "w_out": w_out[l].astype(BF16), "ln2_g": ln2_g[l],
            "w_pq": w_pq[l].astype(BF16), "sub_keys": sub_keys[l], "peer_u": _bf16_pair_words(peer_u[l]), "peer_v": _bf16_pair_words(peer_v[l]),
        }
        parts = []
        for xg in xp:
            if parts:
                (eidx, *rest), states = parts[-1]
                xg, eidx = lax.optimization_barrier((xg, eidx))
                parts[-1] = ((eidx, *rest), states)
            parts.append(_layer_dense(xg, jnp.zeros((xg.shape[0], R_PROJ), F32),
                                      jnp.zeros((xg.shape[0], R_HEADS, R_HD, R_HD), F32), None, bias_p, lp))
        wkv_p, sh_p, k_p, v_p = (jnp.concatenate([pt[1][i] for pt in parts], axis=0) for i in range(4))
        outs_x = []
        for ops, _ in parts:
            if outs_x:
                xo, outs_x[-1] = _layer_experts(ops, lp, after=outs_x[-1])
            else:
                xo, _ = _layer_experts(ops, lp)
            outs_x.append(xo)
        xp = [xo.reshape(xg.shape) for xo, xg in zip(outs_x, xp)]
        ops_s, (wkv_s, sh_s, k_s, v_s) = _layer_dense(
            xs, state_rwkv_shift[l], state_rwkv_wkv[l].astype(F32), (cache_swa_k[l], cache_swa_v[l]),
            bias_s, lp)
        xs = _layer_experts(ops_s, lp)[0].reshape(xs.shape)
        for lst, val in zip(outs, (wkv_p, sh_p, k_p[:, s_p - n_keep:], v_p[:, s_p - n_keep:],
                                   wkv_s, sh_s, k_s, v_s)):
            lst.append(val)
    return (jnp.concatenate(xp, axis=0), xs) + tuple(jnp.stack(o) for o in outs)
```

```python
import functools
import math

import numpy as np
import jax
import jax.numpy as jnp
from jax import lax
from jax.experimental import pallas as pl
from jax.experimental.pallas import tpu as pltpu
from jax.experimental.pallas import tpu_sc as plsc

F32 = jnp.float32
BF16 = jnp.bfloat16
I32 = jnp.int32

D_MODEL = 1024
CHUNK = 64
R_HEADS = 8
R_HD = 64
R_DIM = R_HEADS * R_HD
W_LORA = 64
A_LORA = 64
G_LORA = 128
R_PROJ = 3 * R_DIM + W_LORA + A_LORA + G_LORA
A_HEADS = 8
A_KV = 2
A_GROUP = A_HEADS // A_KV
A_HD = 64
A_DIM = A_HEADS * A_HD
KV_DIM = A_KV * A_HD
IN_COLS = R_PROJ + A_DIM + 2 * KV_DIM
WINDOW = 128
WIN_CHUNKS = WINDOW // CHUNK
NUM_BUCKETS = 32
MAX_DISTANCE = 128
PEER_HEADS = 8
N_KEYS = 128
PK_DIM = 256
PK_HALF = PK_DIM // 2
PEER_TOPK = 16
N_SEL = PEER_HEADS * PEER_TOPK
NORM_EPS = 1e-6
GN_EPS = 64e-5
NEG_INF = -1e30

LANES = 128
N_PAIRS = R_DIM // LANES
VMEM_LIMIT = 48 * 1024 * 1024
CAND_COLS = (16, 8, 8, 4, 4, 4, 4, 4, 1, 1, 1, 1, 1, 1, 1, 1)
assert all(nb >= PEER_TOPK // (a + 1) for a, nb in enumerate(CAND_COLS))
N_CAND = 64
assert sum(CAND_COLS) <= N_CAND
SC_CORES = 2
SC_SUBCORES = 16
SC_LANES = 16
SC_WORKERS = SC_CORES * SC_SUBCORES
SC_TOK_BLK = 64
SC_GROUP = SC_LANES
SC_N_GROUPS = N_SEL // SC_GROUP
SC_NBUF = 4
SC_ROW_WORDS = D_MODEL // 2
HI_MASK = -65536
PROMPT_GROUPS = 8


def _params(sem):
    return pltpu.CompilerParams(dimension_semantics=sem, vmem_limit_bytes=VMEM_LIMIT)


def _mm(a, b):
    return jnp.dot(a, b, preferred_element_type=F32)


def _mm_nt(a, b):
    return lax.dot_general(a, b, (((1,), (1,)), ((), ())), preferred_element_type=F32)


def _mm_tn(a, b):
    return lax.dot_general(a, b, (((0,), (0,)), ((), ())), preferred_element_type=F32)


def _split2(a):
    hi = a.astype(BF16)
    return hi, (a - hi.astype(F32)).astype(BF16)


def _split3(a):
    hi = a.astype(BF16)
    r = a - hi.astype(F32)
    mid = r.astype(BF16)
    return hi, mid, (r - mid.astype(F32)).astype(BF16)


def _x3(mm, a, b):
    ah, al = _split2(a)
    bh, bl = _split2(b)
    return mm(ah, bh) + mm(ah, bl) + mm(al, bh)


def _exact_lhs(mm, a_bf16, b):
    b0, b1, b2 = _split3(b)
    return mm(a_bf16, b0) + mm(a_bf16, b1) + mm(a_bf16, b2)


def _seg_sum(x, e_bf16):
    x0, x1, x2 = _split3(x)
    return _mm(x0, e_bf16) + _mm(x1, e_bf16) + _mm(x2, e_bf16)


def _sigmoid(x):
    return 1.0 / (1.0 + jnp.exp(-x))


def _norm_matmul_kernel(x_ref, g_ref, w_ref, o_ref, h_ref):
    @pl.when(pl.program_id(1) == 0)
    def _():
        x = x_ref[...]
        ms = jnp.mean(x * x, axis=-1, keepdims=True)
        h_ref[...] = (x * lax.rsqrt(ms + NORM_EPS) * g_ref[...]).astype(BF16)

    o_ref[...] = _mm(h_ref[...], w_ref[...])


def _norm_matmul(x, g, w_bf16, tn):
    n, k = x.shape
    m = w_bf16.shape[1]
    tm = min(n, 512)
    return pl.pallas_call(
        _norm_matmul_kernel,
        grid=(n // tm, m // tn),
        in_specs=[pl.BlockSpec((tm, k), lambda i, j: (i, 0)),
                  pl.BlockSpec((1, k), lambda i, j: (0, 0)),
                  pl.BlockSpec((k, tn), lambda i, j: (0, j))],
        out_specs=pl.BlockSpec((tm, tn), lambda i, j: (i, j)),
        out_shape=jax.ShapeDtypeStruct((n, m), F32),
        scratch_shapes=[pltpu.VMEM((tm, k), BF16)],
        compiler_params=_params(("parallel", "arbitrary")),
        name="norm_inproj",
    )(x, g, w_bf16)


def _prep_kernel(z_ref, zp_ref, sh_ref, mu_ref, w0_ref, a0_ref, kk_ref, ka_ref, rk_ref,
                 w2_ref, a2_ref, g2_ref, e_ref, qg_ref, kg_ref,
                 r_o, k_o, v_o, kk_o, b_o, lw_o, g_o, bo_o, qn_o, kn_o):
    i = pl.program_id(1)
    zt = z_ref[0]
    tp = zt.shape[0]
    zr = zt[:, :R_PROJ]
    prev_row = jnp.where(i == 0, sh_ref[0], zp_ref[0][7:8, :R_PROJ])
    row = lax.broadcasted_iota(I32, (tp, 1), 0)
    prev = jnp.where(row == 0, prev_row, pltpu.roll(zr, 1, axis=0))
    zs = zr + (prev - zr) * mu_ref[...]
    r = zs[:, 0:R_DIM]
    k = zs[:, R_DIM:2 * R_DIM]
    v = zs[:, 2 * R_DIM:3 * R_DIM]
    lo = zs[:, 3 * R_DIM:3 * R_DIM + W_LORA + A_LORA]
    g_lo = zs[:, 3 * R_DIM + W_LORA + A_LORA:R_PROJ]
    e = e_ref[...]
    w_in = -(w0_ref[...] + _mm(jnp.tanh(lo).astype(BF16), w2_ref[...]))
    softplus = jnp.maximum(w_in, 0.0) + jnp.log1p(jnp.exp(-jnp.abs(w_in)))
    w_log = -softplus - 0.5
    lw_o[0] = -jnp.exp(w_log)
    a = _sigmoid(a0_ref[...] + _mm(lo.astype(BF16), a2_ref[...]))
    g_o[0] = _mm(_sigmoid(g_lo).astype(BF16), g2_ref[...])
    kk = k * kk_ref[...]
    kk = kk / jnp.maximum(jnp.sqrt(_seg_sum(kk * kk, e)), 1e-12)
    k2 = k * (1.0 + (a - 1.0) * ka_ref[...])
    r_o[0] = r
    k_o[0] = k2
    v_o[0] = v
    kk_o[0] = kk
    b_o[0] = kk * a
    bo_o[0] = _seg_sum(r * k2 * rk_ref[...], e) * v
    q = zt[:, R_PROJ:R_PROJ + A_DIM]
    qn_o[0] = q * lax.rsqrt(_seg_sum(q * q, e) * (1.0 / A_HD) + NORM_EPS) * qg_ref[...]
    kx = zt[:, R_PROJ + A_DIM:R_PROJ + A_DIM + KV_DIM]
    e_kv = e[:KV_DIM, :KV_DIM]
    kn_o[0] = kx * lax.rsqrt(_seg_sum(kx * kx, e_kv) * (1.0 / A_HD) + NORM_EPS) * kg_ref[...]


def _prep(z, shift0, lp):
    b, t, _ = z.shape
    tp = min(t, 256)
    row = lambda a: a.reshape(1, -1)
    vec = lambda n: pl.BlockSpec((1, n), lambda bi, i: (0, 0))
    full = lambda s: pl.BlockSpec(s, lambda bi, i: (0, 0))
    wide = pl.BlockSpec((1, tp, R_DIM), lambda bi, i: (bi, i, 0))
    outs = [jax.ShapeDtypeStruct((b, t, R_DIM), F32)] * 9 + [jax.ShapeDtypeStruct((b, t, KV_DIM), F32)]
    return pl.pallas_call(
        _prep_kernel,
        grid=(b, t // tp),
        in_specs=[pl.BlockSpec((1, tp, IN_COLS), lambda bi, i: (bi, i, 0)),
                  pl.BlockSpec((1, 8, IN_COLS), lambda bi, i: (bi, jnp.maximum(i * (tp // 8) - 1, 0), 0)),
                  pl.BlockSpec((1, 1, R_PROJ), lambda bi, i: (bi, 0, 0)),
                  vec(R_PROJ), vec(R_DIM), vec(R_DIM), vec(R_DIM), vec(R_DIM), vec(R_DIM),
                  full((LANES, R_DIM)), full((LANES, R_DIM)), full((G_LORA, R_DIM)),
                  full((R_DIM, R_DIM)), vec(A_DIM), vec(KV_DIM)],
        out_specs=[wide] * 9 + [pl.BlockSpec((1, tp, KV_DIM), lambda bi, i: (bi, i, 0))],
        out_shape=outs,
        compiler_params=_params(("parallel", "parallel")),
        name="rwkv_prep",
    )(z, z, shift0.reshape(b, 1, R_PROJ), row(lp["mu"]), row(lp["w0"]), row(lp["a0"]), row(lp["k_k"]),
      row(lp["k_a"]), row(lp["r_k"]), lp["w2p"], lp["a2p"], lp["g2"], lp["e64"], row(lp["q_gain"]),
      row(lp["k_gain"]))


def _stack_heads(x, lo_mask):
    return jnp.concatenate([jnp.where(lo_mask, x, 0.0), jnp.where(lo_mask, 0.0, x)], axis=0)


def _chunk_kernel(r_ref, k_ref, v_ref, kk_ref, b_ref, lw_ref, s0_ref, y_ref, sf_ref, s_ref):
    c = pl.program_id(1)
    L = CHUNK

    @pl.when(c == 0)
    def _():
        s_ref[...] = s0_ref[0]

    lane = lax.broadcasted_iota(I32, (1, LANES), 1)
    lo_mask = lane < R_HD
    rr = lax.broadcasted_iota(I32, (L, 2 * L), 0)
    cc = lax.broadcasted_iota(I32, (L, 2 * L), 1)
    cc = jnp.where(cc >= L, cc - L, cc)
    strict = rr > cc
    incl = rr >= cc
    t_r = lax.broadcasted_iota(I32, (L, L), 0)
    t_c = lax.broadcasted_iota(I32, (L, L), 1)
    tri = (t_r >= t_c).astype(BF16)
    col2 = lax.broadcasted_iota(I32, (L, 2 * L), 1) < L
    eye_r = lax.broadcasted_iota(I32, (2 * L, 2 * L), 0)
    eye_c = lax.broadcasted_iota(I32, (2 * L, 2 * L), 1)
    eye = (eye_r == eye_c).astype(F32)

    pairs = range(N_PAIRS)
    sls = [slice(p * LANES, (p + 1) * LANES) for p in pairs]
    r = [r_ref[0][:, sl] for sl in sls]
    k = [k_ref[0][:, sl] for sl in sls]
    v = [v_ref[0][:, sl] for sl in sls]
    kk = [kk_ref[0][:, sl] for sl in sls]
    bb = [b_ref[0][:, sl] for sl in sls]
    lw = [lw_ref[0][:, sl] for sl in sls]
    s0 = [s_ref[p] for p in pairs]

    cl = [_exact_lhs(_mm, tri, lw[p]) for p in pairs]
    cl_last = [cl[p][L - 1:L, :] for p in pairs]
    e_neg = [jnp.exp(-cl[p]) for p in pairs]
    e_last = [jnp.exp(cl_last[p] - cl[p]) for p in pairs]
    rt = [r[p] * jnp.exp(cl[p]) for p in pairs]
    at = [kk[p] * jnp.exp(cl[p] - lw[p]) for p in pairs]

    gm = [_x3(_mm_nt, jnp.concatenate([at[p], rt[p]], axis=0),
              jnp.concatenate([_stack_heads(k[p] * e_neg[p], lo_mask), _stack_heads(bb[p] * e_neg[p], lo_mask)],
                              axis=0)) for p in pairs]
    mk = [jnp.where(strict, gm[p][:L, :2 * L], 0.0) for p in pairs]
    mb = [jnp.where(strict, gm[p][:L, 2 * L:], 0.0) for p in pairs]
    hk = [jnp.where(incl, gm[p][L:, :2 * L], 0.0) for p in pairs]
    hb = [jnp.where(incl, gm[p][L:, 2 * L:], 0.0) for p in pairs]

    nil = [-jnp.concatenate([jnp.where(col2, mb[p], 0.0), jnp.where(col2, 0.0, mb[p])], axis=0) for p in pairs]
    tinv = [eye + nil[p] for p in pairs]
    nil = [_x3(_mm, nil[p], nil[p]) for p in pairs]
    for it in range(5):
        tinv = [tinv[p] + _x3(_mm, nil[p], tinv[p]) for p in pairs]
        if it < 4:
            nil = [_x3(_mm, nil[p], nil[p]) for p in pairs]

    vs = [_stack_heads(v[p], lo_mask) for p in pairs]
    rhs_u = [-(_x3(_mm_nt, at[p], s0[p]) + _x3(_mm, mk[p], vs[p])) for p in pairs]
    us = [_x3(_mm, tinv[p], _stack_heads(rhs_u[p], lo_mask)) for p in pairs]
    for p in pairs:
        y_ref[0, :, sls[p]] = _x3(_mm_nt, rt[p], s0[p]) + _x3(_mm, hk[p], vs[p]) + _x3(_mm, hb[p], us[p])
    for p in pairs:
        s_ref[p] = (s0[p] * jnp.exp(cl_last[p])
                    + _x3(_mm_tn, vs[p], _stack_heads(k[p] * e_last[p], lo_mask))
                    + _x3(_mm_tn, us[p], _stack_heads(bb[p] * e_last[p], lo_mask)))

    @pl.when(c == pl.num_programs(1) - 1)
    def _():
        sf_ref[0] = s_ref[...]


def _chunk_scan(r, k, v, kk, bb, lw, s0_bd):
    b, t, _ = r.shape
    wide = pl.BlockSpec((1, CHUNK, R_DIM), lambda bi, c: (bi, c, 0))
    st = pl.BlockSpec((1, N_PAIRS, LANES, LANES), lambda bi, c: (bi, 0, 0, 0))
    return pl.pallas_call(
        _chunk_kernel,
        grid=(b, t // CHUNK),
        in_specs=[wide] * 6 + [st],
        out_specs=[wide, st],
        out_shape=[jax.ShapeDtypeStruct((b, t, R_DIM), F32),
                   jax.ShapeDtypeStruct((b, N_PAIRS, LANES, LANES), F32)],
        scratch_shapes=[pltpu.VMEM((N_PAIRS, LANES, LANES), F32)],
        compiler_params=_params(("parallel", "arbitrary")),
        name="rwkv_chunk",
    )(r, k, v, kk, bb, lw, s0_bd)


def _swa_kernel(*refs, n_seg, banded):
    q_ref = refs[0]
    k_refs = refs[1:1 + n_seg]
    v_refs = refs[1 + n_seg:1 + 2 * n_seg]
    bias_ref, sink_ref, o_ref = refs[1 + 2 * n_seg:]
    c = pl.program_id(1)
    q = q_ref[0]
    kcat = jnp.concatenate([kr[0] for kr in k_refs], axis=0)
    vcat = jnp.concatenate([vr[0] for vr in v_refs], axis=0)
    n_k = kcat.shape[0]
    lane = lax.broadcasted_iota(I32, (1, LANES), 1)
    lo_mask = lane < A_HD
    k_sw = pltpu.roll(kcat, A_HD, axis=1)
    v_sw = pltpu.roll(vcat, A_HD, axis=1)
    k_dup = [jnp.where(lo_mask, kcat, k_sw).astype(BF16), jnp.where(lo_mask, k_sw, kcat).astype(BF16)]
    v_dup = [jnp.where(lo_mask, vcat, v_sw).astype(BF16), jnp.where(lo_mask, v_sw, vcat).astype(BF16)]
    if banded:
        key_chunk = c - WIN_CHUNKS + lax.broadcasted_iota(I32, (1, n_k), 1) // CHUNK
        valid = key_chunk >= 0
    n_q = q.shape[0]
    hi_mask = jnp.logical_not(lo_mask)
    for kvh in range(A_KV):
        heads = range(kvh * A_GROUP, (kvh + 1) * A_GROUP)
        qs = jnp.concatenate(
            [jnp.where(lo_mask if hq % 2 == 0 else hi_mask, q[:, (hq // 2) * LANES:(hq // 2 + 1) * LANES], 0.0)
             for hq in heads], axis=0).astype(BF16)
        bias = jnp.concatenate([bias_ref[hq] for hq in heads], axis=0)
        sink = jnp.concatenate([jnp.broadcast_to(sink_ref[0:1, hq:hq + 1], (n_q, 1)) for hq in heads], axis=0)
        s = _mm_nt(qs, k_dup[kvh]) * (A_HD ** -0.5) + bias
        if banded:
            s = jnp.where(valid, s, NEG_INF)
        m = jnp.maximum(jnp.max(s, axis=-1, keepdims=True), sink)
        pr = jnp.exp(s - m)
        den = jnp.sum(pr, axis=-1, keepdims=True) + jnp.exp(sink - m)
        o = _mm((pr / den).astype(BF16), v_dup[kvh])
        for i in range(A_GROUP // 2):
            pair = kvh * (A_GROUP // 2) + i
            o_ref[0, :, pair * LANES:(pair + 1) * LANES] = jnp.where(
                lo_mask, o[2 * i * n_q:(2 * i + 1) * n_q], o[(2 * i + 1) * n_q:(2 * i + 2) * n_q])


def _swa_prompt(qn, kn, z, bias, sink):
    b, t, _ = qn.shape
    v_col = (R_PROJ + A_DIM + KV_DIM) // KV_DIM
    seg = lambda s, col: pl.BlockSpec(
        (1, CHUNK, KV_DIM), lambda bi, c: (bi, jnp.maximum(c - WIN_CHUNKS + s, 0), col))
    n_seg = WIN_CHUNKS + 1
    return pl.pallas_call(
        functools.partial(_swa_kernel, n_seg=n_seg, banded=True),
        grid=(b, t // CHUNK),
        in_specs=[pl.BlockSpec((1, CHUNK, A_DIM), lambda bi, c: (bi, c, 0))]
                 + [seg(s, 0) for s in range(n_seg)] + [seg(s, v_col) for s in range(n_seg)]
                 + [pl.BlockSpec(bias.shape, lambda bi, c: (0, 0, 0)),
                    pl.BlockSpec((1, A_HEADS), lambda bi, c: (0, 0))],
        out_specs=pl.BlockSpec((1, CHUNK, A_DIM), lambda bi, c: (bi, c, 0)),
        out_shape=jax.ShapeDtypeStruct((b, t, A_DIM), F32),
        compiler_params=_params(("parallel", "parallel")),
        name="swa_prompt",
    )(qn, *([kn] * n_seg), *([z] * n_seg), bias, sink.reshape(1, A_HEADS))


def _swa_sample(qn, kn, v_new, k_cache, v_cache, bias, sink):
    b, t, _ = qn.shape
    n_cache = k_cache.shape[1]
    cur = lambda w: pl.BlockSpec((1, t, w), lambda bi, c: (bi, 0, 0))
    old = pl.BlockSpec((1, n_cache, KV_DIM), lambda bi, c: (bi, 0, 0))
    return pl.pallas_call(
        functools.partial(_swa_kernel, n_seg=2, banded=False),
        grid=(b, 1),
        in_specs=[cur(A_DIM), old, cur(KV_DIM), old, cur(KV_DIM),
                  pl.BlockSpec(bias.shape, lambda bi, c: (0, 0, 0)),
                  pl.BlockSpec((1, A_HEADS), lambda bi, c: (0, 0))],
        out_specs=cur(A_DIM),
        out_shape=jax.ShapeDtypeStruct((b, t, A_DIM), F32),
        compiler_params=_params(("parallel", "parallel")),
        name="swa_sample",
    )(qn, k_cache, kn, v_cache, v_new, bias, sink.reshape(1, A_HEADS))


def _mix_out_kernel(y_ref, bo_ref, g_ref, a_ref, x_ref, gng_ref, gnb_ref, e_ref, wo_ref, ln_ref, wq_ref,
                    xo_ref, h_ref, qq_ref):
    e = e_ref[...]
    y = y_ref[...]
    mu = _seg_sum(y, e) * (1.0 / R_HD)
    d = y - mu
    var = _seg_sum(d * d, e) * (1.0 / R_HD)
    yn = d * lax.rsqrt(var + GN_EPS) * gng_ref[...] + gnb_ref[...]
    mix_r = ((yn + bo_ref[...]) * g_ref[...]).astype(BF16)
    x = (x_ref[...] + _mm(mix_r, wo_ref[:R_DIM, :]) + _mm(a_ref[...].astype(BF16), wo_ref[R_DIM:, :]))
    xo_ref[...] = x
    ms = jnp.mean(x * x, axis=-1, keepdims=True)
    h = x * lax.rsqrt(ms + NORM_EPS) * ln_ref[...]
    h_ref[...] = h
    qq_ref[...] = _mm(h.astype(BF16), wq_ref[...])


def _mix_out(y, bonus, g, a_out, x, lp):
    n = x.shape[0]
    tm = min(n, 256)
    nq = PEER_HEADS * PK_DIM
    rows = lambda w: pl.BlockSpec((tm, w), lambda i: (i, 0))
    full = lambda s: pl.BlockSpec(s, lambda i: (0, 0))
    return pl.pallas_call(
        _mix_out_kernel,
        grid=(n // tm,),
        in_specs=[rows(R_DIM), rows(R_DIM), rows(R_DIM), rows(A_DIM), rows(D_MODEL),
                  full((1, R_DIM)), full((1, R_DIM)), full((R_DIM, R_DIM)),
                  full((D_MODEL, D_MODEL)), full((1, D_MODEL)), full((D_MODEL, nq))],
        out_specs=[rows(D_MODEL), rows(D_MODEL), rows(nq)],
        out_shape=[jax.ShapeDtypeStruct((n, D_MODEL), F32), jax.ShapeDtypeStruct((n, D_MODEL), F32),
                   jax.ShapeDtypeStruct((n, nq), F32)],
        compiler_params=_params(("parallel",)),
        name="mix_out_query",
    )(y, bonus, g, a_out, x, lp["gn_g"].reshape(1, -1), lp["gn_b"].reshape(1, -1), lp["e64"],
      lp["w_out"], lp["ln2_g"].reshape(1, -1), lp["w_pq"])


def _pick_rounds(s_ref, n_rows, emit):
    tb = s_ref.shape[1]
    rowid = lax.broadcasted_iota(I32, (n_rows, tb), 0)
    for rnd in range(PEER_TOPK):
        s = s_ref[...]
        m = jnp.max(s, axis=0, keepdims=True)
        idx = jnp.min(jnp.where(s == m, rowid, n_rows), axis=0, keepdims=True)
        hit = rowid == idx
        s_ref[...] = jnp.where(hit, -jnp.inf, s)
        emit(rnd, m, idx, hit)


def _topk_kernel(qq_ref, keys_ref, ei_ref, gate_ref, s_ref, sv_ref, si_ref, c_ref, ci_ref, ts_ref):
    tb = qq_ref.shape[0]
    for half in range(2):
        qh = qq_ref[:, half * PK_HALF:(half + 1) * PK_HALF]
        s_ref[...] = _x3(_mm_nt, keys_ref[half], qh)

        def emit1(rnd, m, idx, hit, half=half):
            sv_ref[half, rnd:rnd + 1, :] = m
            si_ref[half, rnd:rnd + 1, :] = idx

        _pick_rounds(s_ref, N_KEYS, emit1)

    row0 = 0
    for a, nb in enumerate(CAND_COLS):
        if nb == 1:
            break
        c_ref[row0:row0 + nb, :] = sv_ref[0, a:a + 1, :] + sv_ref[1, 0:nb, :]
        ci_ref[row0:row0 + nb, :] = si_ref[0, a:a + 1, :] * N_KEYS + si_ref[1, 0:nb, :]
        row0 += nb
    n_one = PEER_TOPK - a
    c_ref[row0:row0 + n_one, :] = sv_ref[0, a:, :] + sv_ref[1, 0:1, :]
    ci_ref[row0:row0 + n_one, :] = si_ref[0, a:, :] * N_KEYS + si_ref[1, 0:1, :]
    row0 += n_one
    c_ref[row0:, :] = jnp.full((N_CAND - row0, tb), -jnp.inf, F32)
    ci_ref[row0:, :] = jnp.zeros((N_CAND - row0, tb), I32)

    def emit2(rnd, m, idx, hit):
        ts_ref[rnd:rnd + 1, :] = m
        ei_ref[0, rnd:rnd + 1, :] = jnp.max(jnp.where(hit, ci_ref[...], -1), axis=0, keepdims=True)

    _pick_rounds(c_ref, N_CAND, emit2)
    ts = ts_ref[...]
    ex = jnp.exp(ts - ts[0:1, :])
    gate_ref[0] = ex / jnp.sum(ex, axis=0, keepdims=True)


def _topk(qq, sub_keys):
    n = qq.shape[0]
    tb = min(n, 512)
    out = pl.BlockSpec((1, PEER_TOPK, tb), lambda i, h: (h, 0, i))
    return pl.pallas_call(
        _topk_kernel,
        grid=(n // tb, PEER_HEADS),
        in_specs=[pl.BlockSpec((tb, PK_DIM), lambda i, h: (i, h)),
                  pl.BlockSpec((2, N_KEYS, PK_HALF), lambda i, h: (0, 0, 0))],
        out_specs=[out, out],
        out_shape=[jax.ShapeDtypeStruct((PEER_HEADS, PEER_TOPK, n), I32),
                   jax.ShapeDtypeStruct((PEER_HEADS, PEER_TOPK, n), F32)],
        scratch_shapes=[pltpu.VMEM((N_KEYS, tb), F32), pltpu.VMEM((2, PEER_TOPK, tb), F32),
                        pltpu.VMEM((2, PEER_TOPK, tb), I32), pltpu.VMEM((N_CAND, tb), F32),
                        pltpu.VMEM((N_CAND, tb), I32), pltpu.VMEM((PEER_TOPK, tb), F32)],
        compiler_params=_params(("parallel", "parallel")),
        name="peer_topk",
    )(qq, sub_keys)


def _expert_gather(tbl_hbm, idx_v, rows_v, sem, tt, g, buf):
    return pltpu.make_async_copy(tbl_hbm.at[idx_v[pl.ds(tt * N_SEL + g * SC_GROUP, SC_GROUP)]],
                                 rows_v.at[buf], sem.at[buf])


def _sc_tok_blk(n):
    return math.gcd(n // SC_WORKERS, SC_TOK_BLK)


def _sc_token_blocks(n, tbl_hbm, idx_v, rows_v, sem, load_block, compute, store_block):
    wid = lax.axis_index("s") * SC_CORES + lax.axis_index("c")
    npw = n // SC_WORKERS
    tb = _sc_tok_blk(n)
    steps = tb * SC_N_GROUPS

    @pl.loop(0, npw // tb)
    def _(blk):
        tok0 = wid * npw + blk * tb
        load_block(tok0)
        for s in range(SC_NBUF - 1):
            _expert_gather(tbl_hbm, idx_v, rows_v, sem, s // SC_N_GROUPS, s % SC_N_GROUPS, s).start()

        @pl.loop(0, steps, step=SC_NBUF)
        def _(s0):
            for b in range(SC_NBUF):
                s = s0 + b
                ahead = s + SC_NBUF - 1

                @pl.when(ahead < steps)
                def _():
                    _expert_gather(tbl_hbm, idx_v, rows_v, sem, ahead // SC_N_GROUPS, ahead % SC_N_GROUPS,
                                   (b + SC_NBUF - 1) % SC_NBUF).start()

                _expert_gather(tbl_hbm, idx_v, rows_v, sem, s // SC_N_GROUPS, s % SC_N_GROUPS, b).wait()
                compute(s // SC_N_GROUPS, s % SC_N_GROUPS, rows_v.at[b])

        store_block(tok0)


def _pair_products(words_a, other_a, words_b, other_b):
    p = plsc.bitcast(words_a, BF16) * other_a + plsc.bitcast(words_b, BF16) * other_b
    p = plsc.bitcast(p, I32)
    return plsc.bitcast(p << 16, F32), plsc.bitcast(p & HI_MASK, F32)


def _expert_cost(n):
    elems = n * N_SEL * D_MODEL
    return pl.CostEstimate(flops=2 * elems, transcendentals=0, bytes_accessed=2 * elems)


def _sc_mesh():
    return plsc.VectorSubcoreMesh(core_axis_name="c", subcore_axis_name="s",
                                  num_cores=SC_CORES, num_subcores=SC_SUBCORES)


def _sc_hidden(eidx, h_words, table):
    n = h_words.shape[0]
    tb = _sc_tok_blk(n)

    @functools.partial(
        pl.kernel, out_type=jax.ShapeDtypeStruct((n * N_SEL,), F32), mesh=_sc_mesh(),
        compiler_params=pltpu.CompilerParams(needs_layout_passes=False),
        scratch_types=[pltpu.VMEM((tb * N_SEL,), I32), pltpu.VMEM((tb * SC_ROW_WORDS,), I32),
                       pltpu.VMEM((SC_NBUF, SC_GROUP, SC_ROW_WORDS), I32), pltpu.VMEM((tb * N_SEL,), F32),
                       pltpu.SemaphoreType.DMA((SC_NBUF,))],
        cost_estimate=_expert_cost(n), name="peer_hidden_sc")
    def run(eidx_hbm, h_hbm, tbl_hbm, hid_hbm, idx_v, h_v, rows_v, hid_v, sem):
        lane = lax.iota(I32, SC_LANES)

        def load_block(tok0):
            pltpu.sync_copy(eidx_hbm.at[pl.ds(tok0 * N_SEL, tb * N_SEL)], idx_v)
            pltpu.sync_copy(h_hbm.at[pl.ds(tok0 * SC_ROW_WORDS, tb * SC_ROW_WORDS)], h_v)

        def compute(tt, g, rows):
            zero = tuple(jnp.zeros((SC_LANES,), F32) for _ in range(SC_GROUP))

            @plsc.parallel_loop(0, SC_ROW_WORDS // SC_LANES, step=2, carry=zero)
            def accs(j, acc):
                at = lambda jj: pl.ds(jj * SC_LANES, SC_LANES)
                hw0 = plsc.bitcast(h_v[pl.ds(tt * SC_ROW_WORDS + j * SC_LANES, SC_LANES)], BF16)
                hw1 = plsc.bitcast(h_v[pl.ds(tt * SC_ROW_WORDS + (j + 1) * SC_LANES, SC_LANES)], BF16)
                out = []
                for r in range(SC_GROUP):
                    lo, hi = _pair_products(rows[r, at(j)], hw0, rows[r, at(j + 1)], hw1)
                    out.append(acc[r] + (lo + hi))
                return tuple(out)

            tot = jnp.zeros((SC_LANES,), F32)
            for r in range(SC_GROUP):
                tot = jnp.where(lane == r, jnp.sum(accs[r]), tot)
            hid_v[pl.ds(tt * N_SEL + g * SC_GROUP, SC_GROUP)] = tot

        def store_block(tok0):
            pltpu.sync_copy(hid_v, hid_hbm.at[pl.ds(tok0 * N_SEL, tb * N_SEL)])

        _sc_token_blocks(n, tbl_hbm, idx_v, rows_v, sem, load_block, compute, store_block)

    return run(eidx.reshape(-1), h_words.reshape(-1), table).reshape(n, N_SEL)


def _sc_combine(eidx, coef_words, x, table):
    n = x.shape[0]
    tb = _sc_tok_blk(n)

    @functools.partial(
        pl.kernel, out_type=jax.ShapeDtypeStruct((n * D_MODEL,), F32), mesh=_sc_mesh(),
        compiler_params=pltpu.CompilerParams(needs_layout_passes=False),
        scratch_types=[pltpu.VMEM((tb * N_SEL,), I32), pltpu.VMEM((tb * N_SEL,), I32),
                       pltpu.VMEM((SC_NBUF, SC_GROUP, SC_ROW_WORDS), I32), pltpu.VMEM((tb * D_MODEL,), F32),
                       pltpu.SemaphoreType.DMA((SC_NBUF,))],
        cost_estimate=_expert_cost(n), name="peer_combine_sc")
    def run(eidx_hbm, c_hbm, x_hbm, tbl_hbm, out_hbm, idx_v, c_v, rows_v, out_v, sem):
        def load_block(tok0):
            pltpu.sync_copy(eidx_hbm.at[pl.ds(tok0 * N_SEL, tb * N_SEL)], idx_v)
            pltpu.sync_copy(c_hbm.at[pl.ds(tok0 * N_SEL, tb * N_SEL)], c_v)
            pltpu.sync_copy(x_hbm.at[pl.ds(tok0 * D_MODEL, tb * D_MODEL)], out_v)

        def compute(tt, g, rows):
            base = tt * N_SEL + g * SC_GROUP
            cs = [plsc.bitcast(plsc.load_gather(c_v, [jnp.full((SC_LANES,), r, I32) + base]), BF16)
                  for r in range(SC_GROUP)]

            @plsc.parallel_loop(0, SC_ROW_WORDS // SC_LANES, unroll=2)
            def _(j):
                first = pl.ds(tt * D_MODEL + j * SC_LANES, SC_LANES)
                second = pl.ds(tt * D_MODEL + SC_ROW_WORDS + j * SC_LANES, SC_LANES)
                acc_lo = out_v[first]
                acc_hi = out_v[second]
                words = pl.ds(j * SC_LANES, SC_LANES)
                for r in range(0, SC_GROUP, 2):
                    lo, hi = _pair_products(rows[r, words], cs[r], rows[r + 1, words], cs[r + 1])
                    acc_lo = acc_lo + lo
                    acc_hi = acc_hi + hi
                out_v[first] = acc_lo
                out_v[second] = acc_hi

        def store_block(tok0):
            pltpu.sync_copy(out_v, out_hbm.at[pl.ds(tok0 * D_MODEL, tb * D_MODEL)])

        _sc_token_blocks(n, tbl_hbm, idx_v, rows_v, sem, load_block, compute, store_block)

    return run(eidx.reshape(-1), coef_words.reshape(-1), x.reshape(-1), table).reshape(n, D_MODEL)


def _gate_act_kernel(hid_ref, gate_ref, o_ref):
    hid = hid_ref[...]
    o_ref[...] = gate_ref[...] * (0.5 * hid * (1.0 + lax.erf(hid * np.float32(np.sqrt(0.5)))))


def _gate_act(hid, gate):
    n = hid.shape[0]
    tm = math.gcd(n, 2048)
    rows = pl.BlockSpec((tm, N_SEL), lambda i: (i, 0))
    return pl.pallas_call(
        _gate_act_kernel, grid=(n // tm,), in_specs=[rows, rows], out_specs=rows,
        out_shape=jax.ShapeDtypeStruct((n, N_SEL), F32), compiler_params=_params(("parallel",)),
        name="peer_gate_act",
    )(hid, gate)


def _t5_bucket(rel):
    nb = NUM_BUCKETS // 2
    max_exact = nb // 2
    ret = jnp.where(rel > 0, nb, 0)
    n = jnp.abs(rel)
    nf = jnp.maximum(n, 1).astype(F32)
    large = max_exact + (jnp.log(nf / max_exact) / math.log(MAX_DISTANCE / max_exact)
                         * (nb - max_exact)).astype(I32)
    large = jnp.minimum(large, nb - 1)
    return ret + jnp.where(n < max_exact, n, large)


def _rel_bias(rel_bias, n_q, n_k, n_before):
    rel = (jnp.arange(n_k)[None, :] - n_before) - jnp.arange(n_q)[:, None]
    return jnp.transpose(rel_bias[_t5_bucket(rel)].astype(F32), (2, 0, 1))


def _state_to_pairs(wkv):
    b = wkv.shape[0]
    s = wkv.reshape(b, N_PAIRS, 2, R_HD, R_HD)
    z = jnp.zeros_like(s[:, :, 0])
    top = jnp.concatenate([s[:, :, 0], z], axis=-1)
    bot = jnp.concatenate([z, s[:, :, 1]], axis=-1)
    return jnp.concatenate([top, bot], axis=-2)


def _pairs_to_state(s_bd):
    b = s_bd.shape[0]
    return jnp.stack([s_bd[:, :, :R_HD, :R_HD], s_bd[:, :, R_HD:, R_HD:]], axis=2).reshape(
        b, R_HEADS, R_HD, R_HD)


def _layer_dense(x, shift0, wkv0, kv_cache, bias, lp):
    b, t, _ = x.shape
    n = b * t
    z = _norm_matmul(x.reshape(n, D_MODEL), lp["ln1_g"].reshape(1, -1), lp["w_in"], 512).reshape(b, t, IN_COLS)
    r, k2, v, kk, bb, lw, g, bonus, qn, kn = _prep(z, shift0, lp)
    v_new = z[:, :, R_PROJ + A_DIM + KV_DIM:]
    if t % CHUNK:
        pad = lambda a: jnp.pad(a, ((0, 0), (0, CHUNK - t % CHUNK), (0, 0)))
        y, s_fin = _chunk_scan(*(pad(a) for a in (r, k2, v, kk, bb, lw)), _state_to_pairs(wkv0))
        y = y[:, :t]
    else:
        y, s_fin = _chunk_scan(r, k2, v, kk, bb, lw, _state_to_pairs(wkv0))
    if kv_cache is None:
        a_out = _swa_prompt(qn, kn, z, bias, lp["sink"])
    else:
        a_out = _swa_sample(qn, kn, v_new, kv_cache[0].reshape(b, -1, KV_DIM),
                            kv_cache[1].reshape(b, -1, KV_DIM), bias, lp["sink"])
    flat = lambda a: a.reshape(n, a.shape[-1])
    x1, h, qq = _mix_out(flat(y), flat(bonus), flat(g), flat(a_out), flat(x), lp)
    eidx, gate = _topk(qq, lp["sub_keys"])
    sel = lambda a: jnp.transpose(a, (2, 0, 1)).reshape(n, N_SEL)
    return ((sel(eidx), sel(gate), h, x1),
            (_pairs_to_state(s_fin), z[:, -1, :R_PROJ],
             kn.reshape(b, t, A_KV, A_HD), v_new.reshape(b, t, A_KV, A_HD)))


def _bf16_pair_words(a):
    k = a.shape[1] // 2
    bits = lax.bitcast_convert_type(a.astype(BF16), jnp.uint16).astype(jnp.uint32)
    return lax.bitcast_convert_type(bits[:, :k] | (bits[:, k:] << 16), I32)


def _layer_experts(ops, lp, after=None):
    eidx, gate, h, x1 = ops
    h_words = _bf16_pair_words(h)
    if after is not None:
        h_words, after = lax.optimization_barrier((h_words, after))
    coef = _gate_act(_sc_hidden(eidx, h_words, lp["peer_u"]), gate)
    coef_words = _bf16_pair_words(jnp.concatenate([coef, coef], axis=1))
    return _sc_combine(eidx, coef_words, x1, lp["peer_v"]), after


def kernel(x_prompt, x_sample, state_rwkv_wkv, state_rwkv_shift, cache_swa_k, cache_swa_v, ln1_g, w_in,
           mu_shift, w0, w2, a0, a2, g2, k_k, k_a, r_k, gn_g, gn_b, q_norm_g, k_norm_g, attn_sink,
           rel_bias, w_out, ln2_g, w_pq, sub_keys, peer_u, peer_v):
    depth = w_in.shape[0]
    b_p, s_p = x_prompt.shape[:2]
    b_s, t_s = x_sample.shape[:2]
    n_cache = cache_swa_k.shape[2]
    n_keep = min(WINDOW, s_p)
    bias_p = _rel_bias(rel_bias, CHUNK, (WIN_CHUNKS + 1) * CHUNK, WIN_CHUNKS * CHUNK)
    bias_s = _rel_bias(rel_bias, t_s, n_cache + t_s, n_cache)
    head_id = jnp.arange(R_DIM) // R_HD
    e64 = (head_id[:, None] == head_id[None, :]).astype(BF16)
    zpad = jnp.zeros((LANES - W_LORA, R_DIM), F32)
    n_groups = PROMPT_GROUPS if b_p % PROMPT_GROUPS == 0 else 1
    xp, xs = jnp.split(x_prompt, n_groups, axis=0), x_sample
    outs = [[] for _ in range(8)]
    for l in range(depth):
        lp = {
            "ln1_g": ln1_g[l], "w_in": w_in[l].astype(BF16), "mu": mu_shift[l], "w0": w0[l], "a0": a0[l],
            "k_k": k_k[l], "k_a": k_a[l], "r_k": r_k[l].reshape(-1), "gn_g": gn_g[l], "gn_b": gn_b[l],
            "w2p": jnp.concatenate([w2[l], zpad], axis=0).astype(BF16),
            "a2p": jnp.concatenate([zpad, a2[l]], axis=0).astype(BF16),
            "g2": g2[l].astype(BF16), "e64": e64,
            "q_gain": jnp.tile(q_norm_g[l], A_HEADS), "k_gain": jnp.tile(k_norm_g[l], A_KV),
            "sink": attn_sink[l].astype(F32), "w_out": w_out[l].astype(BF16), "ln2_g": ln2_g[l],
            "w_pq": w_pq[l].astype(BF16), "sub_keys": sub_keys[l], "peer_u": _bf16_pair_words(peer_u[l]), "peer_v": _bf16_pair_words(peer_v[l]),
        }
        parts = []
        for xg in xp:
            if parts:
                (eidx, *rest), states = parts[-1]
                xg, eidx = lax.optimization_barrier((xg, eidx))
                parts[-1] = ((eidx, *rest), states)
            parts.append(_layer_dense(xg, jnp.zeros((xg.shape[0], R_PROJ), F32),
                                      jnp.zeros((xg.shape[0], R_HEADS, R_HD, R_HD), F32), None, bias_p, lp))
        wkv_p, sh_p, k_p, v_p = (jnp.concatenate([pt[1][i] for pt in parts], axis=0) for i in range(4))
        outs_x = []
        for ops, _ in parts:
            if outs_x:
                xo, outs_x[-1] = _layer_experts(ops, lp, after=outs_x[-1])
            else:
                xo, _ = _layer_experts(ops, lp)
            outs_x.append(xo)
        xp = [xo.reshape(xg.shape) for xo, xg in zip(outs_x, xp)]
        ops_s, (wkv_s, sh_s, k_s, v_s) = _layer_dense(
            xs, state_rwkv_shift[l], state_rwkv_wkv[l].astype(F32), (cache_swa_k[l], cache_swa_v[l]),
            bias_s, lp)
        xs = _layer_experts(ops_s, lp)[0].reshape(xs.shape)
        for lst, val in zip(outs, (wkv_p, sh_p, k_p[:, s_p - n_keep:], v_p[:, s_p - n_keep:],
                                   wkv_s, sh_s, k_s, v_s)):
            lst.append(val)
    return (jnp.concatenate(xp, axis=0), xs) + tuple(jnp.stack(o) for o in outs)
```
